```python
import jax, jax.numpy as jnp
from jax import lax
import numpy as np

D_MODEL = 2048
BATCH = 1
SEQ = 8192
DEPTH = 2

ROPE_THETA = 500000.0
Q_BLOCK = 128
MLA_HEADS = 8
MLA_Q_LORA = 512
MLA_KV_LORA = 256
MLA_NOPE = 128
MLA_ROPE = 64
MLA_V = 128
DSA_HEADS = 8
DSA_KV_HEADS = 2
DSA_HEAD_DIM = 128
DSA_ROT = DSA_HEAD_DIM // 4
IDX_HEADS = 16
IDX_DIM = 64
IDX_ROT = IDX_DIM // 4
IDX_TOPK_MAX = 256
RET_HEADS = 8
RET_QK_DIM = 256
RET_V_DIM = 512
RET_CHUNK = 128
RET_THETA = 10000.0
N_EXPERTS = 64
N_GROUPS = 8
EXPERTS_PER_GROUP = N_EXPERTS // N_GROUPS
TOP_K = 2
D_EXPERT = 512
D_SHARED = 1024
DEEPNORM_ALPHA = (2.0 * DEPTH) ** 0.25
DEEPNORM_BETA = (8.0 * DEPTH) ** -0.25

SPLITS_AB = (MLA_Q_LORA, MLA_KV_LORA, MLA_ROPE, DSA_HEADS * DSA_HEAD_DIM,
             2 * DSA_KV_HEADS * DSA_HEAD_DIM, IDX_HEADS * IDX_DIM, IDX_DIM, IDX_HEADS)
IN_AB = sum(SPLITS_AB)
OUT_AB = MLA_HEADS * MLA_V + DSA_HEADS * DSA_HEAD_DIM
SPLITS_C = (RET_HEADS * RET_QK_DIM, RET_HEADS * RET_QK_DIM, RET_HEADS * RET_V_DIM, RET_HEADS * RET_V_DIM)
IN_C = sum(SPLITS_C)
OUT_C = RET_HEADS * RET_V_DIM

kernel_name = 'hybrid_mla_dsa_retention_grouped_moe'


def split_cols(h, sizes):
    return jnp.split(h, [int(c) for c in np.cumsum(sizes)[:-1]], axis=-1)


def layernorm(x, g, b, eps=1e-5):
    xf = x.astype(jnp.float32)
    mu = xf.mean(-1, keepdims=True)
    var = jnp.square(xf - mu).mean(-1, keepdims=True)
    return ((xf - mu) * lax.rsqrt(var + eps) * g + b).astype(x.dtype)


def rmsnorm(x, g, eps=1e-6):
    xf = x.astype(jnp.float32)
    return (xf * lax.rsqrt(jnp.mean(xf * xf, -1, keepdims=True) + eps) * g).astype(x.dtype)


def rotary(x, pos, rot_dim, theta):
    half = rot_dim // 2
    inv = (1.0 / (theta ** (np.arange(half, dtype=np.float32) / half))).astype(np.float32)
    ang = pos.astype(jnp.float32)[..., None] * inv
    cos = jnp.cos(ang)[:, :, None, :]
    sin = jnp.sin(ang)[:, :, None, :]
    x1 = x[..., :half].astype(jnp.float32)
    x2 = x[..., half:rot_dim].astype(jnp.float32)
    rot = jnp.concatenate([x1 * cos - x2 * sin, x1 * sin + x2 * cos], -1).astype(x.dtype)
    return jnp.concatenate([rot, x[..., rot_dim:]], -1)


def to_blocks(t):
    nb = t.shape[1] // Q_BLOCK
    return jnp.moveaxis(t.reshape((t.shape[0], nb, Q_BLOCK) + t.shape[2:]), 1, 0)


def from_blocks(o):
    return jnp.moveaxis(o, 0, 1).reshape((o.shape[1], o.shape[0] * o.shape[2]) + o.shape[3:])


def dense_causal_attention(q, k, v, scale):
    S = q.shape[1]
    key_idx = jnp.arange(S)

    def block(args):
        i, qi = args
        q_idx = i * Q_BLOCK + jnp.arange(Q_BLOCK)
        s = jnp.einsum('bqhd,bkhd->bhqk', qi, k).astype(jnp.float32) * scale
        s = jnp.where(key_idx[None, :] <= q_idx[:, None], s, -jnp.inf)
        p = jax.nn.softmax(s, axis=-1).astype(v.dtype)
        o = jnp.einsum('bhqk,bkhd->bqhd', p, v)
        return o.reshape(o.shape[0], Q_BLOCK, -1)

    out = lax.map(block, (jnp.arange(S // Q_BLOCK), to_blocks(q)))
    return from_blocks(out)


def dsa_attention(q, k, v, iq, ik, iw):
    B, S = q.shape[:2]
    n_sel = min(IDX_TOPK_MAX, S // 4)
    rep = DSA_HEADS // DSA_KV_HEADS
    key_idx = jnp.arange(S)
    gather = jax.vmap(lambda t, idx: t[idx])

    def block(args):
        i, qi, iqi, iwi = args
        q_idx = i * Q_BLOCK + jnp.arange(Q_BLOCK)
        rel = jax.nn.relu(jnp.einsum('bqhd,bkd->bqhk', iqi, ik).astype(jnp.float32) * IDX_DIM ** -0.5)
        score = jnp.einsum('bqh,bqhk->bqk', iwi.astype(jnp.float32), rel)
        score = jnp.where(key_idx[None, None, :] <= q_idx[None, :, None], score, -jnp.inf)
        _, sel = lax.top_k(score, n_sel)
        valid = sel <= q_idx[None, :, None]
        ks = gather(k, sel)
        vs = gather(v, sel)
        qg = qi.reshape(B, Q_BLOCK, DSA_KV_HEADS, rep, DSA_HEAD_DIM)
        s = jnp.einsum('bqgrd,bqkgd->bqgrk', qg, ks).astype(jnp.float32) * DSA_HEAD_DIM ** -0.5
        s = jnp.where(valid[:, :, None, None, :], s, -jnp.inf)
        p = jax.nn.softmax(s, axis=-1).astype(v.dtype)
        o = jnp.einsum('bqgrk,bqkgd->bqgrd', p, vs)
        return o.reshape(B, Q_BLOCK, DSA_HEADS * DSA_HEAD_DIM)

    out = lax.map(block, (jnp.arange(S // Q_BLOCK), to_blocks(q), to_blocks(iq), to_blocks(iw)))
    return from_blocks(out)


def mla_dsa_mixer(x, pos, w_in, q_norm, w_uq, kv_norm, w_ukv, w_out):
    B, S, _ = x.shape
    cq, ckv, krope, dq, dkv, iq, ik, iw = split_cols(x @ w_in, SPLITS_AB)
    qa = (rmsnorm(cq, q_norm) @ w_uq).reshape(B, S, MLA_HEADS, MLA_NOPE + MLA_ROPE)
    qa = jnp.concatenate([qa[..., :MLA_NOPE], rotary(qa[..., MLA_NOPE:], pos, MLA_ROPE, ROPE_THETA)], -1)
    kva = (rmsnorm(ckv, kv_norm) @ w_ukv).reshape(B, S, MLA_HEADS, MLA_NOPE + MLA_V)
    k_pe = rotary(krope[:, :, None, :], pos, MLA_ROPE, ROPE_THETA)
    ka = jnp.concatenate([kva[..., :MLA_NOPE],
                          jnp.broadcast_to(k_pe, (B, S, MLA_HEADS, MLA_ROPE))], -1)
    va = kva[..., MLA_NOPE:]
    out_a = dense_causal_attention(qa, ka, va, (MLA_NOPE + MLA_ROPE) ** -0.5)
    qb = rotary(dq.reshape(B, S, DSA_HEADS, DSA_HEAD_DIM), pos, DSA_ROT, ROPE_THETA)
    kvb = dkv.reshape(B, S, 2, DSA_KV_HEADS, DSA_HEAD_DIM)
    kb = rotary(kvb[:, :, 0], pos, DSA_ROT, ROPE_THETA)
    vb = kvb[:, :, 1]
    iq = rotary(iq.reshape(B, S, IDX_HEADS, IDX_DIM), pos, IDX_ROT, ROPE_THETA)
    ik = rotary(ik[:, :, None, :], pos, IDX_ROT, ROPE_THETA)[:, :, 0]
    iw = iw * IDX_HEADS ** -0.5
    out_b = dsa_attention(qb, kb, vb, iq, ik, iw)
    return jnp.concatenate([out_a, out_b], -1) @ w_out


def retention_mixer(x, pos, w_in, gn_g, gn_b, w_out):
    B, S, _ = x.shape
    H, DK, DV, C = RET_HEADS, RET_QK_DIM, RET_V_DIM, RET_CHUNK
    q, k, v, g = split_cols(x @ w_in, SPLITS_C)
    q = rotary(q.reshape(B, S, H, DK), pos, DK, RET_THETA)
    k = rotary(k.reshape(B, S, H, DK), pos, DK, RET_THETA) * DK ** -0.5
    v = v.reshape(B, S, H, DV)
    log_g = np.log(1.0 - 2.0 ** (-5.0 - np.arange(H, dtype=np.float32))).astype(np.float32)
    idx = np.arange(C, dtype=np.float32)
    diff = idx[:, None] - idx[None, :]
    decay_in = jnp.where(diff[None] >= 0, jnp.exp(jnp.maximum(diff, 0.0)[None] * log_g[:, None, None]), 0.0)
    xi = jnp.exp((idx + 1.0)[None, :] * log_g[:, None])
    zeta = jnp.exp((C - 1.0 - idx)[None, :] * log_g[:, None])
    chunk_decay = jnp.exp(C * log_g)
    n = S // C
    chunks = lambda t: jnp.transpose(t.reshape(B, n, C, H, t.shape[-1]), (1, 0, 3, 2, 4))

    def step(R, inp):
        qi, ki, vi = inp
        inner = jnp.einsum('bhqd,bhkd->bhqk', qi, ki).astype(jnp.float32) * decay_in
        o = jnp.einsum('bhqk,bhkv->bhqv', inner, vi.astype(jnp.float32)) \
            + jnp.einsum('bhqd,bhdv->bhqv', qi.astype(jnp.float32), R) * xi[None, :, :, None]
        R = chunk_decay[None, :, None, None] * R + jnp.einsum(
            'bhkd,bhkv->bhdv', ki.astype(jnp.float32) * zeta[None, :, :, None], vi.astype(jnp.float32))
        return R, o

    R0 = jnp.zeros((B, H, DK, DV), jnp.float32)
    _, o = lax.scan(step, R0, (chunks(q), chunks(k), chunks(v)))
    o = jnp.transpose(o, (1, 0, 3, 2, 4)).reshape(B, S, H, DV)
    mu = o.mean(-1, keepdims=True)
    var = jnp.square(o - mu).mean(-1, keepdims=True)
    y = ((o - mu) * lax.rsqrt(var + 1e-5)).reshape(B, S, H * DV) * gn_g + gn_b
    return (jax.nn.silu(g.astype(jnp.float32)) * y).astype(x.dtype) @ w_out


def grouped_moe(x, router_w, router_b, w_gate, w_up, w_down, sh_gate, sh_up, sh_down):
    B, S, D = x.shape
    T = B * S
    xt = x.reshape(T, D)
    scores = jax.nn.sigmoid((xt @ router_w).astype(jnp.float32))
    biased = (scores + router_b).reshape(T, N_GROUPS, EXPERTS_PER_GROUP)
    group_score = lax.top_k(biased, 2)[0].sum(-1)
    gsel = jnp.argmax(group_score, axis=-1)
    in_group = jnp.take_along_axis(biased, gsel[:, None, None], axis=1)[:, 0]
    _, loc = lax.top_k(in_group, TOP_K)
    expert = gsel[:, None] * EXPERTS_PER_GROUP + loc
    gate = jnp.take_along_axis(scores, expert, axis=1)
    gate = gate / gate.sum(-1, keepdims=True)
    flat_e = expert.reshape(-1)
    order = jnp.argsort(flat_e)
    tok = order // TOP_K
    xs = xt[tok]
    sizes = jnp.bincount(flat_e, length=N_EXPERTS).astype(jnp.int32)
    h = jax.nn.silu(lax.ragged_dot(xs, w_gate, sizes)) * lax.ragged_dot(xs, w_up, sizes)
    y = lax.ragged_dot(h, w_down, sizes).astype(jnp.float32) * gate.reshape(-1)[order][:, None]
    routed = jnp.zeros((T, D), jnp.float32).at[tok].add(y)
    shared = (jax.nn.silu(xt @ sh_gate) * (xt @ sh_up)) @ sh_down
    return (routed + shared.astype(jnp.float32)).astype(x.dtype).reshape(B, S, D)


def setup_inputs(seed: int = 0) -> dict:
    key = jax.random.key(seed)
    ks = iter(jax.random.split(key, 64))
    nrm = lambda shape, scale: jax.random.normal(next(ks), shape, jnp.float32) * scale
    gain = lambda n: 1.0 + nrm((n,), 0.02)
    bias = lambda n: nrm((n,), 0.02)
    d = D_MODEL
    p = {}
    p['x'] = nrm((BATCH, SEQ, d), 1.0)
    p['positions'] = jnp.broadcast_to(jnp.arange(SEQ, dtype=jnp.int32), (BATCH, SEQ))
    p['router_w'] = nrm((d, N_EXPERTS), d ** -0.5)
    p['router_b'] = nrm((N_EXPERTS,), 0.01)
    p['l0_w_in'] = nrm((d, IN_AB), d ** -0.5)
    p['l0_mla_q_norm'] = gain(MLA_Q_LORA)
    p['l0_mla_w_uq'] = nrm((MLA_Q_LORA, MLA_HEADS * (MLA_NOPE + MLA_ROPE)), MLA_Q_LORA ** -0.5)
    p['l0_mla_kv_norm'] = gain(MLA_KV_LORA)
    p['l0_mla_w_ukv'] = nrm((MLA_KV_LORA, MLA_HEADS * (MLA_NOPE + MLA_V)), MLA_KV_LORA ** -0.5)
    p['l0_w_out'] = nrm((OUT_AB, d), OUT_AB ** -0.5 * DEEPNORM_BETA)
    p['l1_w_in'] = nrm((d, IN_C), d ** -0.5)
    p['l1_ret_gn_g'] = gain(OUT_C)
    p['l1_ret_gn_b'] = bias(OUT_C)
    p['l1_w_out'] = nrm((OUT_C, d), OUT_C ** -0.5 * DEEPNORM_BETA)
    for l in range(DEPTH):
        p['l%d_ln_mix_g' % l] = gain(d)
        p['l%d_ln_mix_b' % l] = bias(d)
        p['l%d_moe_w_gate' % l] = nrm((N_EXPERTS, d, D_EXPERT), d ** -0.5)
        p['l%d_moe_w_up' % l] = nrm((N_EXPERTS, d, D_EXPERT), d ** -0.5)
        p['l%d_moe_w_down' % l] = nrm((N_EXPERTS, D_EXPERT, d), D_EXPERT ** -0.5 * DEEPNORM_BETA)
        p['l%d_sh_gate' % l] = nrm((d, D_SHARED), d ** -0.5)
        p['l%d_sh_up' % l] = nrm((d, D_SHARED), d ** -0.5)
        p['l%d_sh_down' % l] = nrm((D_SHARED, d), D_SHARED ** -0.5 * DEEPNORM_BETA)
        p['l%d_ln_ffn_g' % l] = gain(d)
        p['l%d_ln_ffn_b' % l] = bias(d)
    return p


def reference(x, positions, router_w, router_b,
              l0_w_in, l0_mla_q_norm, l0_mla_w_uq, l0_mla_kv_norm, l0_mla_w_ukv, l0_w_out,
              l1_w_in, l1_ret_gn_g, l1_ret_gn_b, l1_w_out,
              l0_ln_mix_g, l0_ln_mix_b, l0_moe_w_gate, l0_moe_w_up, l0_moe_w_down,
              l0_sh_gate, l0_sh_up, l0_sh_down, l0_ln_ffn_g, l0_ln_ffn_b,
              l1_ln_mix_g, l1_ln_mix_b, l1_moe_w_gate, l1_moe_w_up, l1_moe_w_down,
              l1_sh_gate, l1_sh_up, l1_sh_down, l1_ln_ffn_g, l1_ln_ffn_b):
    ln_mix = ((l0_ln_mix_g, l0_ln_mix_b), (l1_ln_mix_g, l1_ln_mix_b))
    ln_ffn = ((l0_ln_ffn_g, l0_ln_ffn_b), (l1_ln_ffn_g, l1_ln_ffn_b))
    experts = ((l0_moe_w_gate, l0_moe_w_up, l0_moe_w_down, l0_sh_gate, l0_sh_up, l0_sh_down),
               (l1_moe_w_gate, l1_moe_w_up, l1_moe_w_down, l1_sh_gate, l1_sh_up, l1_sh_down))
    for layer in range(DEPTH):
        if layer % 2 == 0:
            mix = mla_dsa_mixer(x, positions, l0_w_in, l0_mla_q_norm, l0_mla_w_uq,
                                l0_mla_kv_norm, l0_mla_w_ukv, l0_w_out)
        else:
            mix = retention_mixer(x, positions, l1_w_in, l1_ret_gn_g, l1_ret_gn_b, l1_w_out)
        x = layernorm(DEEPNORM_ALPHA * x + mix, *ln_mix[layer])
        x = layernorm(DEEPNORM_ALPHA * x + grouped_moe(x, router_w, router_b, *experts[layer]), *ln_ffn[layer])
    return x
```

```python
import functools

import numpy as np
import jax
import jax.numpy as jnp
from jax import lax
from jax.experimental import pallas as pl
from jax.experimental.pallas import tpu as pltpu

F32 = jnp.float32
BF16 = jnp.bfloat16

D_MODEL = 2048
DEPTH = 2
ROPE_THETA = 500000.0
MLA_HEADS = 8
MLA_Q_LORA = 512
MLA_KV_LORA = 256
MLA_NOPE = 128
MLA_ROPE = 64
MLA_V = 128
DSA_HEADS = 8
DSA_KV_HEADS = 2
DSA_HEAD_DIM = 128
DSA_ROT = DSA_HEAD_DIM // 4
IDX_HEADS = 16
IDX_DIM = 64
IDX_ROT = IDX_DIM // 4
IDX_TOPK_MAX = 256
RET_HEADS = 8
RET_QK_DIM = 256
RET_V_DIM = 512
RET_CHUNK = 128
RET_THETA = 10000.0
N_EXPERTS = 64
N_GROUPS = 8
EXPERTS_PER_GROUP = N_EXPERTS // N_GROUPS
TOP_K = 2
D_EXPERT = 512
D_SHARED = 1024
DEEPNORM_ALPHA = (2.0 * DEPTH) ** 0.25

SPLITS_AB = (MLA_Q_LORA, MLA_KV_LORA, MLA_ROPE, DSA_HEADS * DSA_HEAD_DIM,
             2 * DSA_KV_HEADS * DSA_HEAD_DIM, IDX_HEADS * IDX_DIM, IDX_DIM, IDX_HEADS)
IN_AB = sum(SPLITS_AB)

VMEM_LIMIT_BYTES = 56 * 1024 * 1024
LANES = 128
MASKED_SCORE = -1e30
INT32_MIN = -2 ** 31


def _params(*sem):
    return pltpu.CompilerParams(dimension_semantics=sem, vmem_limit_bytes=VMEM_LIMIT_BYTES)


def _mm_kernel(*refs, norm_eps):
    if norm_eps is None:
        a_ref, b_ref, o_ref = refs
        a = a_ref[...]
    else:
        a_ref, g_ref, b_ref, o_ref = refs
        af = a_ref[...].astype(F32)
        a = af * lax.rsqrt(jnp.mean(af * af, axis=-1, keepdims=True) + norm_eps) * g_ref[...]
    o_ref[...] = jnp.dot(a.astype(BF16), b_ref[...].astype(BF16),
                         preferred_element_type=F32).astype(o_ref.dtype)


def matmul(a, b, out_dtype, tm, tn, *, name, a_cols=None, norm_gain=None, norm_eps=1e-6):
    m = a.shape[0]
    k, n = b.shape
    col_blk = 0 if a_cols is None else a_cols[0]
    assert (a.shape[1] == k) if a_cols is None else (a_cols[1] == k)
    assert m % tm == 0 and n % tn == 0
    in_specs = [pl.BlockSpec((tm, k), lambda i, j: (i, col_blk))]
    args = [a]
    if norm_gain is not None:
        in_specs.append(pl.BlockSpec((1, k), lambda i, j: (0, 0)))
        args.append(norm_gain.reshape(1, k))
    in_specs.append(pl.BlockSpec((k, tn), lambda i, j: (0, j)))
    args.append(b)
    return pl.pallas_call(
        functools.partial(_mm_kernel, norm_eps=None if norm_gain is None else norm_eps),
        grid=(m // tm, n // tn),
        in_specs=in_specs,
        out_specs=pl.BlockSpec((tm, tn), lambda i, j: (i, j)),
        out_shape=jax.ShapeDtypeStruct((m, n), out_dtype),
        compiler_params=_params("parallel", "parallel"),
        name=name,
    )(*args)


def _layer_norm(z, g, b):
    mu = jnp.mean(z, axis=-1, keepdims=True)
    zc = z - mu
    var = jnp.mean(zc * zc, axis=-1, keepdims=True)
    return zc * lax.rsqrt(var + 1e-5) * g + b


def _add_ln_kernel(x_ref, y_ref, g_ref, b_ref, o_ref, obf_ref):
    z = DEEPNORM_ALPHA * x_ref[...] + y_ref[...]
    out = _layer_norm(z, g_ref[...], b_ref[...])
    o_ref[...] = out
    obf_ref[...] = out.astype(BF16)


def add_layer_norm(x, y, g, b, tm=256):
    s, d = x.shape
    row = pl.BlockSpec((tm, d), lambda i: (i, 0))
    vec = pl.BlockSpec((1, d), lambda i: (0, 0))
    return pl.pallas_call(
        _add_ln_kernel,
        grid=(s // tm,),
        in_specs=[row, row, vec, vec],
        out_specs=[row, row],
        out_shape=[jax.ShapeDtypeStruct((s, d), F32), jax.ShapeDtypeStruct((s, d), BF16)],
        compiler_params=_params("parallel"),
        name="add_layer_norm",
    )(x, y, g.reshape(1, d), b.reshape(1, d))


def _online_softmax_step(s, v, m_sc, l_sc, acc_sc):
    m_prev = m_sc[...]
    m_new = jnp.maximum(m_prev, jnp.max(s, axis=-1, keepdims=True))
    alpha = jnp.exp(m_prev - m_new)
    p = jnp.exp(s - m_new)
    l_sc[...] = alpha * l_sc[...] + jnp.sum(p, axis=-1, keepdims=True)
    acc_sc[...] = alpha * acc_sc[...] + jnp.dot(p.astype(BF16), v, preferred_element_type=F32)
    m_sc[...] = m_new


def _mla_kernel(q_ref, kn_ref, kp_ref, v_ref, o_ref, m_sc, l_sc, acc_sc, *, tq, tk):
    qi = pl.program_id(1)
    ki = pl.program_id(2)
    last = ((qi + 1) * tq - 1) // tk

    @pl.when(ki == 0)
    def _():
        m_sc[...] = jnp.full_like(m_sc, MASKED_SCORE)
        l_sc[...] = jnp.zeros_like(l_sc)
        acc_sc[...] = jnp.zeros_like(acc_sc)

    def step(masked):
        k = jnp.concatenate([kn_ref[...], kp_ref[...]], axis=1)
        s = lax.dot_general(q_ref[...], k, (((1,), (1,)), ((), ())), preferred_element_type=F32)
        if masked:
            row = qi * tq + lax.broadcasted_iota(jnp.int32, (tq, tk), 0)
            col = ki * tk + lax.broadcasted_iota(jnp.int32, (tq, tk), 1)
            s = jnp.where(col <= row, s, MASKED_SCORE)
        _online_softmax_step(s, v_ref[...], m_sc, l_sc, acc_sc)

    crosses_diagonal = (ki + 1) * tk - 1 > qi * tq

    @pl.when((ki <= last) & crosses_diagonal)
    def _():
        step(True)

    @pl.when((ki <= last) & jnp.logical_not(crosses_diagonal))
    def _():
        step(False)

    @pl.when(ki == last)
    def _():
        o_ref[...] = (acc_sc[...] / l_sc[...]).astype(o_ref.dtype)


def mla_attention(q, k_nope, k_pe, v, tq=256, tk=512):
    s = q.shape[0]
    h = MLA_HEADS
    nk = s // tk

    def kv_blk(hh, qi, ki):
        return (jnp.minimum(ki, ((qi + 1) * tq - 1) // tk), hh)

    return pl.pallas_call(
        functools.partial(_mla_kernel, tq=tq, tk=tk),
        grid=(h, s // tq, nk),
        in_specs=[
            pl.BlockSpec((tq, 2 * LANES), lambda hh, qi, ki: (qi, hh)),
            pl.BlockSpec((tk, LANES), kv_blk),
            pl.BlockSpec((tk, LANES), lambda hh, qi, ki: (kv_blk(hh, qi, ki)[0], 0)),
            pl.BlockSpec((tk, LANES), kv_blk),
        ],
        out_specs=pl.BlockSpec((tq, LANES), lambda hh, qi, ki: (qi, hh)),
        out_shape=jax.ShapeDtypeStruct((s, h * MLA_V), BF16),
        scratch_shapes=[pltpu.VMEM((tq, 1), F32), pltpu.VMEM((tq, 1), F32), pltpu.VMEM((tq, LANES), F32)],
        compiler_params=_params("parallel", "parallel", "arbitrary"),
        name="mla_attention",
    )(q, k_nope, k_pe, v)


def _sortable_key(x):
    bits = pltpu.bitcast(x, jnp.int32)
    return bits ^ ((bits >> 31) & 0x7FFFFFFF)


def _dsa_kernel(iq_ref, iw_ref, ik_ref, q_ref, k_ref, v_ref, o_ref, key_sc, m_sc, l_sc, acc_sc,
                *, tq, tk, n_sel):
    i = pl.program_id(0)
    n_chunks = ((i + 1) * tq + tk - 1) // tk
    row = i * tq + lax.broadcasted_iota(jnp.int32, (tq, tk), 0)
    col0 = lax.broadcasted_iota(jnp.int32, (tq, tk), 1)
    rep = DSA_HEADS // DSA_KV_HEADS

    iq_all = iq_ref[...].reshape(IDX_HEADS * tq, IDX_DIM)
    iw = iw_ref[...]

    def index_chunk(c, carry):
        d = lax.dot_general(iq_all, ik_ref[c], (((1,), (1,)), ((), ())), preferred_element_type=F32)
        d = jnp.maximum(d, 0.0).reshape(IDX_HEADS, tq, tk)
        score = jnp.zeros((tq, tk), F32)
        for h in range(IDX_HEADS):
            score = score + iw[:, h:h + 1] * d[h]
        key = _sortable_key(score)
        key_sc[c] = jnp.where(c * tk + col0 <= row, key, INT32_MIN)
        return carry

    lax.fori_loop(0, n_chunks, index_chunk, 0)

    def search_bit(b, thr):
        cand = thr + lax.shift_left(jnp.int32(1), 31 - b)

        def count_chunk(c, cnt):
            key = key_sc[c]
            for j in range(tk // LANES):
                cnt = cnt + jnp.where(key[:, j * LANES:(j + 1) * LANES] >= cand, 1.0, 0.0)
            return cnt

        cnt = lax.fori_loop(0, n_chunks, count_chunk, jnp.zeros((tq, LANES), F32))
        total = jnp.sum(cnt, axis=-1, keepdims=True)
        return jnp.where(total >= n_sel, cand, thr)

    thr = lax.fori_loop(0, 32, search_bit, jnp.full((tq, LANES), INT32_MIN, jnp.int32))
    thr = jnp.maximum(thr, INT32_MIN + 1)[:, :1]

    for g in range(DSA_KV_HEADS):
        qg = q_ref[g * rep:(g + 1) * rep].reshape(rep * tq, DSA_HEAD_DIM)
        m_sc[...] = jnp.full_like(m_sc, MASKED_SCORE)
        l_sc[...] = jnp.zeros_like(l_sc)
        acc_sc[...] = jnp.zeros_like(acc_sc)

        def attend_chunk(c, carry):
            kc = k_ref[c][:, g * DSA_HEAD_DIM:(g + 1) * DSA_HEAD_DIM]
            vc = v_ref[c][:, g * DSA_HEAD_DIM:(g + 1) * DSA_HEAD_DIM]
            s = lax.dot_general(qg, kc, (((1,), (1,)), ((), ())), preferred_element_type=F32)
            selected = key_sc[c] >= thr
            s = jnp.where(selected[None], s.reshape(rep, tq, tk), MASKED_SCORE).reshape(rep * tq, tk)
            _online_softmax_step(s, vc, m_sc, l_sc, acc_sc)
            return carry

        lax.fori_loop(0, n_chunks, attend_chunk, 0)
        out = acc_sc[...] / l_sc[...]
        for r in range(rep):
            hh = g * rep + r
            o_ref[:, hh * DSA_HEAD_DIM:(hh + 1) * DSA_HEAD_DIM] = out[r * tq:(r + 1) * tq].astype(o_ref.dtype)


def dsa_attention(iq, iw, ik, q, k, v, tq=128, tk=256):
    s = ik.shape[0]
    n_sel = min(IDX_TOPK_MAX, s // 4)
    nc = s // tk
    assert tk >= n_sel and s % tk == 0 and s % tq == 0
    rep = DSA_HEADS // DSA_KV_HEADS
    kvw = DSA_KV_HEADS * DSA_HEAD_DIM
    whole3 = lambda i: (0, 0, 0)
    return pl.pallas_call(
        functools.partial(_dsa_kernel, tq=tq, tk=tk, n_sel=n_sel),
        grid=(s // tq,),
        in_specs=[
            pl.BlockSpec((IDX_HEADS, tq, IDX_DIM), lambda i: (0, i, 0)),
            pl.BlockSpec((tq, IDX_HEADS), lambda i: (i, 0)),
            pl.BlockSpec((nc, tk, IDX_DIM), whole3),
            pl.BlockSpec((DSA_HEADS, tq, DSA_HEAD_DIM), lambda i: (0, i, 0)),
            pl.BlockSpec((nc, tk, kvw), whole3),
            pl.BlockSpec((nc, tk, kvw), whole3),
        ],
        out_specs=pl.BlockSpec((tq, DSA_HEADS * DSA_HEAD_DIM), lambda i: (i, 0)),
        out_shape=jax.ShapeDtypeStruct((s, DSA_HEADS * DSA_HEAD_DIM), BF16),
        scratch_shapes=[
            pltpu.VMEM((nc, tq, tk), jnp.int32),
            pltpu.VMEM((rep * tq, 1), F32),
            pltpu.VMEM((rep * tq, 1), F32),
            pltpu.VMEM((rep * tq, DSA_HEAD_DIM), F32),
        ],
        compiler_params=_params("parallel"),
        name="dsa_attention",
    )(iq, iw, ik.reshape(nc, tk, IDX_DIM), q, k.reshape(nc, tk, kvw), v.reshape(nc, tk, kvw))


def _retention_tables():
    h, c = RET_HEADS, RET_CHUNK
    log_g = np.log(1.0 - 2.0 ** (-5.0 - np.arange(h, dtype=np.float32))).astype(np.float32).astype(np.float64)
    idx = np.arange(c, dtype=np.float64)
    diff = idx[:, None] - idx[None, :]
    decay_in = np.where(diff[None] >= 0, np.exp(np.maximum(diff, 0.0)[None] * log_g[:, None, None]), 0.0)
    xi = np.exp((idx + 1.0)[None, :] * log_g[:, None])
    zeta = np.exp((c - 1.0 - idx)[None, :] * log_g[:, None])
    chunk_decay = np.exp(c * log_g)
    lane = np.ones((1, 1, LANES))
    return (decay_in.astype(np.float32), (xi[:, :, None] * lane).astype(np.float32),
            (zeta[:, :, None] * lane).astype(np.float32),
            (chunk_decay[:, None, None] * np.ones((1, 8, LANES))).astype(np.float32))


def _retention_kernel(q_ref, k_ref, v_ref, g_ref, cos_ref, sin_ref, din_ref, xi_ref, zeta_ref, cd_ref,
                      gng_ref, gnb_ref, o_ref, r_sc):
    n = pl.program_id(1)
    half = RET_QK_DIM // 2

    @pl.when(n == 0)
    def _():
        r_sc[...] = jnp.zeros_like(r_sc)

    cos = cos_ref[...]
    sin = sin_ref[...]

    def rope(x):
        x1 = x[:, :half]
        x2 = x[:, half:]
        return x1 * cos - x2 * sin, x1 * sin + x2 * cos

    q1, q2 = rope(q_ref[...])
    k1, k2 = rope(k_ref[...])
    scale = RET_QK_DIM ** -0.5
    qr = jnp.concatenate([q1, q2], axis=1).astype(BF16)
    kr = jnp.concatenate([k1 * scale, k2 * scale], axis=1)
    v = v_ref[...].astype(BF16)

    inner = lax.dot_general(qr, kr.astype(BF16), (((1,), (1,)), ((), ())), preferred_element_type=F32)
    inner = inner * din_ref[0]
    r = r_sc[...]
    cross = jnp.dot(qr, r.astype(BF16), preferred_element_type=F32)
    xi = xi_ref[0]
    o = jnp.dot(inner.astype(BF16), v, preferred_element_type=F32)
    o = o + cross * jnp.concatenate([xi] * (RET_V_DIM // LANES), axis=1)

    zeta = zeta_ref[0]
    kz = (kr * jnp.concatenate([zeta] * (RET_QK_DIM // LANES), axis=1)).astype(BF16)
    upd = lax.dot_general(kz, v, (((0,), (0,)), ((), ())), preferred_element_type=F32)
    r_sc[...] = cd_ref[0][:1, :1] * r + upd

    mu = jnp.mean(o, axis=-1, keepdims=True)
    oc = o - mu
    var = jnp.mean(oc * oc, axis=-1, keepdims=True)
    y = oc * lax.rsqrt(var + 1e-5) * gng_ref[...] + gnb_ref[...]
    gate = g_ref[...]
    o_ref[...] = (gate * jax.nn.sigmoid(gate) * y).astype(o_ref.dtype)


def retention(h1, cos, sin, gn_g, gn_b):
    s = h1.shape[0]
    hh, dk, dv, c = RET_HEADS, RET_QK_DIM, RET_V_DIM, RET_CHUNK
    din, xi, zeta, cd = (jnp.asarray(t) for t in _retention_tables())
    k_blk0 = hh * dk // dk
    v_blk0 = 2 * hh * dk // dv
    g_blk0 = v_blk0 + hh
    per_head = lambda r, w: pl.BlockSpec((1, r, w), lambda h, n: (h, 0, 0))
    return pl.pallas_call(
        _retention_kernel,
        grid=(hh, s // c),
        in_specs=[
            pl.BlockSpec((c, dk), lambda h, n: (n, h)),
            pl.BlockSpec((c, dk), lambda h, n: (n, k_blk0 + h)),
            pl.BlockSpec((c, dv), lambda h, n: (n, v_blk0 + h)),
            pl.BlockSpec((c, dv), lambda h, n: (n, g_blk0 + h)),
            pl.BlockSpec((c, dk // 2), lambda h, n: (n, 0)),
            pl.BlockSpec((c, dk // 2), lambda h, n: (n, 0)),
            per_head(c, c), per_head(c, LANES), per_head(c, LANES), per_head(8, LANES),
            pl.BlockSpec((1, dv), lambda h, n: (0, h)),
            pl.BlockSpec((1, dv), lambda h, n: (0, h)),
        ],
        out_specs=pl.BlockSpec((c, dv), lambda h, n: (n, h)),
        out_shape=jax.ShapeDtypeStruct((s, hh * dv), BF16),
        scratch_shapes=[pltpu.VMEM((dk, dv), F32)],
        compiler_params=_params("parallel", "arbitrary"),
        name="retention",
    )(h1, h1, h1, h1, cos, sin, din, xi, zeta, cd, gn_g.reshape(1, -1), gn_b.reshape(1, -1))


def _first_argmax(v, idx, n):
    m = jnp.max(v, axis=0, keepdims=True)
    first = jnp.min(jnp.where(v == m, idx, n), axis=0, keepdims=True)
    return m, first


def _router_kernel(x_ref, rwt_ref, rb_ref, e_ref, g_ref):
    tm = x_ref.shape[0]
    epg = EXPERTS_PER_GROUP
    logits = lax.dot_general(rwt_ref[...], x_ref[...], (((1,), (1,)), ((), ())),
                             precision=lax.Precision.HIGHEST, preferred_element_type=F32)
    scores = jax.nn.sigmoid(logits)
    biased = scores + rb_ref[...]
    idx = lax.broadcasted_iota(jnp.int32, (epg, tm), 0)

    group_scores = []
    for g in range(N_GROUPS):
        v = biased[g * epg:(g + 1) * epg]
        m1, first = _first_argmax(v, idx, epg)
        m2 = jnp.max(jnp.where(idx == first, -jnp.inf, v), axis=0, keepdims=True)
        group_scores.append(m1 + m2)
    gmax = group_scores[0]
    for g in range(1, N_GROUPS):
        gmax = jnp.maximum(gmax, group_scores[g])
    gsel = jnp.full((1, tm), N_GROUPS, jnp.int32)
    for g in range(N_GROUPS - 1, -1, -1):
        gsel = jnp.where(group_scores[g] == gmax, g, gsel)

    in_biased = jnp.zeros((epg, tm), F32)
    in_scores = jnp.zeros((epg, tm), F32)
    for g in range(N_GROUPS):
        pick = gsel == g
        in_biased = jnp.where(pick, biased[g * epg:(g + 1) * epg], in_biased)
        in_scores = jnp.where(pick, scores[g * epg:(g + 1) * epg], in_scores)
    _, loc1 = _first_argmax(in_biased, idx, epg)
    _, loc2 = _first_argmax(jnp.where(idx == loc1, -jnp.inf, in_biased), idx, epg)
    s1 = jnp.sum(jnp.where(idx == loc1, in_scores, 0.0), axis=0, keepdims=True)
    s2 = jnp.sum(jnp.where(idx == loc2, in_scores, 0.0), axis=0, keepdims=True)
    denom = s1 + s2
    e_ref[...] = jnp.concatenate([gsel * epg + loc1, gsel * epg + loc2], axis=0)
    g_ref[...] = jnp.concatenate([s1 / denom, s2 / denom], axis=0)


def moe_router(x, router_w, router_b, tm=512):
    s, d = x.shape
    return pl.pallas_call(
        _router_kernel,
        grid=(s // tm,),
        in_specs=[
            pl.BlockSpec((tm, d), lambda i: (i, 0)),
            pl.BlockSpec((N_EXPERTS, d), lambda i: (0, 0)),
            pl.BlockSpec((N_EXPERTS, 1), lambda i: (0, 0)),
        ],
        out_specs=[pl.BlockSpec((TOP_K, tm), lambda i: (0, i)), pl.BlockSpec((TOP_K, tm), lambda i: (0, i))],
        out_shape=[jax.ShapeDtypeStruct((TOP_K, s), jnp.int32), jax.ShapeDtypeStruct((TOP_K, s), F32)],
        compiler_params=_params("parallel"),
        name="moe_router",
    )(x, router_w.T, router_b.reshape(N_EXPERTS, 1))


def _gather_rows_kernel(n_active_ref, idx_ref, src_ref, o_ref, sem, *, rows):
    i = pl.program_id(0)

    @pl.when(i < n_active_ref[0])
    def _():
        def copy(r):
            return pltpu.make_async_copy(src_ref.at[pl.ds(idx_ref[0, 0, r], 1)], o_ref.at[pl.ds(r, 1)], sem)

        def start(r, carry):
            copy(r).start()
            return carry

        def wait(r, carry):
            copy(r).wait()
            return carry

        lax.fori_loop(0, rows, start, 0)
        lax.fori_loop(0, rows, wait, 0)

    @pl.when(i >= n_active_ref[0])
    def _():
        o_ref[...] = jnp.zeros_like(o_ref)


def gather_rows(src, idx, n_active_tiles, rows=128):
    n = idx.shape[0]
    d = src.shape[1]
    nt = n // rows

    def tile_blk(i, na):
        return jnp.minimum(i, na[0] - 1)

    return pl.pallas_call(
        functools.partial(_gather_rows_kernel, rows=rows),
        grid_spec=pltpu.PrefetchScalarGridSpec(
            num_scalar_prefetch=1,
            grid=(nt,),
            in_specs=[
                pl.BlockSpec((1, 1, rows), lambda i, na: (tile_blk(i, na), 0, 0), memory_space=pltpu.SMEM),
                pl.BlockSpec(memory_space=pl.ANY),
            ],
            out_specs=pl.BlockSpec((rows, d), lambda i, na: (i, 0)),
            scratch_shapes=[pltpu.SemaphoreType.DMA(())],
        ),
        out_shape=jax.ShapeDtypeStruct((n, d), src.dtype),
        compiler_params=_params("arbitrary"),
        name="gather_rows",
    )(n_active_tiles, idx.reshape(nt, 1, rows), src)


def _expert_kernel(te_ref, na_ref, x_ref, wg_ref, wu_ref, wd_ref, o_ref, wg_sc, wu_sc, wd_sc):
    j = pl.program_id(0)
    active = j < na_ref[0]
    new_expert = (j == 0) | (te_ref[j] != te_ref[jnp.maximum(j - 1, 0)])

    @pl.when(active & new_expert)
    def _():
        wg_sc[...] = wg_ref[0].astype(BF16)
        wu_sc[...] = wu_ref[0].astype(BF16)
        wd_sc[...] = wd_ref[0].astype(BF16)

    @pl.when(active)
    def _():
        x = x_ref[...].astype(BF16)
        hg = jnp.dot(x, wg_sc[...], preferred_element_type=F32)
        hu = jnp.dot(x, wu_sc[...], preferred_element_type=F32)
        h = hg * jax.nn.sigmoid(hg) * hu
        o_ref[...] = jnp.dot(h.astype(BF16), wd_sc[...], preferred_element_type=F32)

    @pl.when(jnp.logical_not(active))
    def _():
        o_ref[...] = jnp.zeros_like(o_ref)


def routed_experts(xs, tile_expert, n_active_tiles, w_gate, w_up, w_down, tm):
    p, d = xs.shape
    f = w_gate.shape[2]
    nt = p // tm

    def row_blk(j, te, na):
        return (jnp.minimum(j, na[0] - 1), 0)

    return pl.pallas_call(
        _expert_kernel,
        grid_spec=pltpu.PrefetchScalarGridSpec(
            num_scalar_prefetch=2,
            grid=(nt,),
            in_specs=[
                pl.BlockSpec((tm, d), row_blk),
                pl.BlockSpec((1, d, f), lambda j, te, na: (te[j], 0, 0)),
                pl.BlockSpec((1, d, f), lambda j, te, na: (te[j], 0, 0)),
                pl.BlockSpec((1, f, d), lambda j, te, na: (te[j], 0, 0)),
            ],
            out_specs=pl.BlockSpec((tm, d), lambda j, te, na: (j, 0)),
            scratch_shapes=[pltpu.VMEM((d, f), BF16), pltpu.VMEM((d, f), BF16), pltpu.VMEM((f, d), BF16)],
        ),
        out_shape=jax.ShapeDtypeStruct((p, d), F32),
        compiler_params=_params("arbitrary"),
        name="routed_experts",
    )(tile_expert, n_active_tiles, xs, w_gate, w_up, w_down)


def _shared_kernel(x_ref, wg_ref, wu_ref, wd_ref, o_ref):
    x = x_ref[...]
    hg = jnp.dot(x, wg_ref[...], preferred_element_type=F32)
    hu = jnp.dot(x, wu_ref[...], preferred_element_type=F32)
    h = hg * jax.nn.sigmoid(hg) * hu
    o_ref[...] = jnp.dot(h.astype(BF16), wd_ref[...], preferred_element_type=F32)


def shared_expert(x_bf, wg, wu, wd, tm=512):
    s, d = x_bf.shape
    f = wg.shape[1]
    whole = lambda r, c: pl.BlockSpec((r, c), lambda i: (0, 0))
    return pl.pallas_call(
        _shared_kernel,
        grid=(s // tm,),
        in_specs=[pl.BlockSpec((tm, d), lambda i: (i, 0)), whole(d, f), whole(d, f), whole(f, d)],
        out_specs=pl.BlockSpec((tm, d), lambda i: (i, 0)),
        out_shape=jax.ShapeDtypeStruct((s, d), F32),
        compiler_params=_params("parallel"),
        name="shared_expert",
    )(x_bf, wg.astype(BF16), wu.astype(BF16), wd.astype(BF16))


def _combine_kernel(pos_ref, x_ref, sh_ref, gate_ref, ys_ref, g_ref, b_ref, o_ref, obf_ref, buf, sem, *, tm):
    def copy(r, k):
        return pltpu.make_async_copy(ys_ref.at[pl.ds(pos_ref[0, k, r], 1)], buf.at[k, pl.ds(r, 1)], sem)

    def start(r, carry):
        copy(r, 0).start()
        copy(r, 1).start()
        return carry

    def wait(r, carry):
        copy(r, 0).wait()
        copy(r, 1).wait()
        return carry

    lax.fori_loop(0, tm, start, 0)
    lax.fori_loop(0, tm, wait, 0)
    gate = gate_ref[...]
    routed = gate[:, 0:1] * buf[0] + gate[:, 1:2] * buf[1]
    z = DEEPNORM_ALPHA * x_ref[...] + (routed + sh_ref[...])
    out = _layer_norm(z, g_ref[...], b_ref[...])
    o_ref[...] = out
    obf_ref[...] = out.astype(BF16)


def moe_combine(x, shared, gates, pos, ys, g, b, tm=128):
    s, d = x.shape
    nt = s // tm
    row = pl.BlockSpec((tm, d), lambda i: (i, 0))
    vec = pl.BlockSpec((1, d), lambda i: (0, 0))
    return pl.pallas_call(
        functools.partial(_combine_kernel, tm=tm),
        grid=(nt,),
        in_specs=[
            pl.BlockSpec((1, TOP_K, tm), lambda i: (i, 0, 0), memory_space=pltpu.SMEM),
            row, row,
            pl.BlockSpec((tm, TOP_K), lambda i: (i, 0)),
            pl.BlockSpec(memory_space=pl.ANY),
            vec, vec,
        ],
        out_specs=[row, row],
        out_shape=[jax.ShapeDtypeStruct((s, d), F32), jax.ShapeDtypeStruct((s, d), BF16)],
        scratch_shapes=[pltpu.VMEM((TOP_K, tm, d), F32), pltpu.SemaphoreType.DMA(())],
        compiler_params=_params("arbitrary"),
        name="moe_combine",
    )(pos.reshape(TOP_K, nt, tm).transpose(1, 0, 2), x, shared, gates.T, ys, g.reshape(1, d), b.reshape(1, d))


def _rope_tables(pos, rot_dim, theta):
    half = rot_dim // 2
    inv = (1.0 / (theta ** (np.arange(half, dtype=np.float32) / half))).astype(np.float32)
    ang = pos.astype(F32)[:, None] * inv
    return jnp.cos(ang), jnp.sin(ang)


def _rope(x, cos, sin, rot_dim):
    half = rot_dim // 2
    x1 = x[..., :half]
    x2 = x[..., half:rot_dim]
    c = cos[:, None, :]
    s = sin[:, None, :]
    return jnp.concatenate([x1 * c - x2 * s, x1 * s + x2 * c, x[..., rot_dim:]], axis=-1)


def _routing_plan(experts, tm):
    k, s = experts.shape
    flat = experts.reshape(-1)
    onehot = (flat[:, None] == jnp.arange(N_EXPERTS, dtype=jnp.int32)[None, :]).astype(jnp.int32)
    csum = jnp.cumsum(onehot, axis=0)
    rank = jnp.take_along_axis(csum, flat[:, None], axis=1)[:, 0] - 1
    sizes = csum[-1]
    tiles = (sizes + tm - 1) // tm
    tile_end = jnp.cumsum(tiles)
    start = (tile_end - tiles) * tm
    pos = start[flat] + rank
    n_rows = k * s + N_EXPERTS * tm
    n_tiles = n_rows // tm
    row_token = jnp.zeros((n_rows,), jnp.int32).at[pos].set(jnp.tile(jnp.arange(s, dtype=jnp.int32), k))
    n_active = tile_end[-1]
    tile_ids = jnp.minimum(jnp.arange(n_tiles, dtype=jnp.int32), n_active - 1)
    tile_expert = jnp.searchsorted(tile_end, tile_ids, side="right").astype(jnp.int32)
    return pos.reshape(k, s), row_token, tile_expert, n_active.reshape(1).astype(jnp.int32)


def _grouped_moe(x, x_bf, router_w, router_b, w_gate, w_up, w_down, sh_gate, sh_up, sh_down, ln_g, ln_b,
                 expert_tm=256, gather_rows_per_step=128):
    experts, gates = moe_router(x, router_w, router_b)
    pos, row_token, tile_expert, n_active = _routing_plan(experts, expert_tm)
    xs = gather_rows(x, row_token, n_active * (expert_tm // gather_rows_per_step), gather_rows_per_step)
    ys = routed_experts(xs, tile_expert, n_active, w_gate, w_up, w_down, expert_tm)
    shared = shared_expert(x_bf, sh_gate, sh_up, sh_down)
    return moe_combine(x, shared, gates, pos, ys, ln_g, ln_b)


def _mla_dsa_mixer(x, pos, w_in, q_norm, w_uq, kv_norm, w_ukv, w_out):
    s = x.shape[0]
    n_pad = (-IN_AB) % 512
    h0 = matmul(x, jnp.pad(w_in, ((0, 0), (0, n_pad))), F32, 1024, 512, name="l0_in_proj")
    cq, ckv, krope, dq, dkv, iq, ik, iw = jnp.split(h0[:, :IN_AB], [int(c) for c in np.cumsum(SPLITS_AB)[:-1]], axis=-1)
    cos_m, sin_m = _rope_tables(pos, MLA_ROPE, ROPE_THETA)
    cos_d, sin_d = _rope_tables(pos, DSA_ROT, ROPE_THETA)
    cos_i, sin_i = _rope_tables(pos, IDX_ROT, ROPE_THETA)

    dqk = MLA_NOPE + MLA_ROPE
    qa = matmul(h0, w_uq, F32, 1024, 512, name="mla_q_up", a_cols=(0, MLA_Q_LORA),
                norm_gain=q_norm).reshape(s, MLA_HEADS, dqk)
    kva = matmul(h0, w_ukv, F32, 1024, 512, name="mla_kv_up", a_cols=(MLA_Q_LORA // MLA_KV_LORA, MLA_KV_LORA),
                 norm_gain=kv_norm).reshape(s, MLA_HEADS, MLA_NOPE + MLA_V)
    q_pe = _rope(qa[..., MLA_NOPE:], cos_m, sin_m, MLA_ROPE)
    q_cat = jnp.concatenate([qa[..., :MLA_NOPE], q_pe, jnp.zeros((s, MLA_HEADS, 2 * LANES - dqk), F32)], axis=-1)
    q_cat = (q_cat * dqk ** -0.5).astype(BF16).reshape(s, MLA_HEADS * 2 * LANES)
    k_pe = _rope(krope[:, None, :], cos_m, sin_m, MLA_ROPE)[:, 0]
    k_pe = jnp.concatenate([k_pe, jnp.zeros((s, LANES - MLA_ROPE), F32)], axis=-1).astype(BF16)
    k_nope = kva[..., :MLA_NOPE].astype(BF16).reshape(s, MLA_HEADS * MLA_NOPE)
    v_a = kva[..., MLA_NOPE:].astype(BF16).reshape(s, MLA_HEADS * MLA_V)
    out_a = mla_attention(q_cat, k_nope, k_pe, v_a)

    qb = _rope(dq.reshape(s, DSA_HEADS, DSA_HEAD_DIM), cos_d, sin_d, DSA_ROT) * DSA_HEAD_DIM ** -0.5
    qb = qb.astype(BF16).transpose(1, 0, 2)
    kvb = dkv.reshape(s, 2, DSA_KV_HEADS, DSA_HEAD_DIM)
    kb = _rope(kvb[:, 0], cos_d, sin_d, DSA_ROT).astype(BF16).reshape(s, DSA_KV_HEADS * DSA_HEAD_DIM)
    vb = kvb[:, 1].astype(BF16).reshape(s, DSA_KV_HEADS * DSA_HEAD_DIM)
    iq = _rope(iq.reshape(s, IDX_HEADS, IDX_DIM), cos_i, sin_i, IDX_ROT).astype(BF16).transpose(1, 0, 2)
    ik = _rope(ik[:, None, :], cos_i, sin_i, IDX_ROT)[:, 0].astype(BF16)
    iw = iw * (IDX_HEADS ** -0.5 * IDX_DIM ** -0.5)
    out_b = dsa_attention(iq, iw, ik, qb, kb, vb)

    return matmul(jnp.concatenate([out_a, out_b], axis=-1), w_out, F32, 1024, 512, name="l0_out_proj")


def _retention_mixer(x_bf, pos, w_in, gn_g, gn_b, w_out):
    h1 = matmul(x_bf, w_in, F32, 1024, 1024, name="l1_in_proj")
    cos, sin = _rope_tables(pos, RET_QK_DIM, RET_THETA)
    y = retention(h1, cos, sin, gn_g, gn_b)
    return matmul(y, w_out, F32, 1024, 512, name="l1_out_proj")


def kernel(x, positions, router_w, router_b, l0_w_in, l0_mla_q_norm, l0_mla_w_uq, l0_mla_kv_norm, l0_mla_w_ukv, l0_w_out, l1_w_in, l1_ret_gn_g, l1_ret_gn_b, l1_w_out, l0_ln_mix_g, l0_ln_mix_b, l0_moe_w_gate, l0_moe_w_up, l0_moe_w_down, l0_sh_gate, l0_sh_up, l0_sh_down, l0_ln_ffn_g, l0_ln_ffn_b, l1_ln_mix_g, l1_ln_mix_b, l1_moe_w_gate, l1_moe_w_up, l1_moe_w_down, l1_sh_gate, l1_sh_up, l1_sh_down, l1_ln_ffn_g, l1_ln_ffn_b):
    assert x.shape[0] == 1
    xt = x[0]
    pos = positions[0]

    mix = _mla_dsa_mixer(xt, pos, l0_w_in, l0_mla_q_norm, l0_mla_w_uq, l0_mla_kv_norm, l0_mla_w_ukv, l0_w_out)
    xt, xt_bf = add_layer_norm(xt, mix, l0_ln_mix_g, l0_ln_mix_b)
    xt, xt_bf = _grouped_moe(xt, xt_bf, router_w, router_b, l0_moe_w_gate, l0_moe_w_up, l0_moe_w_down,
                             l0_sh_gate, l0_sh_up, l0_sh_down, l0_ln_ffn_g, l0_ln_ffn_b)

    mix = _retention_mixer(xt_bf, pos, l1_w_in, l1_ret_gn_g, l1_ret_gn_b, l1_w_out)
    xt, xt_bf = add_layer_norm(xt, mix, l1_ln_mix_g, l1_ln_mix_b)
    xt, _ = _grouped_moe(xt, xt_bf, router_w, router_b, l1_moe_w_gate, l1_moe_w_up, l1_moe_w_down,
                         l1_sh_gate, l1_sh_up, l1_sh_down, l1_ln_ffn_g, l1_ln_ffn_b)
    return xt[None]
```

```python
import functools

import numpy as np
import jax
import jax.numpy as jnp
from jax import lax
from jax.experimental import pallas as pl
from jax.experimental.pallas import tpu as pltpu

F32 = jnp.float32
BF16 = jnp.bfloat16

D_MODEL = 2048
DEPTH = 2
ROPE_THETA = 500000.0
MLA_HEADS = 8
MLA_Q_LORA = 512
MLA_KV_LORA = 256
MLA_NOPE = 128
MLA_ROPE = 64
MLA_V = 128
DSA_HEADS = 8
DSA_KV_HEADS = 2
DSA_HEAD_DIM = 128
DSA_ROT = DSA_HEAD_DIM // 4
IDX_HEADS = 16
IDX_DIM = 64
IDX_ROT = IDX_DIM // 4
IDX_TOPK_MAX = 256
RET_HEADS = 8
RET_QK_DIM = 256
RET_V_DIM = 512
RET_CHUNK = 128
RET_THETA = 10000.0
N_EXPERTS = 64
N_GROUPS = 8
EXPERTS_PER_GROUP = N_EXPERTS // N_GROUPS
TOP_K = 2
D_EXPERT = 512
D_SHARED = 1024
DEEPNORM_ALPHA = (2.0 * DEPTH) ** 0.25

SPLITS_AB = (MLA_Q_LORA, MLA_KV_LORA, MLA_ROPE, DSA_HEADS * DSA_HEAD_DIM,
             2 * DSA_KV_HEADS * DSA_HEAD_DIM, IDX_HEADS * IDX_DIM, IDX_DIM, IDX_HEADS)
IN_AB = sum(SPLITS_AB)

VMEM_LIMIT_BYTES = 56 * 1024 * 1024
LANES = 128
MASKED_SCORE = -1e30
LOG2_E = 1.4426950408889634
INT32_MIN = -2 ** 31


def _params(*sem):
    return pltpu.CompilerParams(dimension_semantics=sem, vmem_limit_bytes=VMEM_LIMIT_BYTES)


def _mm_kernel(*refs, norm_eps):
    if norm_eps is None:
        a_ref, b_ref, o_ref = refs
        a = a_ref[...]
    else:
        a_ref, g_ref, b_ref, o_ref = refs
        af = a_ref[...].astype(F32)
        a = af * lax.rsqrt(jnp.mean(af * af, axis=-1, keepdims=True) + norm_eps) * g_ref[...]
    o_ref[...] = jnp.dot(a.astype(BF16), b_ref[...].astype(BF16),
                         preferred_element_type=F32).astype(o_ref.dtype)


def matmul(a, b, out_dtype, tm, tn, *, name, a_cols=None, norm_gain=None, norm_eps=1e-6):
    m = a.shape[0]
    k, n = b.shape
    col_blk = 0 if a_cols is None else a_cols[0]
    assert (a.shape[1] == k) if a_cols is None else (a_cols[1] == k)
    assert m % tm == 0 and n % tn == 0
    in_specs = [pl.BlockSpec((tm, k), lambda i, j: (i, col_blk))]
    args = [a]
    if norm_gain is not None:
        in_specs.append(pl.BlockSpec((1, k), lambda i, j: (0, 0)))
        args.append(norm_gain.reshape(1, k))
    in_specs.append(pl.BlockSpec((k, tn), lambda i, j: (0, j)))
    args.append(b)
    return pl.pallas_call(
        functools.partial(_mm_kernel, norm_eps=None if norm_gain is None else norm_eps),
        grid=(m // tm, n // tn),
        in_specs=in_specs,
        out_specs=pl.BlockSpec((tm, tn), lambda i, j: (i, j)),
        out_shape=jax.ShapeDtypeStruct((m, n), out_dtype),
        compiler_params=_params("parallel", "parallel"),
        name=name,
    )(*args)


def _layer_norm(z, g, b):
    mu = jnp.mean(z, axis=-1, keepdims=True)
    zc = z - mu
    var = jnp.mean(zc * zc, axis=-1, keepdims=True)
    return zc * lax.rsqrt(var + 1e-5) * g + b


def _add_ln_kernel(x_ref, y_ref, g_ref, b_ref, o_ref, obf_ref):
    z = DEEPNORM_ALPHA * x_ref[...] + y_ref[...]
    out = _layer_norm(z, g_ref[...], b_ref[...])
    o_ref[...] = out
    obf_ref[...] = out.astype(BF16)


def add_layer_norm(x, y, g, b, tm=256):
    s, d = x.shape
    row = pl.BlockSpec((tm, d), lambda i: (i, 0))
    vec = pl.BlockSpec((1, d), lambda i: (0, 0))
    return pl.pallas_call(
        _add_ln_kernel,
        grid=(s // tm,),
        in_specs=[row, row, vec, vec],
        out_specs=[row, row],
        out_shape=[jax.ShapeDtypeStruct((s, d), F32), jax.ShapeDtypeStruct((s, d), BF16)],
        compiler_params=_params("parallel"),
        name="add_layer_norm",
    )(x, y, g.reshape(1, d), b.reshape(1, d))


def _lane_tile(x, n):
    return jnp.tile(x, (1, n))


def _softmax_step(s, v_ext, m_ref, acc_ref):
    tk = s.shape[1]
    m_prev = m_ref[...]
    m_new = jnp.maximum(m_prev, jnp.max(s, axis=1)[:, None])
    p = jnp.exp2(s - _lane_tile(m_new, tk // LANES))
    alpha = jnp.exp2(m_prev - m_new)
    pv = jnp.dot(p.astype(BF16), v_ext, preferred_element_type=F32)
    acc_ref[...] = _lane_tile(alpha, 2) * acc_ref[...] + pv
    m_ref[...] = m_new


def _mla_kernel(q_ref, kn_ref, kp_ref, v_ref, o_ref, m_sc, acc_sc, *, t):
    qi = pl.program_id(1)
    m_sc[...] = jnp.full_like(m_sc, MASKED_SCORE)
    acc_sc[...] = jnp.zeros_like(acc_sc)
    q = q_ref[...]
    ones = jnp.ones((t, LANES), BF16)

    def chunk(c, masked):
        k = jnp.concatenate([kn_ref[c], kp_ref[c]], axis=1)
        s = lax.dot_general(q, k, (((1,), (1,)), ((), ())), preferred_element_type=F32)
        if masked:
            row = lax.broadcasted_iota(jnp.int32, (t, t), 0)
            col = lax.broadcasted_iota(jnp.int32, (t, t), 1)
            s = jnp.where(col <= row, s, MASKED_SCORE)
        _softmax_step(s, jnp.concatenate([v_ref[c], ones], axis=1), m_sc, acc_sc)

    def below_diagonal(c, carry):
        chunk(c, False)
        return carry

    lax.fori_loop(0, qi, below_diagonal, 0)
    chunk(qi, True)
    acc = acc_sc[...]
    o_ref[...] = (acc[:, :LANES] / acc[:, LANES:]).astype(o_ref.dtype)


def mla_attention(q, k_nope, k_pe, v, t=512):
    s = q.shape[0]
    h = MLA_HEADS
    nc = s // t
    per_head = pl.BlockSpec((nc, t, LANES), lambda hh, qi: (0, 0, hh))
    return pl.pallas_call(
        functools.partial(_mla_kernel, t=t),
        grid=(h, nc),
        in_specs=[
            pl.BlockSpec((t, 2 * LANES), lambda hh, qi: (qi, hh)),
            per_head,
            pl.BlockSpec((nc, t, LANES), lambda hh, qi: (0, 0, 0)),
            per_head,
        ],
        out_specs=pl.BlockSpec((t, LANES), lambda hh, qi: (qi, hh)),
        out_shape=jax.ShapeDtypeStruct((s, h * MLA_V), BF16),
        scratch_shapes=[pltpu.VMEM((t, LANES), F32), pltpu.VMEM((t, 2 * LANES), F32)],
        compiler_params=_params("parallel", "arbitrary"),
        name="mla_attention",
    )(q, k_nope.reshape(nc, t, h * MLA_NOPE), k_pe.reshape(nc, t, LANES), v.reshape(nc, t, h * MLA_V))


def _sortable_key(x):
    bits = pltpu.bitcast(x, jnp.int32)
    return bits ^ ((bits >> 31) & 0x7FFFFFFF)


def _dsa_kernel(iq_ref, iw_ref, ik_ref, q_ref, k_ref, v_ref, o_ref, key_sc, wb_sc, m_sc, acc_sc,
                *, tq, tk, n_sel):
    i = pl.program_id(0)
    n_chunks = ((i + 1) * tq + tk - 1) // tk
    rep = DSA_HEADS // DSA_KV_HEADS
    lane_tiles = tk // LANES

    iq_all = iq_ref[...].reshape(IDX_HEADS * tq, IDX_DIM)
    iw = iw_ref[...]
    for h in range(IDX_HEADS):
        wb_sc[h] = jnp.broadcast_to(iw[:, h:h + 1], (tq, LANES))

    def index_chunk(c, carry):
        d = lax.dot_general(iq_all, ik_ref[c], (((1,), (1,)), ((), ())), preferred_element_type=F32)
        d = jnp.maximum(d, 0.0).reshape(IDX_HEADS, tq, tk)
        score = _lane_tile(wb_sc[0], lane_tiles) * d[0]
        for h in range(1, IDX_HEADS):
            score = score + _lane_tile(wb_sc[h], lane_tiles) * d[h]
        key_sc[c] = _sortable_key(score)
        return carry

    lax.fori_loop(0, n_chunks, index_chunk, 0)
    last = n_chunks - 1
    row = i * tq + lax.broadcasted_iota(jnp.int32, (tq, tk), 0)
    col = last * tk + lax.broadcasted_iota(jnp.int32, (tq, tk), 1)
    key_sc[last] = jnp.where(col <= row, key_sc[last], INT32_MIN)

    def search_bit(b, thr):
        cand = thr + lax.shift_left(jnp.int32(1), 31 - b)

        def count_chunk(c, cnt):
            key = key_sc[c]
            for j in range(lane_tiles):
                cnt = cnt + jnp.where(key[:, j * LANES:(j + 1) * LANES] >= cand, 1.0, 0.0)
            return cnt

        cnt = lax.fori_loop(0, n_chunks, count_chunk, jnp.zeros((tq, LANES), F32))
        return jnp.where(jnp.sum(cnt, axis=1)[:, None] >= n_sel, cand, thr)

    thr = lax.fori_loop(0, 32, search_bit, jnp.full((tq, LANES), INT32_MIN, jnp.int32))
    thr = _lane_tile(jnp.maximum(thr, INT32_MIN + 1), lane_tiles)

    m_sc[...] = jnp.full_like(m_sc, MASKED_SCORE)
    acc_sc[...] = jnp.zeros_like(acc_sc)
    ones = jnp.ones((tk, LANES), BF16)

    def attend_chunk(c, carry):
        bias = jnp.where(key_sc[c] >= thr, 0.0, MASKED_SCORE)
        kc = k_ref[c]
        vc = v_ref[c]
        for g in range(DSA_KV_HEADS):
            cols = slice(g * DSA_HEAD_DIM, (g + 1) * DSA_HEAD_DIM)
            qg = q_ref[g * rep:(g + 1) * rep].reshape(rep * tq, DSA_HEAD_DIM)
            s = lax.dot_general(qg, kc[:, cols], (((1,), (1,)), ((), ())), preferred_element_type=F32)
            s = (s.reshape(rep, tq, tk) + bias[None]).reshape(rep * tq, tk)
            _softmax_step(s, jnp.concatenate([vc[:, cols], ones], axis=1), m_sc.at[g], acc_sc.at[g])
        return carry

    lax.fori_loop(0, n_chunks, attend_chunk, 0)
    for g in range(DSA_KV_HEADS):
        acc = acc_sc[g]
        out = acc[:, :LANES] / acc[:, LANES:]
        for r in range(rep):
            hh = g * rep + r
            o_ref[:, hh * DSA_HEAD_DIM:(hh + 1) * DSA_HEAD_DIM] = out[r * tq:(r + 1) * tq].astype(o_ref.dtype)


def dsa_attention(iq, iw, ik, q, k, v, tq=128, tk=512):
    s = ik.shape[0]
    n_sel = min(IDX_TOPK_MAX, s // 4)
    nc = s // tk
    assert tk >= n_sel and s % tk == 0 and tk % tq == 0 and DSA_HEAD_DIM == LANES
    rep = DSA_HEADS // DSA_KV_HEADS
    kvw = DSA_KV_HEADS * DSA_HEAD_DIM
    whole3 = lambda i: (0, 0, 0)
    return pl.pallas_call(
        functools.partial(_dsa_kernel, tq=tq, tk=tk, n_sel=n_sel),
        grid=(s // tq,),
        in_specs=[
            pl.BlockSpec((IDX_HEADS, tq, IDX_DIM), lambda i: (0, i, 0)),
            pl.BlockSpec((tq, IDX_HEADS), lambda i: (i, 0)),
            pl.BlockSpec((nc, tk, IDX_DIM), whole3),
            pl.BlockSpec((DSA_HEADS, tq, DSA_HEAD_DIM), lambda i: (0, i, 0)),
            pl.BlockSpec((nc, tk, kvw), whole3),
            pl.BlockSpec((nc, tk, kvw), whole3),
        ],
        out_specs=pl.BlockSpec((tq, DSA_HEADS * DSA_HEAD_DIM), lambda i: (i, 0)),
        out_shape=jax.ShapeDtypeStruct((s, DSA_HEADS * DSA_HEAD_DIM), BF16),
        scratch_shapes=[
            pltpu.VMEM((nc, tq, tk), jnp.int32),
            pltpu.VMEM((IDX_HEADS, tq, LANES), F32),
            pltpu.VMEM((DSA_KV_HEADS, rep * tq, LANES), F32),
            pltpu.VMEM((DSA_KV_HEADS, rep * tq, 2 * LANES), F32),
        ],
        compiler_params=_params("parallel"),
        name="dsa_attention",
    )(iq, iw, ik.reshape(nc, tk, IDX_DIM), q, k.reshape(nc, tk, kvw), v.reshape(nc, tk, kvw))


def _retention_tables():
    h, c = RET_HEADS, RET_CHUNK
    log_g = np.log(1.0 - 2.0 ** (-5.0 - np.arange(h, dtype=np.float32))).astype(np.float32).astype(np.float64)
    idx = np.arange(c, dtype=np.float64)
    diff = idx[:, None] - idx[None, :]
    decay_in = np.where(diff[None] >= 0, np.exp(np.maximum(diff, 0.0)[None] * log_g[:, None, None]), 0.0)
    xi = np.exp((idx + 1.0)[None, :] * log_g[:, None])
    zeta = np.exp((c - 1.0 - idx)[None, :] * log_g[:, None])
    chunk_decay = np.exp(c * log_g)
    lane = np.ones((1, 1, LANES))
    return (decay_in.astype(np.float32), (xi[:, :, None] * lane).astype(np.float32),
            (zeta[:, :, None] * lane).astype(np.float32),
            (chunk_decay[:, None, None] * np.ones((1, 8, LANES))).astype(np.float32))


def _retention_kernel(q_ref, k_ref, v_ref, g_ref, cos_ref, sin_ref, din_ref, xi_ref, zeta_ref, cd_ref,
                      gng_ref, gnb_ref, o_ref, r_sc):
    n = pl.program_id(1)
    half = RET_QK_DIM // 2

    @pl.when(n == 0)
    def _():
        r_sc[...] = jnp.zeros_like(r_sc)

    cos = cos_ref[...]
    sin = sin_ref[...]

    def rope(x):
        x1 = x[:, :half]
        x2 = x[:, half:]
        return x1 * cos - x2 * sin, x1 * sin + x2 * cos

    q1, q2 = rope(q_ref[...])
    k1, k2 = rope(k_ref[...])
    scale = RET_QK_DIM ** -0.5
    qr = jnp.concatenate([q1, q2], axis=1).astype(BF16)
    kr = jnp.concatenate([k1 * scale, k2 * scale], axis=1)
    v = v_ref[...].astype(BF16)

    inner = lax.dot_general(qr, kr.astype(BF16), (((1,), (1,)), ((), ())), preferred_element_type=F32)
    inner = inner * din_ref[0]
    r = r_sc[...]
    cross = jnp.dot(qr, r.astype(BF16), preferred_element_type=F32)
    xi = xi_ref[0]
    o = jnp.dot(inner.astype(BF16), v, preferred_element_type=F32)
    o = o + cross * jnp.concatenate([xi] * (RET_V_DIM // LANES), axis=1)

    zeta = zeta_ref[0]
    kz = (kr * jnp.concatenate([zeta] * (RET_QK_DIM // LANES), axis=1)).astype(BF16)
    upd = lax.dot_general(kz, v, (((0,), (0,)), ((), ())), preferred_element_type=F32)
    r_sc[...] = cd_ref[0][:1, :1] * r + upd

    mu = jnp.mean(o, axis=-1, keepdims=True)
    oc = o - mu
    var = jnp.mean(oc * oc, axis=-1, keepdims=True)
    y = oc * lax.rsqrt(var + 1e-5) * gng_ref[...] + gnb_ref[...]
    gate = g_ref[...]
    o_ref[...] = (gate * jax.nn.sigmoid(gate) * y).astype(o_ref.dtype)


def retention(h1, cos, sin, gn_g, gn_b):
    s = h1.shape[0]
    hh, dk, dv, c = RET_HEADS, RET_QK_DIM, RET_V_DIM, RET_CHUNK
    din, xi, zeta, cd = (jnp.asarray(t) for t in _retention_tables())
    k_blk0 = hh * dk // dk
    v_blk0 = 2 * hh * dk // dv
    g_blk0 = v_blk0 + hh
    per_head = lambda r, w: pl.BlockSpec((1, r, w), lambda h, n: (h, 0, 0))
    return pl.pallas_call(
        _retention_kernel,
        grid=(hh, s // c),
        in_specs=[
            pl.BlockSpec((c, dk), lambda h, n: (n, h)),
            pl.BlockSpec((c, dk), lambda h, n: (n, k_blk0 + h)),
            pl.BlockSpec((c, dv), lambda h, n: (n, v_blk0 + h)),
            pl.BlockSpec((c, dv), lambda h, n: (n, g_blk0 + h)),
            pl.BlockSpec((c, dk // 2), lambda h, n: (n, 0)),
            pl.BlockSpec((c, dk // 2), lambda h, n: (n, 0)),
            per_head(c, c), per_head(c, LANES), per_head(c, LANES), per_head(8, LANES),
            pl.BlockSpec((1, dv), lambda h, n: (0, h)),
            pl.BlockSpec((1, dv), lambda h, n: (0, h)),
        ],
        out_specs=pl.BlockSpec((c, dv), lambda h, n: (n, h)),
        out_shape=jax.ShapeDtypeStruct((s, hh * dv), BF16),
        scratch_shapes=[pltpu.VMEM((dk, dv), F32)],
        compiler_params=_params("parallel", "arbitrary"),
        name="retention",
    )(h1, h1, h1, h1, cos, sin, din, xi, zeta, cd, gn_g.reshape(1, -1), gn_b.reshape(1, -1))


def _first_argmax(v, idx, n):
    m = jnp.max(v, axis=0, keepdims=True)
    first = jnp.min(jnp.where(v == m, idx, n), axis=0, keepdims=True)
    return m, first


def _router_kernel(x_ref, rwt_ref, rb_ref, e_ref, g_ref):
    tm = x_ref.shape[0]
    epg = EXPERTS_PER_GROUP
    logits = lax.dot_general(rwt_ref[...], x_ref[...], (((1,), (1,)), ((), ())),
                             precision=lax.Precision.HIGHEST, preferred_element_type=F32)
    scores = jax.nn.sigmoid(logits)
    biased = scores + rb_ref[...]
    idx = lax.broadcasted_iota(jnp.int32, (epg, tm), 0)

    group_scores = []
    for g in range(N_GROUPS):
        v = biased[g * epg:(g + 1) * epg]
        m1, first = _first_argmax(v, idx, epg)
        m2 = jnp.max(jnp.where(idx == first, -jnp.inf, v), axis=0, keepdims=True)
        group_scores.append(m1 + m2)
    gmax = group_scores[0]
    for g in range(1, N_GROUPS):
        gmax = jnp.maximum(gmax, group_scores[g])
    gsel = jnp.full((1, tm), N_GROUPS, jnp.int32)
    for g in range(N_GROUPS - 1, -1, -1):
        gsel = jnp.where(group_scores[g] == gmax, g, gsel)

    in_biased = jnp.zeros((epg, tm), F32)
    in_scores = jnp.zeros((epg, tm), F32)
    for g in range(N_GROUPS):
        pick = gsel == g
        in_biased = jnp.where(pick, biased[g * epg:(g + 1) * epg], in_biased)
        in_scores = jnp.where(pick, scores[g * epg:(g + 1) * epg], in_scores)
    _, loc1 = _first_argmax(in_biased, idx, epg)
    _, loc2 = _first_argmax(jnp.where(idx == loc1, -jnp.inf, in_biased), idx, epg)
    s1 = jnp.sum(jnp.where(idx == loc1, in_scores, 0.0), axis=0, keepdims=True)
    s2 = jnp.sum(jnp.where(idx == loc2, in_scores, 0.0), axis=0, keepdims=True)
    denom = s1 + s2
    e_ref[...] = jnp.concatenate([gsel * epg + loc1, gsel * epg + loc2], axis=0)
    g_ref[...] = jnp.concatenate([s1 / denom, s2 / denom], axis=0)


def moe_router(x, router_w, router_b, tm=512):
    s, d = x.shape
    return pl.pallas_call(
        _router_kernel,
        grid=(s // tm,),
        in_specs=[
            pl.BlockSpec((tm, d), lambda i: (i, 0)),
            pl.BlockSpec((N_EXPERTS, d), lambda i: (0, 0)),
            pl.BlockSpec((N_EXPERTS, 1), lambda i: (0, 0)),
        ],
        out_specs=[pl.BlockSpec((TOP_K, tm), lambda i: (0, i)), pl.BlockSpec((TOP_K, tm), lambda i: (0, i))],
        out_shape=[jax.ShapeDtypeStruct((TOP_K, s), jnp.int32), jax.ShapeDtypeStruct((TOP_K, s), F32)],
        compiler_params=_params("parallel"),
        name="moe_router",
    )(x, router_w.T, router_b.reshape(N_EXPERTS, 1))


def _gather_rows_kernel(n_active_ref, idx_ref, src_ref, o_ref, sem, *, rows):
    i = pl.program_id(0)

    @pl.when(i < n_active_ref[0])
    def _():
        def start(r, carry):
            pltpu.make_async_copy(src_ref.at[pl.ds(idx_ref[0, 0, r], 1)], o_ref.at[pl.ds(r, 1)], sem).start()
            return carry

        lax.fori_loop(0, rows, start, 0, unroll=8)
        pltpu.make_async_copy(src_ref.at[pl.ds(0, rows)], o_ref, sem).wait()

    @pl.when(i >= n_active_ref[0])
    def _():
        o_ref[...] = jnp.zeros_like(o_ref)


def gather_rows(src, idx, n_active_tiles, rows=128):
    n = idx.shape[0]
    d = src.shape[1]
    nt = n // rows

    def tile_blk(i, na):
        return jnp.minimum(i, na[0] - 1)

    return pl.pallas_call(
        functools.partial(_gather_rows_kernel, rows=rows),
        grid_spec=pltpu.PrefetchScalarGridSpec(
            num_scalar_prefetch=1,
            grid=(nt,),
            in_specs=[
                pl.BlockSpec((1, 1, rows), lambda i, na: (tile_blk(i, na), 0, 0), memory_space=pltpu.SMEM),
                pl.BlockSpec(memory_space=pl.ANY),
            ],
            out_specs=pl.BlockSpec((rows, d), lambda i, na: (i, 0)),
            scratch_shapes=[pltpu.SemaphoreType.DMA(())],
        ),
        out_shape=jax.ShapeDtypeStruct((n, d), src.dtype),
        compiler_params=_params("arbitrary"),
        name="gather_rows",
    )(n_active_tiles, idx.reshape(nt, 1, rows), src)


def _expert_kernel(te_ref, na_ref, x_ref, wg_ref, wu_ref, wd_ref, o_ref, wg_sc, wu_sc, wd_sc):
    j = pl.program_id(0)
    active = j < na_ref[0]
    new_expert = (j == 0) | (te_ref[j] != te_ref[jnp.maximum(j - 1, 0)])

    @pl.when(active & new_expert)
    def _():
        wg_sc[...] = wg_ref[0].astype(BF16)
        wu_sc[...] = wu_ref[0].astype(BF16)
        wd_sc[...] = wd_ref[0].astype(BF16)

    @pl.when(active)
    def _():
        x = x_ref[...].astype(BF16)
        hg = jnp.dot(x, wg_sc[...], preferred_element_type=F32)
        hu = jnp.dot(x, wu_sc[...], preferred_element_type=F32)
        h = hg * jax.nn.sigmoid(hg) * hu
        o_ref[...] = jnp.dot(h.astype(BF16), wd_sc[...], preferred_element_type=F32)

    @pl.when(jnp.logical_not(active))
    def _():
        o_ref[...] = jnp.zeros_like(o_ref)


def routed_experts(xs, tile_expert, n_active_tiles, w_gate, w_up, w_down, tm):
    p, d = xs.shape
    f = w_gate.shape[2]
    nt = p // tm

    def row_blk(j, te, na):
        return (jnp.minimum(j, na[0] - 1), 0)

    return pl.pallas_call(
        _expert_kernel,
        grid_spec=pltpu.PrefetchScalarGridSpec(
            num_scalar_prefetch=2,
            grid=(nt,),
            in_specs=[
                pl.BlockSpec((tm, d), row_blk),
                pl.BlockSpec((1, d, f), lambda j, te, na: (te[j], 0, 0)),
                pl.BlockSpec((1, d, f), lambda j, te, na: (te[j], 0, 0)),
                pl.BlockSpec((1, f, d), lambda j, te, na: (te[j], 0, 0)),
            ],
            out_specs=pl.BlockSpec((tm, d), lambda j, te, na: (j, 0)),
            scratch_shapes=[pltpu.VMEM((d, f), BF16), pltpu.VMEM((d, f), BF16), pltpu.VMEM((f, d), BF16)],
        ),
        out_shape=jax.ShapeDtypeStruct((p, d), F32),
        compiler_params=_params("arbitrary"),
        name="routed_experts",
    )(tile_expert, n_active_tiles, xs, w_gate, w_up, w_down)


def _shared_kernel(x_ref, wg_ref, wu_ref, wd_ref, o_ref):
    x = x_ref[...]
    hg = jnp.dot(x, wg_ref[...], preferred_element_type=F32)
    hu = jnp.dot(x, wu_ref[...], preferred_element_type=F32)
    h = hg * jax.nn.sigmoid(hg) * hu
    o_ref[...] = jnp.dot(h.astype(BF16), wd_ref[...], preferred_element_type=F32)


def shared_expert(x_bf, wg, wu, wd, tm=512):
    s, d = x_bf.shape
    f = wg.shape[1]
    whole = lambda r, c: pl.BlockSpec((r, c), lambda i: (0, 0))
    return pl.pallas_call(
        _shared_kernel,
        grid=(s // tm,),
        in_specs=[pl.BlockSpec((tm, d), lambda i: (i, 0)), whole(d, f), whole(d, f), whole(f, d)],
        out_specs=pl.BlockSpec((tm, d), lambda i: (i, 0)),
        out_shape=jax.ShapeDtypeStruct((s, d), F32),
        compiler_params=_params("parallel"),
        name="shared_expert",
    )(x_bf, wg.astype(BF16), wu.astype(BF16), wd.astype(BF16))


def _combine_kernel(pos_ref, x_ref, sh_ref, gate_ref, ys_ref, g_ref, b_ref, o_ref, obf_ref, buf, sem, *, tm):
    def start(r, carry):
        for k in range(TOP_K):
            pltpu.make_async_copy(ys_ref.at[pl.ds(pos_ref[0, k, r], 1)], buf.at[k, pl.ds(r, 1)], sem).start()
        return carry

    lax.fori_loop(0, tm, start, 0, unroll=8)
    for k in range(TOP_K):
        pltpu.make_async_copy(ys_ref.at[pl.ds(0, tm)], buf.at[k], sem).wait()
    gate = gate_ref[...]
    routed = gate[:, 0:1] * buf[0] + gate[:, 1:2] * buf[1]
    z = DEEPNORM_ALPHA * x_ref[...] + (routed + sh_ref[...])
    out = _layer_norm(z, g_ref[...], b_ref[...])
    o_ref[...] = out
    obf_ref[...] = out.astype(BF16)


def moe_combine(x, shared, gates, pos, ys, g, b, tm=128):
    s, d = x.shape
    nt = s // tm
    row = pl.BlockSpec((tm, d), lambda i: (i, 0))
    vec = pl.BlockSpec((1, d), lambda i: (0, 0))
    return pl.pallas_call(
        functools.partial(_combine_kernel, tm=tm),
        grid=(nt,),
        in_specs=[
            pl.BlockSpec((1, TOP_K, tm), lambda i: (i, 0, 0), memory_space=pltpu.SMEM),
            row, row,
            pl.BlockSpec((tm, TOP_K), lambda i: (i, 0)),
            pl.BlockSpec(memory_space=pl.ANY),
            vec, vec,
        ],
        out_specs=[row, row],
        out_shape=[jax.ShapeDtypeStruct((s, d), F32), jax.ShapeDtypeStruct((s, d), BF16)],
        scratch_shapes=[pltpu.VMEM((TOP_K, tm, d), F32), pltpu.SemaphoreType.DMA(())],
        compiler_params=_params("arbitrary"),
        name="moe_combine",
    )(pos.reshape(TOP_K, nt, tm).transpose(1, 0, 2), x, shared, gates.T, ys, g.reshape(1, d), b.reshape(1, d))


def _rope_tables(pos, rot_dim, theta):
    half = rot_dim // 2
    inv = (1.0 / (theta ** (np.arange(half, dtype=np.float32) / half))).astype(np.float32)
    ang = pos.astype(F32)[:, None] * inv
    return jnp.cos(ang), jnp.sin(ang)


def _rope(x, cos, sin, rot_dim):
    half = rot_dim // 2
    x1 = x[..., :half]
    x2 = x[..., half:rot_dim]
    c = cos[:, None, :]
    s = sin[:, None, :]
    return jnp.concatenate([x1 * c - x2 * s, x1 * s + x2 * c, x[..., rot_dim:]], axis=-1)


def _routing_plan(experts, tm):
    k, s = experts.shape
    flat = experts.reshape(-1)
    onehot = (flat[:, None] == jnp.arange(N_EXPERTS, dtype=jnp.int32)[None, :]).astype(jnp.int32)
    csum = jnp.cumsum(onehot, axis=0)
    rank = jnp.take_along_axis(csum, flat[:, None], axis=1)[:, 0] - 1
    sizes = csum[-1]
    tiles = (sizes + tm - 1) // tm
    tile_end = jnp.cumsum(tiles)
    start = (tile_end - tiles) * tm
    pos = start[flat] + rank
    n_rows = k * s + N_EXPERTS * tm
    n_tiles = n_rows // tm
    row_token = jnp.zeros((n_rows,), jnp.int32).at[pos].set(jnp.tile(jnp.arange(s, dtype=jnp.int32), k))
    n_active = tile_end[-1]
    tile_ids = jnp.minimum(jnp.arange(n_tiles, dtype=jnp.int32), n_active - 1)
    tile_expert = jnp.searchsorted(tile_end, tile_ids, side="right").astype(jnp.int32)
    return pos.reshape(k, s), row_token, tile_expert, n_active.reshape(1).astype(jnp.int32)


def _grouped_moe(x, x_bf, router_w, router_b, w_gate, w_up, w_down, sh_gate, sh_up, sh_down, ln_g, ln_b,
                 expert_tm=256, gather_rows_per_step=128):
    experts, gates = moe_router(x, router_w, router_b)
    pos, row_token, tile_expert, n_active = _routing_plan(experts, expert_tm)
    xs = gather_rows(x, row_token, n_active * (expert_tm // gather_rows_per_step), gather_rows_per_step)
    ys = routed_experts(xs, tile_expert, n_active, w_gate, w_up, w_down, expert_tm)
    shared = shared_expert(x_bf, sh_gate, sh_up, sh_down)
    return moe_combine(x, shared, gates, pos, ys, ln_g, ln_b)


def _mla_dsa_mixer(x, pos, w_in, q_norm, w_uq, kv_norm, w_ukv, w_out):
    s = x.shape[0]
    n_pad = (-IN_AB) % 512
    h0 = matmul(x, jnp.pad(w_in, ((0, 0), (0, n_pad))), F32, 1024, 512, name="l0_in_proj")
    cq, ckv, krope, dq, dkv, iq, ik, iw = jnp.split(h0[:, :IN_AB], [int(c) for c in np.cumsum(SPLITS_AB)[:-1]], axis=-1)
    cos_m, sin_m = _rope_tables(pos, MLA_ROPE, ROPE_THETA)
    cos_d, sin_d = _rope_tables(pos, DSA_ROT, ROPE_THETA)
    cos_i, sin_i = _rope_tables(pos, IDX_ROT, ROPE_THETA)

    dqk = MLA_NOPE + MLA_ROPE
    qa = matmul(h0, w_uq, F32, 1024, 512, name="mla_q_up", a_cols=(0, MLA_Q_LORA),
                norm_gain=q_norm).reshape(s, MLA_HEADS, dqk)
    kva = matmul(h0, w_ukv, F32, 1024, 512, name="mla_kv_up", a_cols=(MLA_Q_LORA // MLA_KV_LORA, MLA_KV_LORA),
                 norm_gain=kv_norm).reshape(s, MLA_HEADS, MLA_NOPE + MLA_V)
    q_pe = _rope(qa[..., MLA_NOPE:], cos_m, sin_m, MLA_ROPE)
    q_cat = jnp.concatenate([qa[..., :MLA_NOPE], q_pe, jnp.zeros((s, MLA_HEADS, 2 * LANES - dqk), F32)], axis=-1)
    q_cat = (q_cat * (dqk ** -0.5 * LOG2_E)).astype(BF16).reshape(s, MLA_HEADS * 2 * LANES)
    k_pe = _rope(krope[:, None, :], cos_m, sin_m, MLA_ROPE)[:, 0]
    k_pe = jnp.concatenate([k_pe, jnp.zeros((s, LANES - MLA_ROPE), F32)], axis=-1).astype(BF16)
    k_nope = kva[..., :MLA_NOPE].astype(BF16).reshape(s, MLA_HEADS * MLA_NOPE)
    v_a = kva[..., MLA_NOPE:].astype(BF16).reshape(s, MLA_HEADS * MLA_V)
    out_a = mla_attention(q_cat, k_nope, k_pe, v_a)

    qb = _rope(dq.reshape(s, DSA_HEADS, DSA_HEAD_DIM), cos_d, sin_d, DSA_ROT) * (DSA_HEAD_DIM ** -0.5 * LOG2_E)
    qb = qb.astype(BF16).transpose(1, 0, 2)
    kvb = dkv.reshape(s, 2, DSA_KV_HEADS, DSA_HEAD_DIM)
    kb = _rope(kvb[:, 0], cos_d, sin_d, DSA_ROT).astype(BF16).reshape(s, DSA_KV_HEADS * DSA_HEAD_DIM)
    vb = kvb[:, 1].astype(BF16).reshape(s, DSA_KV_HEADS * DSA_HEAD_DIM)
    iq = _rope(iq.reshape(s, IDX_HEADS, IDX_DIM), cos_i, sin_i, IDX_ROT).astype(BF16).transpose(1, 0, 2)
    ik = _rope(ik[:, None, :], cos_i, sin_i, IDX_ROT)[:, 0].astype(BF16)
    iw = iw * (IDX_HEADS ** -0.5 * IDX_DIM ** -0.5)
    out_b = dsa_attention(iq, iw, ik, qb, kb, vb)

    return matmul(jnp.concatenate([out_a, out_b], axis=-1), w_out, F32, 1024, 512, name="l0_out_proj")


def _retention_mixer(x_bf, pos, w_in, gn_g, gn_b, w_out):
    h1 = matmul(x_bf, w_in, F32, 1024, 1024, name="l1_in_proj")
    cos, sin = _rope_tables(pos, RET_QK_DIM, RET_THETA)
    y = retention(h1, cos, sin, gn_g, gn_b)
    return matmul(y, w_out, F32, 1024, 512, name="l1_out_proj")


def kernel(x, positions, router_w, router_b, l0_w_in, l0_mla_q_norm, l0_mla_w_uq, l0_mla_kv_norm, l0_mla_w_ukv, l0_w_out, l1_w_in, l1_ret_gn_g, l1_ret_gn_b, l1_w_out, l0_ln_mix_g, l0_ln_mix_b, l0_moe_w_gate, l0_moe_w_up, l0_moe_w_down, l0_sh_gate, l0_sh_up, l0_sh_down, l0_ln_ffn_g, l0_ln_ffn_b, l1_ln_mix_g, l1_ln_mix_b, l1_moe_w_gate, l1_moe_w_up, l1_moe_w_down, l1_sh_gate, l1_sh_up, l1_sh_down, l1_ln_ffn_g, l1_ln_ffn_b):
    assert x.shape[0] == 1
    xt = x[0]
    pos = positions[0]

    mix = _mla_dsa_mixer(xt, pos, l0_w_in, l0_mla_q_norm, l0_mla_w_uq, l0_mla_kv_norm, l0_mla_w_ukv, l0_w_out)
    xt, xt_bf = add_layer_norm(xt, mix, l0_ln_mix_g, l0_ln_mix_b)
    xt, xt_bf = _grouped_moe(xt, xt_bf, router_w, router_b, l0_moe_w_gate, l0_moe_w_up, l0_moe_w_down,
                             l0_sh_gate, l0_sh_up, l0_sh_down, l0_ln_ffn_g, l0_ln_ffn_b)

    mix = _retention_mixer(xt_bf, pos, l1_w_in, l1_ret_gn_g, l1_ret_gn_b, l1_w_out)
    xt, xt_bf = add_layer_norm(xt, mix, l1_ln_mix_g, l1_ln_mix_b)
    xt, _ = _grouped_moe(xt, xt_bf, router_w, router_b, l1_moe_w_gate, l1_moe_w_up, l1_moe_w_down,
                         l1_sh_gate, l1_sh_up, l1_sh_down, l1_ln_ffn_g, l1_ln_ffn_b)
    return xt[None]
```

```python
import functools

import numpy as np
import jax
import jax.numpy as jnp
from jax import lax
from jax.experimental import pallas as pl
from jax.experimental.pallas import tpu as pltpu

F32 = jnp.float32
BF16 = jnp.bfloat16

D_MODEL = 2048
DEPTH = 2
ROPE_THETA = 500000.0
MLA_HEADS = 8
MLA_Q_LORA = 512
MLA_KV_LORA = 256
MLA_NOPE = 128
MLA_ROPE = 64
MLA_V = 128
DSA_HEADS = 8
DSA_KV_HEADS = 2
DSA_HEAD_DIM = 128
DSA_ROT = DSA_HEAD_DIM // 4
IDX_HEADS = 16
IDX_DIM = 64
IDX_ROT = IDX_DIM // 4
IDX_TOPK_MAX = 256
RET_HEADS = 8
RET_QK_DIM = 256
RET_V_DIM = 512
RET_CHUNK = 128
RET_THETA = 10000.0
N_EXPERTS = 64
N_GROUPS = 8
EXPERTS_PER_GROUP = N_EXPERTS // N_GROUPS
TOP_K = 2
D_EXPERT = 512
D_SHARED = 1024
DEEPNORM_ALPHA = (2.0 * DEPTH) ** 0.25

SPLITS_AB = (MLA_Q_LORA, MLA_KV_LORA, MLA_ROPE, DSA_HEADS * DSA_HEAD_DIM,
             2 * DSA_KV_HEADS * DSA_HEAD_DIM, IDX_HEADS * IDX_DIM, IDX_DIM, IDX_HEADS)
IN_AB = sum(SPLITS_AB)

VMEM_LIMIT_BYTES = 56 * 1024 * 1024
LANES = 128
MASKED_SCORE = -1e30
LOG2_E = 1.4426950408889634
INT32_MIN = -2 ** 31


def _params(*sem):
    return pltpu.CompilerParams(dimension_semantics=sem, vmem_limit_bytes=VMEM_LIMIT_BYTES)


def _mm_kernel(*refs, norm_eps):
    if norm_eps is None:
        a_ref, b_ref, o_ref = refs
        a = a_ref[...]
    else:
        a_ref, g_ref, b_ref, o_ref = refs
        af = a_ref[...].astype(F32)
        a = af * lax.rsqrt(jnp.mean(af * af, axis=-1, keepdims=True) + norm_eps) * g_ref[...]
    o_ref[...] = jnp.dot(a.astype(BF16), b_ref[...].astype(BF16),
                         preferred_element_type=F32).astype(o_ref.dtype)


def matmul(a, b, out_dtype, tm, tn, *, name, a_cols=None, norm_gain=None, norm_eps=1e-6):
    m = a.shape[0]
    k, n = b.shape
    col_blk = 0 if a_cols is None else a_cols[0]
    assert (a.shape[1] == k) if a_cols is None else (a_cols[1] == k)
    assert m % tm == 0 and n % tn == 0
    in_specs = [pl.BlockSpec((tm, k), lambda i, j: (i, col_blk))]
    args = [a]
    if norm_gain is not None:
        in_specs.append(pl.BlockSpec((1, k), lambda i, j: (0, 0)))
        args.append(norm_gain.reshape(1, k))
    in_specs.append(pl.BlockSpec((k, tn), lambda i, j: (0, j)))
    args.append(b)
    return pl.pallas_call(
        functools.partial(_mm_kernel, norm_eps=None if norm_gain is None else norm_eps),
        grid=(m // tm, n // tn),
        in_specs=in_specs,
        out_specs=pl.BlockSpec((tm, tn), lambda i, j: (i, j)),
        out_shape=jax.ShapeDtypeStruct((m, n), out_dtype),
        compiler_params=_params("parallel", "parallel"),
        name=name,
    )(*args)


def _layer_norm(z, g, b):
    mu = jnp.mean(z, axis=-1, keepdims=True)
    zc = z - mu
    var = jnp.mean(zc * zc, axis=-1, keepdims=True)
    return zc * lax.rsqrt(var + 1e-5) * g + b


def _add_ln_kernel(x_ref, y_ref, g_ref, b_ref, o_ref, obf_ref):
    z = DEEPNORM_ALPHA * x_ref[...] + y_ref[...]
    out = _layer_norm(z, g_ref[...], b_ref[...])
    o_ref[...] = out
    obf_ref[...] = out.astype(BF16)


def add_layer_norm(x, y, g, b, tm=256):
    s, d = x.shape
    row = pl.BlockSpec((tm, d), lambda i: (i, 0))
    vec = pl.BlockSpec((1, d), lambda i: (0, 0))
    return pl.pallas_call(
        _add_ln_kernel,
        grid=(s // tm,),
        in_specs=[row, row, vec, vec],
        out_specs=[row, row],
        out_shape=[jax.ShapeDtypeStruct((s, d), F32), jax.ShapeDtypeStruct((s, d), BF16)],
        compiler_params=_params("parallel"),
        name="add_layer_norm",
    )(x, y, g.reshape(1, d), b.reshape(1, d))


def _lane_tile(x, n):
    return jnp.tile(x, (1, n))


def _softmax_step(s, v_ext, m_ref, acc_ref):
    tk = s.shape[1]
    m_prev = m_ref[...]
    m_new = jnp.maximum(m_prev, jnp.max(s, axis=1)[:, None])
    p = jnp.exp2(s - _lane_tile(m_new, tk // LANES))
    alpha = jnp.exp2(m_prev - m_new)
    pv = jnp.dot(p.astype(BF16), v_ext, preferred_element_type=F32)
    acc_ref[...] = _lane_tile(alpha, 2) * acc_ref[...] + pv
    m_ref[...] = m_new


def _mla_kernel(q_ref, kn_ref, kp_ref, v_ref, o_ref, m_sc, acc_sc, *, tq, tk, hp):
    qi = pl.program_id(1)
    m_sc[...] = jnp.full_like(m_sc, MASKED_SCORE)
    acc_sc[...] = jnp.zeros_like(acc_sc)
    ones = jnp.ones((tk, LANES), BF16)
    per_block = tq // tk

    def attend(c, diagonal_offset):
        kp = kp_ref[c]
        kn = kn_ref[c]
        v = v_ref[c]
        for j in range(hp):
            q = q_ref[:, j * 2 * LANES:(j + 1) * 2 * LANES]
            k = jnp.concatenate([kn[:, j * LANES:(j + 1) * LANES], kp], axis=1)
            s = lax.dot_general(q, k, (((1,), (1,)), ((), ())), preferred_element_type=F32)
            if diagonal_offset is not None:
                row = lax.broadcasted_iota(jnp.int32, (tq, tk), 0)
                col = diagonal_offset * tk + lax.broadcasted_iota(jnp.int32, (tq, tk), 1)
                s = jnp.where(col <= row, s, MASKED_SCORE)
            v_ext = jnp.concatenate([v[:, j * LANES:(j + 1) * LANES], ones], axis=1)
            _softmax_step(s, v_ext, m_sc.at[j], acc_sc.at[j])

    def below_diagonal(c, carry):
        attend(c, None)
        return carry

    lax.fori_loop(0, qi * per_block, below_diagonal, 0)
    for d in range(per_block):
        attend(qi * per_block + d, d)
    for j in range(hp):
        acc = acc_sc[j]
        o_ref[:, j * LANES:(j + 1) * LANES] = (acc[:, :LANES] / acc[:, LANES:]).astype(o_ref.dtype)


def mla_attention(q, k_nope, k_pe, v, tq=512, tk=512, hp=2):
    s = q.shape[0]
    h = MLA_HEADS
    nc = s // tk
    assert tq % tk == 0 and h % hp == 0
    per_head = pl.BlockSpec((nc, tk, hp * LANES), lambda hh, qi: (0, 0, hh))
    return pl.pallas_call(
        functools.partial(_mla_kernel, tq=tq, tk=tk, hp=hp),
        grid=(h // hp, s // tq),
        in_specs=[
            pl.BlockSpec((tq, hp * 2 * LANES), lambda hh, qi: (qi, hh)),
            per_head,
            pl.BlockSpec((nc, tk, LANES), lambda hh, qi: (0, 0, 0)),
            per_head,
        ],
        out_specs=pl.BlockSpec((tq, hp * LANES), lambda hh, qi: (qi, hh)),
        out_shape=jax.ShapeDtypeStruct((s, h * MLA_V), BF16),
        scratch_shapes=[pltpu.VMEM((hp, tq, LANES), F32), pltpu.VMEM((hp, tq, 2 * LANES), F32)],
        compiler_params=_params("parallel", "arbitrary"),
        name="mla_attention",
    )(q, k_nope.reshape(nc, tk, h * MLA_NOPE), k_pe.reshape(nc, tk, LANES), v.reshape(nc, tk, h * MLA_V))


def _sortable_key(x):
    bits = pltpu.bitcast(x, jnp.int32)
    return bits ^ ((bits >> 31) & 0x7FFFFFFF)


def _dsa_kernel(iq_ref, iw_ref, ik_ref, q_ref, k_ref, v_ref, o_ref, key_sc, wb_sc, m_sc, acc_sc,
                *, tq, tk, n_sel):
    i = pl.program_id(0)
    n_chunks = ((i + 1) * tq + tk - 1) // tk
    rep = DSA_HEADS // DSA_KV_HEADS
    lane_tiles = tk // LANES

    iq_all = iq_ref[...].reshape(IDX_HEADS * tq, IDX_DIM)
    iw = iw_ref[...]
    for h in range(IDX_HEADS):
        wb_sc[h] = jnp.broadcast_to(iw[:, h:h + 1], (tq, LANES))

    def index_chunk(c, carry):
        d = lax.dot_general(iq_all, ik_ref[c], (((1,), (1,)), ((), ())), preferred_element_type=F32)
        d = jnp.maximum(d, 0.0).reshape(IDX_HEADS, tq, tk)
        score = _lane_tile(wb_sc[0], lane_tiles) * d[0]
        for h in range(1, IDX_HEADS):
            score = score + _lane_tile(wb_sc[h], lane_tiles) * d[h]
        key_sc[c] = _sortable_key(score)
        return carry

    lax.fori_loop(0, n_chunks, index_chunk, 0)
    last = n_chunks - 1
    row = i * tq + lax.broadcasted_iota(jnp.int32, (tq, tk), 0)
    col = last * tk + lax.broadcasted_iota(jnp.int32, (tq, tk), 1)
    key_sc[last] = jnp.where(col <= row, key_sc[last], INT32_MIN)

    def search_bit(b, thr):
        cand = thr + lax.shift_left(jnp.int32(1), 31 - b)

        def count_chunk(c, cnt):
            key = key_sc[c]
            for j in range(lane_tiles):
                cnt = cnt + jnp.where(key[:, j * LANES:(j + 1) * LANES] >= cand, 1.0, 0.0)
            return cnt

        cnt = lax.fori_loop(0, n_chunks, count_chunk, jnp.zeros((tq, LANES), F32))
        return jnp.where(jnp.sum(cnt, axis=1)[:, None] >= n_sel, cand, thr)

    thr = lax.fori_loop(0, 32, search_bit, jnp.full((tq, LANES), INT32_MIN, jnp.int32))
    thr = _lane_tile(jnp.maximum(thr, INT32_MIN + 1), lane_tiles)

    m_sc[...] = jnp.full_like(m_sc, MASKED_SCORE)
    acc_sc[...] = jnp.zeros_like(acc_sc)
    ones = jnp.ones((tk, LANES), BF16)

    def attend_chunk(c, carry):
        bias = jnp.where(key_sc[c] >= thr, 0.0, MASKED_SCORE)
        kc = k_ref[c]
        vc = v_ref[c]
        for g in range(DSA_KV_HEADS):
            cols = slice(g * DSA_HEAD_DIM, (g + 1) * DSA_HEAD_DIM)
            qg = q_ref[g * rep:(g + 1) * rep].reshape(rep * tq, DSA_HEAD_DIM)
            s = lax.dot_general(qg, kc[:, cols], (((1,), (1,)), ((), ())), preferred_element_type=F32)
            s = (s.reshape(rep, tq, tk) + bias[None]).reshape(rep * tq, tk)
            _softmax_step(s, jnp.concatenate([vc[:, cols], ones], axis=1), m_sc.at[g], acc_sc.at[g])
        return carry

    lax.fori_loop(0, n_chunks, attend_chunk, 0)
    for g in range(DSA_KV_HEADS):
        acc = acc_sc[g]
        out = acc[:, :LANES] / acc[:, LANES:]
        for r in range(rep):
            hh = g * rep + r
            o_ref[:, hh * DSA_HEAD_DIM:(hh + 1) * DSA_HEAD_DIM] = out[r * tq:(r + 1) * tq].astype(o_ref.dtype)


def dsa_attention(iq, iw, ik, q, k, v, tq=128, tk=512):
    s = ik.shape[0]
    n_sel = min(IDX_TOPK_MAX, s // 4)
    nc = s // tk
    assert tk >= n_sel and s % tk == 0 and tk % tq == 0 and DSA_HEAD_DIM == LANES
    rep = DSA_HEADS // DSA_KV_HEADS
    kvw = DSA_KV_HEADS * DSA_HEAD_DIM
    whole3 = lambda i: (0, 0, 0)
    return pl.pallas_call(
        functools.partial(_dsa_kernel, tq=tq, tk=tk, n_sel=n_sel),
        grid=(s // tq,),
        in_specs=[
            pl.BlockSpec((IDX_HEADS, tq, IDX_DIM), lambda i: (0, i, 0)),
            pl.BlockSpec((tq, IDX_HEADS), lambda i: (i, 0)),
            pl.BlockSpec((nc, tk, IDX_DIM), whole3),
            pl.BlockSpec((DSA_HEADS, tq, DSA_HEAD_DIM), lambda i: (0, i, 0)),
            pl.BlockSpec((nc, tk, kvw), whole3),
            pl.BlockSpec((nc, tk, kvw), whole3),
        ],
        out_specs=pl.BlockSpec((tq, DSA_HEADS * DSA_HEAD_DIM), lambda i: (i, 0)),
        out_shape=jax.ShapeDtypeStruct((s, DSA_HEADS * DSA_HEAD_DIM), BF16),
        scratch_shapes=[
            pltpu.VMEM((nc, tq, tk), jnp.int32),
            pltpu.VMEM((IDX_HEADS, tq, LANES), F32),
            pltpu.VMEM((DSA_KV_HEADS, rep * tq, LANES), F32),
            pltpu.VMEM((DSA_KV_HEADS, rep * tq, 2 * LANES), F32),
        ],
        compiler_params=_params("parallel"),
        name="dsa_attention",
    )(iq, iw, ik.reshape(nc, tk, IDX_DIM), q, k.reshape(nc, tk, kvw), v.reshape(nc, tk, kvw))


def _retention_tables():
    h, c = RET_HEADS, RET_CHUNK
    log_g = np.log(1.0 - 2.0 ** (-5.0 - np.arange(h, dtype=np.float32))).astype(np.float32).astype(np.float64)
    idx = np.arange(c, dtype=np.float64)
    diff = idx[:, None] - idx[None, :]
    decay_in = np.where(diff[None] >= 0, np.exp(np.maximum(diff, 0.0)[None] * log_g[:, None, None]), 0.0)
    xi = np.exp((idx + 1.0)[None, :] * log_g[:, None])
    zeta = np.exp((c - 1.0 - idx)[None, :] * log_g[:, None])
    chunk_decay = np.exp(c * log_g)
    lane = np.ones((1, 1, LANES))
    return (decay_in.astype(np.float32), (xi[:, :, None] * lane).astype(np.float32),
            (zeta[:, :, None] * lane).astype(np.float32),
            (chunk_decay[:, None, None] * np.ones((1, 8, LANES))).astype(np.float32))


def _retention_kernel(q_ref, k_ref, v_ref, g_ref, cos_ref, sin_ref, din_ref, xi_ref, zeta_ref, cd_ref,
                      gng_ref, gnb_ref, o_ref, r_sc):
    n = pl.program_id(1)
    half = RET_QK_DIM // 2

    @pl.when(n == 0)
    def _():
        r_sc[...] = jnp.zeros_like(r_sc)

    cos = cos_ref[...]
    sin = sin_ref[...]

    def rope(x):
        x1 = x[:, :half]
        x2 = x[:, half:]
        return x1 * cos - x2 * sin, x1 * sin + x2 * cos

    q1, q2 = rope(q_ref[...])
    k1, k2 = rope(k_ref[...])
    scale = RET_QK_DIM ** -0.5
    qr = jnp.concatenate([q1, q2], axis=1).astype(BF16)
    kr = jnp.concatenate([k1 * scale, k2 * scale], axis=1)
    v = v_ref[...].astype(BF16)

    inner = lax.dot_general(qr, kr.astype(BF16), (((1,), (1,)), ((), ())), preferred_element_type=F32)
    inner = inner * din_ref[0]
    r = r_sc[...]
    cross = jnp.dot(qr, r.astype(BF16), preferred_element_type=F32)
    xi = xi_ref[0]
    o = jnp.dot(inner.astype(BF16), v, preferred_element_type=F32)
    o = o + cross * jnp.concatenate([xi] * (RET_V_DIM // LANES), axis=1)

    zeta = zeta_ref[0]
    kz = (kr * jnp.concatenate([zeta] * (RET_QK_DIM // LANES), axis=1)).astype(BF16)
    upd = lax.dot_general(kz, v, (((0,), (0,)), ((), ())), preferred_element_type=F32)
    r_sc[...] = cd_ref[0][:1, :1] * r + upd

    mu = jnp.mean(o, axis=-1, keepdims=True)
    oc = o - mu
    var = jnp.mean(oc * oc, axis=-1, keepdims=True)
    y = oc * lax.rsqrt(var + 1e-5) * gng_ref[...] + gnb_ref[...]
    gate = g_ref[...]
    o_ref[...] = (gate * jax.nn.sigmoid(gate) * y).astype(o_ref.dtype)


def retention(h1, cos, sin, gn_g, gn_b):
    s = h1.shape[0]
    hh, dk, dv, c = RET_HEADS, RET_QK_DIM, RET_V_DIM, RET_CHUNK
    din, xi, zeta, cd = (jnp.asarray(t) for t in _retention_tables())
    k_blk0 = hh * dk // dk
    v_blk0 = 2 * hh * dk // dv
    g_blk0 = v_blk0 + hh
    per_head = lambda r, w: pl.BlockSpec((1, r, w), lambda h, n: (h, 0, 0))
    return pl.pallas_call(
        _retention_kernel,
        grid=(hh, s // c),
        in_specs=[
            pl.BlockSpec((c, dk), lambda h, n: (n, h)),
            pl.BlockSpec((c, dk), lambda h, n: (n, k_blk0 + h)),
            pl.BlockSpec((c, dv), lambda h, n: (n, v_blk0 + h)),
            pl.BlockSpec((c, dv), lambda h, n: (n, g_blk0 + h)),
            pl.BlockSpec((c, dk // 2), lambda h, n: (n, 0)),
            pl.BlockSpec((c, dk // 2), lambda h, n: (n, 0)),
            per_head(c, c), per_head(c, LANES), per_head(c, LANES), per_head(8, LANES),
            pl.BlockSpec((1, dv), lambda h, n: (0, h)),
            pl.BlockSpec((1, dv), lambda h, n: (0, h)),
        ],
        out_specs=pl.BlockSpec((c, dv), lambda h, n: (n, h)),
        out_shape=jax.ShapeDtypeStruct((s, hh * dv), BF16),
        scratch_shapes=[pltpu.VMEM((dk, dv), F32)],
        compiler_params=_params("parallel", "arbitrary"),
        name="retention",
    )(h1, h1, h1, h1, cos, sin, din, xi, zeta, cd, gn_g.reshape(1, -1), gn_b.reshape(1, -1))


def _first_argmax(v, idx, n):
    m = jnp.max(v, axis=0, keepdims=True)
    first = jnp.min(jnp.where(v == m, idx, n), axis=0, keepdims=True)
    return m, first


def _router_kernel(x_ref, rwt_ref, rb_ref, e_ref, g_ref):
    tm = x_ref.shape[0]
    epg = EXPERTS_PER_GROUP
    logits = lax.dot_general(rwt_ref[...], x_ref[...], (((1,), (1,)), ((), ())),
                             precision=lax.Precision.HIGHEST, preferred_element_type=F32)
    scores = jax.nn.sigmoid(logits)
    biased = scores + rb_ref[...]
    idx = lax.broadcasted_iota(jnp.int32, (epg, tm), 0)

    group_scores = []
    for g in range(N_GROUPS):
        v = biased[g * epg:(g + 1) * epg]
        m1, first = _first_argmax(v, idx, epg)
        m2 = jnp.max(jnp.where(idx == first, -jnp.inf, v), axis=0, keepdims=True)
        group_scores.append(m1 + m2)
    gmax = group_scores[0]
    for g in range(1, N_GROUPS):
        gmax = jnp.maximum(gmax, group_scores[g])
    gsel = jnp.full((1, tm), N_GROUPS, jnp.int32)
    for g in range(N_GROUPS - 1, -1, -1):
        gsel = jnp.where(group_scores[g] == gmax, g, gsel)

    in_biased = jnp.zeros((epg, tm), F32)
    in_scores = jnp.zeros((epg, tm), F32)
    for g in range(N_GROUPS):
        pick = gsel == g
        in_biased = jnp.where(pick, biased[g * epg:(g + 1) * epg], in_biased)
        in_scores = jnp.where(pick, scores[g * epg:(g + 1) * epg], in_scores)
    _, loc1 = _first_argmax(in_biased, idx, epg)
    _, loc2 = _first_argmax(jnp.where(idx == loc1, -jnp.inf, in_biased), idx, epg)
    s1 = jnp.sum(jnp.where(idx == loc1, in_scores, 0.0), axis=0, keepdims=True)
    s2 = jnp.sum(jnp.where(idx == loc2, in_scores, 0.0), axis=0, keepdims=True)
    denom = s1 + s2
    e_ref[...] = jnp.concatenate([gsel * epg + loc1, gsel * epg + loc2], axis=0)
    g_ref[...] = jnp.concatenate([s1 / denom, s2 / denom], axis=0)


def moe_router(x, router_w, router_b, tm=512):
    s, d = x.shape
    return pl.pallas_call(
        _router_kernel,
        grid=(s // tm,),
        in_specs=[
            pl.BlockSpec((tm, d), lambda i: (i, 0)),
            pl.BlockSpec((N_EXPERTS, d), lambda i: (0, 0)),
            pl.BlockSpec((N_EXPERTS, 1), lambda i: (0, 0)),
        ],
        out_specs=[pl.BlockSpec((TOP_K, tm), lambda i: (0, i)), pl.BlockSpec((TOP_K, tm), lambda i: (0, i))],
        out_shape=[jax.ShapeDtypeStruct((TOP_K, s), jnp.int32), jax.ShapeDtypeStruct((TOP_K, s), F32)],
        compiler_params=_params("parallel"),
        name="moe_router",
    )(x, router_w.T, router_b.reshape(N_EXPERTS, 1))


def _gather_rows_kernel(n_active_ref, idx_ref, src_ref, o_ref, sem, *, rows):
    i = pl.program_id(0)

    @pl.when(i < n_active_ref[0])
    def _():
        def start(r, carry):
            pltpu.make_async_copy(src_ref.at[pl.ds(idx_ref[0, 0, r], 1)], o_ref.at[pl.ds(r, 1)], sem).start()
            return carry

        lax.fori_loop(0, rows, start, 0, unroll=8)
        pltpu.make_async_copy(src_ref.at[pl.ds(0, rows)], o_ref, sem).wait()

    @pl.when(i >= n_active_ref[0])
    def _():
        o_ref[...] = jnp.zeros_like(o_ref)


def gather_rows(src, idx, n_active_tiles, rows=128):
    n = idx.shape[0]
    d = src.shape[1]
    nt = n // rows

    def tile_blk(i, na):
        return jnp.minimum(i, na[0] - 1)

    return pl.pallas_call(
        functools.partial(_gather_rows_kernel, rows=rows),
        grid_spec=pltpu.PrefetchScalarGridSpec(
            num_scalar_prefetch=1,
            grid=(nt,),
            in_specs=[
                pl.BlockSpec((1, 1, rows), lambda i, na: (tile_blk(i, na), 0, 0), memory_space=pltpu.SMEM),
                pl.BlockSpec(memory_space=pl.ANY),
            ],
            out_specs=pl.BlockSpec((rows, d), lambda i, na: (i, 0)),
            scratch_shapes=[pltpu.SemaphoreType.DMA(())],
        ),
        out_shape=jax.ShapeDtypeStruct((n, d), src.dtype),
        compiler_params=_params("arbitrary"),
        name="gather_rows",
    )(n_active_tiles, idx.reshape(nt, 1, rows), src)


def _expert_kernel(te_ref, na_ref, x_ref, wg_ref, wu_ref, wd_ref, o_ref, wg_sc, wu_sc, wd_sc):
    j = pl.program_id(0)
    active = j < na_ref[0]
    new_expert = (j == 0) | (te_ref[j] != te_ref[jnp.maximum(j - 1, 0)])

    @pl.when(active & new_expert)
    def _():
        wg_sc[...] = wg_ref[0].astype(BF16)
        wu_sc[...] = wu_ref[0].astype(BF16)
        wd_sc[...] = wd_ref[0].astype(BF16)

    @pl.when(active)
    def _():
        x = x_ref[...].astype(BF16)
        hg = jnp.dot(x, wg_sc[...], preferred_element_type=F32)
        hu = jnp.dot(x, wu_sc[...], preferred_element_type=F32)
        h = hg * jax.nn.sigmoid(hg) * hu
        o_ref[...] = jnp.dot(h.astype(BF16), wd_sc[...], preferred_element_type=F32)

    @pl.when(jnp.logical_not(active))
    def _():
        o_ref[...] = jnp.zeros_like(o_ref)


def routed_experts(xs, tile_expert, n_active_tiles, w_gate, w_up, w_down, tm):
    p, d = xs.shape
    f = w_gate.shape[2]
    nt = p // tm

    def row_blk(j, te, na):
        return (jnp.minimum(j, na[0] - 1), 0)

    return pl.pallas_call(
        _expert_kernel,
        grid_spec=pltpu.PrefetchScalarGridSpec(
            num_scalar_prefetch=2,
            grid=(nt,),
            in_specs=[
                pl.BlockSpec((tm, d), row_blk),
                pl.BlockSpec((1, d, f), lambda j, te, na: (te[j], 0, 0)),
                pl.BlockSpec((1, d, f), lambda j, te, na: (te[j], 0, 0)),
                pl.BlockSpec((1, f, d), lambda j, te, na: (te[j], 0, 0)),
            ],
            out_specs=pl.BlockSpec((tm, d), lambda j, te, na: (j, 0)),
            scratch_shapes=[pltpu.VMEM((d, f), BF16), pltpu.VMEM((d, f), BF16), pltpu.VMEM((f, d), BF16)],
        ),
        out_shape=jax.ShapeDtypeStruct((p, d), F32),
        compiler_params=_params("arbitrary"),
        name="routed_experts",
    )(tile_expert, n_active_tiles, xs, w_gate, w_up, w_down)


def _shared_kernel(x_ref, wg_ref, wu_ref, wd_ref, o_ref):
    x = x_ref[...]
    hg = jnp.dot(x, wg_ref[...], preferred_element_type=F32)
    hu = jnp.dot(x, wu_ref[...], preferred_element_type=F32)
    h = hg * jax.nn.sigmoid(hg) * hu
    o_ref[...] = jnp.dot(h.astype(BF16), wd_ref[...], preferred_element_type=F32)


def shared_expert(x_bf, wg, wu, wd, tm=512):
    s, d = x_bf.shape
    f = wg.shape[1]
    whole = lambda r, c: pl.BlockSpec((r, c), lambda i: (0, 0))
    return pl.pallas_call(
        _shared_kernel,
        grid=(s // tm,),
        in_specs=[pl.BlockSpec((tm, d), lambda i: (i, 0)), whole(d, f), whole(d, f), whole(f, d)],
        out_specs=pl.BlockSpec((tm, d), lambda i: (i, 0)),
        out_shape=jax.ShapeDtypeStruct((s, d), F32),
        compiler_params=_params("parallel"),
        name="shared_expert",
    )(x_bf, wg.astype(BF16), wu.astype(BF16), wd.astype(BF16))


def _combine_kernel(pos_ref, x_ref, sh_ref, gate_ref, ys_ref, g_ref, b_ref, o_ref, obf_ref, buf, sem, *, tm):
    def start(r, carry):
        for k in range(TOP_K):
            pltpu.make_async_copy(ys_ref.at[pl.ds(pos_ref[0, k, r], 1)], buf.at[k, pl.ds(r, 1)], sem).start()
        return carry

    lax.fori_loop(0, tm, start, 0, unroll=8)
    for k in range(TOP_K):
        pltpu.make_async_copy(ys_ref.at[pl.ds(0, tm)], buf.at[k], sem).wait()
    gate = gate_ref[...]
    routed = gate[:, 0:1] * buf[0] + gate[:, 1:2] * buf[1]
    z = DEEPNORM_ALPHA * x_ref[...] + (routed + sh_ref[...])
    out = _layer_norm(z, g_ref[...], b_ref[...])
    o_ref[...] = out
    obf_ref[...] = out.astype(BF16)


def moe_combine(x, shared, gates, pos, ys, g, b, tm=256):
    s, d = x.shape
    nt = s // tm
    row = pl.BlockSpec((tm, d), lambda i: (i, 0))
    vec = pl.BlockSpec((1, d), lambda i: (0, 0))
    return pl.pallas_call(
        functools.partial(_combine_kernel, tm=tm),
        grid=(nt,),
        in_specs=[
            pl.BlockSpec((1, TOP_K, tm), lambda i: (i, 0, 0), memory_space=pltpu.SMEM),
            row, row,
            pl.BlockSpec((tm, TOP_K), lambda i: (i, 0)),
            pl.BlockSpec(memory_space=pl.ANY),
            vec, vec,
        ],
        out_specs=[row, row],
        out_shape=[jax.ShapeDtypeStruct((s, d), F32), jax.ShapeDtypeStruct((s, d), BF16)],
        scratch_shapes=[pltpu.VMEM((TOP_K, tm, d), F32), pltpu.SemaphoreType.DMA(())],
        compiler_params=_params("arbitrary"),
        name="moe_combine",
    )(pos.reshape(TOP_K, nt, tm).transpose(1, 0, 2), x, shared, gates.T, ys, g.reshape(1, d), b.reshape(1, d))


def _rope_tables(pos, rot_dim, theta):
    half = rot_dim // 2
    inv = (1.0 / (theta ** (np.arange(half, dtype=np.float32) / half))).astype(np.float32)
    ang = pos.astype(F32)[:, None] * inv
    return jnp.cos(ang), jnp.sin(ang)


def _rope(x, cos, sin, rot_dim):
    half = rot_dim // 2
    x1 = x[..., :half]
    x2 = x[..., half:rot_dim]
    c = cos[:, None, :]
    s = sin[:, None, :]
    return jnp.concatenate([x1 * c - x2 * s, x1 * s + x2 * c, x[..., rot_dim:]], axis=-1)


def _routing_plan(experts, tm):
    k, s = experts.shape
    flat = experts.reshape(-1)
    onehot = (flat[:, None] == jnp.arange(N_EXPERTS, dtype=jnp.int32)[None, :]).astype(jnp.int32)
    csum = jnp.cumsum(onehot, axis=0)
    rank = jnp.take_along_axis(csum, flat[:, None], axis=1)[:, 0] - 1
    sizes = csum[-1]
    tiles = (sizes + tm - 1) // tm
    tile_end = jnp.cumsum(tiles)
    start = (tile_end - tiles) * tm
    pos = start[flat] + rank
    n_rows = k * s + N_EXPERTS * tm
    n_tiles = n_rows // tm
    row_token = jnp.zeros((n_rows,), jnp.int32).at[pos].set(jnp.tile(jnp.arange(s, dtype=jnp.int32), k))
    n_active = tile_end[-1]
    tile_ids = jnp.minimum(jnp.arange(n_tiles, dtype=jnp.int32), n_active - 1)
    tile_expert = jnp.searchsorted(tile_end, tile_ids, side="right").astype(jnp.int32)
    return pos.reshape(k, s), row_token, tile_expert, n_active.reshape(1).astype(jnp.int32)


def _grouped_moe(x, x_bf, router_w, router_b, w_gate, w_up, w_down, sh_gate, sh_up, sh_down, ln_g, ln_b,
                 expert_tm=256, gather_rows_per_step=512):
    experts, gates = moe_router(x, router_w, router_b)
    pos, row_token, tile_expert, n_active = _routing_plan(experts, expert_tm)
    n_gather_steps = (n_active * expert_tm + gather_rows_per_step - 1) // gather_rows_per_step
    xs = gather_rows(x, row_token, n_gather_steps, gather_rows_per_step)
    ys = routed_experts(xs, tile_expert, n_active, w_gate, w_up, w_down, expert_tm)
    shared = shared_expert(x_bf, sh_gate, sh_up, sh_down)
    return moe_combine(x, shared, gates, pos, ys, ln_g, ln_b)


def _mla_dsa_mixer(x, pos, w_in, q_norm, w_uq, kv_norm, w_ukv, w_out):
    s = x.shape[0]
    n_pad = (-IN_AB) % 512
    h0 = matmul(x, jnp.pad(w_in, ((0, 0), (0, n_pad))), F32, 1024, 512, name="l0_in_proj")
    cq, ckv, krope, dq, dkv, iq, ik, iw = jnp.split(h0[:, :IN_AB], [int(c) for c in np.cumsum(SPLITS_AB)[:-1]], axis=-1)
    cos_m, sin_m = _rope_tables(pos, MLA_ROPE, ROPE_THETA)
    cos_d, sin_d = _rope_tables(pos, DSA_ROT, ROPE_THETA)
    cos_i, sin_i = _rope_tables(pos, IDX_ROT, ROPE_THETA)

    dqk = MLA_NOPE + MLA_ROPE
    qa = matmul(h0, w_uq, F32, 1024, 512, name="mla_q_up", a_cols=(0, MLA_Q_LORA),
                norm_gain=q_norm).reshape(s, MLA_HEADS, dqk)
    kva = matmul(h0, w_ukv, F32, 1024, 512, name="mla_kv_up", a_cols=(MLA_Q_LORA // MLA_KV_LORA, MLA_KV_LORA),
                 norm_gain=kv_norm).reshape(s, MLA_HEADS, MLA_NOPE + MLA_V)
    q_pe = _rope(qa[..., MLA_NOPE:], cos_m, sin_m, MLA_ROPE)
    q_cat = jnp.concatenate([qa[..., :MLA_NOPE], q_pe, jnp.zeros((s, MLA_HEADS, 2 * LANES - dqk), F32)], axis=-1)
    q_cat = (q_cat * (dqk ** -0.5 * LOG2_E)).astype(BF16).reshape(s, MLA_HEADS * 2 * LANES)
    k_pe = _rope(krope[:, None, :], cos_m, sin_m, MLA_ROPE)[:, 0]
    k_pe = jnp.concatenate([k_pe, jnp.zeros((s, LANES - MLA_ROPE), F32)], axis=-1).astype(BF16)
    k_nope = kva[..., :MLA_NOPE].astype(BF16).reshape(s, MLA_HEADS * MLA_NOPE)
    v_a = kva[..., MLA_NOPE:].astype(BF16).reshape(s, MLA_HEADS * MLA_V)
    out_a = mla_attention(q_cat, k_nope, k_pe, v_a)

    qb = _rope(dq.reshape(s, DSA_HEADS, DSA_HEAD_DIM), cos_d, sin_d, DSA_ROT) * (DSA_HEAD_DIM ** -0.5 * LOG2_E)
    qb = qb.astype(BF16).transpose(1, 0, 2)
    kvb = dkv.reshape(s, 2, DSA_KV_HEADS, DSA_HEAD_DIM)
    kb = _rope(kvb[:, 0], cos_d, sin_d, DSA_ROT).astype(BF16).reshape(s, DSA_KV_HEADS * DSA_HEAD_DIM)
    vb = kvb[:, 1].astype(BF16).reshape(s, DSA_KV_HEADS * DSA_HEAD_DIM)
    iq = _rope(iq.reshape(s, IDX_HEADS, IDX_DIM), cos_i, sin_i, IDX_ROT).astype(BF16).transpose(1, 0, 2)
    ik = _rope(ik[:, None, :], cos_i, sin_i, IDX_ROT)[:, 0].astype(BF16)
    iw = iw * (IDX_HEADS ** -0.5 * IDX_DIM ** -0.5)
    out_b = dsa_attention(iq, iw, ik, qb, kb, vb)

    return matmul(jnp.concatenate([out_a, out_b], axis=-1), w_out, F32, 1024, 512, name="l0_out_proj")


def _retention_mixer(x_bf, pos, w_in, gn_g, gn_b, w_out):
    h1 = matmul(x_bf, w_in, F32, 1024, 1024, name="l1_in_proj")
    cos, sin = _rope_tables(pos, RET_QK_DIM, RET_THETA)
    y = retention(h1, cos, sin, gn_g, gn_b)
    return matmul(y, w_out, F32, 1024, 512, name="l1_out_proj")


def kernel(x, positions, router_w, router_b, l0_w_in, l0_mla_q_norm, l0_mla_w_uq, l0_mla_kv_norm, l0_mla_w_ukv, l0_w_out, l1_w_in, l1_ret_gn_g, l1_ret_gn_b, l1_w_out, l0_ln_mix_g, l0_ln_mix_b, l0_moe_w_gate, l0_moe_w_up, l0_moe_w_down, l0_sh_gate, l0_sh_up, l0_sh_down, l0_ln_ffn_g, l0_ln_ffn_b, l1_ln_mix_g, l1_ln_mix_b, l1_moe_w_gate, l1_moe_w_up, l1_moe_w_down, l1_sh_gate, l1_sh_up, l1_sh_down, l1_ln_ffn_g, l1_ln_ffn_b):
    assert x.shape[0] == 1
    xt = x[0]
    pos = positions[0]

    mix = _mla_dsa_mixer(xt, pos, l0_w_in, l0_mla_q_norm, l0_mla_w_uq, l0_mla_kv_norm, l0_mla_w_ukv, l0_w_out)
    xt, xt_bf = add_layer_norm(xt, mix, l0_ln_mix_g, l0_ln_mix_b)
    xt, xt_bf = _grouped_moe(xt, xt_bf, router_w, router_b, l0_moe_w_gate, l0_moe_w_up, l0_moe_w_down,
                             l0_sh_gate, l0_sh_up, l0_sh_down, l0_ln_ffn_g, l0_ln_ffn_b)

    mix = _retention_mixer(xt_bf, pos, l1_w_in, l1_ret_gn_g, l1_ret_gn_b, l1_w_out)
    xt, xt_bf = add_layer_norm(xt, mix, l1_ln_mix_g, l1_ln_mix_b)
    xt, _ = _grouped_moe(xt, xt_bf, router_w, router_b, l1_moe_w_gate, l1_moe_w_up, l1_moe_w_down,
                         l1_sh_gate, l1_sh_up, l1_sh_down, l1_ln_ffn_g, l1_ln_ffn_b)
    return xt[None]
```

```python
import functools

import numpy as np
import jax
import jax.numpy as jnp
from jax import lax
from jax.experimental import pallas as pl
from jax.experimental.pallas import tpu as pltpu

F32 = jnp.float32
BF16 = jnp.bfloat16

D_MODEL = 2048
DEPTH = 2
ROPE_THETA = 500000.0
MLA_HEADS = 8
MLA_Q_LORA = 512
MLA_KV_LORA = 256
MLA_NOPE = 128
MLA_ROPE = 64
MLA_V = 128
DSA_HEADS = 8
DSA_KV_HEADS = 2
DSA_HEAD_DIM = 128
DSA_ROT = DSA_HEAD_DIM // 4
IDX_HEADS = 16
IDX_DIM = 64
IDX_ROT = IDX_DIM // 4
IDX_TOPK_MAX = 256
RET_HEADS = 8
RET_QK_DIM = 256
RET_V_DIM = 512
RET_CHUNK = 128
RET_THETA = 10000.0
N_EXPERTS = 64
N_GROUPS = 8
EXPERTS_PER_GROUP = N_EXPERTS // N_GROUPS
TOP_K = 2
D_EXPERT = 512
D_SHARED = 1024
DEEPNORM_ALPHA = (2.0 * DEPTH) ** 0.25

SPLITS_AB = (MLA_Q_LORA, MLA_KV_LORA, MLA_ROPE, DSA_HEADS * DSA_HEAD_DIM,
             2 * DSA_KV_HEADS * DSA_HEAD_DIM, IDX_HEADS * IDX_DIM, IDX_DIM, IDX_HEADS)
IN_AB = sum(SPLITS_AB)

VMEM_LIMIT_BYTES = 56 * 1024 * 1024
LANES = 128
MASKED_SCORE = -1e30
LOG2_E = 1.4426950408889634
DMA_ISSUE_UNROLL = 8
INT32_MIN = -2 ** 31


def _params(*sem):
    return pltpu.CompilerParams(dimension_semantics=sem, vmem_limit_bytes=VMEM_LIMIT_BYTES)


def _mm_kernel(*refs, norm_eps):
    if norm_eps is None:
        a_ref, b_ref, o_ref = refs
        a = a_ref[...]
    else:
        a_ref, g_ref, b_ref, o_ref = refs
        af = a_ref[...].astype(F32)
        a = af * lax.rsqrt(jnp.mean(af * af, axis=-1, keepdims=True) + norm_eps) * g_ref[...]
    o_ref[...] = jnp.dot(a.astype(BF16), b_ref[...].astype(BF16),
                         preferred_element_type=F32).astype(o_ref.dtype)


def matmul(a, b, out_dtype, tm, tn, *, name, a_cols=None, norm_gain=None, norm_eps=1e-6):
    m = a.shape[0]
    k, n = b.shape
    col_blk = 0 if a_cols is None else a_cols[0]
    assert (a.shape[1] == k) if a_cols is None else (a_cols[1] == k)
    assert m % tm == 0 and n % tn == 0
    in_specs = [pl.BlockSpec((tm, k), lambda i, j: (i, col_blk))]
    args = [a]
    if norm_gain is not None:
        in_specs.append(pl.BlockSpec((1, k), lambda i, j: (0, 0)))
        args.append(norm_gain.reshape(1, k))
    in_specs.append(pl.BlockSpec((k, tn), lambda i, j: (0, j)))
    args.append(b)
    return pl.pallas_call(
        functools.partial(_mm_kernel, norm_eps=None if norm_gain is None else norm_eps),
        grid=(m // tm, n // tn),
        in_specs=in_specs,
        out_specs=pl.BlockSpec((tm, tn), lambda i, j: (i, j)),
        out_shape=jax.ShapeDtypeStruct((m, n), out_dtype),
        compiler_params=_params("parallel", "parallel"),
        name=name,
    )(*args)


def _layer_norm(z, g, b):
    mu = jnp.mean(z, axis=-1, keepdims=True)
    zc = z - mu
    var = jnp.mean(zc * zc, axis=-1, keepdims=True)
    return zc * lax.rsqrt(var + 1e-5) * g + b


def _add_ln_kernel(x_ref, y_ref, g_ref, b_ref, o_ref, obf_ref):
    z = DEEPNORM_ALPHA * x_ref[...] + y_ref[...]
    out = _layer_norm(z, g_ref[...], b_ref[...])
    o_ref[...] = out
    obf_ref[...] = out.astype(BF16)


def add_layer_norm(x, y, g, b, tm=256):
    s, d = x.shape
    row = pl.BlockSpec((tm, d), lambda i: (i, 0))
    vec = pl.BlockSpec((1, d), lambda i: (0, 0))
    return pl.pallas_call(
        _add_ln_kernel,
        grid=(s // tm,),
        in_specs=[row, row, vec, vec],
        out_specs=[row, row],
        out_shape=[jax.ShapeDtypeStruct((s, d), F32), jax.ShapeDtypeStruct((s, d), BF16)],
        compiler_params=_params("parallel"),
        name="add_layer_norm",
    )(x, y, g.reshape(1, d), b.reshape(1, d))


def _lane_tile(x, n):
    return jnp.tile(x, (1, n))


def _softmax_step(s, v_ext, m_ref, acc_ref):
    tk = s.shape[1]
    m_prev = m_ref[...]
    m_new = jnp.maximum(m_prev, jnp.max(s, axis=1)[:, None])
    p = jnp.exp2(s - _lane_tile(m_new, tk // LANES))
    alpha = jnp.exp2(m_prev - m_new)
    pv = jnp.dot(p.astype(BF16), v_ext, preferred_element_type=F32)
    acc_ref[...] = _lane_tile(alpha, 2) * acc_ref[...] + pv
    m_ref[...] = m_new


def _mla_kernel(q_ref, kn_ref, kp_ref, v_ref, o_ref, m_sc, acc_sc, *, tq, tk, hp):
    qi = pl.program_id(1)
    m_sc[...] = jnp.full_like(m_sc, MASKED_SCORE)
    acc_sc[...] = jnp.zeros_like(acc_sc)
    ones = jnp.ones((tk, LANES), BF16)
    per_block = tq // tk

    def attend(c, diagonal_offset):
        kp = kp_ref[c]
        kn = kn_ref[c]
        v = v_ref[c]
        for j in range(hp):
            q = q_ref[:, j * 2 * LANES:(j + 1) * 2 * LANES]
            k = jnp.concatenate([kn[:, j * LANES:(j + 1) * LANES], kp], axis=1)
            s = lax.dot_general(q, k, (((1,), (1,)), ((), ())), preferred_element_type=F32)
            if diagonal_offset is not None:
                row = lax.broadcasted_iota(jnp.int32, (tq, tk), 0)
                col = diagonal_offset * tk + lax.broadcasted_iota(jnp.int32, (tq, tk), 1)
                s = jnp.where(col <= row, s, MASKED_SCORE)
            v_ext = jnp.concatenate([v[:, j * LANES:(j + 1) * LANES], ones], axis=1)
            _softmax_step(s, v_ext, m_sc.at[j], acc_sc.at[j])

    def below_diagonal(c, carry):
        attend(c, None)
        return carry

    lax.fori_loop(0, qi * per_block, below_diagonal, 0)
    for d in range(per_block):
        attend(qi * per_block + d, d)
    for j in range(hp):
        acc = acc_sc[j]
        o_ref[:, j * LANES:(j + 1) * LANES] = (acc[:, :LANES] / acc[:, LANES:]).astype(o_ref.dtype)


def mla_attention(q, k_nope, k_pe, v, tq=512, tk=512, hp=2):
    s = q.shape[0]
    h = MLA_HEADS
    nc = s // tk
    assert tq % tk == 0 and h % hp == 0
    per_head = pl.BlockSpec((nc, tk, hp * LANES), lambda hh, qi: (0, 0, hh))
    return pl.pallas_call(
        functools.partial(_mla_kernel, tq=tq, tk=tk, hp=hp),
        grid=(h // hp, s // tq),
        in_specs=[
            pl.BlockSpec((tq, hp * 2 * LANES), lambda hh, qi: (qi, hh)),
            per_head,
            pl.BlockSpec((nc, tk, LANES), lambda hh, qi: (0, 0, 0)),
            per_head,
        ],
        out_specs=pl.BlockSpec((tq, hp * LANES), lambda hh, qi: (qi, hh)),
        out_shape=jax.ShapeDtypeStruct((s, h * MLA_V), BF16),
        scratch_shapes=[pltpu.VMEM((hp, tq, LANES), F32), pltpu.VMEM((hp, tq, 2 * LANES), F32)],
        compiler_params=_params("parallel", "arbitrary"),
        name="mla_attention",
    )(q, k_nope.reshape(nc, tk, h * MLA_NOPE), k_pe.reshape(nc, tk, LANES), v.reshape(nc, tk, h * MLA_V))


def _sortable_key(x):
    bits = pltpu.bitcast(x, jnp.int32)
    return bits ^ ((bits >> 31) & 0x7FFFFFFF)


def _dsa_kernel(iq_ref, iw_ref, ik_ref, q_ref, k_ref, v_ref, o_ref, key_sc, wb_sc, m_sc, acc_sc,
                *, tq, tk, n_sel):
    i = pl.program_id(0)
    n_chunks = ((i + 1) * tq + tk - 1) // tk
    rep = DSA_HEADS // DSA_KV_HEADS
    lane_tiles = tk // LANES

    iq_all = iq_ref[...].reshape(IDX_HEADS * tq, IDX_DIM)
    iw = iw_ref[...]
    for h in range(IDX_HEADS):
        wb_sc[h] = jnp.broadcast_to(iw[:, h:h + 1], (tq, LANES))

    def index_chunk(c, carry):
        d = lax.dot_general(iq_all, ik_ref[c], (((1,), (1,)), ((), ())), preferred_element_type=F32)
        d = jnp.maximum(d, 0.0).reshape(IDX_HEADS, tq, tk)
        score = _lane_tile(wb_sc[0], lane_tiles) * d[0]
        for h in range(1, IDX_HEADS):
            score = score + _lane_tile(wb_sc[h], lane_tiles) * d[h]
        key_sc[c] = _sortable_key(score)
        return carry

    lax.fori_loop(0, n_chunks, index_chunk, 0)
    last = n_chunks - 1
    row = i * tq + lax.broadcasted_iota(jnp.int32, (tq, tk), 0)
    col = last * tk + lax.broadcasted_iota(jnp.int32, (tq, tk), 1)
    key_sc[last] = jnp.where(col <= row, key_sc[last], INT32_MIN)

    def search_bit(b, thr):
        cand = thr + lax.shift_left(jnp.int32(1), 31 - b)

        def count_chunk(c, cnt):
            key = key_sc[c]
            for j in range(lane_tiles):
                cnt = cnt + jnp.where(key[:, j * LANES:(j + 1) * LANES] >= cand, 1.0, 0.0)
            return cnt

        cnt = lax.fori_loop(0, n_chunks, count_chunk, jnp.zeros((tq, LANES), F32))
        return jnp.where(jnp.sum(cnt, axis=1)[:, None] >= n_sel, cand, thr)

    thr = lax.fori_loop(0, 32, search_bit, jnp.full((tq, LANES), INT32_MIN, jnp.int32))
    thr = _lane_tile(jnp.maximum(thr, INT32_MIN + 1), lane_tiles)

    m_sc[...] = jnp.full_like(m_sc, MASKED_SCORE)
    acc_sc[...] = jnp.zeros_like(acc_sc)
    ones = jnp.ones((tk, LANES), BF16)

    def attend_chunk(c, carry):
        bias = jnp.where(key_sc[c] >= thr, 0.0, MASKED_SCORE)
        kc = k_ref[c]
        vc = v_ref[c]
        for g in range(DSA_KV_HEADS):
            cols = slice(g * DSA_HEAD_DIM, (g + 1) * DSA_HEAD_DIM)
            qg = q_ref[g * rep:(g + 1) * rep].reshape(rep * tq, DSA_HEAD_DIM)
            s = lax.dot_general(qg, kc[:, cols], (((1,), (1,)), ((), ())), preferred_element_type=F32)
            s = (s.reshape(rep, tq, tk) + bias[None]).reshape(rep * tq, tk)
            _softmax_step(s, jnp.concatenate([vc[:, cols], ones], axis=1), m_sc.at[g], acc_sc.at[g])
        return carry

    lax.fori_loop(0, n_chunks, attend_chunk, 0)
    for g in range(DSA_KV_HEADS):
        acc = acc_sc[g]
        out = acc[:, :LANES] / acc[:, LANES:]
        for r in range(rep):
            hh = g * rep + r
            o_ref[:, hh * DSA_HEAD_DIM:(hh + 1) * DSA_HEAD_DIM] = out[r * tq:(r + 1) * tq].astype(o_ref.dtype)


def dsa_attention(iq, iw, ik, q, k, v, tq=128, tk=512):
    s = ik.shape[0]
    n_sel = min(IDX_TOPK_MAX, s // 4)
    nc = s // tk
    assert tk >= n_sel and s % tk == 0 and tk % tq == 0 and DSA_HEAD_DIM == LANES
    rep = DSA_HEADS // DSA_KV_HEADS
    kvw = DSA_KV_HEADS * DSA_HEAD_DIM
    whole3 = lambda i: (0, 0, 0)
    return pl.pallas_call(
        functools.partial(_dsa_kernel, tq=tq, tk=tk, n_sel=n_sel),
        grid=(s // tq,),
        in_specs=[
            pl.BlockSpec((IDX_HEADS, tq, IDX_DIM), lambda i: (0, i, 0)),
            pl.BlockSpec((tq, IDX_HEADS), lambda i: (i, 0)),
            pl.BlockSpec((nc, tk, IDX_DIM), whole3),
            pl.BlockSpec((DSA_HEADS, tq, DSA_HEAD_DIM), lambda i: (0, i, 0)),
            pl.BlockSpec((nc, tk, kvw), whole3),
            pl.BlockSpec((nc, tk, kvw), whole3),
        ],
        out_specs=pl.BlockSpec((tq, DSA_HEADS * DSA_HEAD_DIM), lambda i: (i, 0)),
        out_shape=jax.ShapeDtypeStruct((s, DSA_HEADS * DSA_HEAD_DIM), BF16),
        scratch_shapes=[
            pltpu.VMEM((nc, tq, tk), jnp.int32),
            pltpu.VMEM((IDX_HEADS, tq, LANES), F32),
            pltpu.VMEM((DSA_KV_HEADS, rep * tq, LANES), F32),
            pltpu.VMEM((DSA_KV_HEADS, rep * tq, 2 * LANES), F32),
        ],
        compiler_params=_params("parallel"),
        name="dsa_attention",
    )(iq, iw, ik.reshape(nc, tk, IDX_DIM), q, k.reshape(nc, tk, kvw), v.reshape(nc, tk, kvw))


def _retention_tables():
    h, c = RET_HEADS, RET_CHUNK
    log_g = np.log(1.0 - 2.0 ** (-5.0 - np.arange(h, dtype=np.float32))).astype(np.float32).astype(np.float64)
    idx = np.arange(c, dtype=np.float64)
    diff = idx[:, None] - idx[None, :]
    decay_in = np.where(diff[None] >= 0, np.exp(np.maximum(diff, 0.0)[None] * log_g[:, None, None]), 0.0)
    xi = np.exp((idx + 1.0)[None, :] * log_g[:, None])
    zeta = np.exp((c - 1.0 - idx)[None, :] * log_g[:, None])
    chunk_decay = np.exp(c * log_g)
    lane = np.ones((1, 1, LANES))
    return (decay_in.astype(np.float32), (xi[:, :, None] * lane).astype(np.float32),
            (zeta[:, :, None] * lane).astype(np.float32),
            (chunk_decay[:, None, None] * np.ones((1, 8, LANES))).astype(np.float32))


def _retention_kernel(q_ref, k_ref, v_ref, g_ref, cos_ref, sin_ref, din_ref, xi_ref, zeta_ref, cd_ref,
                      gng_ref, gnb_ref, o_ref, r_sc):
    n = pl.program_id(1)
    half = RET_QK_DIM // 2

    @pl.when(n == 0)
    def _():
        r_sc[...] = jnp.zeros_like(r_sc)

    cos = cos_ref[...]
    sin = sin_ref[...]

    def rope(x):
        x1 = x[:, :half]
        x2 = x[:, half:]
        return x1 * cos - x2 * sin, x1 * sin + x2 * cos

    q1, q2 = rope(q_ref[...])
    k1, k2 = rope(k_ref[...])
    scale = RET_QK_DIM ** -0.5
    qr = jnp.concatenate([q1, q2], axis=1).astype(BF16)
    kr = jnp.concatenate([k1 * scale, k2 * scale], axis=1)
    v = v_ref[...].astype(BF16)

    inner = lax.dot_general(qr, kr.astype(BF16), (((1,), (1,)), ((), ())), preferred_element_type=F32)
    inner = inner * din_ref[0]
    r = r_sc[...]
    cross = jnp.dot(qr, r.astype(BF16), preferred_element_type=F32)
    xi = xi_ref[0]
    o = jnp.dot(inner.astype(BF16), v, preferred_element_type=F32)
    o = o + cross * jnp.concatenate([xi] * (RET_V_DIM // LANES), axis=1)

    zeta = zeta_ref[0]
    kz = (kr * jnp.concatenate([zeta] * (RET_QK_DIM // LANES), axis=1)).astype(BF16)
    upd = lax.dot_general(kz, v, (((0,), (0,)), ((), ())), preferred_element_type=F32)
    r_sc[...] = cd_ref[0][:1, :1] * r + upd

    mu = jnp.mean(o, axis=-1, keepdims=True)
    oc = o - mu
    var = jnp.mean(oc * oc, axis=-1, keepdims=True)
    y = oc * lax.rsqrt(var + 1e-5) * gng_ref[...] + gnb_ref[...]
    gate = g_ref[...]
    o_ref[...] = (gate * jax.nn.sigmoid(gate) * y).astype(o_ref.dtype)


def retention(h1, cos, sin, gn_g, gn_b):
    s = h1.shape[0]
    hh, dk, dv, c = RET_HEADS, RET_QK_DIM, RET_V_DIM, RET_CHUNK
    din, xi, zeta, cd = (jnp.asarray(t) for t in _retention_tables())
    k_blk0 = hh * dk // dk
    v_blk0 = 2 * hh * dk // dv
    g_blk0 = v_blk0 + hh
    per_head = lambda r, w: pl.BlockSpec((1, r, w), lambda h, n: (h, 0, 0))
    return pl.pallas_call(
        _retention_kernel,
        grid=(hh, s // c),
        in_specs=[
            pl.BlockSpec((c, dk), lambda h, n: (n, h)),
            pl.BlockSpec((c, dk), lambda h, n: (n, k_blk0 + h)),
            pl.BlockSpec((c, dv), lambda h, n: (n, v_blk0 + h)),
            pl.BlockSpec((c, dv), lambda h, n: (n, g_blk0 + h)),
            pl.BlockSpec((c, dk // 2), lambda h, n: (n, 0)),
            pl.BlockSpec((c, dk // 2), lambda h, n: (n, 0)),
            per_head(c, c), per_head(c, LANES), per_head(c, LANES), per_head(8, LANES),
            pl.BlockSpec((1, dv), lambda h, n: (0, h)),
            pl.BlockSpec((1, dv), lambda h, n: (0, h)),
        ],
        out_specs=pl.BlockSpec((c, dv), lambda h, n: (n, h)),
        out_shape=jax.ShapeDtypeStruct((s, hh * dv), BF16),
        scratch_shapes=[pltpu.VMEM((dk, dv), F32)],
        compiler_params=_params("parallel", "arbitrary"),
        name="retention",
    )(h1, h1, h1, h1, cos, sin, din, xi, zeta, cd, gn_g.reshape(1, -1), gn_b.reshape(1, -1))


def _first_argmax(v, idx, n):
    m = jnp.max(v, axis=0, keepdims=True)
    first = jnp.min(jnp.where(v == m, idx, n), axis=0, keepdims=True)
    return m, first


def _router_kernel(x_ref, rwt_ref, rb_ref, e_ref, g_ref):
    tm = x_ref.shape[0]
    epg = EXPERTS_PER_GROUP
    logits = lax.dot_general(rwt_ref[...], x_ref[...], (((1,), (1,)), ((), ())),
                             precision=lax.Precision.HIGHEST, preferred_element_type=F32)
    scores = jax.nn.sigmoid(logits)
    biased = scores + rb_ref[...]
    idx = lax.broadcasted_iota(jnp.int32, (epg, tm), 0)

    group_scores = []
    for g in range(N_GROUPS):
        v = biased[g * epg:(g + 1) * epg]
        m1, first = _first_argmax(v, idx, epg)
        m2 = jnp.max(jnp.where(idx == first, -jnp.inf, v), axis=0, keepdims=True)
        group_scores.append(m1 + m2)
    gmax = group_scores[0]
    for g in range(1, N_GROUPS):
        gmax = jnp.maximum(gmax, group_scores[g])
    gsel = jnp.full((1, tm), N_GROUPS, jnp.int32)
    for g in range(N_GROUPS - 1, -1, -1):
        gsel = jnp.where(group_scores[g] == gmax, g, gsel)

    in_biased = jnp.zeros((epg, tm), F32)
    in_scores = jnp.zeros((epg, tm), F32)
    for g in range(N_GROUPS):
        pick = gsel == g
        in_biased = jnp.where(pick, biased[g * epg:(g + 1) * epg], in_biased)
        in_scores = jnp.where(pick, scores[g * epg:(g + 1) * epg], in_scores)
    _, loc1 = _first_argmax(in_biased, idx, epg)
    _, loc2 = _first_argmax(jnp.where(idx == loc1, -jnp.inf, in_biased), idx, epg)
    s1 = jnp.sum(jnp.where(idx == loc1, in_scores, 0.0), axis=0, keepdims=True)
    s2 = jnp.sum(jnp.where(idx == loc2, in_scores, 0.0), axis=0, keepdims=True)
    denom = s1 + s2
    e_ref[...] = jnp.concatenate([gsel * epg + loc1, gsel * epg + loc2], axis=0)
    g_ref[...] = jnp.concatenate([s1 / denom, s2 / denom], axis=0)


def moe_router(x, router_w, router_b, tm=512):
    s, d = x.shape
    return pl.pallas_call(
        _router_kernel,
        grid=(s // tm,),
        in_specs=[
            pl.BlockSpec((tm, d), lambda i: (i, 0)),
            pl.BlockSpec((N_EXPERTS, d), lambda i: (0, 0)),
            pl.BlockSpec((N_EXPERTS, 1), lambda i: (0, 0)),
        ],
        out_specs=[pl.BlockSpec((TOP_K, tm), lambda i: (0, i)), pl.BlockSpec((TOP_K, tm), lambda i: (0, i))],
        out_shape=[jax.ShapeDtypeStruct((TOP_K, s), jnp.int32), jax.ShapeDtypeStruct((TOP_K, s), F32)],
        compiler_params=_params("parallel"),
        name="moe_router",
    )(x, router_w.T, router_b.reshape(N_EXPERTS, 1))


def _expert_kernel(te_ref, na_ref, tok_ref, tok_next_ref, x_hbm, wg_ref, wu_ref, wd_ref, o_ref,
                   wg_sc, wu_sc, wd_sc, xbuf, gsem, *, tm):
    j = pl.program_id(0)
    n_active = na_ref[0]
    active = j < n_active
    slot = j % 2
    new_expert = (j == 0) | (te_ref[j] != te_ref[jnp.maximum(j - 1, 0)])

    def start_gather(idx_ref, dst_slot):
        def start(i, carry):
            for u in range(DMA_ISSUE_UNROLL):
                r = i * DMA_ISSUE_UNROLL + u
                pltpu.make_async_copy(x_hbm.at[pl.ds(idx_ref[0, 0, r], 1)], xbuf.at[dst_slot, pl.ds(r, 1)],
                                      gsem.at[dst_slot]).start(priority=u % 2)
            return carry

        lax.fori_loop(0, tm // DMA_ISSUE_UNROLL, start, 0)

    @pl.when(j == 0)
    def _():
        start_gather(tok_ref, 0)

    @pl.when(j + 1 < n_active)
    def _():
        start_gather(tok_next_ref, 1 - slot)

    @pl.when(active & new_expert)
    def _():
        wg_sc[...] = wg_ref[0].astype(BF16)
        wu_sc[...] = wu_ref[0].astype(BF16)
        wd_sc[...] = wd_ref[0].astype(BF16)

    @pl.when(active)
    def _():
        pltpu.make_async_copy(x_hbm.at[pl.ds(0, tm)], xbuf.at[slot], gsem.at[slot]).wait()
        x = xbuf[slot].astype(BF16)
        hg = jnp.dot(x, wg_sc[...], preferred_element_type=F32)
        hu = jnp.dot(x, wu_sc[...], preferred_element_type=F32)
        h = hg * jax.nn.sigmoid(hg) * hu
        o_ref[...] = jnp.dot(h.astype(BF16), wd_sc[...], preferred_element_type=F32)

    @pl.when(jnp.logical_not(active))
    def _():
        o_ref[...] = jnp.zeros_like(o_ref)


def routed_experts(x, row_token, tile_expert, n_active_tiles, w_gate, w_up, w_down, tm):
    p = row_token.shape[0]
    d = x.shape[1]
    f = w_gate.shape[2]
    nt = p // tm

    def tok_blk(j, te, na):
        return (jnp.minimum(j, na[0] - 1), 0, 0)

    def tok_next_blk(j, te, na):
        return (jnp.minimum(j + 1, na[0] - 1), 0, 0)

    tokens = row_token.reshape(nt, 1, tm)
    return pl.pallas_call(
        functools.partial(_expert_kernel, tm=tm),
        grid_spec=pltpu.PrefetchScalarGridSpec(
            num_scalar_prefetch=2,
            grid=(nt,),
            in_specs=[
                pl.BlockSpec((1, 1, tm), tok_blk, memory_space=pltpu.SMEM),
                pl.BlockSpec((1, 1, tm), tok_next_blk, memory_space=pltpu.SMEM),
                pl.BlockSpec(memory_space=pl.ANY),
                pl.BlockSpec((1, d, f), lambda j, te, na: (te[j], 0, 0)),
                pl.BlockSpec((1, d, f), lambda j, te, na: (te[j], 0, 0)),
                pl.BlockSpec((1, f, d), lambda j, te, na: (te[j], 0, 0)),
            ],
            out_specs=pl.BlockSpec((tm, d), lambda j, te, na: (j, 0)),
            scratch_shapes=[pltpu.VMEM((d, f), BF16), pltpu.VMEM((d, f), BF16), pltpu.VMEM((f, d), BF16),
                            pltpu.VMEM((2, tm, d), x.dtype), pltpu.SemaphoreType.DMA((2,))],
        ),
        out_shape=jax.ShapeDtypeStruct((p, d), F32),
        compiler_params=_params("arbitrary"),
        name="routed_experts",
    )(tile_expert, n_active_tiles, tokens, tokens, x, w_gate, w_up, w_down)


def _shared_kernel(x_ref, wg_ref, wu_ref, wd_ref, o_ref):
    x = x_ref[...]
    hg = jnp.dot(x, wg_ref[...], preferred_element_type=F32)
    hu = jnp.dot(x, wu_ref[...], preferred_element_type=F32)
    h = hg * jax.nn.sigmoid(hg) * hu
    o_ref[...] = jnp.dot(h.astype(BF16), wd_ref[...], preferred_element_type=F32)


def shared_expert(x_bf, wg, wu, wd, tm=512):
    s, d = x_bf.shape
    f = wg.shape[1]
    whole = lambda r, c: pl.BlockSpec((r, c), lambda i: (0, 0))
    return pl.pallas_call(
        _shared_kernel,
        grid=(s // tm,),
        in_specs=[pl.BlockSpec((tm, d), lambda i: (i, 0)), whole(d, f), whole(d, f), whole(f, d)],
        out_specs=pl.BlockSpec((tm, d), lambda i: (i, 0)),
        out_shape=jax.ShapeDtypeStruct((s, d), F32),
        compiler_params=_params("parallel"),
        name="shared_expert",
    )(x_bf, wg.astype(BF16), wu.astype(BF16), wd.astype(BF16))


def _combine_kernel(pos_ref, x_ref, sh_ref, gate_ref, ys_ref, g_ref, b_ref, o_ref, obf_ref, buf, sem, *, tm):
    def start(i, carry):
        for u in range(DMA_ISSUE_UNROLL):
            r = i * DMA_ISSUE_UNROLL + u
            for k in range(TOP_K):
                pltpu.make_async_copy(ys_ref.at[pl.ds(pos_ref[0, k, r], 1)], buf.at[k, pl.ds(r, 1)],
                                      sem).start(priority=k % 2)
        return carry

    lax.fori_loop(0, tm // DMA_ISSUE_UNROLL, start, 0)
    for k in range(TOP_K):
        pltpu.make_async_copy(ys_ref.at[pl.ds(0, tm)], buf.at[k], sem).wait()
    gate = gate_ref[...]
    routed = gate[:, 0:1] * buf[0] + gate[:, 1:2] * buf[1]
    z = DEEPNORM_ALPHA * x_ref[...] + (routed + sh_ref[...])
    out = _layer_norm(z, g_ref[...], b_ref[...])
    o_ref[...] = out
    obf_ref[...] = out.astype(BF16)


def moe_combine(x, shared, gates, pos, ys, g, b, tm=256):
    s, d = x.shape
    nt = s // tm
    row = pl.BlockSpec((tm, d), lambda i: (i, 0))
    vec = pl.BlockSpec((1, d), lambda i: (0, 0))
    return pl.pallas_call(
        functools.partial(_combine_kernel, tm=tm),
        grid=(nt,),
        in_specs=[
            pl.BlockSpec((1, TOP_K, tm), lambda i: (i, 0, 0), memory_space=pltpu.SMEM),
            row, row,
            pl.BlockSpec((tm, TOP_K), lambda i: (i, 0)),
            pl.BlockSpec(memory_space=pl.ANY),
            vec, vec,
        ],
        out_specs=[row, row],
        out_shape=[jax.ShapeDtypeStruct((s, d), F32), jax.ShapeDtypeStruct((s, d), BF16)],
        scratch_shapes=[pltpu.VMEM((TOP_K, tm, d), F32), pltpu.SemaphoreType.DMA(())],
        compiler_params=_params("arbitrary"),
        name="moe_combine",
    )(pos.reshape(TOP_K, nt, tm).transpose(1, 0, 2), x, shared, gates.T, ys, g.reshape(1, d), b.reshape(1, d))


def _rope_tables(pos, rot_dim, theta):
    half = rot_dim // 2
    inv = (1.0 / (theta ** (np.arange(half, dtype=np.float32) / half))).astype(np.float32)
    ang = pos.astype(F32)[:, None] * inv
    return jnp.cos(ang), jnp.sin(ang)


def _rope(x, cos, sin, rot_dim):
    half = rot_dim // 2
    x1 = x[..., :half]
    x2 = x[..., half:rot_dim]
    c = cos[:, None, :]
    s = sin[:, None, :]
    return jnp.concatenate([x1 * c - x2 * s, x1 * s + x2 * c, x[..., rot_dim:]], axis=-1)


def _routing_plan(experts, tm):
    k, s = experts.shape
    flat = experts.reshape(-1)
    onehot = (flat[:, None] == jnp.arange(N_EXPERTS, dtype=jnp.int32)[None, :]).astype(jnp.int32)
    csum = jnp.cumsum(onehot, axis=0)
    rank = jnp.take_along_axis(csum, flat[:, None], axis=1)[:, 0] - 1
    sizes = csum[-1]
    tiles = (sizes + tm - 1) // tm
    tile_end = jnp.cumsum(tiles)
    start = (tile_end - tiles) * tm
    pos = start[flat] + rank
    n_rows = k * s + N_EXPERTS * tm
    n_tiles = n_rows // tm
    row_token = jnp.zeros((n_rows,), jnp.int32).at[pos].set(jnp.tile(jnp.arange(s, dtype=jnp.int32), k))
    n_active = tile_end[-1]
    tile_ids = jnp.minimum(jnp.arange(n_tiles, dtype=jnp.int32), n_active - 1)
    tile_expert = jnp.searchsorted(tile_end, tile_ids, side="right").astype(jnp.int32)
    return pos.reshape(k, s), row_token, tile_expert, n_active.reshape(1).astype(jnp.int32)


def _grouped_moe(x, x_bf, router_w, router_b, w_gate, w_up, w_down, sh_gate, sh_up, sh_down, ln_g, ln_b,
                 expert_tm=256):
    experts, gates = moe_router(x, router_w, router_b)
    pos, row_token, tile_expert, n_active = _routing_plan(experts, expert_tm)
    ys = routed_experts(x, row_token, tile_expert, n_active, w_gate, w_up, w_down, expert_tm)
    shared = shared_expert(x_bf, sh_gate, sh_up, sh_down)
    return moe_combine(x, shared, gates, pos, ys, ln_g, ln_b)


def _mla_dsa_mixer(x, pos, w_in, q_norm, w_uq, kv_norm, w_ukv, w_out):
    s = x.shape[0]
    n_pad = (-IN_AB) % 512
    h0 = matmul(x, jnp.pad(w_in, ((0, 0), (0, n_pad))), F32, 1024, 512, name="l0_in_proj")
    cq, ckv, krope, dq, dkv, iq, ik, iw = jnp.split(h0[:, :IN_AB], [int(c) for c in np.cumsum(SPLITS_AB)[:-1]], axis=-1)
    cos_m, sin_m = _rope_tables(pos, MLA_ROPE, ROPE_THETA)
    cos_d, sin_d = _rope_tables(pos, DSA_ROT, ROPE_THETA)
    cos_i, sin_i = _rope_tables(pos, IDX_ROT, ROPE_THETA)

    dqk = MLA_NOPE + MLA_ROPE
    qa = matmul(h0, w_uq, F32, 1024, 512, name="mla_q_up", a_cols=(0, MLA_Q_LORA),
                norm_gain=q_norm).reshape(s, MLA_HEADS, dqk)
    kva = matmul(h0, w_ukv, F32, 1024, 512, name="mla_kv_up", a_cols=(MLA_Q_LORA // MLA_KV_LORA, MLA_KV_LORA),
                 norm_gain=kv_norm).reshape(s, MLA_HEADS, MLA_NOPE + MLA_V)
    q_pe = _rope(qa[..., MLA_NOPE:], cos_m, sin_m, MLA_ROPE)
    q_cat = jnp.concatenate([qa[..., :MLA_NOPE], q_pe, jnp.zeros((s, MLA_HEADS, 2 * LANES - dqk), F32)], axis=-1)
    q_cat = (q_cat * (dqk ** -0.5 * LOG2_E)).astype(BF16).reshape(s, MLA_HEADS * 2 * LANES)
    k_pe = _rope(krope[:, None, :], cos_m, sin_m, MLA_ROPE)[:, 0]
    k_pe = jnp.concatenate([k_pe, jnp.zeros((s, LANES - MLA_ROPE), F32)], axis=-1).astype(BF16)
    k_nope = kva[..., :MLA_NOPE].astype(BF16).reshape(s, MLA_HEADS * MLA_NOPE)
    v_a = kva[..., MLA_NOPE:].astype(BF16).reshape(s, MLA_HEADS * MLA_V)
    out_a = mla_attention(q_cat, k_nope, k_pe, v_a)

    qb = _rope(dq.reshape(s, DSA_HEADS, DSA_HEAD_DIM), cos_d, sin_d, DSA_ROT) * (DSA_HEAD_DIM ** -0.5 * LOG2_E)
    qb = qb.astype(BF16).transpose(1, 0, 2)
    kvb = dkv.reshape(s, 2, DSA_KV_HEADS, DSA_HEAD_DIM)
    kb = _rope(kvb[:, 0], cos_d, sin_d, DSA_ROT).astype(BF16).reshape(s, DSA_KV_HEADS * DSA_HEAD_DIM)
    vb = kvb[:, 1].astype(BF16).reshape(s, DSA_KV_HEADS * DSA_HEAD_DIM)
    iq = _rope(iq.reshape(s, IDX_HEADS, IDX_DIM), cos_i, sin_i, IDX_ROT).astype(BF16).transpose(1, 0, 2)
    ik = _rope(ik[:, None, :], cos_i, sin_i, IDX_ROT)[:, 0].astype(BF16)
    iw = iw * (IDX_HEADS ** -0.5 * IDX_DIM ** -0.5)
    out_b = dsa_attention(iq, iw, ik, qb, kb, vb)

    return matmul(jnp.concatenate([out_a, out_b], axis=-1), w_out, F32, 1024, 512, name="l0_out_proj")


def _retention_mixer(x_bf, pos, w_in, gn_g, gn_b, w_out):
    h1 = matmul(x_bf, w_in, F32, 1024, 1024, name="l1_in_proj")
    cos, sin = _rope_tables(pos, RET_QK_DIM, RET_THETA)
    y = retention(h1, cos, sin, gn_g, gn_b)
    return matmul(y, w_out, F32, 1024, 512, name="l1_out_proj")


def kernel(x, positions, router_w, router_b, l0_w_in, l0_mla_q_norm, l0_mla_w_uq, l0_mla_kv_norm, l0_mla_w_ukv, l0_w_out, l1_w_in, l1_ret_gn_g, l1_ret_gn_b, l1_w_out, l0_ln_mix_g, l0_ln_mix_b, l0_moe_w_gate, l0_moe_w_up, l0_moe_w_down, l0_sh_gate, l0_sh_up, l0_sh_down, l0_ln_ffn_g, l0_ln_ffn_b, l1_ln_mix_g, l1_ln_mix_b, l1_moe_w_gate, l1_moe_w_up, l1_moe_w_down, l1_sh_gate, l1_sh_up, l1_sh_down, l1_ln_ffn_g, l1_ln_ffn_b):
    assert x.shape[0] == 1
    xt = x[0]
    pos = positions[0]

    mix = _mla_dsa_mixer(xt, pos, l0_w_in, l0_mla_q_norm, l0_mla_w_uq, l0_mla_kv_norm, l0_mla_w_ukv, l0_w_out)
    xt, xt_bf = add_layer_norm(xt, mix, l0_ln_mix_g, l0_ln_mix_b)
    xt, xt_bf = _grouped_moe(xt, xt_bf, router_w, router_b, l0_moe_w_gate, l0_moe_w_up, l0_moe_w_down,
                             l0_sh_gate, l0_sh_up, l0_sh_down, l0_ln_ffn_g, l0_ln_ffn_b)

    mix = _retention_mixer(xt_bf, pos, l1_w_in, l1_ret_gn_g, l1_ret_gn_b, l1_w_out)
    xt, xt_bf = add_layer_norm(xt, mix, l1_ln_mix_g, l1_ln_mix_b)
    xt, _ = _grouped_moe(xt, xt_bf, router_w, router_b, l1_moe_w_gate, l1_moe_w_up, l1_moe_w_down,
                         l1_sh_gate, l1_sh_up, l1_sh_down, l1_ln_ffn_g, l1_ln_ffn_b)
    return xt[None]
```

```python
import functools

import numpy as np
import jax
import jax.numpy as jnp
from jax import lax
from jax.experimental import pallas as pl
from jax.experimental.pallas import tpu as pltpu

F32 = jnp.float32
BF16 = jnp.bfloat16

D_MODEL = 2048
DEPTH = 2
ROPE_THETA = 500000.0
MLA_HEADS = 8
MLA_Q_LORA = 512
MLA_KV_LORA = 256
MLA_NOPE = 128
MLA_ROPE = 64
MLA_V = 128
DSA_HEADS = 8
DSA_KV_HEADS = 2
DSA_HEAD_DIM = 128
DSA_ROT = DSA_HEAD_DIM // 4
IDX_HEADS = 16
IDX_DIM = 64
IDX_ROT = IDX_DIM // 4
IDX_TOPK_MAX = 256
RET_HEADS = 8
RET_QK_DIM = 256
RET_V_DIM = 512
RET_CHUNK = 128
RET_THETA = 10000.0
N_EXPERTS = 64
N_GROUPS = 8
EXPERTS_PER_GROUP = N_EXPERTS // N_GROUPS
TOP_K = 2
D_EXPERT = 512
D_SHARED = 1024
DEEPNORM_ALPHA = (2.0 * DEPTH) ** 0.25

SPLITS_AB = (MLA_Q_LORA, MLA_KV_LORA, MLA_ROPE, DSA_HEADS * DSA_HEAD_DIM,
             2 * DSA_KV_HEADS * DSA_HEAD_DIM, IDX_HEADS * IDX_DIM, IDX_DIM, IDX_HEADS)
IN_AB = sum(SPLITS_AB)

VMEM_LIMIT_BYTES = 56 * 1024 * 1024
LANES = 128
MASKED_SCORE = -1e30
LOG2_E = 1.4426950408889634
DMA_ISSUE_UNROLL = 8
INT32_MIN = -2 ** 31


def _params(*sem):
    return pltpu.CompilerParams(dimension_semantics=sem, vmem_limit_bytes=VMEM_LIMIT_BYTES)


def _mm_kernel(*refs, norm_eps):
    if norm_eps is None:
        a_ref, b_ref, o_ref = refs
        a = a_ref[...]
    else:
        a_ref, g_ref, b_ref, o_ref = refs
        af = a_ref[...].astype(F32)
        a = af * lax.rsqrt(jnp.mean(af * af, axis=-1, keepdims=True) + norm_eps) * g_ref[...]
    o_ref[...] = jnp.dot(a.astype(BF16), b_ref[...].astype(BF16),
                         preferred_element_type=F32).astype(o_ref.dtype)


def matmul(a, b, out_dtype, tm, tn, *, name, a_cols=None, norm_gain=None, norm_eps=1e-6):
    m = a.shape[0]
    k, n = b.shape
    col_blk = 0 if a_cols is None else a_cols[0]
    assert (a.shape[1] == k) if a_cols is None else (a_cols[1] == k)
    assert m % tm == 0 and n % tn == 0
    in_specs = [pl.BlockSpec((tm, k), lambda i, j: (i, col_blk))]
    args = [a]
    if norm_gain is not None:
        in_specs.append(pl.BlockSpec((1, k), lambda i, j: (0, 0)))
        args.append(norm_gain.reshape(1, k))
    in_specs.append(pl.BlockSpec((k, tn), lambda i, j: (0, j)))
    args.append(b)
    return pl.pallas_call(
        functools.partial(_mm_kernel, norm_eps=None if norm_gain is None else norm_eps),
        grid=(m // tm, n // tn),
        in_specs=in_specs,
        out_specs=pl.BlockSpec((tm, tn), lambda i, j: (i, j)),
        out_shape=jax.ShapeDtypeStruct((m, n), out_dtype),
        compiler_params=_params("parallel", "parallel"),
        name=name,
    )(*args)


def _layer_norm(z, g, b):
    mu = jnp.mean(z, axis=-1, keepdims=True)
    zc = z - mu
    var = jnp.mean(zc * zc, axis=-1, keepdims=True)
    return zc * lax.rsqrt(var + 1e-5) * g + b


def _store_row_tiles(ref, value):
    for j in range(ref.shape[1]):
        ref[:, j, :] = value[:, j * LANES:(j + 1) * LANES]


def _load_row_tiles(ref):
    return jnp.concatenate([ref[:, j, :] for j in range(ref.shape[1])], axis=1)


def _add_ln_kernel(x_ref, y_ref, g_ref, b_ref, o_ref, obf_ref, orow_ref):
    z = DEEPNORM_ALPHA * x_ref[...] + y_ref[...]
    out = _layer_norm(z, g_ref[...], b_ref[...])
    o_ref[...] = out
    obf_ref[...] = out.astype(BF16)
    _store_row_tiles(orow_ref, out)


def add_layer_norm(x, y, g, b, tm=256):
    s, d = x.shape
    row = pl.BlockSpec((tm, d), lambda i: (i, 0))
    vec = pl.BlockSpec((1, d), lambda i: (0, 0))
    return pl.pallas_call(
        _add_ln_kernel,
        grid=(s // tm,),
        in_specs=[row, row, vec, vec],
        out_specs=[row, row, pl.BlockSpec((tm, d // LANES, LANES), lambda i: (i, 0, 0))],
        out_shape=[jax.ShapeDtypeStruct((s, d), F32), jax.ShapeDtypeStruct((s, d), BF16),
                   jax.ShapeDtypeStruct((s, d // LANES, LANES), F32)],
        compiler_params=_params("parallel"),
        name="add_layer_norm",
    )(x, y, g.reshape(1, d), b.reshape(1, d))


def _lane_tile(x, n):
    return jnp.tile(x, (1, n))


def _softmax_step(s, v_ext, m_ref, acc_ref):
    tk = s.shape[1]
    m_prev = m_ref[...]
    m_new = jnp.maximum(m_prev, jnp.max(s, axis=1)[:, None])
    p = jnp.exp2(s - _lane_tile(m_new, tk // LANES))
    alpha = jnp.exp2(m_prev - m_new)
    pv = jnp.dot(p.astype(BF16), v_ext, preferred_element_type=F32)
    acc_ref[...] = _lane_tile(alpha, 2) * acc_ref[...] + pv
    m_ref[...] = m_new


def _mla_kernel(q_ref, kn_ref, kp_ref, v_ref, o_ref, m_sc, acc_sc, *, tq, tk, hp):
    qi = pl.program_id(1)
    m_sc[...] = jnp.full_like(m_sc, MASKED_SCORE)
    acc_sc[...] = jnp.zeros_like(acc_sc)
    ones = jnp.ones((tk, LANES), BF16)
    per_block = tq // tk

    def attend(c, diagonal_offset):
        kp = kp_ref[c]
        kn = kn_ref[c]
        v = v_ref[c]
        for j in range(hp):
            q = q_ref[:, j * 2 * LANES:(j + 1) * 2 * LANES]
            k = jnp.concatenate([kn[:, j * LANES:(j + 1) * LANES], kp], axis=1)
            s = lax.dot_general(q, k, (((1,), (1,)), ((), ())), preferred_element_type=F32)
            if diagonal_offset is not None:
                row = lax.broadcasted_iota(jnp.int32, (tq, tk), 0)
                col = diagonal_offset * tk + lax.broadcasted_iota(jnp.int32, (tq, tk), 1)
                s = jnp.where(col <= row, s, MASKED_SCORE)
            v_ext = jnp.concatenate([v[:, j * LANES:(j + 1) * LANES], ones], axis=1)
            _softmax_step(s, v_ext, m_sc.at[j], acc_sc.at[j])

    def below_diagonal(c, carry):
        attend(c, None)
        return carry

    lax.fori_loop(0, qi * per_block, below_diagonal, 0)
    for d in range(per_block):
        attend(qi * per_block + d, d)
    for j in range(hp):
        acc = acc_sc[j]
        o_ref[:, j * LANES:(j + 1) * LANES] = (acc[:, :LANES] / acc[:, LANES:]).astype(o_ref.dtype)


def mla_attention(q, k_nope, k_pe, v, tq=512, tk=512, hp=2):
    s = q.shape[0]
    h = MLA_HEADS
    nc = s // tk
    assert tq % tk == 0 and h % hp == 0
    per_head = pl.BlockSpec((nc, tk, hp * LANES), lambda hh, qi: (0, 0, hh))
    return pl.pallas_call(
        functools.partial(_mla_kernel, tq=tq, tk=tk, hp=hp),
        grid=(h // hp, s // tq),
        in_specs=[
            pl.BlockSpec((tq, hp * 2 * LANES), lambda hh, qi: (qi, hh)),
            per_head,
            pl.BlockSpec((nc, tk, LANES), lambda hh, qi: (0, 0, 0)),
            per_head,
        ],
        out_specs=pl.BlockSpec((tq, hp * LANES), lambda hh, qi: (qi, hh)),
        out_shape=jax.ShapeDtypeStruct((s, h * MLA_V), BF16),
        scratch_shapes=[pltpu.VMEM((hp, tq, LANES), F32), pltpu.VMEM((hp, tq, 2 * LANES), F32)],
        compiler_params=_params("parallel", "arbitrary"),
        name="mla_attention",
    )(q, k_nope.reshape(nc, tk, h * MLA_NOPE), k_pe.reshape(nc, tk, LANES), v.reshape(nc, tk, h * MLA_V))


def _sortable_key(x):
    bits = pltpu.bitcast(x, jnp.int32)
    return bits ^ ((bits >> 31) & 0x7FFFFFFF)


def _dsa_kernel(iq_ref, iw_ref, ik_ref, q_ref, k_ref, v_ref, o_ref, key_sc, wb_sc, m_sc, acc_sc,
                *, tq, tk, n_sel):
    i = pl.program_id(0)
    n_chunks = ((i + 1) * tq + tk - 1) // tk
    rep = DSA_HEADS // DSA_KV_HEADS
    lane_tiles = tk // LANES

    iq_all = iq_ref[...].reshape(IDX_HEADS * tq, IDX_DIM)
    iw = iw_ref[...]
    for h in range(IDX_HEADS):
        wb_sc[h] = jnp.broadcast_to(iw[:, h:h + 1], (tq, LANES))

    def index_chunk(c, carry):
        d = lax.dot_general(iq_all, ik_ref[c], (((1,), (1,)), ((), ())), preferred_element_type=F32)
        d = jnp.maximum(d, 0.0).reshape(IDX_HEADS, tq, tk)
        score = _lane_tile(wb_sc[0], lane_tiles) * d[0]
        for h in range(1, IDX_HEADS):
            score = score + _lane_tile(wb_sc[h], lane_tiles) * d[h]
        key_sc[c] = _sortable_key(score)
        return carry

    lax.fori_loop(0, n_chunks, index_chunk, 0)
    last = n_chunks - 1
    row = i * tq + lax.broadcasted_iota(jnp.int32, (tq, tk), 0)
    col = last * tk + lax.broadcasted_iota(jnp.int32, (tq, tk), 1)
    key_sc[last] = jnp.where(col <= row, key_sc[last], INT32_MIN)

    def search_bit(b, thr):
        cand = thr + lax.shift_left(jnp.int32(1), 31 - b)

        def count_chunk(c, cnt):
            key = key_sc[c]
            for j in range(lane_tiles):
                cnt = cnt + jnp.where(key[:, j * LANES:(j + 1) * LANES] >= cand, 1.0, 0.0)
            return cnt

        cnt = lax.fori_loop(0, n_chunks, count_chunk, jnp.zeros((tq, LANES), F32))
        return jnp.where(jnp.sum(cnt, axis=1)[:, None] >= n_sel, cand, thr)

    thr = lax.fori_loop(0, 32, search_bit, jnp.full((tq, LANES), INT32_MIN, jnp.int32))
    thr = _lane_tile(jnp.maximum(thr, INT32_MIN + 1), lane_tiles)

    m_sc[...] = jnp.full_like(m_sc, MASKED_SCORE)
    acc_sc[...] = jnp.zeros_like(acc_sc)
    ones = jnp.ones((tk, LANES), BF16)

    def attend_chunk(c, carry):
        bias = jnp.where(key_sc[c] >= thr, 0.0, MASKED_SCORE)
        kc = k_ref[c]
        vc = v_ref[c]
        for g in range(DSA_KV_HEADS):
            cols = slice(g * DSA_HEAD_DIM, (g + 1) * DSA_HEAD_DIM)
            qg = q_ref[g * rep:(g + 1) * rep].reshape(rep * tq, DSA_HEAD_DIM)
            s = lax.dot_general(qg, kc[:, cols], (((1,), (1,)), ((), ())), preferred_element_type=F32)
            s = (s.reshape(rep, tq, tk) + bias[None]).reshape(rep * tq, tk)
            _softmax_step(s, jnp.concatenate([vc[:, cols], ones], axis=1), m_sc.at[g], acc_sc.at[g])
        return carry

    lax.fori_loop(0, n_chunks, attend_chunk, 0)
    for g in range(DSA_KV_HEADS):
        acc = acc_sc[g]
        out = acc[:, :LANES] / acc[:, LANES:]
        for r in range(rep):
            hh = g * rep + r
            o_ref[:, hh * DSA_HEAD_DIM:(hh + 1) * DSA_HEAD_DIM] = out[r * tq:(r + 1) * tq].astype(o_ref.dtype)


def dsa_attention(iq, iw, ik, q, k, v, tq=128, tk=512):
    s = ik.shape[0]
    n_sel = min(IDX_TOPK_MAX, s // 4)
    nc = s // tk
    assert tk >= n_sel and s % tk == 0 and tk % tq == 0 and DSA_HEAD_DIM == LANES
    rep = DSA_HEADS // DSA_KV_HEADS
    kvw = DSA_KV_HEADS * DSA_HEAD_DIM
    whole3 = lambda i: (0, 0, 0)
    return pl.pallas_call(
        functools.partial(_dsa_kernel, tq=tq, tk=tk, n_sel=n_sel),
        grid=(s // tq,),
        in_specs=[
            pl.BlockSpec((IDX_HEADS, tq, IDX_DIM), lambda i: (0, i, 0)),
            pl.BlockSpec((tq, IDX_HEADS), lambda i: (i, 0)),
            pl.BlockSpec((nc, tk, IDX_DIM), whole3),
            pl.BlockSpec((DSA_HEADS, tq, DSA_HEAD_DIM), lambda i: (0, i, 0)),
            pl.BlockSpec((nc, tk, kvw), whole3),
            pl.BlockSpec((nc, tk, kvw), whole3),
        ],
        out_specs=pl.BlockSpec((tq, DSA_HEADS * DSA_HEAD_DIM), lambda i: (i, 0)),
        out_shape=jax.ShapeDtypeStruct((s, DSA_HEADS * DSA_HEAD_DIM), BF16),
        scratch_shapes=[
            pltpu.VMEM((nc, tq, tk), jnp.int32),
            pltpu.VMEM((IDX_HEADS, tq, LANES), F32),
            pltpu.VMEM((DSA_KV_HEADS, rep * tq, LANES), F32),
            pltpu.VMEM((DSA_KV_HEADS, rep * tq, 2 * LANES), F32),
        ],
        compiler_params=_params("parallel"),
        name="dsa_attention",
    )(iq, iw, ik.reshape(nc, tk, IDX_DIM), q, k.reshape(nc, tk, kvw), v.reshape(nc, tk, kvw))


def _retention_tables():
    h, c = RET_HEADS, RET_CHUNK
    log_g = np.log(1.0 - 2.0 ** (-5.0 - np.arange(h, dtype=np.float32))).astype(np.float32).astype(np.float64)
    idx = np.arange(c, dtype=np.float64)
    diff = idx[:, None] - idx[None, :]
    decay_in = np.where(diff[None] >= 0, np.exp(np.maximum(diff, 0.0)[None] * log_g[:, None, None]), 0.0)
    xi = np.exp((idx + 1.0)[None, :] * log_g[:, None])
    zeta = np.exp((c - 1.0 - idx)[None, :] * log_g[:, None])
    chunk_decay = np.exp(c * log_g)
    lane = np.ones((1, 1, LANES))
    return (decay_in.astype(np.float32), (xi[:, :, None] * lane).astype(np.float32),
            (zeta[:, :, None] * lane).astype(np.float32),
            (chunk_decay[:, None, None] * np.ones((1, 8, LANES))).astype(np.float32))


def _retention_kernel(q_ref, k_ref, v_ref, g_ref, cos_ref, sin_ref, din_ref, xi_ref, zeta_ref, cd_ref,
                      gng_ref, gnb_ref, o_ref, r_sc):
    n = pl.program_id(1)
    half = RET_QK_DIM // 2

    @pl.when(n == 0)
    def _():
        r_sc[...] = jnp.zeros_like(r_sc)

    cos = cos_ref[...]
    sin = sin_ref[...]

    def rope(x):
        x1 = x[:, :half]
        x2 = x[:, half:]
        return x1 * cos - x2 * sin, x1 * sin + x2 * cos

    q1, q2 = rope(q_ref[...])
    k1, k2 = rope(k_ref[...])
    scale = RET_QK_DIM ** -0.5
    qr = jnp.concatenate([q1, q2], axis=1).astype(BF16)
    kr = jnp.concatenate([k1 * scale, k2 * scale], axis=1)
    v = v_ref[...].astype(BF16)

    inner = lax.dot_general(qr, kr.astype(BF16), (((1,), (1,)), ((), ())), preferred_element_type=F32)
    inner = inner * din_ref[0]
    r = r_sc[...]
    cross = jnp.dot(qr, r.astype(BF16), preferred_element_type=F32)
    xi = xi_ref[0]
    o = jnp.dot(inner.astype(BF16), v, preferred_element_type=F32)
    o = o + cross * jnp.concatenate([xi] * (RET_V_DIM // LANES), axis=1)

    zeta = zeta_ref[0]
    kz = (kr * jnp.concatenate([zeta] * (RET_QK_DIM // LANES), axis=1)).astype(BF16)
    upd = lax.dot_general(kz, v, (((0,), (0,)), ((), ())), preferred_element_type=F32)
    r_sc[...] = cd_ref[0][:1, :1] * r + upd

    mu = jnp.mean(o, axis=-1, keepdims=True)
    oc = o - mu
    var = jnp.mean(oc * oc, axis=-1, keepdims=True)
    y = oc * lax.rsqrt(var + 1e-5) * gng_ref[...] + gnb_ref[...]
    gate = g_ref[...]
    o_ref[...] = (gate * jax.nn.sigmoid(gate) * y).astype(o_ref.dtype)


def retention(h1, cos, sin, gn_g, gn_b):
    s = h1.shape[0]
    hh, dk, dv, c = RET_HEADS, RET_QK_DIM, RET_V_DIM, RET_CHUNK
    din, xi, zeta, cd = (jnp.asarray(t) for t in _retention_tables())
    k_blk0 = hh * dk // dk
    v_blk0 = 2 * hh * dk // dv
    g_blk0 = v_blk0 + hh
    per_head = lambda r, w: pl.BlockSpec((1, r, w), lambda h, n: (h, 0, 0))
    return pl.pallas_call(
        _retention_kernel,
        grid=(hh, s // c),
        in_specs=[
            pl.BlockSpec((c, dk), lambda h, n: (n, h)),
            pl.BlockSpec((c, dk), lambda h, n: (n, k_blk0 + h)),
            pl.BlockSpec((c, dv), lambda h, n: (n, v_blk0 + h)),
            pl.BlockSpec((c, dv), lambda h, n: (n, g_blk0 + h)),
            pl.BlockSpec((c, dk // 2), lambda h, n: (n, 0)),
            pl.BlockSpec((c, dk // 2), lambda h, n: (n, 0)),
            per_head(c, c), per_head(c, LANES), per_head(c, LANES), per_head(8, LANES),
            pl.BlockSpec((1, dv), lambda h, n: (0, h)),
            pl.BlockSpec((1, dv), lambda h, n: (0, h)),
        ],
        out_specs=pl.BlockSpec((c, dv), lambda h, n: (n, h)),
        out_shape=jax.ShapeDtypeStruct((s, hh * dv), BF16),
        scratch_shapes=[pltpu.VMEM((dk, dv), F32)],
        compiler_params=_params("parallel", "arbitrary"),
        name="retention",
    )(h1, h1, h1, h1, cos, sin, din, xi, zeta, cd, gn_g.reshape(1, -1), gn_b.reshape(1, -1))


def _first_argmax(v, idx, n):
    m = jnp.max(v, axis=0, keepdims=True)
    first = jnp.min(jnp.where(v == m, idx, n), axis=0, keepdims=True)
    return m, first


def _router_kernel(x_ref, rwt_ref, rb_ref, e_ref, g_ref):
    tm = x_ref.shape[0]
    epg = EXPERTS_PER_GROUP
    logits = lax.dot_general(rwt_ref[...], x_ref[...], (((1,), (1,)), ((), ())),
                             precision=lax.Precision.HIGHEST, preferred_element_type=F32)
    scores = jax.nn.sigmoid(logits)
    biased = scores + rb_ref[...]
    idx = lax.broadcasted_iota(jnp.int32, (epg, tm), 0)

    group_scores = []
    for g in range(N_GROUPS):
        v = biased[g * epg:(g + 1) * epg]
        m1, first = _first_argmax(v, idx, epg)
        m2 = jnp.max(jnp.where(idx == first, -jnp.inf, v), axis=0, keepdims=True)
        group_scores.append(m1 + m2)
    gmax = group_scores[0]
    for g in range(1, N_GROUPS):
        gmax = jnp.maximum(gmax, group_scores[g])
    gsel = jnp.full((1, tm), N_GROUPS, jnp.int32)
    for g in range(N_GROUPS - 1, -1, -1):
        gsel = jnp.where(group_scores[g] == gmax, g, gsel)

    in_biased = jnp.zeros((epg, tm), F32)
    in_scores = jnp.zeros((epg, tm), F32)
    for g in range(N_GROUPS):
        pick = gsel == g
        in_biased = jnp.where(pick, biased[g * epg:(g + 1) * epg], in_biased)
        in_scores = jnp.where(pick, scores[g * epg:(g + 1) * epg], in_scores)
    _, loc1 = _first_argmax(in_biased, idx, epg)
    _, loc2 = _first_argmax(jnp.where(idx == loc1, -jnp.inf, in_biased), idx, epg)
    s1 = jnp.sum(jnp.where(idx == loc1, in_scores, 0.0), axis=0, keepdims=True)
    s2 = jnp.sum(jnp.where(idx == loc2, in_scores, 0.0), axis=0, keepdims=True)
    denom = s1 + s2
    e_ref[...] = jnp.concatenate([gsel * epg + loc1, gsel * epg + loc2], axis=0)
    g_ref[...] = jnp.concatenate([s1 / denom, s2 / denom], axis=0)


def moe_router(x, router_w, router_b, tm=512):
    s, d = x.shape
    return pl.pallas_call(
        _router_kernel,
        grid=(s // tm,),
        in_specs=[
            pl.BlockSpec((tm, d), lambda i: (i, 0)),
            pl.BlockSpec((N_EXPERTS, d), lambda i: (0, 0)),
            pl.BlockSpec((N_EXPERTS, 1), lambda i: (0, 0)),
        ],
        out_specs=[pl.BlockSpec((TOP_K, tm), lambda i: (0, i)), pl.BlockSpec((TOP_K, tm), lambda i: (0, i))],
        out_shape=[jax.ShapeDtypeStruct((TOP_K, s), jnp.int32), jax.ShapeDtypeStruct((TOP_K, s), F32)],
        compiler_params=_params("parallel"),
        name="moe_router",
    )(x, router_w.T, router_b.reshape(N_EXPERTS, 1))


def _expert_kernel(te_ref, na_ref, tok_ref, tok_next_ref, x_hbm, wg_ref, wu_ref, wd_ref, o_ref,
                   wg_sc, wu_sc, wd_sc, xbuf, gsem, *, tm):
    j = pl.program_id(0)
    n_active = na_ref[0]
    active = j < n_active
    slot = j % 2
    new_expert = (j == 0) | (te_ref[j] != te_ref[jnp.maximum(j - 1, 0)])

    def start_gather(idx_ref, dst_slot):
        def start(i, carry):
            for u in range(DMA_ISSUE_UNROLL):
                r = i * DMA_ISSUE_UNROLL + u
                pltpu.make_async_copy(x_hbm.at[pl.ds(idx_ref[0, 0, r], 1)], xbuf.at[dst_slot, pl.ds(r, 1)],
                                      gsem.at[dst_slot]).start(priority=u % 2)
            return carry

        lax.fori_loop(0, tm // DMA_ISSUE_UNROLL, start, 0)

    @pl.when(j == 0)
    def _():
        start_gather(tok_ref, 0)

    @pl.when(j + 1 < n_active)
    def _():
        start_gather(tok_next_ref, 1 - slot)

    @pl.when(active & new_expert)
    def _():
        wg_sc[...] = wg_ref[0].astype(BF16)
        wu_sc[...] = wu_ref[0].astype(BF16)
        wd_sc[...] = wd_ref[0].astype(BF16)

    @pl.when(active)
    def _():
        pltpu.make_async_copy(x_hbm.at[pl.ds(0, tm)], xbuf.at[slot], gsem.at[slot]).wait()
        x = _load_row_tiles(xbuf.at[slot]).astype(BF16)
        hg = jnp.dot(x, wg_sc[...], preferred_element_type=F32)
        hu = jnp.dot(x, wu_sc[...], preferred_element_type=F32)
        h = hg * jax.nn.sigmoid(hg) * hu
        _store_row_tiles(o_ref, jnp.dot(h.astype(BF16), wd_sc[...], preferred_element_type=F32))

    @pl.when(jnp.logical_not(active))
    def _():
        o_ref[...] = jnp.zeros_like(o_ref)


def routed_experts(x, row_token, tile_expert, n_active_tiles, w_gate, w_up, w_down, tm):
    p = row_token.shape[0]
    d = x.shape[1] * x.shape[2]
    row_tiles = x.shape[1:]
    f = w_gate.shape[2]
    nt = p // tm

    def tok_blk(j, te, na):
        return (jnp.minimum(j, na[0] - 1), 0, 0)

    def tok_next_blk(j, te, na):
        return (jnp.minimum(j + 1, na[0] - 1), 0, 0)

    tokens = row_token.reshape(nt, 1, tm)
    return pl.pallas_call(
        functools.partial(_expert_kernel, tm=tm),
        grid_spec=pltpu.PrefetchScalarGridSpec(
            num_scalar_prefetch=2,
            grid=(nt,),
            in_specs=[
                pl.BlockSpec((1, 1, tm), tok_blk, memory_space=pltpu.SMEM),
                pl.BlockSpec((1, 1, tm), tok_next_blk, memory_space=pltpu.SMEM),
                pl.BlockSpec(memory_space=pl.ANY),
                pl.BlockSpec((1, d, f), lambda j, te, na: (te[j], 0, 0)),
                pl.BlockSpec((1, d, f), lambda j, te, na: (te[j], 0, 0)),
                pl.BlockSpec((1, f, d), lambda j, te, na: (te[j], 0, 0)),
            ],
            out_specs=pl.BlockSpec((tm,) + row_tiles, lambda j, te, na: (j, 0, 0)),
            scratch_shapes=[pltpu.VMEM((d, f), BF16), pltpu.VMEM((d, f), BF16), pltpu.VMEM((f, d), BF16),
                            pltpu.VMEM((2, tm) + row_tiles, x.dtype), pltpu.SemaphoreType.DMA((2,))],
        ),
        out_shape=jax.ShapeDtypeStruct((p,) + row_tiles, F32),
        compiler_params=_params("arbitrary"),
        name="routed_experts",
    )(tile_expert, n_active_tiles, tokens, tokens, x, w_gate, w_up, w_down)


def _shared_kernel(x_ref, wg_ref, wu_ref, wd_ref, o_ref):
    x = x_ref[...]
    hg = jnp.dot(x, wg_ref[...], preferred_element_type=F32)
    hu = jnp.dot(x, wu_ref[...], preferred_element_type=F32)
    h = hg * jax.nn.sigmoid(hg) * hu
    o_ref[...] = jnp.dot(h.astype(BF16), wd_ref[...], preferred_element_type=F32)


def shared_expert(x_bf, wg, wu, wd, tm=512):
    s, d = x_bf.shape
    f = wg.shape[1]
    whole = lambda r, c: pl.BlockSpec((r, c), lambda i: (0, 0))
    return pl.pallas_call(
        _shared_kernel,
        grid=(s // tm,),
        in_specs=[pl.BlockSpec((tm, d), lambda i: (i, 0)), whole(d, f), whole(d, f), whole(f, d)],
        out_specs=pl.BlockSpec((tm, d), lambda i: (i, 0)),
        out_shape=jax.ShapeDtypeStruct((s, d), F32),
        compiler_params=_params("parallel"),
        name="shared_expert",
    )(x_bf, wg.astype(BF16), wu.astype(BF16), wd.astype(BF16))


def _combine_kernel(pos_ref, x_ref, sh_ref, gate_ref, ys_ref, g_ref, b_ref, o_ref, obf_ref, buf, sem, *, tm):
    def start(i, carry):
        for u in range(DMA_ISSUE_UNROLL):
            r = i * DMA_ISSUE_UNROLL + u
            for k in range(TOP_K):
                pltpu.make_async_copy(ys_ref.at[pl.ds(pos_ref[0, k, r], 1)], buf.at[k, pl.ds(r, 1)],
                                      sem).start(priority=k % 2)
        return carry

    lax.fori_loop(0, tm // DMA_ISSUE_UNROLL, start, 0)
    for k in range(TOP_K):
        pltpu.make_async_copy(ys_ref.at[pl.ds(0, tm)], buf.at[k], sem).wait()
    gate = gate_ref[...]
    routed = gate[:, 0:1] * _load_row_tiles(buf.at[0]) + gate[:, 1:2] * _load_row_tiles(buf.at[1])
    z = DEEPNORM_ALPHA * x_ref[...] + (routed + sh_ref[...])
    out = _layer_norm(z, g_ref[...], b_ref[...])
    o_ref[...] = out
    obf_ref[...] = out.astype(BF16)


def moe_combine(x, shared, gates, pos, ys, g, b, tm=256):
    s, d = x.shape
    nt = s // tm
    row = pl.BlockSpec((tm, d), lambda i: (i, 0))
    vec = pl.BlockSpec((1, d), lambda i: (0, 0))
    return pl.pallas_call(
        functools.partial(_combine_kernel, tm=tm),
        grid=(nt,),
        in_specs=[
            pl.BlockSpec((1, TOP_K, tm), lambda i: (i, 0, 0), memory_space=pltpu.SMEM),
            row, row,
            pl.BlockSpec((tm, TOP_K), lambda i: (i, 0)),
            pl.BlockSpec(memory_space=pl.ANY),
            vec, vec,
        ],
        out_specs=[row, row],
        out_shape=[jax.ShapeDtypeStruct((s, d), F32), jax.ShapeDtypeStruct((s, d), BF16)],
        scratch_shapes=[pltpu.VMEM((TOP_K, tm, d // LANES, LANES), F32), pltpu.SemaphoreType.DMA(())],
        compiler_params=_params("arbitrary"),
        name="moe_combine",
    )(pos.reshape(TOP_K, nt, tm).transpose(1, 0, 2), x, shared, gates.T, ys, g.reshape(1, d), b.reshape(1, d))


def _rope_tables(pos, rot_dim, theta):
    half = rot_dim // 2
    inv = (1.0 / (theta ** (np.arange(half, dtype=np.float32) / half))).astype(np.float32)
    ang = pos.astype(F32)[:, None] * inv
    return jnp.cos(ang), jnp.sin(ang)


def _rope(x, cos, sin, rot_dim):
    half = rot_dim // 2
    x1 = x[..., :half]
    x2 = x[..., half:rot_dim]
    c = cos[:, None, :]
    s = sin[:, None, :]
    return jnp.concatenate([x1 * c - x2 * s, x1 * s + x2 * c, x[..., rot_dim:]], axis=-1)


def _routing_plan(experts, tm):
    k, s = experts.shape
    flat = experts.reshape(-1)
    onehot = (flat[:, None] == jnp.arange(N_EXPERTS, dtype=jnp.int32)[None, :]).astype(jnp.int32)
    csum = jnp.cumsum(onehot, axis=0)
    rank = jnp.take_along_axis(csum, flat[:, None], axis=1)[:, 0] - 1
    sizes = csum[-1]
    tiles = (sizes + tm - 1) // tm
    tile_end = jnp.cumsum(tiles)
    start = (tile_end - tiles) * tm
    pos = start[flat] + rank
    n_rows = k * s + N_EXPERTS * tm
    n_tiles = n_rows // tm
    row_token = jnp.zeros((n_rows,), jnp.int32).at[pos].set(jnp.tile(jnp.arange(s, dtype=jnp.int32), k))
    n_active = tile_end[-1]
    tile_ids = jnp.minimum(jnp.arange(n_tiles, dtype=jnp.int32), n_active - 1)
    tile_expert = jnp.searchsorted(tile_end, tile_ids, side="right").astype(jnp.int32)
    return pos.reshape(k, s), row_token, tile_expert, n_active.reshape(1).astype(jnp.int32)


def _grouped_moe(x, x_bf, x_rows, router_w, router_b, w_gate, w_up, w_down, sh_gate, sh_up, sh_down, ln_g, ln_b,
                 expert_tm=256):
    experts, gates = moe_router(x, router_w, router_b)
    pos, row_token, tile_expert, n_active = _routing_plan(experts, expert_tm)
    ys = routed_experts(x_rows, row_token, tile_expert, n_active, w_gate, w_up, w_down, expert_tm)
    shared = shared_expert(x_bf, sh_gate, sh_up, sh_down)
    return moe_combine(x, shared, gates, pos, ys, ln_g, ln_b)


def _mla_dsa_mixer(x, pos, w_in, q_norm, w_uq, kv_norm, w_ukv, w_out):
    s = x.shape[0]
    n_pad = (-IN_AB) % 512
    h0 = matmul(x, jnp.pad(w_in, ((0, 0), (0, n_pad))), F32, 1024, 512, name="l0_in_proj")
    cq, ckv, krope, dq, dkv, iq, ik, iw = jnp.split(h0[:, :IN_AB], [int(c) for c in np.cumsum(SPLITS_AB)[:-1]], axis=-1)
    cos_m, sin_m = _rope_tables(pos, MLA_ROPE, ROPE_THETA)
    cos_d, sin_d = _rope_tables(pos, DSA_ROT, ROPE_THETA)
    cos_i, sin_i = _rope_tables(pos, IDX_ROT, ROPE_THETA)

    dqk = MLA_NOPE + MLA_ROPE
    qa = matmul(h0, w_uq, F32, 1024, 512, name="mla_q_up", a_cols=(0, MLA_Q_LORA),
                norm_gain=q_norm).reshape(s, MLA_HEADS, dqk)
    kva = matmul(h0, w_ukv, F32, 1024, 512, name="mla_kv_up", a_cols=(MLA_Q_LORA // MLA_KV_LORA, MLA_KV_LORA),
                 norm_gain=kv_norm).reshape(s, MLA_HEADS, MLA_NOPE + MLA_V)
    q_pe = _rope(qa[..., MLA_NOPE:], cos_m, sin_m, MLA_ROPE)
    q_cat = jnp.concatenate([qa[..., :MLA_NOPE], q_pe, jnp.zeros((s, MLA_HEADS, 2 * LANES - dqk), F32)], axis=-1)
    q_cat = (q_cat * (dqk ** -0.5 * LOG2_E)).astype(BF16).reshape(s, MLA_HEADS * 2 * LANES)
    k_pe = _rope(krope[:, None, :], cos_m, sin_m, MLA_ROPE)[:, 0]
    k_pe = jnp.concatenate([k_pe, jnp.zeros((s, LANES - MLA_ROPE), F32)], axis=-1).astype(BF16)
    k_nope = kva[..., :MLA_NOPE].astype(BF16).reshape(s, MLA_HEADS * MLA_NOPE)
    v_a = kva[..., MLA_NOPE:].astype(BF16).reshape(s, MLA_HEADS * MLA_V)
    out_a = mla_attention(q_cat, k_nope, k_pe, v_a)

    qb = _rope(dq.reshape(s, DSA_HEADS, DSA_HEAD_DIM), cos_d, sin_d, DSA_ROT) * (DSA_HEAD_DIM ** -0.5 * LOG2_E)
    qb = qb.astype(BF16).transpose(1, 0, 2)
    kvb = dkv.reshape(s, 2, DSA_KV_HEADS, DSA_HEAD_DIM)
    kb = _rope(kvb[:, 0], cos_d, sin_d, DSA_ROT).astype(BF16).reshape(s, DSA_KV_HEADS * DSA_HEAD_DIM)
    vb = kvb[:, 1].astype(BF16).reshape(s, DSA_KV_HEADS * DSA_HEAD_DIM)
    iq = _rope(iq.reshape(s, IDX_HEADS, IDX_DIM), cos_i, sin_i, IDX_ROT).astype(BF16).transpose(1, 0, 2)
    ik = _rope(ik[:, None, :], cos_i, sin_i, IDX_ROT)[:, 0].astype(BF16)
    iw = iw * (IDX_HEADS ** -0.5 * IDX_DIM ** -0.5)
    out_b = dsa_attention(iq, iw, ik, qb, kb, vb)

    return matmul(jnp.concatenate([out_a, out_b], axis=-1), w_out, F32, 1024, 512, name="l0_out_proj")


def _retention_mixer(x_bf, pos, w_in, gn_g, gn_b, w_out):
    h1 = matmul(x_bf, w_in, F32, 1024, 1024, name="l1_in_proj")
    cos, sin = _rope_tables(pos, RET_QK_DIM, RET_THETA)
    y = retention(h1, cos, sin, gn_g, gn_b)
    return matmul(y, w_out, F32, 1024, 512, name="l1_out_proj")


def kernel(x, positions, router_w, router_b, l0_w_in, l0_mla_q_norm, l0_mla_w_uq, l0_mla_kv_norm, l0_mla_w_ukv, l0_w_out, l1_w_in, l1_ret_gn_g, l1_ret_gn_b, l1_w_out, l0_ln_mix_g, l0_ln_mix_b, l0_moe_w_gate, l0_moe_w_up, l0_moe_w_down, l0_sh_gate, l0_sh_up, l0_sh_down, l0_ln_ffn_g, l0_ln_ffn_b, l1_ln_mix_g, l1_ln_mix_b, l1_moe_w_gate, l1_moe_w_up, l1_moe_w_down, l1_sh_gate, l1_sh_up, l1_sh_down, l1_ln_ffn_g, l1_ln_ffn_b):
    assert x.shape[0] == 1
    xt = x[0]
    pos = positions[0]

    mix = _mla_dsa_mixer(xt, pos, l0_w_in, l0_mla_q_norm, l0_mla_w_uq, l0_mla_kv_norm, l0_mla_w_ukv, l0_w_out)
    xt, xt_bf, xt_rows = add_layer_norm(xt, mix, l0_ln_mix_g, l0_ln_mix_b)
    xt, xt_bf = _grouped_moe(xt, xt_bf, xt_rows, router_w, router_b, l0_moe_w_gate, l0_moe_w_up, l0_moe_w_down,
                             l0_sh_gate, l0_sh_up, l0_sh_down, l0_ln_ffn_g, l0_ln_ffn_b)

    mix = _retention_mixer(xt_bf, pos, l1_w_in, l1_ret_gn_g, l1_ret_gn_b, l1_w_out)
    xt, xt_bf, xt_rows = add_layer_norm(xt, mix, l1_ln_mix_g, l1_ln_mix_b)
    xt, _ = _grouped_moe(xt, xt_bf, xt_rows, router_w, router_b, l1_moe_w_gate, l1_moe_w_up, l1_moe_w_down,
                         l1_sh_gate, l1_sh_up, l1_sh_down, l1_ln_ffn_g, l1_ln_ffn_b)
    return xt[None]
```

```python
import functools
from typing import NamedTuple

import numpy as np
import jax
import jax.numpy as jnp
from jax import lax
from jax.experimental import pallas as pl
from jax.experimental.pallas import tpu as pltpu

F32 = jnp.float32
BF16 = jnp.bfloat16

D_MODEL = 2048
DEPTH = 2
ROPE_THETA = 500000.0
MLA_HEADS = 8
MLA_Q_LORA = 512
MLA_KV_LORA = 256
MLA_NOPE = 128
MLA_ROPE = 64
MLA_V = 128
DSA_HEADS = 8
DSA_KV_HEADS = 2
DSA_HEAD_DIM = 128
DSA_ROT = DSA_HEAD_DIM // 4
IDX_HEADS = 16
IDX_DIM = 64
IDX_ROT = IDX_DIM // 4
IDX_TOPK_MAX = 256
RET_HEADS = 8
RET_QK_DIM = 256
RET_V_DIM = 512
RET_CHUNK = 128
RET_THETA = 10000.0
N_EXPERTS = 64
N_GROUPS = 8
EXPERTS_PER_GROUP = N_EXPERTS // N_GROUPS
TOP_K = 2
D_EXPERT = 512
D_SHARED = 1024
DEEPNORM_ALPHA = (2.0 * DEPTH) ** 0.25

SPLITS_AB = (MLA_Q_LORA, MLA_KV_LORA, MLA_ROPE, DSA_HEADS * DSA_HEAD_DIM,
             2 * DSA_KV_HEADS * DSA_HEAD_DIM, IDX_HEADS * IDX_DIM, IDX_DIM, IDX_HEADS)
IN_AB = sum(SPLITS_AB)

VMEM_LIMIT_BYTES = 56 * 1024 * 1024
LANES = 128
MASKED_SCORE = -1e30
LOG2_E = 1.4426950408889634
DMA_ISSUE_UNROLL = 8
INT32_MIN = -2 ** 31


def _params(*sem):
    return pltpu.CompilerParams(dimension_semantics=sem, vmem_limit_bytes=VMEM_LIMIT_BYTES)


def _mm_kernel(*refs, norm_eps):
    if norm_eps is None:
        a_ref, b_ref, o_ref = refs
        a = a_ref[...]
    else:
        a_ref, g_ref, b_ref, o_ref = refs
        af = a_ref[...].astype(F32)
        a = af * lax.rsqrt(jnp.mean(af * af, axis=-1, keepdims=True) + norm_eps) * g_ref[...]
    o_ref[...] = jnp.dot(a.astype(BF16), b_ref[...].astype(BF16),
                         preferred_element_type=F32).astype(o_ref.dtype)


def matmul(a, b, out_dtype, tm, tn, *, name, a_cols=None, norm_gain=None, norm_eps=1e-6):
    m = a.shape[0]
    k, n = b.shape
    col_blk = 0 if a_cols is None else a_cols[0]
    assert (a.shape[1] == k) if a_cols is None else (a_cols[1] == k)
    assert m % tm == 0 and n % tn == 0
    in_specs = [pl.BlockSpec((tm, k), lambda i, j: (i, col_blk))]
    args = [a]
    if norm_gain is not None:
        in_specs.append(pl.BlockSpec((1, k), lambda i, j: (0, 0)))
        args.append(norm_gain.reshape(1, k))
    in_specs.append(pl.BlockSpec((k, tn), lambda i, j: (0, j)))
    args.append(b)
    return pl.pallas_call(
        functools.partial(_mm_kernel, norm_eps=None if norm_gain is None else norm_eps),
        grid=(m // tm, n // tn),
        in_specs=in_specs,
        out_specs=pl.BlockSpec((tm, tn), lambda i, j: (i, j)),
        out_shape=jax.ShapeDtypeStruct((m, n), out_dtype),
        compiler_params=_params("parallel", "parallel"),
        name=name,
    )(*args)


def _layer_norm(z, g, b):
    mu = jnp.mean(z, axis=-1, keepdims=True)
    zc = z - mu
    var = jnp.mean(zc * zc, axis=-1, keepdims=True)
    return zc * lax.rsqrt(var + 1e-5) * g + b


def _add_ln_kernel(x_ref, y_ref, g_ref, b_ref, o_ref, obf_ref):
    z = DEEPNORM_ALPHA * x_ref[...] + y_ref[...]
    out = _layer_norm(z, g_ref[...], b_ref[...])
    o_ref[...] = out
    obf_ref[...] = out.astype(BF16)


def add_layer_norm(x, y, g, b, tm=256):
    s, d = x.shape
    row = pl.BlockSpec((tm, d), lambda i: (i, 0))
    vec = pl.BlockSpec((1, d), lambda i: (0, 0))
    return pl.pallas_call(
        _add_ln_kernel,
        grid=(s // tm,),
        in_specs=[row, row, vec, vec],
        out_specs=[row, row],
        out_shape=[jax.ShapeDtypeStruct((s, d), F32), jax.ShapeDtypeStruct((s, d), BF16)],
        compiler_params=_params("parallel"),
        name="add_layer_norm",
    )(x, y, g.reshape(1, d), b.reshape(1, d))


def _lane_tile(x, n):
    return jnp.tile(x, (1, n))


def _softmax_step(s, v_ext, m_ref, acc_ref):
    tk = s.shape[1]
    m_prev = m_ref[...]
    m_new = jnp.maximum(m_prev, jnp.max(s, axis=1)[:, None])
    p = jnp.exp2(s - _lane_tile(m_new, tk // LANES))
    alpha = jnp.exp2(m_prev - m_new)
    pv = jnp.dot(p.astype(BF16), v_ext, preferred_element_type=F32)
    acc_ref[...] = _lane_tile(alpha, 2) * acc_ref[...] + pv
    m_ref[...] = m_new


def _mla_kernel(q_ref, kn_ref, kp_ref, v_ref, o_ref, m_sc, acc_sc, *, tq, tk, hp):
    qi = pl.program_id(1)
    m_sc[...] = jnp.full_like(m_sc, MASKED_SCORE)
    acc_sc[...] = jnp.zeros_like(acc_sc)
    ones = jnp.ones((tk, LANES), BF16)
    per_block = tq // tk

    def attend(c, diagonal_offset):
        kp = kp_ref[c]
        kn = kn_ref[c]
        v = v_ref[c]
        for j in range(hp):
            q = q_ref[:, j * 2 * LANES:(j + 1) * 2 * LANES]
            k = jnp.concatenate([kn[:, j * LANES:(j + 1) * LANES], kp], axis=1)
            s = lax.dot_general(q, k, (((1,), (1,)), ((), ())), preferred_element_type=F32)
            if diagonal_offset is not None:
                row = lax.broadcasted_iota(jnp.int32, (tq, tk), 0)
                col = diagonal_offset * tk + lax.broadcasted_iota(jnp.int32, (tq, tk), 1)
                s = jnp.where(col <= row, s, MASKED_SCORE)
            v_ext = jnp.concatenate([v[:, j * LANES:(j + 1) * LANES], ones], axis=1)
            _softmax_step(s, v_ext, m_sc.at[j], acc_sc.at[j])

    def below_diagonal(c, carry):
        attend(c, None)
        return carry

    lax.fori_loop(0, qi * per_block, below_diagonal, 0)
    for d in range(per_block):
        attend(qi * per_block + d, d)
    for j in range(hp):
        acc = acc_sc[j]
        o_ref[:, j * LANES:(j + 1) * LANES] = (acc[:, :LANES] / acc[:, LANES:]).astype(o_ref.dtype)


def mla_attention(q, k_nope, k_pe, v, tq=512, tk=512, hp=2):
    s = q.shape[0]
    h = MLA_HEADS
    nc = s // tk
    assert tq % tk == 0 and h % hp == 0
    per_head = pl.BlockSpec((nc, tk, hp * LANES), lambda hh, qi: (0, 0, hh))
    return pl.pallas_call(
        functools.partial(_mla_kernel, tq=tq, tk=tk, hp=hp),
        grid=(h // hp, s // tq),
        in_specs=[
            pl.BlockSpec((tq, hp * 2 * LANES), lambda hh, qi: (qi, hh)),
            per_head,
            pl.BlockSpec((nc, tk, LANES), lambda hh, qi: (0, 0, 0)),
            per_head,
        ],
        out_specs=pl.BlockSpec((tq, hp * LANES), lambda hh, qi: (qi, hh)),
        out_shape=jax.ShapeDtypeStruct((s, h * MLA_V), BF16),
        scratch_shapes=[pltpu.VMEM((hp, tq, LANES), F32), pltpu.VMEM((hp, tq, 2 * LANES), F32)],
        compiler_params=_params("parallel", "arbitrary"),
        name="mla_attention",
    )(q, k_nope.reshape(nc, tk, h * MLA_NOPE), k_pe.reshape(nc, tk, LANES), v.reshape(nc, tk, h * MLA_V))


def _sortable_key(x):
    bits = pltpu.bitcast(x, jnp.int32)
    return bits ^ ((bits >> 31) & 0x7FFFFFFF)


def _dsa_kernel(iq_ref, iw_ref, ik_ref, q_ref, k_ref, v_ref, o_ref, key_sc, wb_sc, m_sc, acc_sc,
                *, tq, tk, n_sel):
    i = pl.program_id(0)
    n_chunks = ((i + 1) * tq + tk - 1) // tk
    rep = DSA_HEADS // DSA_KV_HEADS
    lane_tiles = tk // LANES

    iq_all = iq_ref[...].reshape(IDX_HEADS * tq, IDX_DIM)
    iw = iw_ref[...]
    for h in range(IDX_HEADS):
        wb_sc[h] = jnp.broadcast_to(iw[:, h:h + 1], (tq, LANES))

    def index_chunk(c, carry):
        d = lax.dot_general(iq_all, ik_ref[c], (((1,), (1,)), ((), ())), preferred_element_type=F32)
        d = jnp.maximum(d, 0.0).reshape(IDX_HEADS, tq, tk)
        score = _lane_tile(wb_sc[0], lane_tiles) * d[0]
        for h in range(1, IDX_HEADS):
            score = score + _lane_tile(wb_sc[h], lane_tiles) * d[h]
        key_sc[c] = _sortable_key(score)
        return carry

    lax.fori_loop(0, n_chunks, index_chunk, 0)
    last = n_chunks - 1
    row = i * tq + lax.broadcasted_iota(jnp.int32, (tq, tk), 0)
    col = last * tk + lax.broadcasted_iota(jnp.int32, (tq, tk), 1)
    key_sc[last] = jnp.where(col <= row, key_sc[last], INT32_MIN)

    def search_bit(b, thr):
        cand = thr + lax.shift_left(jnp.int32(1), 31 - b)

        def count_chunk(c, cnt):
            key = key_sc[c]
            for j in range(lane_tiles):
                cnt = cnt + jnp.where(key[:, j * LANES:(j + 1) * LANES] >= cand, 1.0, 0.0)
            return cnt

        cnt = lax.fori_loop(0, n_chunks, count_chunk, jnp.zeros((tq, LANES), F32))
        return jnp.where(jnp.sum(cnt, axis=1)[:, None] >= n_sel, cand, thr)

    thr = lax.fori_loop(0, 32, search_bit, jnp.full((tq, LANES), INT32_MIN, jnp.int32))
    thr = _lane_tile(jnp.maximum(thr, INT32_MIN + 1), lane_tiles)

    m_sc[...] = jnp.full_like(m_sc, MASKED_SCORE)
    acc_sc[...] = jnp.zeros_like(acc_sc)
    ones = jnp.ones((tk, LANES), BF16)

    def attend_chunk(c, carry):
        bias = jnp.where(key_sc[c] >= thr, 0.0, MASKED_SCORE)
        kc = k_ref[c]
        vc = v_ref[c]
        for g in range(DSA_KV_HEADS):
            cols = slice(g * DSA_HEAD_DIM, (g + 1) * DSA_HEAD_DIM)
            qg = q_ref[g * rep:(g + 1) * rep].reshape(rep * tq, DSA_HEAD_DIM)
            s = lax.dot_general(qg, kc[:, cols], (((1,), (1,)), ((), ())), preferred_element_type=F32)
            s = (s.reshape(rep, tq, tk) + bias[None]).reshape(rep * tq, tk)
            _softmax_step(s, jnp.concatenate([vc[:, cols], ones], axis=1), m_sc.at[g], acc_sc.at[g])
        return carry

    lax.fori_loop(0, n_chunks, attend_chunk, 0)
    for g in range(DSA_KV_HEADS):
        acc = acc_sc[g]
        out = acc[:, :LANES] / acc[:, LANES:]
        for r in range(rep):
            hh = g * rep + r
            o_ref[:, hh * DSA_HEAD_DIM:(hh + 1) * DSA_HEAD_DIM] = out[r * tq:(r + 1) * tq].astype(o_ref.dtype)


def dsa_attention(iq, iw, ik, q, k, v, tq=128, tk=512):
    s = ik.shape[0]
    n_sel = min(IDX_TOPK_MAX, s // 4)
    nc = s // tk
    assert tk >= n_sel and s % tk == 0 and tk % tq == 0 and DSA_HEAD_DIM == LANES
    rep = DSA_HEADS // DSA_KV_HEADS
    kvw = DSA_KV_HEADS * DSA_HEAD_DIM
    whole3 = lambda i: (0, 0, 0)
    return pl.pallas_call(
        functools.partial(_dsa_kernel, tq=tq, tk=tk, n_sel=n_sel),
        grid=(s // tq,),
        in_specs=[
            pl.BlockSpec((IDX_HEADS, tq, IDX_DIM), lambda i: (0, i, 0)),
            pl.BlockSpec((tq, IDX_HEADS), lambda i: (i, 0)),
            pl.BlockSpec((nc, tk, IDX_DIM), whole3),
            pl.BlockSpec((DSA_HEADS, tq, DSA_HEAD_DIM), lambda i: (0, i, 0)),
            pl.BlockSpec((nc, tk, kvw), whole3),
            pl.BlockSpec((nc, tk, kvw), whole3),
        ],
        out_specs=pl.BlockSpec((tq, DSA_HEADS * DSA_HEAD_DIM), lambda i: (i, 0)),
        out_shape=jax.ShapeDtypeStruct((s, DSA_HEADS * DSA_HEAD_DIM), BF16),
        scratch_shapes=[
            pltpu.VMEM((nc, tq, tk), jnp.int32),
            pltpu.VMEM((IDX_HEADS, tq, LANES), F32),
            pltpu.VMEM((DSA_KV_HEADS, rep * tq, LANES), F32),
            pltpu.VMEM((DSA_KV_HEADS, rep * tq, 2 * LANES), F32),
        ],
        compiler_params=_params("parallel"),
        name="dsa_attention",
    )(iq, iw, ik.reshape(nc, tk, IDX_DIM), q, k.reshape(nc, tk, kvw), v.reshape(nc, tk, kvw))


def _retention_tables():
    h, c = RET_HEADS, RET_CHUNK
    log_g = np.log(1.0 - 2.0 ** (-5.0 - np.arange(h, dtype=np.float32))).astype(np.float32).astype(np.float64)
    idx = np.arange(c, dtype=np.float64)
    diff = idx[:, None] - idx[None, :]
    decay_in = np.where(diff[None] >= 0, np.exp(np.maximum(diff, 0.0)[None] * log_g[:, None, None]), 0.0)
    xi = np.exp((idx + 1.0)[None, :] * log_g[:, None])
    zeta = np.exp((c - 1.0 - idx)[None, :] * log_g[:, None])
    chunk_decay = np.exp(c * log_g)
    lane = np.ones((1, 1, LANES))
    return (decay_in.astype(np.float32), (xi[:, :, None] * lane).astype(np.float32),
            (zeta[:, :, None] * lane).astype(np.float32),
            (chunk_decay[:, None, None] * np.ones((1, 8, LANES))).astype(np.float32))


def _retention_kernel(q_ref, k_ref, v_ref, g_ref, cos_ref, sin_ref, din_ref, xi_ref, zeta_ref, cd_ref,
                      gng_ref, gnb_ref, o_ref, r_sc):
    n = pl.program_id(1)
    half = RET_QK_DIM // 2

    @pl.when(n == 0)
    def _():
        r_sc[...] = jnp.zeros_like(r_sc)

    cos = cos_ref[...]
    sin = sin_ref[...]

    def rope(x):
        x1 = x[:, :half]
        x2 = x[:, half:]
        return x1 * cos - x2 * sin, x1 * sin + x2 * cos

    q1, q2 = rope(q_ref[...])
    k1, k2 = rope(k_ref[...])
    scale = RET_QK_DIM ** -0.5
    qr = jnp.concatenate([q1, q2], axis=1).astype(BF16)
    kr = jnp.concatenate([k1 * scale, k2 * scale], axis=1)
    v = v_ref[...].astype(BF16)

    inner = lax.dot_general(qr, kr.astype(BF16), (((1,), (1,)), ((), ())), preferred_element_type=F32)
    inner = inner * din_ref[0]
    r = r_sc[...]
    cross = jnp.dot(qr, r.astype(BF16), preferred_element_type=F32)
    xi = xi_ref[0]
    o = jnp.dot(inner.astype(BF16), v, preferred_element_type=F32)
    o = o + cross * jnp.concatenate([xi] * (RET_V_DIM // LANES), axis=1)

    zeta = zeta_ref[0]
    kz = (kr * jnp.concatenate([zeta] * (RET_QK_DIM // LANES), axis=1)).astype(BF16)
    upd = lax.dot_general(kz, v, (((0,), (0,)), ((), ())), preferred_element_type=F32)
    r_sc[...] = cd_ref[0][:1, :1] * r + upd

    mu = jnp.mean(o, axis=-1, keepdims=True)
    oc = o - mu
    var = jnp.mean(oc * oc, axis=-1, keepdims=True)
    y = oc * lax.rsqrt(var + 1e-5) * gng_ref[...] + gnb_ref[...]
    gate = g_ref[...]
    o_ref[...] = (gate * jax.nn.sigmoid(gate) * y).astype(o_ref.dtype)


def retention(h1, cos, sin, gn_g, gn_b):
    s = h1.shape[0]
    hh, dk, dv, c = RET_HEADS, RET_QK_DIM, RET_V_DIM, RET_CHUNK
    din, xi, zeta, cd = (jnp.asarray(t) for t in _retention_tables())
    k_blk0 = hh * dk // dk
    v_blk0 = 2 * hh * dk // dv
    g_blk0 = v_blk0 + hh
    per_head = lambda r, w: pl.BlockSpec((1, r, w), lambda h, n: (h, 0, 0))
    return pl.pallas_call(
        _retention_kernel,
        grid=(hh, s // c),
        in_specs=[
            pl.BlockSpec((c, dk), lambda h, n: (n, h)),
            pl.BlockSpec((c, dk), lambda h, n: (n, k_blk0 + h)),
            pl.BlockSpec((c, dv), lambda h, n: (n, v_blk0 + h)),
            pl.BlockSpec((c, dv), lambda h, n: (n, g_blk0 + h)),
            pl.BlockSpec((c, dk // 2), lambda h, n: (n, 0)),
            pl.BlockSpec((c, dk // 2), lambda h, n: (n, 0)),
            per_head(c, c), per_head(c, LANES), per_head(c, LANES), per_head(8, LANES),
            pl.BlockSpec((1, dv), lambda h, n: (0, h)),
            pl.BlockSpec((1, dv), lambda h, n: (0, h)),
        ],
        out_specs=pl.BlockSpec((c, dv), lambda h, n: (n, h)),
        out_shape=jax.ShapeDtypeStruct((s, hh * dv), BF16),
        scratch_shapes=[pltpu.VMEM((dk, dv), F32)],
        compiler_params=_params("parallel", "arbitrary"),
        name="retention",
    )(h1, h1, h1, h1, cos, sin, din, xi, zeta, cd, gn_g.reshape(1, -1), gn_b.reshape(1, -1))


def _first_argmax(v, idx, n):
    m = jnp.max(v, axis=0, keepdims=True)
    first = jnp.min(jnp.where(v == m, idx, n), axis=0, keepdims=True)
    return m, first


def _router_kernel(x_ref, rwt_ref, rb_ref, e_ref, g_ref):
    tm = x_ref.shape[0]
    epg = EXPERTS_PER_GROUP
    logits = lax.dot_general(rwt_ref[...], x_ref[...], (((1,), (1,)), ((), ())),
                             precision=lax.Precision.HIGHEST, preferred_element_type=F32)
    scores = jax.nn.sigmoid(logits)
    biased = scores + rb_ref[...]
    idx = lax.broadcasted_iota(jnp.int32, (epg, tm), 0)

    group_scores = []
    for g in range(N_GROUPS):
        v = biased[g * epg:(g + 1) * epg]
        m1, first = _first_argmax(v, idx, epg)
        m2 = jnp.max(jnp.where(idx == first, -jnp.inf, v), axis=0, keepdims=True)
        group_scores.append(m1 + m2)
    gmax = group_scores[0]
    for g in range(1, N_GROUPS):
        gmax = jnp.maximum(gmax, group_scores[g])
    gsel = jnp.full((1, tm), N_GROUPS, jnp.int32)
    for g in range(N_GROUPS - 1, -1, -1):
        gsel = jnp.where(group_scores[g] == gmax, g, gsel)

    in_biased = jnp.zeros((epg, tm), F32)
    in_scores = jnp.zeros((epg, tm), F32)
    for g in range(N_GROUPS):
        pick = gsel == g
        in_biased = jnp.where(pick, biased[g * epg:(g + 1) * epg], in_biased)
        in_scores = jnp.where(pick, scores[g * epg:(g + 1) * epg], in_scores)
    _, loc1 = _first_argmax(in_biased, idx, epg)
    _, loc2 = _first_argmax(jnp.where(idx == loc1, -jnp.inf, in_biased), idx, epg)
    s1 = jnp.sum(jnp.where(idx == loc1, in_scores, 0.0), axis=0, keepdims=True)
    s2 = jnp.sum(jnp.where(idx == loc2, in_scores, 0.0), axis=0, keepdims=True)
    denom = s1 + s2
    e_ref[...] = jnp.concatenate([gsel * epg + loc1, gsel * epg + loc2], axis=0)
    g_ref[...] = jnp.concatenate([s1 / denom, s2 / denom], axis=0)


def moe_router(x, router_w, router_b, tm=512):
    s, d = x.shape
    return pl.pallas_call(
        _router_kernel,
        grid=(s // tm,),
        in_specs=[
            pl.BlockSpec((tm, d), lambda i: (i, 0)),
            pl.BlockSpec((N_EXPERTS, d), lambda i: (0, 0)),
            pl.BlockSpec((N_EXPERTS, 1), lambda i: (0, 0)),
        ],
        out_specs=[pl.BlockSpec((TOP_K, tm), lambda i: (0, i)), pl.BlockSpec((TOP_K, tm), lambda i: (0, i))],
        out_shape=[jax.ShapeDtypeStruct((TOP_K, s), jnp.int32), jax.ShapeDtypeStruct((TOP_K, s), F32)],
        compiler_params=_params("parallel"),
        name="moe_router",
    )(x, router_w.T, router_b.reshape(N_EXPERTS, 1))


def _scatter_rows_kernel(pad_ref, pos_ref, x_ref, xs_hbm, zero_sc, sem, zsem, *, tm, n_pad):
    i = pl.program_id(0)

    def zero_copy(p):
        return pltpu.make_async_copy(zero_sc, xs_hbm.at[pl.ds(pad_ref[p], 1)], zsem)

    @pl.when(i == 0)
    def _():
        zero_sc[...] = jnp.zeros_like(zero_sc)

        def start_zero(p, carry):
            zero_copy(p).start()
            return carry

        lax.fori_loop(0, n_pad, start_zero, 0)

    def start(g, carry):
        for u in range(DMA_ISSUE_UNROLL):
            r = g * DMA_ISSUE_UNROLL + u
            for k in range(TOP_K):
                pltpu.make_async_copy(x_ref.at[pl.ds(r, 1)], xs_hbm.at[pl.ds(pos_ref[0, k, r], 1)],
                                      sem).start(priority=k % 2)
        return carry

    lax.fori_loop(0, tm // DMA_ISSUE_UNROLL, start, 0)
    for k in range(TOP_K):
        pltpu.make_async_copy(x_ref, xs_hbm.at[pl.ds(0, tm)], sem).wait()

    @pl.when(i == 0)
    def _():
        def wait_zero(p, carry):
            zero_copy(p).wait()
            return carry

        lax.fori_loop(0, n_pad, wait_zero, 0)


def scatter_rows(x, pos, pad_pos, n_rows, tm=256):
    s, d = x.shape
    nt = s // tm
    n_pad = pad_pos.shape[0]
    return pl.pallas_call(
        functools.partial(_scatter_rows_kernel, tm=tm, n_pad=n_pad),
        grid_spec=pltpu.PrefetchScalarGridSpec(
            num_scalar_prefetch=1,
            grid=(nt,),
            in_specs=[
                pl.BlockSpec((1, TOP_K, tm), lambda i, pad: (i, 0, 0), memory_space=pltpu.SMEM),
                pl.BlockSpec((tm, d), lambda i, pad: (i, 0)),
            ],
            out_specs=pl.BlockSpec(memory_space=pl.ANY),
            scratch_shapes=[pltpu.VMEM((1, d), x.dtype), pltpu.SemaphoreType.DMA(()), pltpu.SemaphoreType.DMA(())],
        ),
        out_shape=jax.ShapeDtypeStruct((n_rows, d), x.dtype),
        compiler_params=_params("arbitrary"),
        name="scatter_rows",
    )(pad_pos, pos.reshape(TOP_K, nt, tm).transpose(1, 0, 2), x)


def _expert_kernel(te_ref, rs_ref, nv_ref, na_ref, dst_ref, xs_hbm, wg_ref, wu_ref, wd_ref, y_hbm,
                   wg_sc, wu_sc, wd_sc, xbuf, ybuf, lsem, ssem, *, tm):
    j = pl.program_id(0)
    n_active = na_ref[0]
    active = j < n_active
    slot = j % 2
    new_expert = (j == 0) | (te_ref[j] != te_ref[jnp.maximum(j - 1, 0)])
    chunk_sizes = [c for c in (256, 128, 64, 32, 16, 8) if c <= tm]

    def load_tile(t, dst_slot, wait):
        n8 = ((nv_ref[t] + 7) // 8) * 8
        base = rs_ref[t]
        off = jnp.int32(0)
        for c in chunk_sizes:
            take = (n8 & c) != 0

            @pl.when(take)
            def _(off=off, c=c):
                copy = pltpu.make_async_copy(
                    xs_hbm.at[pl.ds(pl.multiple_of(base + off, 8), c)],
                    xbuf.at[dst_slot, pl.ds(pl.multiple_of(off, 8), c)], lsem.at[dst_slot])
                if wait:
                    copy.wait()
                else:
                    copy.start()

            off = off + jnp.where(take, c, 0)

    def wait_scatter(src_slot, n):
        def wait_row(r, carry):
            pltpu.make_async_copy(ybuf.at[src_slot, pl.ds(0, 1)], y_hbm.at[pl.ds(0, 1)], ssem.at[src_slot]).wait()
            return carry

        lax.fori_loop(0, n, wait_row, 0)

    @pl.when(j == 0)
    def _():
        xbuf[...] = jnp.zeros_like(xbuf)
        load_tile(0, 0, False)

    @pl.when(j + 1 < n_active)
    def _():
        load_tile(j + 1, 1 - slot, False)

    @pl.when(active & new_expert)
    def _():
        wg_sc[...] = wg_ref[0].astype(BF16)
        wu_sc[...] = wu_ref[0].astype(BF16)
        wd_sc[...] = wd_ref[0].astype(BF16)

    @pl.when(active)
    def _():
        load_tile(j, slot, True)

        @pl.when(j >= 2)
        def _():
            wait_scatter(slot, nv_ref[jnp.maximum(j - 2, 0)])

        x = xbuf[slot].astype(BF16)
        hg = jnp.dot(x, wg_sc[...], preferred_element_type=F32)
        hu = jnp.dot(x, wu_sc[...], preferred_element_type=F32)
        h = hg * jax.nn.sigmoid(hg) * hu
        ybuf[slot] = jnp.dot(h.astype(BF16), wd_sc[...], preferred_element_type=F32)

        def scatter_row(r, carry):
            pltpu.make_async_copy(ybuf.at[slot, pl.ds(r, 1)], y_hbm.at[pl.ds(dst_ref[0, 0, r], 1)],
                                  ssem.at[slot]).start()
            return carry

        lax.fori_loop(0, nv_ref[j], scatter_row, 0)

        @pl.when(j == n_active - 1)
        def _():
            wait_scatter(slot, nv_ref[j])

            @pl.when(j >= 1)
            def _():
                wait_scatter(1 - slot, nv_ref[jnp.maximum(j - 1, 0)])


def routed_experts(xs, plan, n_out_rows, w_gate, w_up, w_down, tm):
    d = xs.shape[1]
    f = w_gate.shape[2]
    nt = plan.tile_expert.shape[0]

    def dst_blk(j, te, rs, nv, na):
        return (jnp.minimum(j, na[0] - 1), 0, 0)

    def weight_blk(j, te, rs, nv, na):
        return (te[j], 0, 0)

    return pl.pallas_call(
        functools.partial(_expert_kernel, tm=tm),
        grid_spec=pltpu.PrefetchScalarGridSpec(
            num_scalar_prefetch=4,
            grid=(nt,),
            in_specs=[
                pl.BlockSpec((1, 1, tm), dst_blk, memory_space=pltpu.SMEM),
                pl.BlockSpec(memory_space=pl.ANY),
                pl.BlockSpec((1, d, f), weight_blk),
                pl.BlockSpec((1, d, f), weight_blk),
                pl.BlockSpec((1, f, d), weight_blk),
            ],
            out_specs=pl.BlockSpec(memory_space=pl.ANY),
            scratch_shapes=[pltpu.VMEM((d, f), BF16), pltpu.VMEM((d, f), BF16), pltpu.VMEM((f, d), BF16),
                            pltpu.VMEM((2, tm, d), xs.dtype), pltpu.VMEM((2, tm, d), F32),
                            pltpu.SemaphoreType.DMA((2,)), pltpu.SemaphoreType.DMA((2,))],
        ),
        out_shape=jax.ShapeDtypeStruct((n_out_rows, d), F32),
        compiler_params=_params("arbitrary"),
        name="routed_experts",
    )(plan.tile_expert, plan.tile_row_start, plan.n_valid, plan.n_active, plan.dst_tiles.reshape(nt, 1, tm),
      xs, w_gate, w_up, w_down)


def _shared_kernel(x_ref, wg_ref, wu_ref, wd_ref, o_ref):
    x = x_ref[...]
    hg = jnp.dot(x, wg_ref[...], preferred_element_type=F32)
    hu = jnp.dot(x, wu_ref[...], preferred_element_type=F32)
    h = hg * jax.nn.sigmoid(hg) * hu
    o_ref[...] = jnp.dot(h.astype(BF16), wd_ref[...], preferred_element_type=F32)


def shared_expert(x_bf, wg, wu, wd, tm=512):
    s, d = x_bf.shape
    f = wg.shape[1]
    whole = lambda r, c: pl.BlockSpec((r, c), lambda i: (0, 0))
    return pl.pallas_call(
        _shared_kernel,
        grid=(s // tm,),
        in_specs=[pl.BlockSpec((tm, d), lambda i: (i, 0)), whole(d, f), whole(d, f), whole(f, d)],
        out_specs=pl.BlockSpec((tm, d), lambda i: (i, 0)),
        out_shape=jax.ShapeDtypeStruct((s, d), F32),
        compiler_params=_params("parallel"),
        name="shared_expert",
    )(x_bf, wg.astype(BF16), wu.astype(BF16), wd.astype(BF16))


def _combine_kernel(x_ref, sh_ref, gate_ref, y0_ref, y1_ref, g_ref, b_ref, o_ref, obf_ref):
    gate = gate_ref[...]
    routed = gate[:, 0:1] * y0_ref[0] + gate[:, 1:2] * y1_ref[0]
    z = DEEPNORM_ALPHA * x_ref[...] + (routed + sh_ref[...])
    out = _layer_norm(z, g_ref[...], b_ref[...])
    o_ref[...] = out
    obf_ref[...] = out.astype(BF16)


def moe_combine(x, shared, gates, y, g, b, tm=256):
    s, d = x.shape
    assert TOP_K == 2
    row = pl.BlockSpec((tm, d), lambda i: (i, 0))
    vec = pl.BlockSpec((1, d), lambda i: (0, 0))
    return pl.pallas_call(
        _combine_kernel,
        grid=(s // tm,),
        in_specs=[
            row, row,
            pl.BlockSpec((tm, TOP_K), lambda i: (i, 0)),
            pl.BlockSpec((1, tm, d), lambda i: (0, i, 0)),
            pl.BlockSpec((1, tm, d), lambda i: (1, i, 0)),
            vec, vec,
        ],
        out_specs=[row, row],
        out_shape=[jax.ShapeDtypeStruct((s, d), F32), jax.ShapeDtypeStruct((s, d), BF16)],
        compiler_params=_params("parallel"),
        name="moe_combine",
    )(x, shared, gates.T, y, y, g.reshape(1, d), b.reshape(1, d))


def _rope_tables(pos, rot_dim, theta):
    half = rot_dim // 2
    inv = (1.0 / (theta ** (np.arange(half, dtype=np.float32) / half))).astype(np.float32)
    ang = pos.astype(F32)[:, None] * inv
    return jnp.cos(ang), jnp.sin(ang)


def _rope(x, cos, sin, rot_dim):
    half = rot_dim // 2
    x1 = x[..., :half]
    x2 = x[..., half:rot_dim]
    c = cos[:, None, :]
    s = sin[:, None, :]
    return jnp.concatenate([x1 * c - x2 * s, x1 * s + x2 * c, x[..., rot_dim:]], axis=-1)


class _RoutingPlan(NamedTuple):
    pos: jax.Array
    pad_pos: jax.Array
    n_rows: int
    tile_expert: jax.Array
    tile_row_start: jax.Array
    n_valid: jax.Array
    n_active: jax.Array
    dst_tiles: jax.Array


ROW_ALIGN = 8


def _routing_plan(experts, tm):
    k, s = experts.shape
    flat = experts.reshape(-1)
    expert_ids = jnp.arange(N_EXPERTS, dtype=jnp.int32)
    onehot = (flat[:, None] == expert_ids[None, :]).astype(jnp.int32)
    csum = jnp.cumsum(onehot, axis=0)
    rank = jnp.take_along_axis(csum, flat[:, None], axis=1)[:, 0] - 1
    sizes = csum[-1]
    padded = (sizes + ROW_ALIGN - 1) // ROW_ALIGN * ROW_ALIGN
    seg_end = jnp.cumsum(padded)
    seg_start = seg_end - padded
    pos = seg_start[flat] + rank
    n_rows = k * s + N_EXPERTS * (ROW_ALIGN - 1)

    i_pad = jnp.arange(ROW_ALIGN - 1, dtype=jnp.int32)
    is_pad = i_pad[None, :] < (padded - sizes)[:, None]
    spare = jnp.cumsum(jnp.logical_not(is_pad).reshape(-1).astype(jnp.int32)).reshape(is_pad.shape) - 1
    pad_pos = jnp.where(is_pad, (seg_start + sizes)[:, None] + i_pad[None, :], seg_end[-1] + spare).reshape(-1)

    tiles = (sizes + tm - 1) // tm
    tile_end = jnp.cumsum(tiles)
    n_active = tile_end[-1]
    n_tiles = k * s // tm + N_EXPERTS
    tile_ids = jnp.minimum(jnp.arange(n_tiles, dtype=jnp.int32), n_active - 1)
    tile_expert = jnp.searchsorted(tile_end, tile_ids, side="right").astype(jnp.int32)
    local = tile_ids - (tile_end - tiles)[tile_expert]
    tile_row_start = seg_start[tile_expert] + local * tm
    n_valid = jnp.clip(sizes[tile_expert] - local * tm, 0, tm)
    slot_of_row = jnp.zeros((n_rows,), jnp.int32).at[pos].set(jnp.arange(k * s, dtype=jnp.int32))
    rows = jnp.minimum(tile_row_start[:, None] + jnp.arange(tm, dtype=jnp.int32)[None, :], n_rows - 1)
    return _RoutingPlan(pos.reshape(k, s), pad_pos.astype(jnp.int32), n_rows, tile_expert,
                        tile_row_start.astype(jnp.int32), n_valid.astype(jnp.int32),
                        n_active.reshape(1).astype(jnp.int32), slot_of_row[rows])


def _grouped_moe(x, x_bf, router_w, router_b, w_gate, w_up, w_down, sh_gate, sh_up, sh_down, ln_g, ln_b,
                 expert_tm=256):
    s, d = x.shape
    experts, gates = moe_router(x, router_w, router_b)
    plan = _routing_plan(experts, expert_tm)
    xs = scatter_rows(x, plan.pos, plan.pad_pos, plan.n_rows)
    y = routed_experts(xs, plan, TOP_K * s, w_gate, w_up, w_down, expert_tm)
    shared = shared_expert(x_bf, sh_gate, sh_up, sh_down)
    return moe_combine(x, shared, gates, y.reshape(TOP_K, s, d), ln_g, ln_b)


def _mla_dsa_mixer(x, pos, w_in, q_norm, w_uq, kv_norm, w_ukv, w_out):
    s = x.shape[0]
    n_pad = (-IN_AB) % 512
    h0 = matmul(x, jnp.pad(w_in, ((0, 0), (0, n_pad))), F32, 1024, 512, name="l0_in_proj")
    cq, ckv, krope, dq, dkv, iq, ik, iw = jnp.split(h0[:, :IN_AB], [int(c) for c in np.cumsum(SPLITS_AB)[:-1]], axis=-1)
    cos_m, sin_m = _rope_tables(pos, MLA_ROPE, ROPE_THETA)
    cos_d, sin_d = _rope_tables(pos, DSA_ROT, ROPE_THETA)
    cos_i, sin_i = _rope_tables(pos, IDX_ROT, ROPE_THETA)

    dqk = MLA_NOPE + MLA_ROPE
    qa = matmul(h0, w_uq, F32, 1024, 512, name="mla_q_up", a_cols=(0, MLA_Q_LORA),
                norm_gain=q_norm).reshape(s, MLA_HEADS, dqk)
    kva = matmul(h0, w_ukv, F32, 1024, 512, name="mla_kv_up", a_cols=(MLA_Q_LORA // MLA_KV_LORA, MLA_KV_LORA),
                 norm_gain=kv_norm).reshape(s, MLA_HEADS, MLA_NOPE + MLA_V)
    q_pe = _rope(qa[..., MLA_NOPE:], cos_m, sin_m, MLA_ROPE)
    q_cat = jnp.concatenate([qa[..., :MLA_NOPE], q_pe, jnp.zeros((s, MLA_HEADS, 2 * LANES - dqk), F32)], axis=-1)
    q_cat = (q_cat * (dqk ** -0.5 * LOG2_E)).astype(BF16).reshape(s, MLA_HEADS * 2 * LANES)
    k_pe = _rope(krope[:, None, :], cos_m, sin_m, MLA_ROPE)[:, 0]
    k_pe = jnp.concatenate([k_pe, jnp.zeros((s, LANES - MLA_ROPE), F32)], axis=-1).astype(BF16)
    k_nope = kva[..., :MLA_NOPE].astype(BF16).reshape(s, MLA_HEADS * MLA_NOPE)
    v_a = kva[..., MLA_NOPE:].astype(BF16).reshape(s, MLA_HEADS * MLA_V)
    out_a = mla_attention(q_cat, k_nope, k_pe, v_a)

    qb = _rope(dq.reshape(s, DSA_HEADS, DSA_HEAD_DIM), cos_d, sin_d, DSA_ROT) * (DSA_HEAD_DIM ** -0.5 * LOG2_E)
    qb = qb.astype(BF16).transpose(1, 0, 2)
    kvb = dkv.reshape(s, 2, DSA_KV_HEADS, DSA_HEAD_DIM)
    kb = _rope(kvb[:, 0], cos_d, sin_d, DSA_ROT).astype(BF16).reshape(s, DSA_KV_HEADS * DSA_HEAD_DIM)
    vb = kvb[:, 1].astype(BF16).reshape(s, DSA_KV_HEADS * DSA_HEAD_DIM)
    iq = _rope(iq.reshape(s, IDX_HEADS, IDX_DIM), cos_i, sin_i, IDX_ROT).astype(BF16).transpose(1, 0, 2)
    ik = _rope(ik[:, None, :], cos_i, sin_i, IDX_ROT)[:, 0].astype(BF16)
    iw = iw * (IDX_HEADS ** -0.5 * IDX_DIM ** -0.5)
    out_b = dsa_attention(iq, iw, ik, qb, kb, vb)

    return matmul(jnp.concatenate([out_a, out_b], axis=-1), w_out, F32, 1024, 512, name="l0_out_proj")


def _retention_mixer(x_bf, pos, w_in, gn_g, gn_b, w_out):
    h1 = matmul(x_bf, w_in, F32, 1024, 1024, name="l1_in_proj")
    cos, sin = _rope_tables(pos, RET_QK_DIM, RET_THETA)
    y = retention(h1, cos, sin, gn_g, gn_b)
    return matmul(y, w_out, F32, 1024, 512, name="l1_out_proj")


def kernel(x, positions, router_w, router_b, l0_w_in, l0_mla_q_norm, l0_mla_w_uq, l0_mla_kv_norm, l0_mla_w_ukv, l0_w_out, l1_w_in, l1_ret_gn_g, l1_ret_gn_b, l1_w_out, l0_ln_mix_g, l0_ln_mix_b, l0_moe_w_gate, l0_moe_w_up, l0_moe_w_down, l0_sh_gate, l0_sh_up, l0_sh_down, l0_ln_ffn_g, l0_ln_ffn_b, l1_ln_mix_g, l1_ln_mix_b, l1_moe_w_gate, l1_moe_w_up, l1_moe_w_down, l1_sh_gate, l1_sh_up, l1_sh_down, l1_ln_ffn_g, l1_ln_ffn_b):
    assert x.shape[0] == 1
    xt = x[0]
    pos = positions[0]

    mix = _mla_dsa_mixer(xt, pos, l0_w_in, l0_mla_q_norm, l0_mla_w_uq, l0_mla_kv_norm, l0_mla_w_ukv, l0_w_out)
    xt, xt_bf = add_layer_norm(xt, mix, l0_ln_mix_g, l0_ln_mix_b)
    xt, xt_bf = _grouped_moe(xt, xt_bf, router_w, router_b, l0_moe_w_gate, l0_moe_w_up, l0_moe_w_down,
                             l0_sh_gate, l0_sh_up, l0_sh_down, l0_ln_ffn_g, l0_ln_ffn_b)

    mix = _retention_mixer(xt_bf, pos, l1_w_in, l1_ret_gn_g, l1_ret_gn_b, l1_w_out)
    xt, xt_bf = add_layer_norm(xt, mix, l1_ln_mix_g, l1_ln_mix_b)
    xt, _ = _grouped_moe(xt, xt_bf, router_w, router_b, l1_moe_w_gate, l1_moe_w_up, l1_moe_w_down,
                         l1_sh_gate, l1_sh_up, l1_sh_down, l1_ln_ffn_g, l1_ln_ffn_b)
    return xt[None]
```

```python
import functools
from typing import NamedTuple

import numpy as np
import jax
import jax.numpy as jnp
from jax import lax
from jax.experimental import pallas as pl
from jax.experimental.pallas import tpu as pltpu

F32 = jnp.float32
BF16 = jnp.bfloat16

D_MODEL = 2048
DEPTH = 2
ROPE_THETA = 500000.0
MLA_HEADS = 8
MLA_Q_LORA = 512
MLA_KV_LORA = 256
MLA_NOPE = 128
MLA_ROPE = 64
MLA_V = 128
DSA_HEADS = 8
DSA_KV_HEADS = 2
DSA_HEAD_DIM = 128
DSA_ROT = DSA_HEAD_DIM // 4
IDX_HEADS = 16
IDX_DIM = 64
IDX_ROT = IDX_DIM // 4
IDX_TOPK_MAX = 256
RET_HEADS = 8
RET_QK_DIM = 256
RET_V_DIM = 512
RET_CHUNK = 128
RET_THETA = 10000.0
N_EXPERTS = 64
N_GROUPS = 8
EXPERTS_PER_GROUP = N_EXPERTS // N_GROUPS
TOP_K = 2
D_EXPERT = 512
D_SHARED = 1024
DEEPNORM_ALPHA = (2.0 * DEPTH) ** 0.25

SPLITS_AB = (MLA_Q_LORA, MLA_KV_LORA, MLA_ROPE, DSA_HEADS * DSA_HEAD_DIM,
             2 * DSA_KV_HEADS * DSA_HEAD_DIM, IDX_HEADS * IDX_DIM, IDX_DIM, IDX_HEADS)
IN_AB = sum(SPLITS_AB)

VMEM_LIMIT_BYTES = 56 * 1024 * 1024
LANES = 128
MASKED_SCORE = -1e30
LOG2_E = 1.4426950408889634
DMA_ISSUE_UNROLL = 8
INT32_MIN = -2 ** 31


def _params(*sem):
    return pltpu.CompilerParams(dimension_semantics=sem, vmem_limit_bytes=VMEM_LIMIT_BYTES)


def _mm_kernel(*refs, norm_eps):
    if norm_eps is None:
        a_ref, b_ref, o_ref = refs
        a = a_ref[...]
    else:
        a_ref, g_ref, b_ref, o_ref = refs
        af = a_ref[...].astype(F32)
        a = af * lax.rsqrt(jnp.mean(af * af, axis=-1, keepdims=True) + norm_eps) * g_ref[...]
    o_ref[...] = jnp.dot(a.astype(BF16), b_ref[...].astype(BF16),
                         preferred_element_type=F32).astype(o_ref.dtype)


def matmul(a, b, out_dtype, tm, tn, *, name, a_cols=None, norm_gain=None, norm_eps=1e-6):
    m = a.shape[0]
    k, n = b.shape
    col_blk = 0 if a_cols is None else a_cols[0]
    assert (a.shape[1] == k) if a_cols is None else (a_cols[1] == k)
    assert m % tm == 0 and n % tn == 0
    in_specs = [pl.BlockSpec((tm, k), lambda i, j: (i, col_blk))]
    args = [a]
    if norm_gain is not None:
        in_specs.append(pl.BlockSpec((1, k), lambda i, j: (0, 0)))
        args.append(norm_gain.reshape(1, k))
    in_specs.append(pl.BlockSpec((k, tn), lambda i, j: (0, j)))
    args.append(b)
    return pl.pallas_call(
        functools.partial(_mm_kernel, norm_eps=None if norm_gain is None else norm_eps),
        grid=(m // tm, n // tn),
        in_specs=in_specs,
        out_specs=pl.BlockSpec((tm, tn), lambda i, j: (i, j)),
        out_shape=jax.ShapeDtypeStruct((m, n), out_dtype),
        compiler_params=_params("parallel", "parallel"),
        name=name,
    )(*args)


def _mm2_kernel(a1_ref, a2_ref, b1_ref, b2_ref, o_ref):
    acc = jnp.dot(a1_ref[...].astype(BF16), b1_ref[...].astype(BF16), preferred_element_type=F32)
    acc = acc + jnp.dot(a2_ref[...].astype(BF16), b2_ref[...].astype(BF16), preferred_element_type=F32)
    o_ref[...] = acc.astype(o_ref.dtype)


def matmul_concat(a1, a2, b, out_dtype, tm, tn, *, name):
    m, k1 = a1.shape
    n = b.shape[1]
    assert a2.shape == (m, k1) and b.shape[0] == 2 * k1 and m % tm == 0 and n % tn == 0
    a_spec = pl.BlockSpec((tm, k1), lambda i, j: (i, 0))
    return pl.pallas_call(
        _mm2_kernel,
        grid=(m // tm, n // tn),
        in_specs=[a_spec, a_spec, pl.BlockSpec((k1, tn), lambda i, j: (0, j)), pl.BlockSpec((k1, tn), lambda i, j: (1, j))],
        out_specs=pl.BlockSpec((tm, tn), lambda i, j: (i, j)),
        out_shape=jax.ShapeDtypeStruct((m, n), out_dtype),
        compiler_params=_params("parallel", "parallel"),
        name=name,
    )(a1, a2, b, b)


_H0_CQ = 0
_H0_CKV = _H0_CQ + MLA_Q_LORA
_H0_DQ = _H0_CKV + MLA_KV_LORA
_H0_DKV = _H0_DQ + DSA_HEADS * DSA_HEAD_DIM
_H0_IQ = _H0_DKV + 2 * DSA_KV_HEADS * DSA_HEAD_DIM
_H0_KROPE_IK = _H0_IQ + IDX_HEADS * IDX_DIM
_H0_IW = _H0_KROPE_IK + LANES
_H0_WIDTH = _H0_IW + LANES


def _rope_lanes(x, tab_ref, shifts):
    out = x * tab_ref[0]
    for i, shift in enumerate(shifts):
        out = out + pltpu.roll(x, shift, 1) * tab_ref[1 + i]
    return out


def _l0_prep_kernel(h_ref, qn_ref, kvn_ref, wuq_ref, wukv_ref, tm_ref, td_ref, ti_ref, tki_ref,
                    qcat_ref, knope_ref, va_ref, kpe_ref, qb_ref, kb_ref, vb_ref, iq_ref, iklo_ref, ikhi_ref,
                    iw_ref):
    def rms(x, g_ref):
        return (x * lax.rsqrt(jnp.mean(x * x, axis=-1, keepdims=True) + 1e-6) * g_ref[...]).astype(BF16)

    def slab(col, j=0):
        return h_ref[:, col + j * LANES:col + (j + 1) * LANES]

    half_m, half_d, half_i = MLA_ROPE // 2, DSA_ROT // 2, IDX_ROT // 2
    shifts_m = (half_m, LANES - half_m)
    shifts_d = (half_d, LANES - half_d)
    shifts_i = (half_i, LANES - half_i)

    qa = jnp.dot(rms(h_ref[:, _H0_CQ:_H0_CQ + MLA_Q_LORA], qn_ref), wuq_ref[...], preferred_element_type=F32)
    kva = jnp.dot(rms(h_ref[:, _H0_CKV:_H0_CKV + MLA_KV_LORA], kvn_ref), wukv_ref[...],
                  preferred_element_type=F32)
    n_nope = MLA_HEADS * MLA_NOPE
    knope_ref[...] = kva[:, :n_nope].astype(BF16)
    va_ref[...] = kva[:, n_nope:].astype(BF16)
    q_scale = (MLA_NOPE + MLA_ROPE) ** -0.5 * LOG2_E
    for h in range(MLA_HEADS):
        nope = qa[:, h * LANES:(h + 1) * LANES]
        pe = qa[:, n_nope + h * LANES:n_nope + (h + 1) * LANES]
        qcat_ref[:, 2 * h * LANES:(2 * h + 1) * LANES] = (nope * q_scale).astype(BF16)
        qcat_ref[:, (2 * h + 1) * LANES:(2 * h + 2) * LANES] = (_rope_lanes(pe, tm_ref, shifts_m) * q_scale).astype(BF16)

    d_scale = DSA_HEAD_DIM ** -0.5 * LOG2_E
    for h in range(DSA_HEADS):
        qb_ref[:, h * LANES:(h + 1) * LANES] = (_rope_lanes(slab(_H0_DQ, h), td_ref, shifts_d) * d_scale).astype(BF16)
    for g in range(DSA_KV_HEADS):
        kb_ref[:, g * LANES:(g + 1) * LANES] = _rope_lanes(slab(_H0_DKV, g), td_ref, shifts_d).astype(BF16)
    vb_ref[...] = h_ref[:, _H0_DKV + DSA_KV_HEADS * LANES:_H0_DKV + 2 * DSA_KV_HEADS * LANES].astype(BF16)
    for j in range(IDX_HEADS * IDX_DIM // LANES):
        iq_ref[:, j * LANES:(j + 1) * LANES] = _rope_lanes(slab(_H0_IQ, j), ti_ref, shifts_i).astype(BF16)

    ki = _rope_lanes(slab(_H0_KROPE_IK), tki_ref, shifts_m + shifts_i)
    lane = lax.broadcasted_iota(jnp.int32, ki.shape, 1)
    kpe_ref[...] = jnp.where(lane < MLA_ROPE, ki, 0.0).astype(BF16)
    ik_hi = jnp.where(lane >= MLA_ROPE, ki, 0.0)
    ikhi_ref[...] = ik_hi.astype(BF16)
    iklo_ref[...] = pltpu.roll(ik_hi, LANES - MLA_ROPE, 1).astype(BF16)
    iw_ref[...] = h_ref[:, _H0_IW:_H0_IW + IDX_HEADS] * (IDX_HEADS ** -0.5 * IDX_DIM ** -0.5)


def l0_attention_operands(h0, q_norm, w_uq, kv_norm, w_ukv, tables, tm=256):
    s = h0.shape[0]
    bf = lambda n: jax.ShapeDtypeStruct((s, n), BF16)
    row = lambda n: pl.BlockSpec((tm, n), lambda i: (i, 0))
    whole = lambda a: pl.BlockSpec(a.shape, lambda i: (0,) * a.ndim)
    tab = lambda t: pl.BlockSpec((t.shape[0], tm, LANES), lambda i: (0, i, 0))
    widths = [2 * MLA_HEADS * LANES, MLA_HEADS * MLA_NOPE, MLA_HEADS * MLA_V, LANES, DSA_HEADS * DSA_HEAD_DIM,
              DSA_KV_HEADS * DSA_HEAD_DIM, DSA_KV_HEADS * DSA_HEAD_DIM, IDX_HEADS * IDX_DIM, LANES, LANES]
    w_uq = w_uq.astype(BF16)
    w_ukv = w_ukv.astype(BF16)
    q_norm = q_norm.reshape(1, -1)
    kv_norm = kv_norm.reshape(1, -1)
    return pl.pallas_call(
        _l0_prep_kernel,
        grid=(s // tm,),
        in_specs=[row(_H0_WIDTH), whole(q_norm), whole(kv_norm), whole(w_uq), whole(w_ukv)] + [tab(t) for t in tables],
        out_specs=[row(n) for n in widths] + [row(IDX_HEADS)],
        out_shape=[bf(n) for n in widths] + [jax.ShapeDtypeStruct((s, IDX_HEADS), F32)],
        compiler_params=_params("parallel"),
        name="l0_attention_operands",
    )(h0, q_norm, kv_norm, w_uq, w_ukv, *tables)


def _layer_norm(z, g, b):
    mu = jnp.mean(z, axis=-1, keepdims=True)
    zc = z - mu
    var = jnp.mean(zc * zc, axis=-1, keepdims=True)
    return zc * lax.rsqrt(var + 1e-5) * g + b


def _add_ln_kernel(x_ref, y_ref, g_ref, b_ref, o_ref, obf_ref):
    z = DEEPNORM_ALPHA * x_ref[...] + y_ref[...]
    out = _layer_norm(z, g_ref[...], b_ref[...])
    o_ref[...] = out
    obf_ref[...] = out.astype(BF16)


def add_layer_norm(x, y, g, b, tm=256):
    s, d = x.shape
    row = pl.BlockSpec((tm, d), lambda i: (i, 0))
    vec = pl.BlockSpec((1, d), lambda i: (0, 0))
    return pl.pallas_call(
        _add_ln_kernel,
        grid=(s // tm,),
        in_specs=[row, row, vec, vec],
        out_specs=[row, row],
        out_shape=[jax.ShapeDtypeStruct((s, d), F32), jax.ShapeDtypeStruct((s, d), BF16)],
        compiler_params=_params("parallel"),
        name="add_layer_norm",
    )(x, y, g.reshape(1, d), b.reshape(1, d))


def _lane_tile(x, n):
    return jnp.tile(x, (1, n))


def _softmax_step(s, v_ext, m_ref, acc_ref):
    tk = s.shape[1]
    m_prev = m_ref[...]
    m_new = jnp.maximum(m_prev, jnp.max(s, axis=1)[:, None])
    p = jnp.exp2(s - _lane_tile(m_new, tk // LANES))
    alpha = jnp.exp2(m_prev - m_new)
    pv = jnp.dot(p.astype(BF16), v_ext, preferred_element_type=F32)
    acc_ref[...] = _lane_tile(alpha, 2) * acc_ref[...] + pv
    m_ref[...] = m_new


def _mla_kernel(q_ref, kn_ref, kp_ref, v_ref, o_ref, m_sc, acc_sc, *, tq, tk, hp):
    qi = pl.program_id(1)
    m_sc[...] = jnp.full_like(m_sc, MASKED_SCORE)
    acc_sc[...] = jnp.zeros_like(acc_sc)
    ones = jnp.ones((tk, LANES), BF16)
    per_block = tq // tk

    def attend(c, diagonal_offset):
        kp = kp_ref[c]
        kn = kn_ref[c]
        v = v_ref[c]
        for j in range(hp):
            q = q_ref[:, j * 2 * LANES:(j + 1) * 2 * LANES]
            k = jnp.concatenate([kn[:, j * LANES:(j + 1) * LANES], kp], axis=1)
            s = lax.dot_general(q, k, (((1,), (1,)), ((), ())), preferred_element_type=F32)
            if diagonal_offset is not None:
                row = lax.broadcasted_iota(jnp.int32, (tq, tk), 0)
                col = diagonal_offset * tk + lax.broadcasted_iota(jnp.int32, (tq, tk), 1)
                s = jnp.where(col <= row, s, MASKED_SCORE)
            v_ext = jnp.concatenate([v[:, j * LANES:(j + 1) * LANES], ones], axis=1)
            _softmax_step(s, v_ext, m_sc.at[j], acc_sc.at[j])

    def below_diagonal(c, carry):
        attend(c, None)
        return carry

    lax.fori_loop(0, qi * per_block, below_diagonal, 0)
    for d in range(per_block):
        attend(qi * per_block + d, d)
    for j in range(hp):
        acc = acc_sc[j]
        o_ref[:, j * LANES:(j + 1) * LANES] = (acc[:, :LANES] / acc[:, LANES:]).astype(o_ref.dtype)


def mla_attention(q, k_nope, k_pe, v, tq=512, tk=512, hp=2):
    s = q.shape[0]
    h = MLA_HEADS
    nc = s // tk
    assert tq % tk == 0 and h % hp == 0
    per_head = pl.BlockSpec((nc, tk, hp * LANES), lambda hh, qi: (0, 0, hh))
    return pl.pallas_call(
        functools.partial(_mla_kernel, tq=tq, tk=tk, hp=hp),
        grid=(h // hp, s // tq),
        in_specs=[
            pl.BlockSpec((tq, hp * 2 * LANES), lambda hh, qi: (qi, hh)),
            per_head,
            pl.BlockSpec((nc, tk, LANES), lambda hh, qi: (0, 0, 0)),
            per_head,
        ],
        out_specs=pl.BlockSpec((tq, hp * LANES), lambda hh, qi: (qi, hh)),
        out_shape=jax.ShapeDtypeStruct((s, h * MLA_V), BF16),
        scratch_shapes=[pltpu.VMEM((hp, tq, LANES), F32), pltpu.VMEM((hp, tq, 2 * LANES), F32)],
        compiler_params=_params("parallel", "arbitrary"),
        name="mla_attention",
    )(q, k_nope.reshape(nc, tk, h * MLA_NOPE), k_pe.reshape(nc, tk, LANES), v.reshape(nc, tk, h * MLA_V))


def _sortable_key(x):
    bits = pltpu.bitcast(x, jnp.int32)
    return bits ^ ((bits >> 31) & 0x7FFFFFFF)


def _dsa_kernel(iq_ref, iw_ref, iklo_ref, ikhi_ref, q_ref, k_ref, v_ref, o_ref, key_sc, wb_sc, m_sc, acc_sc,
                *, tq, tk, n_sel):
    i = pl.program_id(0)
    n_chunks = ((i + 1) * tq + tk - 1) // tk
    rep = DSA_HEADS // DSA_KV_HEADS
    lane_tiles = tk // LANES
    heads_per_block = LANES // IDX_DIM

    assert heads_per_block == 2
    iq_blocks = jnp.concatenate(
        [iq_ref[:, j * LANES:(j + 1) * LANES] for j in range(IDX_HEADS // heads_per_block)], axis=0)
    iw = iw_ref[...]
    for h in range(IDX_HEADS):
        wb_sc[h] = jnp.broadcast_to(iw[:, h:h + 1], (tq, LANES))

    def index_chunk(c, carry):
        ik = jnp.concatenate([iklo_ref[c], ikhi_ref[c]], axis=0)
        d = lax.dot_general(iq_blocks, ik, (((1,), (1,)), ((), ())), preferred_element_type=F32)
        d = jnp.maximum(d, 0.0)
        score = None
        for h in range(IDX_HEADS):
            j, part = divmod(h, heads_per_block)
            term = _lane_tile(wb_sc[h], lane_tiles) * d[j * tq:(j + 1) * tq, part * tk:(part + 1) * tk]
            score = term if score is None else score + term
        key_sc[c] = _sortable_key(score)
        return carry

    lax.fori_loop(0, n_chunks, index_chunk, 0)
    last = n_chunks - 1
    row = i * tq + lax.broadcasted_iota(jnp.int32, (tq, tk), 0)
    col = last * tk + lax.broadcasted_iota(jnp.int32, (tq, tk), 1)
    key_sc[last] = jnp.where(col <= row, key_sc[last], INT32_MIN)

    def search_bit(b, thr):
        cand = thr + lax.shift_left(jnp.int32(1), 31 - b)

        def count_chunk(c, cnt):
            key = key_sc[c]
            for j in range(lane_tiles):
                cnt = cnt + jnp.where(key[:, j * LANES:(j + 1) * LANES] >= cand, 1.0, 0.0)
            return cnt

        cnt = lax.fori_loop(0, n_chunks, count_chunk, jnp.zeros((tq, LANES), F32))
        return jnp.where(jnp.sum(cnt, axis=1)[:, None] >= n_sel, cand, thr)

    thr = lax.fori_loop(0, 32, search_bit, jnp.full((tq, LANES), INT32_MIN, jnp.int32))
    thr = _lane_tile(jnp.maximum(thr, INT32_MIN + 1), lane_tiles)

    m_sc[...] = jnp.full_like(m_sc, MASKED_SCORE)
    acc_sc[...] = jnp.zeros_like(acc_sc)
    ones = jnp.ones((tk, LANES), BF16)

    def attend_chunk(c, carry):
        bias = jnp.where(key_sc[c] >= thr, 0.0, MASKED_SCORE)
        kc = k_ref[c]
        vc = v_ref[c]
        for g in range(DSA_KV_HEADS):
            cols = slice(g * DSA_HEAD_DIM, (g + 1) * DSA_HEAD_DIM)
            qg = jnp.concatenate(
                [q_ref[:, (g * rep + r) * DSA_HEAD_DIM:(g * rep + r + 1) * DSA_HEAD_DIM] for r in range(rep)], axis=0)
            s = lax.dot_general(qg, kc[:, cols], (((1,), (1,)), ((), ())), preferred_element_type=F32)
            s = (s.reshape(rep, tq, tk) + bias[None]).reshape(rep * tq, tk)
            _softmax_step(s, jnp.concatenate([vc[:, cols], ones], axis=1), m_sc.at[g], acc_sc.at[g])
        return carry

    lax.fori_loop(0, n_chunks, attend_chunk, 0)
    for g in range(DSA_KV_HEADS):
        acc = acc_sc[g]
        out = acc[:, :LANES] / acc[:, LANES:]
        for r in range(rep):
            hh = g * rep + r
            o_ref[:, hh * DSA_HEAD_DIM:(hh + 1) * DSA_HEAD_DIM] = out[r * tq:(r + 1) * tq].astype(o_ref.dtype)


def dsa_attention(iq, iw, ik_lo, ik_hi, q, k, v, tq=128, tk=512):
    s = q.shape[0]
    n_sel = min(IDX_TOPK_MAX, s // 4)
    nc = s // tk
    assert tk >= n_sel and s % tk == 0 and tk % tq == 0 and DSA_HEAD_DIM == LANES
    rep = DSA_HEADS // DSA_KV_HEADS
    kvw = DSA_KV_HEADS * DSA_HEAD_DIM
    whole3 = lambda i: (0, 0, 0)
    return pl.pallas_call(
        functools.partial(_dsa_kernel, tq=tq, tk=tk, n_sel=n_sel),
        grid=(s // tq,),
        in_specs=[
            pl.BlockSpec((tq, IDX_HEADS * IDX_DIM), lambda i: (i, 0)),
            pl.BlockSpec((tq, IDX_HEADS), lambda i: (i, 0)),
            pl.BlockSpec((nc, tk, LANES), whole3),
            pl.BlockSpec((nc, tk, LANES), whole3),
            pl.BlockSpec((tq, DSA_HEADS * DSA_HEAD_DIM), lambda i: (i, 0)),
            pl.BlockSpec((nc, tk, kvw), whole3),
            pl.BlockSpec((nc, tk, kvw), whole3),
        ],
        out_specs=pl.BlockSpec((tq, DSA_HEADS * DSA_HEAD_DIM), lambda i: (i, 0)),
        out_shape=jax.ShapeDtypeStruct((s, DSA_HEADS * DSA_HEAD_DIM), BF16),
        scratch_shapes=[
            pltpu.VMEM((nc, tq, tk), jnp.int32),
            pltpu.VMEM((IDX_HEADS, tq, LANES), F32),
            pltpu.VMEM((DSA_KV_HEADS, rep * tq, LANES), F32),
            pltpu.VMEM((DSA_KV_HEADS, rep * tq, 2 * LANES), F32),
        ],
        compiler_params=_params("parallel"),
        name="dsa_attention",
    )(iq, iw, ik_lo.reshape(nc, tk, LANES), ik_hi.reshape(nc, tk, LANES), q, k.reshape(nc, tk, kvw),
      v.reshape(nc, tk, kvw))


def _retention_tables():
    h, c = RET_HEADS, RET_CHUNK
    log_g = np.log(1.0 - 2.0 ** (-5.0 - np.arange(h, dtype=np.float32))).astype(np.float32).astype(np.float64)
    idx = np.arange(c, dtype=np.float64)
    diff = idx[:, None] - idx[None, :]
    decay_in = np.where(diff[None] >= 0, np.exp(np.maximum(diff, 0.0)[None] * log_g[:, None, None]), 0.0)
    xi = np.exp((idx + 1.0)[None, :] * log_g[:, None])
    zeta = np.exp((c - 1.0 - idx)[None, :] * log_g[:, None])
    chunk_decay = np.exp(c * log_g)
    lane = np.ones((1, 1, LANES))
    return (decay_in.astype(np.float32), (xi[:, :, None] * lane).astype(np.float32),
            (zeta[:, :, None] * lane).astype(np.float32),
            (chunk_decay[:, None, None] * np.ones((1, 8, LANES))).astype(np.float32))


def _retention_kernel(q_ref, k_ref, v_ref, g_ref, cos_ref, sin_ref, din_ref, xi_ref, zeta_ref, cd_ref,
                      gng_ref, gnb_ref, o_ref, r_sc):
    n = pl.program_id(1)
    half = RET_QK_DIM // 2

    @pl.when(n == 0)
    def _():
        r_sc[...] = jnp.zeros_like(r_sc)

    cos = cos_ref[...]
    sin = sin_ref[...]

    def rope(x):
        x1 = x[:, :half]
        x2 = x[:, half:]
        return x1 * cos - x2 * sin, x1 * sin + x2 * cos

    q1, q2 = rope(q_ref[...])
    k1, k2 = rope(k_ref[...])
    scale = RET_QK_DIM ** -0.5
    qr = jnp.concatenate([q1, q2], axis=1).astype(BF16)
    kr = jnp.concatenate([k1 * scale, k2 * scale], axis=1)
    v = v_ref[...].astype(BF16)

    inner = lax.dot_general(qr, kr.astype(BF16), (((1,), (1,)), ((), ())), preferred_element_type=F32)
    inner = inner * din_ref[0]
    r = r_sc[...]
    cross = jnp.dot(qr, r.astype(BF16), preferred_element_type=F32)
    xi = xi_ref[0]
    o = jnp.dot(inner.astype(BF16), v, preferred_element_type=F32)
    o = o + cross * jnp.concatenate([xi] * (RET_V_DIM // LANES), axis=1)

    zeta = zeta_ref[0]
    kz = (kr * jnp.concatenate([zeta] * (RET_QK_DIM // LANES), axis=1)).astype(BF16)
    upd = lax.dot_general(kz, v, (((0,), (0,)), ((), ())), preferred_element_type=F32)
    r_sc[...] = cd_ref[0][:1, :1] * r + upd

    mu = jnp.mean(o, axis=-1, keepdims=True)
    oc = o - mu
    var = jnp.mean(oc * oc, axis=-1, keepdims=True)
    y = oc * lax.rsqrt(var + 1e-5) * gng_ref[...] + gnb_ref[...]
    gate = g_ref[...]
    o_ref[...] = (gate * jax.nn.sigmoid(gate) * y).astype(o_ref.dtype)


def retention(h1, cos, sin, gn_g, gn_b):
    s = h1.shape[0]
    hh, dk, dv, c = RET_HEADS, RET_QK_DIM, RET_V_DIM, RET_CHUNK
    din, xi, zeta, cd = (jnp.asarray(t) for t in _retention_tables())
    k_blk0 = hh * dk // dk
    v_blk0 = 2 * hh * dk // dv
    g_blk0 = v_blk0 + hh
    per_head = lambda r, w: pl.BlockSpec((1, r, w), lambda h, n: (h, 0, 0))
    return pl.pallas_call(
        _retention_kernel,
        grid=(hh, s // c),
        in_specs=[
            pl.BlockSpec((c, dk), lambda h, n: (n, h)),
            pl.BlockSpec((c, dk), lambda h, n: (n, k_blk0 + h)),
            pl.BlockSpec((c, dv), lambda h, n: (n, v_blk0 + h)),
            pl.BlockSpec((c, dv), lambda h, n: (n, g_blk0 + h)),
            pl.BlockSpec((c, dk // 2), lambda h, n: (n, 0)),
            pl.BlockSpec((c, dk // 2), lambda h, n: (n, 0)),
            per_head(c, c), per_head(c, LANES), per_head(c, LANES), per_head(8, LANES),
            pl.BlockSpec((1, dv), lambda h, n: (0, h)),
            pl.BlockSpec((1, dv), lambda h, n: (0, h)),
        ],
        out_specs=pl.BlockSpec((c, dv), lambda h, n: (n, h)),
        out_shape=jax.ShapeDtypeStruct((s, hh * dv), BF16),
        scratch_shapes=[pltpu.VMEM((dk, dv), F32)],
        compiler_params=_params("parallel", "arbitrary"),
        name="retention",
    )(h1, h1, h1, h1, cos, sin, din, xi, zeta, cd, gn_g.reshape(1, -1), gn_b.reshape(1, -1))


def _first_argmax(v, idx, n):
    m = jnp.max(v, axis=0, keepdims=True)
    first = jnp.min(jnp.where(v == m, idx, n), axis=0, keepdims=True)
    return m, first


def _router_kernel(x_ref, rwt_ref, rb_ref, e_ref, g_ref):
    tm = x_ref.shape[0]
    epg = EXPERTS_PER_GROUP
    logits = lax.dot_general(rwt_ref[...], x_ref[...], (((1,), (1,)), ((), ())),
                             precision=lax.Precision.HIGHEST, preferred_element_type=F32)
    scores = jax.nn.sigmoid(logits)
    biased = scores + rb_ref[...]
    idx = lax.broadcasted_iota(jnp.int32, (epg, tm), 0)

    group_scores = []
    for g in range(N_GROUPS):
        v = biased[g * epg:(g + 1) * epg]
        m1, first = _first_argmax(v, idx, epg)
        m2 = jnp.max(jnp.where(idx == first, -jnp.inf, v), axis=0, keepdims=True)
        group_scores.append(m1 + m2)
    gmax = group_scores[0]
    for g in range(1, N_GROUPS):
        gmax = jnp.maximum(gmax, group_scores[g])
    gsel = jnp.full((1, tm), N_GROUPS, jnp.int32)
    for g in range(N_GROUPS - 1, -1, -1):
        gsel = jnp.where(group_scores[g] == gmax, g, gsel)

    in_biased = jnp.zeros((epg, tm), F32)
    in_scores = jnp.zeros((epg, tm), F32)
    for g in range(N_GROUPS):
        pick = gsel == g
        in_biased = jnp.where(pick, biased[g * epg:(g + 1) * epg], in_biased)
        in_scores = jnp.where(pick, scores[g * epg:(g + 1) * epg], in_scores)
    _, loc1 = _first_argmax(in_biased, idx, epg)
    _, loc2 = _first_argmax(jnp.where(idx == loc1, -jnp.inf, in_biased), idx, epg)
    s1 = jnp.sum(jnp.where(idx == loc1, in_scores, 0.0), axis=0, keepdims=True)
    s2 = jnp.sum(jnp.where(idx == loc2, in_scores, 0.0), axis=0, keepdims=True)
    denom = s1 + s2
    e_ref[...] = jnp.concatenate([gsel * epg + loc1, gsel * epg + loc2], axis=0)
    g_ref[...] = jnp.concatenate([s1 / denom, s2 / denom], axis=0)


def moe_router(x, router_w, router_b, tm=512):
    s, d = x.shape
    return pl.pallas_call(
        _router_kernel,
        grid=(s // tm,),
        in_specs=[
            pl.BlockSpec((tm, d), lambda i: (i, 0)),
            pl.BlockSpec((N_EXPERTS, d), lambda i: (0, 0)),
            pl.BlockSpec((N_EXPERTS, 1), lambda i: (0, 0)),
        ],
        out_specs=[pl.BlockSpec((TOP_K, tm), lambda i: (0, i)), pl.BlockSpec((TOP_K, tm), lambda i: (0, i))],
        out_shape=[jax.ShapeDtypeStruct((TOP_K, s), jnp.int32), jax.ShapeDtypeStruct((TOP_K, s), F32)],
        compiler_params=_params("parallel"),
        name="moe_router",
    )(x, router_w.T, router_b.reshape(N_EXPERTS, 1))


def _scatter_rows_kernel(pad_ref, pos_ref, x_ref, xs_hbm, zero_sc, sem, zsem, *, tm, n_pad):
    i = pl.program_id(0)

    def zero_copy(p):
        return pltpu.make_async_copy(zero_sc, xs_hbm.at[pl.ds(pad_ref[p], 1)], zsem)

    @pl.when(i == 0)
    def _():
        zero_sc[...] = jnp.zeros_like(zero_sc)

        def start_zero(p, carry):
            zero_copy(p).start()
            return carry

        lax.fori_loop(0, n_pad, start_zero, 0)

    def start(g, carry):
        for u in range(DMA_ISSUE_UNROLL):
            r = g * DMA_ISSUE_UNROLL + u
            for k in range(TOP_K):
                pltpu.make_async_copy(x_ref.at[pl.ds(r, 1)], xs_hbm.at[pl.ds(pos_ref[0, k, r], 1)],
                                      sem).start(priority=k % 2)
        return carry

    lax.fori_loop(0, tm // DMA_ISSUE_UNROLL, start, 0)
    for k in range(TOP_K):
        pltpu.make_async_copy(x_ref, xs_hbm.at[pl.ds(0, tm)], sem).wait()

    @pl.when(i == 0)
    def _():
        def wait_zero(p, carry):
            zero_copy(p).wait()
            return carry

        lax.fori_loop(0, n_pad, wait_zero, 0)


def scatter_rows(x, pos, pad_pos, n_rows, tm=256):
    s, d = x.shape
    nt = s // tm
    n_pad = pad_pos.shape[0]
    return pl.pallas_call(
        functools.partial(_scatter_rows_kernel, tm=tm, n_pad=n_pad),
        grid_spec=pltpu.PrefetchScalarGridSpec(
            num_scalar_prefetch=1,
            grid=(nt,),
            in_specs=[
                pl.BlockSpec((1, TOP_K, tm), lambda i, pad: (i, 0, 0), memory_space=pltpu.SMEM),
                pl.BlockSpec((tm, d), lambda i, pad: (i, 0)),
            ],
            out_specs=pl.BlockSpec(memory_space=pl.ANY),
            scratch_shapes=[pltpu.VMEM((1, d), x.dtype), pltpu.SemaphoreType.DMA(()), pltpu.SemaphoreType.DMA(())],
        ),
        out_shape=jax.ShapeDtypeStruct((n_rows, d), x.dtype),
        compiler_params=_params("arbitrary"),
        name="scatter_rows",
    )(pad_pos, pos.reshape(TOP_K, nt, tm).transpose(1, 0, 2), x)


def _expert_kernel(te_ref, rs_ref, nv_ref, na_ref, dst_ref, xs_hbm, wg_ref, wu_ref, wd_ref, y_hbm,
                   wg_sc, wu_sc, wd_sc, xbuf, ybuf, lsem, ssem, *, tm):
    j = pl.program_id(0)
    n_active = na_ref[0]
    active = j < n_active
    slot = j % 2
    new_expert = (j == 0) | (te_ref[j] != te_ref[jnp.maximum(j - 1, 0)])
    chunk_sizes = [c for c in (256, 128, 64, 32, 16, 8) if c <= tm]

    def load_tile(t, dst_slot, wait):
        n8 = ((nv_ref[t] + 7) // 8) * 8
        base = rs_ref[t]
        off = jnp.int32(0)
        for c in chunk_sizes:
            take = (n8 & c) != 0

            @pl.when(take)
            def _(off=off, c=c):
                copy = pltpu.make_async_copy(
                    xs_hbm.at[pl.ds(pl.multiple_of(base + off, 8), c)],
                    xbuf.at[dst_slot, pl.ds(pl.multiple_of(off, 8), c)], lsem.at[dst_slot])
                if wait:
                    copy.wait()
                else:
                    copy.start()

            off = off + jnp.where(take, c, 0)

    def wait_scatter(src_slot, n):
        def wait_row(r, carry):
            pltpu.make_async_copy(ybuf.at[src_slot, pl.ds(0, 1)], y_hbm.at[pl.ds(0, 1)], ssem.at[src_slot]).wait()
            return carry

        lax.fori_loop(0, n, wait_row, 0)

    @pl.when(j == 0)
    def _():
        xbuf[...] = jnp.zeros_like(xbuf)
        load_tile(0, 0, False)

    @pl.when(j + 1 < n_active)
    def _():
        load_tile(j + 1, 1 - slot, False)

    @pl.when(active & new_expert)
    def _():
        wg_sc[...] = wg_ref[0].astype(BF16)
        wu_sc[...] = wu_ref[0].astype(BF16)
        wd_sc[...] = wd_ref[0].astype(BF16)

    @pl.when(active)
    def _():
        load_tile(j, slot, True)

        @pl.when(j >= 2)
        def _():
            wait_scatter(slot, nv_ref[jnp.maximum(j - 2, 0)])

        x = xbuf[slot].astype(BF16)
        hg = jnp.dot(x, wg_sc[...], preferred_element_type=F32)
        hu = jnp.dot(x, wu_sc[...], preferred_element_type=F32)
        h = hg * jax.nn.sigmoid(hg) * hu
        ybuf[slot] = jnp.dot(h.astype(BF16), wd_sc[...], preferred_element_type=F32)

        def scatter_row(r, carry):
            pltpu.make_async_copy(ybuf.at[slot, pl.ds(r, 1)], y_hbm.at[pl.ds(dst_ref[0, 0, r], 1)],
                                  ssem.at[slot]).start()
            return carry

        lax.fori_loop(0, nv_ref[j], scatter_row, 0)

        @pl.when(j == n_active - 1)
        def _():
            wait_scatter(slot, nv_ref[j])

            @pl.when(j >= 1)
            def _():
                wait_scatter(1 - slot, nv_ref[jnp.maximum(j - 1, 0)])


def routed_experts(xs, plan, n_out_rows, w_gate, w_up, w_down, tm):
    d = xs.shape[1]
    f = w_gate.shape[2]
    nt = plan.tile_expert.shape[0]

    def dst_blk(j, te, rs, nv, na):
        return (jnp.minimum(j, na[0] - 1), 0, 0)

    def weight_blk(j, te, rs, nv, na):
        return (te[j], 0, 0)

    return pl.pallas_call(
        functools.partial(_expert_kernel, tm=tm),
        grid_spec=pltpu.PrefetchScalarGridSpec(
            num_scalar_prefetch=4,
            grid=(nt,),
            in_specs=[
                pl.BlockSpec((1, 1, tm), dst_blk, memory_space=pltpu.SMEM),
                pl.BlockSpec(memory_space=pl.ANY),
                pl.BlockSpec((1, d, f), weight_blk),
                pl.BlockSpec((1, d, f), weight_blk),
                pl.BlockSpec((1, f, d), weight_blk),
            ],
            out_specs=pl.BlockSpec(memory_space=pl.ANY),
            scratch_shapes=[pltpu.VMEM((d, f), BF16), pltpu.VMEM((d, f), BF16), pltpu.VMEM((f, d), BF16),
                            pltpu.VMEM((2, tm, d), xs.dtype), pltpu.VMEM((2, tm, d), F32),
                            pltpu.SemaphoreType.DMA((2,)), pltpu.SemaphoreType.DMA((2,))],
        ),
        out_shape=jax.ShapeDtypeStruct((n_out_rows, d), F32),
        compiler_params=_params("arbitrary"),
        name="routed_experts",
    )(plan.tile_expert, plan.tile_row_start, plan.n_valid, plan.n_active, plan.dst_tiles.reshape(nt, 1, tm),
      xs, w_gate, w_up, w_down)


def _shared_kernel(x_ref, wg_ref, wu_ref, wd_ref, o_ref):
    x = x_ref[...]
    hg = jnp.dot(x, wg_ref[...], preferred_element_type=F32)
    hu = jnp.dot(x, wu_ref[...], preferred_element_type=F32)
    h = hg * jax.nn.sigmoid(hg) * hu
    o_ref[...] = jnp.dot(h.astype(BF16), wd_ref[...], preferred_element_type=F32)


def shared_expert(x_bf, wg, wu, wd, tm=512):
    s, d = x_bf.shape
    f = wg.shape[1]
    whole = lambda r, c: pl.BlockSpec((r, c), lambda i: (0, 0))
    return pl.pallas_call(
        _shared_kernel,
        grid=(s // tm,),
        in_specs=[pl.BlockSpec((tm, d), lambda i: (i, 0)), whole(d, f), whole(d, f), whole(f, d)],
        out_specs=pl.BlockSpec((tm, d), lambda i: (i, 0)),
        out_shape=jax.ShapeDtypeStruct((s, d), F32),
        compiler_params=_params("parallel"),
        name="shared_expert",
    )(x_bf, wg.astype(BF16), wu.astype(BF16), wd.astype(BF16))


def _combine_kernel(x_ref, sh_ref, gate_ref, y0_ref, y1_ref, g_ref, b_ref, o_ref, obf_ref):
    gate = gate_ref[...]
    routed = gate[:, 0:1] * y0_ref[0] + gate[:, 1:2] * y1_ref[0]
    z = DEEPNORM_ALPHA * x_ref[...] + (routed + sh_ref[...])
    out = _layer_norm(z, g_ref[...], b_ref[...])
    o_ref[...] = out
    obf_ref[...] = out.astype(BF16)


def moe_combine(x, shared, gates, y, g, b, tm=256):
    s, d = x.shape
    assert TOP_K == 2
    row = pl.BlockSpec((tm, d), lambda i: (i, 0))
    vec = pl.BlockSpec((1, d), lambda i: (0, 0))
    return pl.pallas_call(
        _combine_kernel,
        grid=(s // tm,),
        in_specs=[
            row, row,
            pl.BlockSpec((tm, TOP_K), lambda i: (i, 0)),
            pl.BlockSpec((1, tm, d), lambda i: (0, i, 0)),
            pl.BlockSpec((1, tm, d), lambda i: (1, i, 0)),
            vec, vec,
        ],
        out_specs=[row, row],
        out_shape=[jax.ShapeDtypeStruct((s, d), F32), jax.ShapeDtypeStruct((s, d), BF16)],
        compiler_params=_params("parallel"),
        name="moe_combine",
    )(x, shared, gates.T, y, y, g.reshape(1, d), b.reshape(1, d))


def _rope_tables(pos, rot_dim, theta):
    half = rot_dim // 2
    inv = (1.0 / (theta ** (np.arange(half, dtype=np.float32) / half))).astype(np.float32)
    ang = pos.astype(F32)[:, None] * inv
    return jnp.cos(ang), jnp.sin(ang)


def _rope(x, cos, sin, rot_dim):
    half = rot_dim // 2
    x1 = x[..., :half]
    x2 = x[..., half:rot_dim]
    c = cos[:, None, :]
    s = sin[:, None, :]
    return jnp.concatenate([x1 * c - x2 * s, x1 * s + x2 * c, x[..., rot_dim:]], axis=-1)


class _RoutingPlan(NamedTuple):
    pos: jax.Array
    pad_pos: jax.Array
    n_rows: int
    tile_expert: jax.Array
    tile_row_start: jax.Array
    n_valid: jax.Array
    n_active: jax.Array
    dst_tiles: jax.Array


ROW_ALIGN = 8


def _routing_plan(experts, tm):
    k, s = experts.shape
    flat = experts.reshape(-1)
    expert_ids = jnp.arange(N_EXPERTS, dtype=jnp.int32)
    onehot = (flat[:, None] == expert_ids[None, :]).astype(jnp.int32)
    csum = jnp.cumsum(onehot, axis=0)
    rank = jnp.take_along_axis(csum, flat[:, None], axis=1)[:, 0] - 1
    sizes = csum[-1]
    padded = (sizes + ROW_ALIGN - 1) // ROW_ALIGN * ROW_ALIGN
    seg_end = jnp.cumsum(padded)
    seg_start = seg_end - padded
    pos = seg_start[flat] + rank
    n_rows = k * s + N_EXPERTS * (ROW_ALIGN - 1)

    i_pad = jnp.arange(ROW_ALIGN - 1, dtype=jnp.int32)
    is_pad = i_pad[None, :] < (padded - sizes)[:, None]
    spare = jnp.cumsum(jnp.logical_not(is_pad).reshape(-1).astype(jnp.int32)).reshape(is_pad.shape) - 1
    pad_pos = jnp.where(is_pad, (seg_start + sizes)[:, None] + i_pad[None, :], seg_end[-1] + spare).reshape(-1)

    tiles = (sizes + tm - 1) // tm
    tile_end = jnp.cumsum(tiles)
    n_active = tile_end[-1]
    n_tiles = k * s // tm + N_EXPERTS
    tile_ids = jnp.minimum(jnp.arange(n_tiles, dtype=jnp.int32), n_active - 1)
    tile_expert = jnp.searchsorted(tile_end, tile_ids, side="right").astype(jnp.int32)
    local = tile_ids - (tile_end - tiles)[tile_expert]
    tile_row_start = seg_start[tile_expert] + local * tm
    n_valid = jnp.clip(sizes[tile_expert] - local * tm, 0, tm)
    slot_of_row = jnp.zeros((n_rows,), jnp.int32).at[pos].set(jnp.arange(k * s, dtype=jnp.int32))
    rows = jnp.minimum(tile_row_start[:, None] + jnp.arange(tm, dtype=jnp.int32)[None, :], n_rows - 1)
    return _RoutingPlan(pos.reshape(k, s), pad_pos.astype(jnp.int32), n_rows, tile_expert,
                        tile_row_start.astype(jnp.int32), n_valid.astype(jnp.int32),
                        n_active.reshape(1).astype(jnp.int32), slot_of_row[rows])


def _grouped_moe(x, x_bf, router_w, router_b, w_gate, w_up, w_down, sh_gate, sh_up, sh_down, ln_g, ln_b,
                 expert_tm=256):
    s, d = x.shape
    experts, gates = moe_router(x, router_w, router_b)
    plan = _routing_plan(experts, expert_tm)
    xs = scatter_rows(x, plan.pos, plan.pad_pos, plan.n_rows)
    y = routed_experts(xs, plan, TOP_K * s, w_gate, w_up, w_down, expert_tm)
    shared = shared_expert(x_bf, sh_gate, sh_up, sh_down)
    return moe_combine(x, shared, gates, y.reshape(TOP_K, s, d), ln_g, ln_b)


def _rope_lane_tables(pos, half, theta, period, offset=0, limit=None):
    cos, sin = _rope_tables(pos, 2 * half, theta)
    lane = np.arange(LANES)
    p = lane % period - offset
    in_range = np.ones(LANES, bool) if limit is None else lane < limit
    first = (p >= 0) & (p < half) & in_range
    second = (p >= half) & (p < 2 * half) & in_range
    idx = np.where(first | second, p % half, 0)
    cos_l, sin_l = cos[:, idx], sin[:, idx]
    return jnp.stack([jnp.where(first | second, cos_l, 1.0),
                      jnp.where(second, sin_l, 0.0),
                      jnp.where(first, -sin_l, 0.0)])


def _mla_dsa_mixer(x, pos, w_in, q_norm, w_uq, kv_norm, w_ukv, w_out):
    cq, ckv, krope, dq, dkv, iq, ik, iw = jnp.split(w_in, [int(c) for c in np.cumsum(SPLITS_AB)[:-1]], axis=1)
    w_in_l = jnp.concatenate([cq, ckv, dq, dkv, iq, krope, ik, iw,
                              jnp.zeros((w_in.shape[0], LANES - IDX_HEADS), w_in.dtype)], axis=1)
    w_uq3 = w_uq.reshape(MLA_Q_LORA, MLA_HEADS, MLA_NOPE + MLA_ROPE)
    w_uq_l = jnp.concatenate([w_uq3[:, :, :MLA_NOPE].reshape(MLA_Q_LORA, -1),
                              jnp.pad(w_uq3[:, :, MLA_NOPE:], ((0, 0), (0, 0), (0, LANES - MLA_ROPE))
                                      ).reshape(MLA_Q_LORA, -1)], axis=1)
    w_ukv3 = w_ukv.reshape(MLA_KV_LORA, MLA_HEADS, MLA_NOPE + MLA_V)
    w_ukv_l = jnp.concatenate([w_ukv3[:, :, :MLA_NOPE].reshape(MLA_KV_LORA, -1),
                               w_ukv3[:, :, MLA_NOPE:].reshape(MLA_KV_LORA, -1)], axis=1)

    tab_m = _rope_lane_tables(pos, MLA_ROPE // 2, ROPE_THETA, LANES)
    tab_d = _rope_lane_tables(pos, DSA_ROT // 2, ROPE_THETA, LANES)
    tab_i = _rope_lane_tables(pos, IDX_ROT // 2, ROPE_THETA, IDX_DIM)
    tab_k = _rope_lane_tables(pos, MLA_ROPE // 2, ROPE_THETA, LANES, limit=MLA_ROPE)
    tab_ik = _rope_lane_tables(pos, IDX_ROT // 2, ROPE_THETA, LANES, offset=MLA_ROPE)
    tab_ki = jnp.concatenate([(tab_k[0] * tab_ik[0])[None], tab_k[1:], tab_ik[1:]])

    h0 = matmul(x, w_in_l, F32, 1024, 512, name="l0_in_proj")
    q_cat, k_nope, v_a, k_pe, qb, kb, vb, iq_r, ik_lo, ik_hi, iw_s = l0_attention_operands(
        h0, q_norm, w_uq_l, kv_norm, w_ukv_l, (tab_m, tab_d, tab_i, tab_ki))
    out_a = mla_attention(q_cat, k_nope, k_pe, v_a)
    out_b = dsa_attention(iq_r, iw_s, ik_lo, ik_hi, qb, kb, vb)
    return matmul_concat(out_a, out_b, w_out, F32, 1024, 512, name="l0_out_proj")


def _retention_mixer(x_bf, pos, w_in, gn_g, gn_b, w_out):
    h1 = matmul(x_bf, w_in, F32, 1024, 1024, name="l1_in_proj")
    cos, sin = _rope_tables(pos, RET_QK_DIM, RET_THETA)
    y = retention(h1, cos, sin, gn_g, gn_b)
    return matmul(y, w_out, F32, 1024, 512, name="l1_out_proj")


def kernel(x, positions, router_w, router_b, l0_w_in, l0_mla_q_norm, l0_mla_w_uq, l0_mla_kv_norm, l0_mla_w_ukv, l0_w_out, l1_w_in, l1_ret_gn_g, l1_ret_gn_b, l1_w_out, l0_ln_mix_g, l0_ln_mix_b, l0_moe_w_gate, l0_moe_w_up, l0_moe_w_down, l0_sh_gate, l0_sh_up, l0_sh_down, l0_ln_ffn_g, l0_ln_ffn_b, l1_ln_mix_g, l1_ln_mix_b, l1_moe_w_gate, l1_moe_w_up, l1_moe_w_down, l1_sh_gate, l1_sh_up, l1_sh_down, l1_ln_ffn_g, l1_ln_ffn_b):
    assert x.shape[0] == 1
    xt = x[0]
    pos = positions[0]

    mix = _mla_dsa_mixer(xt, pos, l0_w_in, l0_mla_q_norm, l0_mla_w_uq, l0_mla_kv_norm, l0_mla_w_ukv, l0_w_out)
    xt, xt_bf = add_layer_norm(xt, mix, l0_ln_mix_g, l0_ln_mix_b)
    xt, xt_bf = _grouped_moe(xt, xt_bf, router_w, router_b, l0_moe_w_gate, l0_moe_w_up, l0_moe_w_down,
                             l0_sh_gate, l0_sh_up, l0_sh_down, l0_ln_ffn_g, l0_ln_ffn_b)

    mix = _retention_mixer(xt_bf, pos, l1_w_in, l1_ret_gn_g, l1_ret_gn_b, l1_w_out)
    xt, xt_bf = add_layer_norm(xt, mix, l1_ln_mix_g, l1_ln_mix_b)
    xt, _ = _grouped_moe(xt, xt_bf, router_w, router_b, l1_moe_w_gate, l1_moe_w_up, l1_moe_w_down,
                         l1_sh_gate, l1_sh_up, l1_sh_down, l1_ln_ffn_g, l1_ln_ffn_b)
    return xt[None]
```

```python
import functools
from typing import NamedTuple

import numpy as np
import jax
import jax.numpy as jnp
from jax import lax
from jax.experimental import pallas as pl
from jax.experimental.pallas import tpu as pltpu

F32 = jnp.float32
BF16 = jnp.bfloat16

D_MODEL = 2048
DEPTH = 2
ROPE_THETA = 500000.0
MLA_HEADS = 8
MLA_Q_LORA = 512
MLA_KV_LORA = 256
MLA_NOPE = 128
MLA_ROPE = 64
MLA_V = 128
DSA_HEADS = 8
DSA_KV_HEADS = 2
DSA_HEAD_DIM = 128
DSA_ROT = DSA_HEAD_DIM // 4
IDX_HEADS = 16
IDX_DIM = 64
IDX_ROT = IDX_DIM // 4
IDX_TOPK_MAX = 256
RET_HEADS = 8
RET_QK_DIM = 256
RET_V_DIM = 512
RET_CHUNK = 128
RET_THETA = 10000.0
N_EXPERTS = 64
N_GROUPS = 8
EXPERTS_PER_GROUP = N_EXPERTS // N_GROUPS
TOP_K = 2
D_EXPERT = 512
D_SHARED = 1024
DEEPNORM_ALPHA = (2.0 * DEPTH) ** 0.25

SPLITS_AB = (MLA_Q_LORA, MLA_KV_LORA, MLA_ROPE, DSA_HEADS * DSA_HEAD_DIM,
             2 * DSA_KV_HEADS * DSA_HEAD_DIM, IDX_HEADS * IDX_DIM, IDX_DIM, IDX_HEADS)
IN_AB = sum(SPLITS_AB)

VMEM_LIMIT_BYTES = 56 * 1024 * 1024
LANES = 128
MASKED_SCORE = -1e30
LOG2_E = 1.4426950408889634
DMA_ISSUE_UNROLL = 8
INT32_MIN = -2 ** 31
INT16_MIN = -2 ** 15


def _params(*sem):
    return pltpu.CompilerParams(dimension_semantics=sem, vmem_limit_bytes=VMEM_LIMIT_BYTES)


def _mm_kernel(*refs, norm_eps):
    if norm_eps is None:
        a_ref, b_ref, o_ref = refs
        a = a_ref[...]
    else:
        a_ref, g_ref, b_ref, o_ref = refs
        af = a_ref[...].astype(F32)
        a = af * lax.rsqrt(jnp.mean(af * af, axis=-1, keepdims=True) + norm_eps) * g_ref[...]
    o_ref[...] = jnp.dot(a.astype(BF16), b_ref[...].astype(BF16),
                         preferred_element_type=F32).astype(o_ref.dtype)


def matmul(a, b, out_dtype, tm, tn, *, name, a_cols=None, norm_gain=None, norm_eps=1e-6):
    m = a.shape[0]
    k, n = b.shape
    col_blk = 0 if a_cols is None else a_cols[0]
    assert (a.shape[1] == k) if a_cols is None else (a_cols[1] == k)
    assert m % tm == 0 and n % tn == 0
    in_specs = [pl.BlockSpec((tm, k), lambda i, j: (i, col_blk))]
    args = [a]
    if norm_gain is not None:
        in_specs.append(pl.BlockSpec((1, k), lambda i, j: (0, 0)))
        args.append(norm_gain.reshape(1, k))
    in_specs.append(pl.BlockSpec((k, tn), lambda i, j: (0, j)))
    args.append(b)
    return pl.pallas_call(
        functools.partial(_mm_kernel, norm_eps=None if norm_gain is None else norm_eps),
        grid=(m // tm, n // tn),
        in_specs=in_specs,
        out_specs=pl.BlockSpec((tm, tn), lambda i, j: (i, j)),
        out_shape=jax.ShapeDtypeStruct((m, n), out_dtype),
        compiler_params=_params("parallel", "parallel"),
        name=name,
    )(*args)


def _mm2_kernel(a1_ref, a2_ref, b1_ref, b2_ref, o_ref):
    acc = jnp.dot(a1_ref[...].astype(BF16), b1_ref[...].astype(BF16), preferred_element_type=F32)
    acc = acc + jnp.dot(a2_ref[...].astype(BF16), b2_ref[...].astype(BF16), preferred_element_type=F32)
    o_ref[...] = acc.astype(o_ref.dtype)


def matmul_concat(a1, a2, b, out_dtype, tm, tn, *, name):
    m, k1 = a1.shape
    n = b.shape[1]
    assert a2.shape == (m, k1) and b.shape[0] == 2 * k1 and m % tm == 0 and n % tn == 0
    a_spec = pl.BlockSpec((tm, k1), lambda i, j: (i, 0))
    return pl.pallas_call(
        _mm2_kernel,
        grid=(m // tm, n // tn),
        in_specs=[a_spec, a_spec, pl.BlockSpec((k1, tn), lambda i, j: (0, j)), pl.BlockSpec((k1, tn), lambda i, j: (1, j))],
        out_specs=pl.BlockSpec((tm, tn), lambda i, j: (i, j)),
        out_shape=jax.ShapeDtypeStruct((m, n), out_dtype),
        compiler_params=_params("parallel", "parallel"),
        name=name,
    )(a1, a2, b, b)


_H0_CQ = 0
_H0_CKV = _H0_CQ + MLA_Q_LORA
_H0_DQ = _H0_CKV + MLA_KV_LORA
_H0_DKV = _H0_DQ + DSA_HEADS * DSA_HEAD_DIM
_H0_IQ = _H0_DKV + 2 * DSA_KV_HEADS * DSA_HEAD_DIM
_H0_KROPE_IK = _H0_IQ + IDX_HEADS * IDX_DIM
_H0_IW = _H0_KROPE_IK + LANES
_H0_WIDTH = _H0_IW + LANES


def _rope_lanes(x, tab_ref, shifts):
    out = x * tab_ref[0]
    for i, shift in enumerate(shifts):
        out = out + pltpu.roll(x, shift, 1) * tab_ref[1 + i]
    return out


def _l0_prep_kernel(h_ref, qn_ref, kvn_ref, wuq_ref, wukv_ref, tm_ref, td_ref, ti_ref, tki_ref,
                    qcat_ref, knope_ref, va_ref, kpe_ref, qb_ref, kb_ref, vb_ref, iq_ref, iklo_ref, ikhi_ref,
                    iw_ref):
    def rms(x, g_ref):
        return (x * lax.rsqrt(jnp.mean(x * x, axis=-1, keepdims=True) + 1e-6) * g_ref[...]).astype(BF16)

    def slab(col, j=0):
        return h_ref[:, col + j * LANES:col + (j + 1) * LANES]

    half_m, half_d, half_i = MLA_ROPE // 2, DSA_ROT // 2, IDX_ROT // 2
    shifts_m = (half_m, LANES - half_m)
    shifts_d = (half_d, LANES - half_d)
    shifts_i = (half_i, LANES - half_i)

    qa = jnp.dot(rms(h_ref[:, _H0_CQ:_H0_CQ + MLA_Q_LORA], qn_ref), wuq_ref[...], preferred_element_type=F32)
    kva = jnp.dot(rms(h_ref[:, _H0_CKV:_H0_CKV + MLA_KV_LORA], kvn_ref), wukv_ref[...],
                  preferred_element_type=F32)
    n_nope = MLA_HEADS * MLA_NOPE
    knope_ref[...] = kva[:, :n_nope].astype(BF16)
    va_ref[...] = kva[:, n_nope:].astype(BF16)
    q_scale = (MLA_NOPE + MLA_ROPE) ** -0.5 * LOG2_E
    for h in range(MLA_HEADS):
        nope = qa[:, h * LANES:(h + 1) * LANES]
        pe = qa[:, n_nope + h * LANES:n_nope + (h + 1) * LANES]
        qcat_ref[:, 2 * h * LANES:(2 * h + 1) * LANES] = (nope * q_scale).astype(BF16)
        qcat_ref[:, (2 * h + 1) * LANES:(2 * h + 2) * LANES] = (_rope_lanes(pe, tm_ref, shifts_m) * q_scale).astype(BF16)

    d_scale = DSA_HEAD_DIM ** -0.5 * LOG2_E
    for h in range(DSA_HEADS):
        qb_ref[:, h * LANES:(h + 1) * LANES] = (_rope_lanes(slab(_H0_DQ, h), td_ref, shifts_d) * d_scale).astype(BF16)
    for g in range(DSA_KV_HEADS):
        kb_ref[:, g * LANES:(g + 1) * LANES] = _rope_lanes(slab(_H0_DKV, g), td_ref, shifts_d).astype(BF16)
    vb_ref[...] = h_ref[:, _H0_DKV + DSA_KV_HEADS * LANES:_H0_DKV + 2 * DSA_KV_HEADS * LANES].astype(BF16)
    for j in range(IDX_HEADS * IDX_DIM // LANES):
        iq_ref[:, j * LANES:(j + 1) * LANES] = _rope_lanes(slab(_H0_IQ, j), ti_ref, shifts_i).astype(BF16)

    ki = _rope_lanes(slab(_H0_KROPE_IK), tki_ref, shifts_m + shifts_i)
    lane = lax.broadcasted_iota(jnp.int32, ki.shape, 1)
    kpe_ref[...] = jnp.where(lane < MLA_ROPE, ki, 0.0).astype(BF16)
    ik_hi = jnp.where(lane >= MLA_ROPE, ki, 0.0)
    ikhi_ref[...] = ik_hi.astype(BF16)
    iklo_ref[...] = pltpu.roll(ik_hi, LANES - MLA_ROPE, 1).astype(BF16)
    iw_ref[...] = h_ref[:, _H0_IW:_H0_IW + IDX_HEADS] * (IDX_HEADS ** -0.5 * IDX_DIM ** -0.5)


def l0_attention_operands(h0, q_norm, w_uq, kv_norm, w_ukv, tables, tm=256):
    s = h0.shape[0]
    bf = lambda n: jax.ShapeDtypeStruct((s, n), BF16)
    row = lambda n: pl.BlockSpec((tm, n), lambda i: (i, 0))
    whole = lambda a: pl.BlockSpec(a.shape, lambda i: (0,) * a.ndim)
    tab = lambda t: pl.BlockSpec((t.shape[0], tm, LANES), lambda i: (0, i, 0))
    widths = [2 * MLA_HEADS * LANES, MLA_HEADS * MLA_NOPE, MLA_HEADS * MLA_V, LANES, DSA_HEADS * DSA_HEAD_DIM,
              DSA_KV_HEADS * DSA_HEAD_DIM, DSA_KV_HEADS * DSA_HEAD_DIM, IDX_HEADS * IDX_DIM, LANES, LANES]
    w_uq = w_uq.astype(BF16)
    w_ukv = w_ukv.astype(BF16)
    q_norm = q_norm.reshape(1, -1)
    kv_norm = kv_norm.reshape(1, -1)
    return pl.pallas_call(
        _l0_prep_kernel,
        grid=(s // tm,),
        in_specs=[row(_H0_WIDTH), whole(q_norm), whole(kv_norm), whole(w_uq), whole(w_ukv)] + [tab(t) for t in tables],
        out_specs=[row(n) for n in widths] + [row(IDX_HEADS)],
        out_shape=[bf(n) for n in widths] + [jax.ShapeDtypeStruct((s, IDX_HEADS), F32)],
        compiler_params=_params("parallel"),
        name="l0_attention_operands",
    )(h0, q_norm, kv_norm, w_uq, w_ukv, *tables)


def _layer_norm(z, g, b):
    mu = jnp.mean(z, axis=-1, keepdims=True)
    zc = z - mu
    var = jnp.mean(zc * zc, axis=-1, keepdims=True)
    return zc * lax.rsqrt(var + 1e-5) * g + b


def _add_ln_kernel(x_ref, y_ref, g_ref, b_ref, o_ref, obf_ref):
    z = DEEPNORM_ALPHA * x_ref[...] + y_ref[...]
    out = _layer_norm(z, g_ref[...], b_ref[...])
    o_ref[...] = out
    obf_ref[...] = out.astype(BF16)


def add_layer_norm(x, y, g, b, tm=256):
    s, d = x.shape
    row = pl.BlockSpec((tm, d), lambda i: (i, 0))
    vec = pl.BlockSpec((1, d), lambda i: (0, 0))
    return pl.pallas_call(
        _add_ln_kernel,
        grid=(s // tm,),
        in_specs=[row, row, vec, vec],
        out_specs=[row, row],
        out_shape=[jax.ShapeDtypeStruct((s, d), F32), jax.ShapeDtypeStruct((s, d), BF16)],
        compiler_params=_params("parallel"),
        name="add_layer_norm",
    )(x, y, g.reshape(1, d), b.reshape(1, d))


def _lane_tile(x, n):
    return jnp.tile(x, (1, n))


def _softmax_step(s, v_ext, m_ref, acc_ref):
    tk = s.shape[1]
    m_prev = m_ref[...]
    m_new = jnp.maximum(m_prev, jnp.max(s, axis=1)[:, None])
    p = jnp.exp2(s - _lane_tile(m_new, tk // LANES))
    alpha = jnp.exp2(m_prev - m_new)
    pv = jnp.dot(p.astype(BF16), v_ext, preferred_element_type=F32)
    acc_ref[...] = _lane_tile(alpha, 2) * acc_ref[...] + pv
    m_ref[...] = m_new


def _mla_kernel(q_ref, kn_ref, kp_ref, v_ref, o_ref, m_sc, acc_sc, *, tq, tk, hp):
    qi = pl.program_id(1)
    m_sc[...] = jnp.full_like(m_sc, MASKED_SCORE)
    acc_sc[...] = jnp.zeros_like(acc_sc)
    ones = jnp.ones((tk, LANES), BF16)
    per_block = tq // tk

    def attend(c, diagonal_offset):
        kp = kp_ref[c]
        kn = kn_ref[c]
        v = v_ref[c]
        for j in range(hp):
            q = q_ref[:, j * 2 * LANES:(j + 1) * 2 * LANES]
            k = jnp.concatenate([kn[:, j * LANES:(j + 1) * LANES], kp], axis=1)
            s = lax.dot_general(q, k, (((1,), (1,)), ((), ())), preferred_element_type=F32)
            if diagonal_offset is not None:
                row = lax.broadcasted_iota(jnp.int32, (tq, tk), 0)
                col = diagonal_offset * tk + lax.broadcasted_iota(jnp.int32, (tq, tk), 1)
                s = jnp.where(col <= row, s, MASKED_SCORE)
            v_ext = jnp.concatenate([v[:, j * LANES:(j + 1) * LANES], ones], axis=1)
            _softmax_step(s, v_ext, m_sc.at[j], acc_sc.at[j])

    def below_diagonal(c, carry):
        attend(c, None)
        return carry

    lax.fori_loop(0, qi * per_block, below_diagonal, 0)
    for d in range(per_block):
        attend(qi * per_block + d, d)
    for j in range(hp):
        acc = acc_sc[j]
        o_ref[:, j * LANES:(j + 1) * LANES] = (acc[:, :LANES] / acc[:, LANES:]).astype(o_ref.dtype)


def mla_attention(q, k_nope, k_pe, v, tq=512, tk=512, hp=2):
    s = q.shape[0]
    h = MLA_HEADS
    nc = s // tk
    assert tq % tk == 0 and h % hp == 0
    per_head = pl.BlockSpec((nc, tk, hp * LANES), lambda hh, qi: (0, 0, hh))
    return pl.pallas_call(
        functools.partial(_mla_kernel, tq=tq, tk=tk, hp=hp),
        grid=(h // hp, s // tq),
        in_specs=[
            pl.BlockSpec((tq, hp * 2 * LANES), lambda hh, qi: (qi, hh)),
            per_head,
            pl.BlockSpec((nc, tk, LANES), lambda hh, qi: (0, 0, 0)),
            per_head,
        ],
        out_specs=pl.BlockSpec((tq, hp * LANES), lambda hh, qi: (qi, hh)),
        out_shape=jax.ShapeDtypeStruct((s, h * MLA_V), BF16),
        scratch_shapes=[pltpu.VMEM((hp, tq, LANES), F32), pltpu.VMEM((hp, tq, 2 * LANES), F32)],
        compiler_params=_params("parallel", "arbitrary"),
        name="mla_attention",
    )(q, k_nope.reshape(nc, tk, h * MLA_NOPE), k_pe.reshape(nc, tk, LANES), v.reshape(nc, tk, h * MLA_V))


def _sortable_key(x):
    bits = pltpu.bitcast(x, jnp.int32)
    return bits ^ ((bits >> 31) & 0x7FFFFFFF)


def _dsa_kernel(iq_ref, iw_ref, iklo_ref, ikhi_ref, q_ref, k_ref, v_ref, o_ref, key_sc, half_sc, wb_sc, m_sc,
                acc_sc, *, tq, tk, n_sel):
    i = pl.program_id(0)
    n_chunks = ((i + 1) * tq + tk - 1) // tk
    rep = DSA_HEADS // DSA_KV_HEADS
    lane_tiles = tk // LANES
    heads_per_block = LANES // IDX_DIM

    assert heads_per_block == 2
    iq_blocks = jnp.concatenate(
        [iq_ref[:, j * LANES:(j + 1) * LANES] for j in range(IDX_HEADS // heads_per_block)], axis=0)
    iw = iw_ref[...]
    for h in range(IDX_HEADS):
        wb_sc[h] = jnp.broadcast_to(iw[:, h:h + 1], (tq, LANES))

    def index_chunk(c, carry):
        ik = jnp.concatenate([iklo_ref[c], ikhi_ref[c]], axis=0)
        d = lax.dot_general(iq_blocks, ik, (((1,), (1,)), ((), ())), preferred_element_type=F32)
        d = jnp.maximum(d, 0.0)
        score = None
        for h in range(IDX_HEADS):
            j, part = divmod(h, heads_per_block)
            term = _lane_tile(wb_sc[h], lane_tiles) * d[j * tq:(j + 1) * tq, part * tk:(part + 1) * tk]
            score = term if score is None else score + term
        key = _sortable_key(score)
        key_sc[c] = key
        half_sc[c] = (key >> 16).astype(jnp.int16)
        return carry

    lax.fori_loop(0, n_chunks, index_chunk, 0)
    last = n_chunks - 1
    row = i * tq + lax.broadcasted_iota(jnp.int32, (tq, tk), 0)
    col = last * tk + lax.broadcasted_iota(jnp.int32, (tq, tk), 1)
    masked_last = jnp.where(col <= row, key_sc[last], INT32_MIN)
    key_sc[last] = masked_last
    half_sc[last] = (masked_last >> 16).astype(jnp.int16)

    def search16(base_count):
        def search_bit(b, state):
            thr, above = state
            cand = thr + lax.shift_left(jnp.int32(1), 15 - b)
            cand16 = cand.astype(jnp.int16)

            def count_chunk(c, cnt):
                half = half_sc[c]
                for j in range(lane_tiles):
                    cnt = cnt + jnp.where(half[:, j * LANES:(j + 1) * LANES] >= cand16, jnp.int16(1), jnp.int16(0))
                return cnt

            cnt = lax.fori_loop(0, n_chunks, count_chunk, jnp.zeros((tq, LANES), jnp.int16))
            total = base_count + jnp.sum(cnt.astype(jnp.int32).astype(F32), axis=1)[:, None]
            ok = total >= n_sel
            return jnp.where(ok, cand, thr), jnp.where(ok, above, total)

        init = (jnp.full((tq, LANES), INT16_MIN, jnp.int32), jnp.broadcast_to(base_count, (tq, LANES)))
        return lax.fori_loop(0, 16, search_bit, init)

    thr_hi, n_above = search16(jnp.zeros((tq, LANES), F32))
    thr_hi_t = _lane_tile(thr_hi, lane_tiles)

    def low_halves(c, carry):
        key = key_sc[c]
        low = (key & 0xFFFF) + INT16_MIN
        half_sc[c] = jnp.where((key >> 16) == thr_hi_t, low, INT16_MIN).astype(jnp.int16)
        return carry

    lax.fori_loop(0, n_chunks, low_halves, 0)
    thr_lo, _ = search16(n_above)
    thr = (thr_hi << 16) | (thr_lo - INT16_MIN)
    thr = _lane_tile(jnp.maximum(thr, INT32_MIN + 1), lane_tiles)

    m_sc[...] = jnp.full_like(m_sc, MASKED_SCORE)
    acc_sc[...] = jnp.zeros_like(acc_sc)
    ones = jnp.ones((tk, LANES), BF16)

    def attend_chunk(c, carry):
        bias = jnp.where(key_sc[c] >= thr, 0.0, MASKED_SCORE)
        kc = k_ref[c]
        vc = v_ref[c]
        for g in range(DSA_KV_HEADS):
            cols = slice(g * DSA_HEAD_DIM, (g + 1) * DSA_HEAD_DIM)
            qg = jnp.concatenate(
                [q_ref[:, (g * rep + r) * DSA_HEAD_DIM:(g * rep + r + 1) * DSA_HEAD_DIM] for r in range(rep)], axis=0)
            s = lax.dot_general(qg, kc[:, cols], (((1,), (1,)), ((), ())), preferred_element_type=F32)
            s = (s.reshape(rep, tq, tk) + bias[None]).reshape(rep * tq, tk)
            _softmax_step(s, jnp.concatenate([vc[:, cols], ones], axis=1), m_sc.at[g], acc_sc.at[g])
        return carry

    lax.fori_loop(0, n_chunks, attend_chunk, 0)
    for g in range(DSA_KV_HEADS):
        acc = acc_sc[g]
        out = acc[:, :LANES] / acc[:, LANES:]
        for r in range(rep):
            hh = g * rep + r
            o_ref[:, hh * DSA_HEAD_DIM:(hh + 1) * DSA_HEAD_DIM] = out[r * tq:(r + 1) * tq].astype(o_ref.dtype)


def dsa_attention(iq, iw, ik_lo, ik_hi, q, k, v, tq=256, tk=512):
    s = q.shape[0]
    n_sel = min(IDX_TOPK_MAX, s // 4)
    nc = s // tk
    assert tk >= n_sel and s % tk == 0 and tk % tq == 0 and DSA_HEAD_DIM == LANES
    rep = DSA_HEADS // DSA_KV_HEADS
    kvw = DSA_KV_HEADS * DSA_HEAD_DIM
    whole3 = lambda i: (0, 0, 0)
    return pl.pallas_call(
        functools.partial(_dsa_kernel, tq=tq, tk=tk, n_sel=n_sel),
        grid=(s // tq,),
        in_specs=[
            pl.BlockSpec((tq, IDX_HEADS * IDX_DIM), lambda i: (i, 0)),
            pl.BlockSpec((tq, IDX_HEADS), lambda i: (i, 0)),
            pl.BlockSpec((nc, tk, LANES), whole3),
            pl.BlockSpec((nc, tk, LANES), whole3),
            pl.BlockSpec((tq, DSA_HEADS * DSA_HEAD_DIM), lambda i: (i, 0)),
            pl.BlockSpec((nc, tk, kvw), whole3),
            pl.BlockSpec((nc, tk, kvw), whole3),
        ],
        out_specs=pl.BlockSpec((tq, DSA_HEADS * DSA_HEAD_DIM), lambda i: (i, 0)),
        out_shape=jax.ShapeDtypeStruct((s, DSA_HEADS * DSA_HEAD_DIM), BF16),
        scratch_shapes=[
            pltpu.VMEM((nc, tq, tk), jnp.int32),
            pltpu.VMEM((nc, tq, tk), jnp.int16),
            pltpu.VMEM((IDX_HEADS, tq, LANES), F32),
            pltpu.VMEM((DSA_KV_HEADS, rep * tq, LANES), F32),
            pltpu.VMEM((DSA_KV_HEADS, rep * tq, 2 * LANES), F32),
        ],
        compiler_params=_params("parallel"),
        name="dsa_attention",
    )(iq, iw, ik_lo.reshape(nc, tk, LANES), ik_hi.reshape(nc, tk, LANES), q, k.reshape(nc, tk, kvw),
      v.reshape(nc, tk, kvw))


def _retention_tables():
    h, c = RET_HEADS, RET_CHUNK
    log_g = np.log(1.0 - 2.0 ** (-5.0 - np.arange(h, dtype=np.float32))).astype(np.float32).astype(np.float64)
    idx = np.arange(c, dtype=np.float64)
    diff = idx[:, None] - idx[None, :]
    decay_in = np.where(diff[None] >= 0, np.exp(np.maximum(diff, 0.0)[None] * log_g[:, None, None]), 0.0)
    xi = np.exp((idx + 1.0)[None, :] * log_g[:, None])
    zeta = np.exp((c - 1.0 - idx)[None, :] * log_g[:, None])
    chunk_decay = np.exp(c * log_g)
    lane = np.ones((1, 1, LANES))
    return (decay_in.astype(np.float32), (xi[:, :, None] * lane).astype(np.float32),
            (zeta[:, :, None] * lane).astype(np.float32),
            (chunk_decay[:, None, None] * np.ones((1, 8, LANES))).astype(np.float32))


def _retention_kernel(q_ref, k_ref, v_ref, g_ref, cos_ref, sin_ref, din_ref, xi_ref, zeta_ref, cd_ref,
                      gng_ref, gnb_ref, o_ref, r_sc):
    n = pl.program_id(1)
    half = RET_QK_DIM // 2

    @pl.when(n == 0)
    def _():
        r_sc[...] = jnp.zeros_like(r_sc)

    cos = cos_ref[...]
    sin = sin_ref[...]

    def rope(x):
        x1 = x[:, :half]
        x2 = x[:, half:]
        return x1 * cos - x2 * sin, x1 * sin + x2 * cos

    q1, q2 = rope(q_ref[...])
    k1, k2 = rope(k_ref[...])
    scale = RET_QK_DIM ** -0.5
    qr = jnp.concatenate([q1, q2], axis=1).astype(BF16)
    kr = jnp.concatenate([k1 * scale, k2 * scale], axis=1)
    v = v_ref[...].astype(BF16)

    inner = lax.dot_general(qr, kr.astype(BF16), (((1,), (1,)), ((), ())), preferred_element_type=F32)
    inner = inner * din_ref[0]
    r = r_sc[...]
    cross = jnp.dot(qr, r.astype(BF16), preferred_element_type=F32)
    xi = xi_ref[0]
    o = jnp.dot(inner.astype(BF16), v, preferred_element_type=F32)
    o = o + cross * jnp.concatenate([xi] * (RET_V_DIM // LANES), axis=1)

    zeta = zeta_ref[0]
    kz = (kr * jnp.concatenate([zeta] * (RET_QK_DIM // LANES), axis=1)).astype(BF16)
    upd = lax.dot_general(kz, v, (((0,), (0,)), ((), ())), preferred_element_type=F32)
    r_sc[...] = cd_ref[0][:1, :1] * r + upd

    mu = jnp.mean(o, axis=-1, keepdims=True)
    oc = o - mu
    var = jnp.mean(oc * oc, axis=-1, keepdims=True)
    y = oc * lax.rsqrt(var + 1e-5) * gng_ref[...] + gnb_ref[...]
    gate = g_ref[...]
    o_ref[...] = (gate * jax.nn.sigmoid(gate) * y).astype(o_ref.dtype)


def retention(h1, cos, sin, gn_g, gn_b):
    s = h1.shape[0]
    hh, dk, dv, c = RET_HEADS, RET_QK_DIM, RET_V_DIM, RET_CHUNK
    din, xi, zeta, cd = (jnp.asarray(t) for t in _retention_tables())
    k_blk0 = hh * dk // dk
    v_blk0 = 2 * hh * dk // dv
    g_blk0 = v_blk0 + hh
    per_head = lambda r, w: pl.BlockSpec((1, r, w), lambda h, n: (h, 0, 0))
    return pl.pallas_call(
        _retention_kernel,
        grid=(hh, s // c),
        in_specs=[
            pl.BlockSpec((c, dk), lambda h, n: (n, h)),
            pl.BlockSpec((c, dk), lambda h, n: (n, k_blk0 + h)),
            pl.BlockSpec((c, dv), lambda h, n: (n, v_blk0 + h)),
            pl.BlockSpec((c, dv), lambda h, n: (n, g_blk0 + h)),
            pl.BlockSpec((c, dk // 2), lambda h, n: (n, 0)),
            pl.BlockSpec((c, dk // 2), lambda h, n: (n, 0)),
            per_head(c, c), per_head(c, LANES), per_head(c, LANES), per_head(8, LANES),
            pl.BlockSpec((1, dv), lambda h, n: (0, h)),
            pl.BlockSpec((1, dv), lambda h, n: (0, h)),
        ],
        out_specs=pl.BlockSpec((c, dv), lambda h, n: (n, h)),
        out_shape=jax.ShapeDtypeStruct((s, hh * dv), BF16),
        scratch_shapes=[pltpu.VMEM((dk, dv), F32)],
        compiler_params=_params("parallel", "arbitrary"),
        name="retention",
    )(h1, h1, h1, h1, cos, sin, din, xi, zeta, cd, gn_g.reshape(1, -1), gn_b.reshape(1, -1))


def _first_argmax(v, idx, n):
    m = jnp.max(v, axis=0, keepdims=True)
    first = jnp.min(jnp.where(v == m, idx, n), axis=0, keepdims=True)
    return m, first


def _router_kernel(x_ref, rwt_ref, rb_ref, e_ref, g_ref):
    tm = x_ref.shape[0]
    epg = EXPERTS_PER_GROUP
    logits = lax.dot_general(rwt_ref[...], x_ref[...], (((1,), (1,)), ((), ())),
                             precision=lax.Precision.HIGHEST, preferred_element_type=F32)
    scores = jax.nn.sigmoid(logits)
    biased = scores + rb_ref[...]
    idx = lax.broadcasted_iota(jnp.int32, (epg, tm), 0)

    group_scores = []
    for g in range(N_GROUPS):
        v = biased[g * epg:(g + 1) * epg]
        m1, first = _first_argmax(v, idx, epg)
        m2 = jnp.max(jnp.where(idx == first, -jnp.inf, v), axis=0, keepdims=True)
        group_scores.append(m1 + m2)
    gmax = group_scores[0]
    for g in range(1, N_GROUPS):
        gmax = jnp.maximum(gmax, group_scores[g])
    gsel = jnp.full((1, tm), N_GROUPS, jnp.int32)
    for g in range(N_GROUPS - 1, -1, -1):
        gsel = jnp.where(group_scores[g] == gmax, g, gsel)

    in_biased = jnp.zeros((epg, tm), F32)
    in_scores = jnp.zeros((epg, tm), F32)
    for g in range(N_GROUPS):
        pick = gsel == g
        in_biased = jnp.where(pick, biased[g * epg:(g + 1) * epg], in_biased)
        in_scores = jnp.where(pick, scores[g * epg:(g + 1) * epg], in_scores)
    _, loc1 = _first_argmax(in_biased, idx, epg)
    _, loc2 = _first_argmax(jnp.where(idx == loc1, -jnp.inf, in_biased), idx, epg)
    s1 = jnp.sum(jnp.where(idx == loc1, in_scores, 0.0), axis=0, keepdims=True)
    s2 = jnp.sum(jnp.where(idx == loc2, in_scores, 0.0), axis=0, keepdims=True)
    denom = s1 + s2
    e_ref[...] = jnp.concatenate([gsel * epg + loc1, gsel * epg + loc2], axis=0)
    g_ref[...] = jnp.concatenate([s1 / denom, s2 / denom], axis=0)


def moe_router(x, router_w, router_b, tm=512):
    s, d = x.shape
    return pl.pallas_call(
        _router_kernel,
        grid=(s // tm,),
        in_specs=[
            pl.BlockSpec((tm, d), lambda i: (i, 0)),
            pl.BlockSpec((N_EXPERTS, d), lambda i: (0, 0)),
            pl.BlockSpec((N_EXPERTS, 1), lambda i: (0, 0)),
        ],
        out_specs=[pl.BlockSpec((TOP_K, tm), lambda i: (0, i)), pl.BlockSpec((TOP_K, tm), lambda i: (0, i))],
        out_shape=[jax.ShapeDtypeStruct((TOP_K, s), jnp.int32), jax.ShapeDtypeStruct((TOP_K, s), F32)],
        compiler_params=_params("parallel"),
        name="moe_router",
    )(x, router_w.T, router_b.reshape(N_EXPERTS, 1))


def _scatter_rows_kernel(pad_ref, pos_ref, x_ref, xs_hbm, zero_sc, sem, zsem, *, tm, n_pad):
    i = pl.program_id(0)

    def zero_copy(p):
        return pltpu.make_async_copy(zero_sc, xs_hbm.at[pl.ds(pad_ref[p], 1)], zsem)

    @pl.when(i == 0)
    def _():
        zero_sc[...] = jnp.zeros_like(zero_sc)

        def start_zero(p, carry):
            zero_copy(p).start()
            return carry

        lax.fori_loop(0, n_pad, start_zero, 0)

    def start(g, carry):
        for u in range(DMA_ISSUE_UNROLL):
            r = g * DMA_ISSUE_UNROLL + u
            for k in range(TOP_K):
                pltpu.make_async_copy(x_ref.at[pl.ds(r, 1)], xs_hbm.at[pl.ds(pos_ref[0, k, r], 1)],
                                      sem).start(priority=k % 2)
        return carry

    lax.fori_loop(0, tm // DMA_ISSUE_UNROLL, start, 0)
    for k in range(TOP_K):
        pltpu.make_async_copy(x_ref, xs_hbm.at[pl.ds(0, tm)], sem).wait()

    @pl.when(i == 0)
    def _():
        def wait_zero(p, carry):
            zero_copy(p).wait()
            return carry

        lax.fori_loop(0, n_pad, wait_zero, 0)


def scatter_rows(x, pos, pad_pos, n_rows, tm=256):
    s, d = x.shape
    nt = s // tm
    n_pad = pad_pos.shape[0]
    return pl.pallas_call(
        functools.partial(_scatter_rows_kernel, tm=tm, n_pad=n_pad),
        grid_spec=pltpu.PrefetchScalarGridSpec(
            num_scalar_prefetch=1,
            grid=(nt,),
            in_specs=[
                pl.BlockSpec((1, TOP_K, tm), lambda i, pad: (i, 0, 0), memory_space=pltpu.SMEM),
                pl.BlockSpec((tm, d), lambda i, pad: (i, 0)),
            ],
            out_specs=pl.BlockSpec(memory_space=pl.ANY),
            scratch_shapes=[pltpu.VMEM((1, d), x.dtype), pltpu.SemaphoreType.DMA(()), pltpu.SemaphoreType.DMA(())],
        ),
        out_shape=jax.ShapeDtypeStruct((n_rows, d), x.dtype),
        compiler_params=_params("arbitrary"),
        name="scatter_rows",
    )(pad_pos, pos.reshape(TOP_K, nt, tm).transpose(1, 0, 2), x)


def _expert_kernel(te_ref, rs_ref, nv_ref, na_ref, dst_ref, xs_hbm, wg_ref, wu_ref, wd_ref, y_hbm,
                   wg_sc, wu_sc, wd_sc, xbuf, ybuf, lsem, ssem, *, tm):
    j = pl.program_id(0)
    n_active = na_ref[0]
    active = j < n_active
    slot = j % 2
    new_expert = (j == 0) | (te_ref[j] != te_ref[jnp.maximum(j - 1, 0)])
    chunk_sizes = [c for c in (256, 128, 64, 32, 16, 8) if c <= tm]

    def load_tile(t, dst_slot, wait):
        n8 = ((nv_ref[t] + 7) // 8) * 8
        base = rs_ref[t]
        off = jnp.int32(0)
        for c in chunk_sizes:
            take = (n8 & c) != 0

            @pl.when(take)
            def _(off=off, c=c):
                copy = pltpu.make_async_copy(
                    xs_hbm.at[pl.ds(pl.multiple_of(base + off, 8), c)],
                    xbuf.at[dst_slot, pl.ds(pl.multiple_of(off, 8), c)], lsem.at[dst_slot])
                if wait:
                    copy.wait()
                else:
                    copy.start()

            off = off + jnp.where(take, c, 0)

    def wait_scatter(src_slot, n):
        def wait_row(r, carry):
            pltpu.make_async_copy(ybuf.at[src_slot, pl.ds(0, 1)], y_hbm.at[pl.ds(0, 1)], ssem.at[src_slot]).wait()
            return carry

        lax.fori_loop(0, n, wait_row, 0)

    @pl.when(j == 0)
    def _():
        xbuf[...] = jnp.zeros_like(xbuf)
        load_tile(0, 0, False)

    @pl.when(j + 1 < n_active)
    def _():
        load_tile(j + 1, 1 - slot, False)

    @pl.when(active & new_expert)
    def _():
        wg_sc[...] = wg_ref[0].astype(BF16)
        wu_sc[...] = wu_ref[0].astype(BF16)
        wd_sc[...] = wd_ref[0].astype(BF16)

    @pl.when(active)
    def _():
        load_tile(j, slot, True)

        @pl.when(j >= 2)
        def _():
            wait_scatter(slot, nv_ref[jnp.maximum(j - 2, 0)])

        x = xbuf[slot].astype(BF16)
        hg = jnp.dot(x, wg_sc[...], preferred_element_type=F32)
        hu = jnp.dot(x, wu_sc[...], preferred_element_type=F32)
        h = hg * jax.nn.sigmoid(hg) * hu
        ybuf[slot] = jnp.dot(h.astype(BF16), wd_sc[...], preferred_element_type=F32)

        def scatter_row(r, carry):
            pltpu.make_async_copy(ybuf.at[slot, pl.ds(r, 1)], y_hbm.at[pl.ds(dst_ref[0, 0, r], 1)],
                                  ssem.at[slot]).start()
            return carry

        lax.fori_loop(0, nv_ref[j], scatter_row, 0)

        @pl.when(j == n_active - 1)
        def _():
            wait_scatter(slot, nv_ref[j])

            @pl.when(j >= 1)
            def _():
                wait_scatter(1 - slot, nv_ref[jnp.maximum(j - 1, 0)])


def routed_experts(xs, plan, n_out_rows, w_gate, w_up, w_down, tm):
    d = xs.shape[1]
    f = w_gate.shape[2]
    nt = plan.tile_expert.shape[0]

    def dst_blk(j, te, rs, nv, na):
        return (jnp.minimum(j, na[0] - 1), 0, 0)

    def weight_blk(j, te, rs, nv, na):
        return (te[j], 0, 0)

    return pl.pallas_call(
        functools.partial(_expert_kernel, tm=tm),
        grid_spec=pltpu.PrefetchScalarGridSpec(
            num_scalar_prefetch=4,
            grid=(nt,),
            in_specs=[
                pl.BlockSpec((1, 1, tm), dst_blk, memory_space=pltpu.SMEM),
                pl.BlockSpec(memory_space=pl.ANY),
                pl.BlockSpec((1, d, f), weight_blk),
                pl.BlockSpec((1, d, f), weight_blk),
                pl.BlockSpec((1, f, d), weight_blk),
            ],
            out_specs=pl.BlockSpec(memory_space=pl.ANY),
            scratch_shapes=[pltpu.VMEM((d, f), BF16), pltpu.VMEM((d, f), BF16), pltpu.VMEM((f, d), BF16),
                            pltpu.VMEM((2, tm, d), xs.dtype), pltpu.VMEM((2, tm, d), F32),
                            pltpu.SemaphoreType.DMA((2,)), pltpu.SemaphoreType.DMA((2,))],
        ),
        out_shape=jax.ShapeDtypeStruct((n_out_rows, d), F32),
        compiler_params=_params("arbitrary"),
        name="routed_experts",
    )(plan.tile_expert, plan.tile_row_start, plan.n_valid, plan.n_active, plan.dst_tiles.reshape(nt, 1, tm),
      xs, w_gate, w_up, w_down)


def _shared_kernel(x_ref, wg_ref, wu_ref, wd_ref, o_ref):
    x = x_ref[...]
    hg = jnp.dot(x, wg_ref[...], preferred_element_type=F32)
    hu = jnp.dot(x, wu_ref[...], preferred_element_type=F32)
    h = hg * jax.nn.sigmoid(hg) * hu
    o_ref[...] = jnp.dot(h.astype(BF16), wd_ref[...], preferred_element_type=F32)


def shared_expert(x_bf, wg, wu, wd, tm=512):
    s, d = x_bf.shape
    f = wg.shape[1]
    whole = lambda r, c: pl.BlockSpec((r, c), lambda i: (0, 0))
    return pl.pallas_call(
        _shared_kernel,
        grid=(s // tm,),
        in_specs=[pl.BlockSpec((tm, d), lambda i: (i, 0)), whole(d, f), whole(d, f), whole(f, d)],
        out_specs=pl.BlockSpec((tm, d), lambda i: (i, 0)),
        out_shape=jax.ShapeDtypeStruct((s, d), F32),
        compiler_params=_params("parallel"),
        name="shared_expert",
    )(x_bf, wg.astype(BF16), wu.astype(BF16), wd.astype(BF16))


def _combine_kernel(x_ref, sh_ref, gate_ref, y0_ref, y1_ref, g_ref, b_ref, o_ref, obf_ref):
    gate = gate_ref[...]
    routed = gate[:, 0:1] * y0_ref[0] + gate[:, 1:2] * y1_ref[0]
    z = DEEPNORM_ALPHA * x_ref[...] + (routed + sh_ref[...])
    out = _layer_norm(z, g_ref[...], b_ref[...])
    o_ref[...] = out
    obf_ref[...] = out.astype(BF16)


def moe_combine(x, shared, gates, y, g, b, tm=256):
    s, d = x.shape
    assert TOP_K == 2
    row = pl.BlockSpec((tm, d), lambda i: (i, 0))
    vec = pl.BlockSpec((1, d), lambda i: (0, 0))
    return pl.pallas_call(
        _combine_kernel,
        grid=(s // tm,),
        in_specs=[
            row, row,
            pl.BlockSpec((tm, TOP_K), lambda i: (i, 0)),
            pl.BlockSpec((1, tm, d), lambda i: (0, i, 0)),
            pl.BlockSpec((1, tm, d), lambda i: (1, i, 0)),
            vec, vec,
        ],
        out_specs=[row, row],
        out_shape=[jax.ShapeDtypeStruct((s, d), F32), jax.ShapeDtypeStruct((s, d), BF16)],
        compiler_params=_params("parallel"),
        name="moe_combine",
    )(x, shared, gates.T, y, y, g.reshape(1, d), b.reshape(1, d))


def _rope_tables(pos, rot_dim, theta):
    half = rot_dim // 2
    inv = (1.0 / (theta ** (np.arange(half, dtype=np.float32) / half))).astype(np.float32)
    ang = pos.astype(F32)[:, None] * inv
    return jnp.cos(ang), jnp.sin(ang)


def _rope(x, cos, sin, rot_dim):
    half = rot_dim // 2
    x1 = x[..., :half]
    x2 = x[..., half:rot_dim]
    c = cos[:, None, :]
    s = sin[:, None, :]
    return jnp.concatenate([x1 * c - x2 * s, x1 * s + x2 * c, x[..., rot_dim:]], axis=-1)


class _RoutingPlan(NamedTuple):
    pos: jax.Array
    pad_pos: jax.Array
    n_rows: int
    tile_expert: jax.Array
    tile_row_start: jax.Array
    n_valid: jax.Array
    n_active: jax.Array
    dst_tiles: jax.Array


ROW_ALIGN = 8


def _routing_plan(experts, tm):
    k, s = experts.shape
    flat = experts.reshape(-1)
    expert_ids = jnp.arange(N_EXPERTS, dtype=jnp.int32)
    onehot = (flat[:, None] == expert_ids[None, :]).astype(jnp.int32)
    csum = jnp.cumsum(onehot, axis=0)
    rank = jnp.take_along_axis(csum, flat[:, None], axis=1)[:, 0] - 1
    sizes = csum[-1]
    padded = (sizes + ROW_ALIGN - 1) // ROW_ALIGN * ROW_ALIGN
    seg_end = jnp.cumsum(padded)
    seg_start = seg_end - padded
    pos = seg_start[flat] + rank
    n_rows = k * s + N_EXPERTS * (ROW_ALIGN - 1)

    i_pad = jnp.arange(ROW_ALIGN - 1, dtype=jnp.int32)
    is_pad = i_pad[None, :] < (padded - sizes)[:, None]
    spare = jnp.cumsum(jnp.logical_not(is_pad).reshape(-1).astype(jnp.int32)).reshape(is_pad.shape) - 1
    pad_pos = jnp.where(is_pad, (seg_start + sizes)[:, None] + i_pad[None, :], seg_end[-1] + spare).reshape(-1)

    tiles = (sizes + tm - 1) // tm
    tile_end = jnp.cumsum(tiles)
    n_active = tile_end[-1]
    n_tiles = k * s // tm + N_EXPERTS
    tile_ids = jnp.minimum(jnp.arange(n_tiles, dtype=jnp.int32), n_active - 1)
    tile_expert = jnp.searchsorted(tile_end, tile_ids, side="right").astype(jnp.int32)
    local = tile_ids - (tile_end - tiles)[tile_expert]
    tile_row_start = seg_start[tile_expert] + local * tm
    n_valid = jnp.clip(sizes[tile_expert] - local * tm, 0, tm)
    slot_of_row = jnp.zeros((n_rows,), jnp.int32).at[pos].set(jnp.arange(k * s, dtype=jnp.int32))
    rows = jnp.minimum(tile_row_start[:, None] + jnp.arange(tm, dtype=jnp.int32)[None, :], n_rows - 1)
    return _RoutingPlan(pos.reshape(k, s), pad_pos.astype(jnp.int32), n_rows, tile_expert,
                        tile_row_start.astype(jnp.int32), n_valid.astype(jnp.int32),
                        n_active.reshape(1).astype(jnp.int32), slot_of_row[rows])


def _grouped_moe(x, x_bf, router_w, router_b, w_gate, w_up, w_down, sh_gate, sh_up, sh_down, ln_g, ln_b,
                 expert_tm=256):
    s, d = x.shape
    experts, gates = moe_router(x, router_w, router_b)
    plan = _routing_plan(experts, expert_tm)
    xs = scatter_rows(x, plan.pos, plan.pad_pos, plan.n_rows)
    y = routed_experts(xs, plan, TOP_K * s, w_gate, w_up, w_down, expert_tm)
    shared = shared_expert(x_bf, sh_gate, sh_up, sh_down)
    return moe_combine(x, shared, gates, y.reshape(TOP_K, s, d), ln_g, ln_b)


def _rope_lane_tables(pos, half, theta, period, offset=0, limit=None):
    cos, sin = _rope_tables(pos, 2 * half, theta)
    lane = np.arange(LANES)
    p = lane % period - offset
    in_range = np.ones(LANES, bool) if limit is None else lane < limit
    first = (p >= 0) & (p < half) & in_range
    second = (p >= half) & (p < 2 * half) & in_range
    idx = np.where(first | second, p % half, 0)
    cos_l, sin_l = cos[:, idx], sin[:, idx]
    return jnp.stack([jnp.where(first | second, cos_l, 1.0),
                      jnp.where(second, sin_l, 0.0),
                      jnp.where(first, -sin_l, 0.0)])


def _mla_dsa_mixer(x, pos, w_in, q_norm, w_uq, kv_norm, w_ukv, w_out):
    cq, ckv, krope, dq, dkv, iq, ik, iw = jnp.split(w_in, [int(c) for c in np.cumsum(SPLITS_AB)[:-1]], axis=1)
    w_in_l = jnp.concatenate([cq, ckv, dq, dkv, iq, krope, ik, iw,
                              jnp.zeros((w_in.shape[0], LANES - IDX_HEADS), w_in.dtype)], axis=1)
    w_uq3 = w_uq.reshape(MLA_Q_LORA, MLA_HEADS, MLA_NOPE + MLA_ROPE)
    w_uq_l = jnp.concatenate([w_uq3[:, :, :MLA_NOPE].reshape(MLA_Q_LORA, -1),
                              jnp.pad(w_uq3[:, :, MLA_NOPE:], ((0, 0), (0, 0), (0, LANES - MLA_ROPE))
                                      ).reshape(MLA_Q_LORA, -1)], axis=1)
    w_ukv3 = w_ukv.reshape(MLA_KV_LORA, MLA_HEADS, MLA_NOPE + MLA_V)
    w_ukv_l = jnp.concatenate([w_ukv3[:, :, :MLA_NOPE].reshape(MLA_KV_LORA, -1),
                               w_ukv3[:, :, MLA_NOPE:].reshape(MLA_KV_LORA, -1)], axis=1)

    tab_m = _rope_lane_tables(pos, MLA_ROPE // 2, ROPE_THETA, LANES)
    tab_d = _rope_lane_tables(pos, DSA_ROT // 2, ROPE_THETA, LANES)
    tab_i = _rope_lane_tables(pos, IDX_ROT // 2, ROPE_THETA, IDX_DIM)
    tab_k = _rope_lane_tables(pos, MLA_ROPE // 2, ROPE_THETA, LANES, limit=MLA_ROPE)
    tab_ik = _rope_lane_tables(pos, IDX_ROT // 2, ROPE_THETA, LANES, offset=MLA_ROPE)
    tab_ki = jnp.concatenate([(tab_k[0] * tab_ik[0])[None], tab_k[1:], tab_ik[1:]])

    h0 = matmul(x, w_in_l, F32, 1024, 512, name="l0_in_proj")
    q_cat, k_nope, v_a, k_pe, qb, kb, vb, iq_r, ik_lo, ik_hi, iw_s = l0_attention_operands(
        h0, q_norm, w_uq_l, kv_norm, w_ukv_l, (tab_m, tab_d, tab_i, tab_ki))
    out_a = mla_attention(q_cat, k_nope, k_pe, v_a)
    out_b = dsa_attention(iq_r, iw_s, ik_lo, ik_hi, qb, kb, vb)
    return matmul_concat(out_a, out_b, w_out, F32, 1024, 512, name="l0_out_proj")


def _retention_mixer(x_bf, pos, w_in, gn_g, gn_b, w_out):
    h1 = matmul(x_bf, w_in, F32, 1024, 1024, name="l1_in_proj")
    cos, sin = _rope_tables(pos, RET_QK_DIM, RET_THETA)
    y = retention(h1, cos, sin, gn_g, gn_b)
    return matmul(y, w_out, F32, 1024, 512, name="l1_out_proj")


def kernel(x, positions, router_w, router_b, l0_w_in, l0_mla_q_norm, l0_mla_w_uq, l0_mla_kv_norm, l0_mla_w_ukv, l0_w_out, l1_w_in, l1_ret_gn_g, l1_ret_gn_b, l1_w_out, l0_ln_mix_g, l0_ln_mix_b, l0_moe_w_gate, l0_moe_w_up, l0_moe_w_down, l0_sh_gate, l0_sh_up, l0_sh_down, l0_ln_ffn_g, l0_ln_ffn_b, l1_ln_mix_g, l1_ln_mix_b, l1_moe_w_gate, l1_moe_w_up, l1_moe_w_down, l1_sh_gate, l1_sh_up, l1_sh_down, l1_ln_ffn_g, l1_ln_ffn_b):
    assert x.shape[0] == 1
    xt = x[0]
    pos = positions[0]

    mix = _mla_dsa_mixer(xt, pos, l0_w_in, l0_mla_q_norm, l0_mla_w_uq, l0_mla_kv_norm, l0_mla_w_ukv, l0_w_out)
    xt, xt_bf = add_layer_norm(xt, mix, l0_ln_mix_g, l0_ln_mix_b)
    xt, xt_bf = _grouped_moe(xt, xt_bf, router_w, router_b, l0_moe_w_gate, l0_moe_w_up, l0_moe_w_down,
                             l0_sh_gate, l0_sh_up, l0_sh_down, l0_ln_ffn_g, l0_ln_ffn_b)

    mix = _retention_mixer(xt_bf, pos, l1_w_in, l1_ret_gn_g, l1_ret_gn_b, l1_w_out)
    xt, xt_bf = add_layer_norm(xt, mix, l1_ln_mix_g, l1_ln_mix_b)
    xt, _ = _grouped_moe(xt, xt_bf, router_w, router_b, l1_moe_w_gate, l1_moe_w_up, l1_moe_w_down,
                         l1_sh_gate, l1_sh_up, l1_sh_down, l1_ln_ffn_g, l1_ln_ffn_b)
    return xt[None]
```

```python
import functools
from typing import NamedTuple

import numpy as np
import jax
import jax.numpy as jnp
from jax import lax
from jax.experimental import pallas as pl
from jax.experimental.pallas import tpu as pltpu

F32 = jnp.float32
BF16 = jnp.bfloat16

D_MODEL = 2048
DEPTH = 2
ROPE_THETA = 500000.0
MLA_HEADS = 8
MLA_Q_LORA = 512
MLA_KV_LORA = 256
MLA_NOPE = 128
MLA_ROPE = 64
MLA_V = 128
DSA_HEADS = 8
DSA_KV_HEADS = 2
DSA_HEAD_DIM = 128
DSA_ROT = DSA_HEAD_DIM // 4
IDX_HEADS = 16
IDX_DIM = 64
IDX_ROT = IDX_DIM // 4
IDX_TOPK_MAX = 256
RET_HEADS = 8
RET_QK_DIM = 256
RET_V_DIM = 512
RET_CHUNK = 128
RET_THETA = 10000.0
N_EXPERTS = 64
N_GROUPS = 8
EXPERTS_PER_GROUP = N_EXPERTS // N_GROUPS
TOP_K = 2
D_EXPERT = 512
D_SHARED = 1024
DEEPNORM_ALPHA = (2.0 * DEPTH) ** 0.25

SPLITS_AB = (MLA_Q_LORA, MLA_KV_LORA, MLA_ROPE, DSA_HEADS * DSA_HEAD_DIM,
             2 * DSA_KV_HEADS * DSA_HEAD_DIM, IDX_HEADS * IDX_DIM, IDX_DIM, IDX_HEADS)
IN_AB = sum(SPLITS_AB)

VMEM_LIMIT_BYTES = 56 * 1024 * 1024
LANES = 128
MASKED_SCORE = -1e30
LOG2_E = 1.4426950408889634
DMA_ISSUE_UNROLL = 8
INT32_MIN = -2 ** 31
INT16_MIN = -2 ** 15


def _params(*sem):
    return pltpu.CompilerParams(dimension_semantics=sem, vmem_limit_bytes=VMEM_LIMIT_BYTES)


def _mm_kernel(*refs, norm_eps):
    if norm_eps is None:
        a_ref, b_ref, o_ref = refs
        a = a_ref[...]
    else:
        a_ref, g_ref, b_ref, o_ref = refs
        af = a_ref[...].astype(F32)
        a = af * lax.rsqrt(jnp.mean(af * af, axis=-1, keepdims=True) + norm_eps) * g_ref[...]
    o_ref[...] = jnp.dot(a.astype(BF16), b_ref[...].astype(BF16),
                         preferred_element_type=F32).astype(o_ref.dtype)


def matmul(a, b, out_dtype, tm, tn, *, name, a_cols=None, norm_gain=None, norm_eps=1e-6):
    m = a.shape[0]
    k, n = b.shape
    col_blk = 0 if a_cols is None else a_cols[0]
    assert (a.shape[1] == k) if a_cols is None else (a_cols[1] == k)
    assert m % tm == 0 and n % tn == 0
    in_specs = [pl.BlockSpec((tm, k), lambda i, j: (i, col_blk))]
    args = [a]
    if norm_gain is not None:
        in_specs.append(pl.BlockSpec((1, k), lambda i, j: (0, 0)))
        args.append(norm_gain.reshape(1, k))
    in_specs.append(pl.BlockSpec((k, tn), lambda i, j: (0, j)))
    args.append(b)
    return pl.pallas_call(
        functools.partial(_mm_kernel, norm_eps=None if norm_gain is None else norm_eps),
        grid=(m // tm, n // tn),
        in_specs=in_specs,
        out_specs=pl.BlockSpec((tm, tn), lambda i, j: (i, j)),
        out_shape=jax.ShapeDtypeStruct((m, n), out_dtype),
        compiler_params=_params("parallel", "parallel"),
        name=name,
    )(*args)


def _mm2_kernel(a1_ref, a2_ref, b1_ref, b2_ref, o_ref):
    acc = jnp.dot(a1_ref[...].astype(BF16), b1_ref[...].astype(BF16), preferred_element_type=F32)
    acc = acc + jnp.dot(a2_ref[...].astype(BF16), b2_ref[...].astype(BF16), preferred_element_type=F32)
    o_ref[...] = acc.astype(o_ref.dtype)


def matmul_concat(a1, a2, b, out_dtype, tm, tn, *, name):
    m, k1 = a1.shape
    n = b.shape[1]
    assert a2.shape == (m, k1) and b.shape[0] == 2 * k1 and m % tm == 0 and n % tn == 0
    a_spec = pl.BlockSpec((tm, k1), lambda i, j: (i, 0))
    return pl.pallas_call(
        _mm2_kernel,
        grid=(m // tm, n // tn),
        in_specs=[a_spec, a_spec, pl.BlockSpec((k1, tn), lambda i, j: (0, j)), pl.BlockSpec((k1, tn), lambda i, j: (1, j))],
        out_specs=pl.BlockSpec((tm, tn), lambda i, j: (i, j)),
        out_shape=jax.ShapeDtypeStruct((m, n), out_dtype),
        compiler_params=_params("parallel", "parallel"),
        name=name,
    )(a1, a2, b, b)


_H0_CQ = 0
_H0_CKV = _H0_CQ + MLA_Q_LORA
_H0_DQ = _H0_CKV + MLA_KV_LORA
_H0_DKV = _H0_DQ + DSA_HEADS * DSA_HEAD_DIM
_H0_IQ = _H0_DKV + 2 * DSA_KV_HEADS * DSA_HEAD_DIM
_H0_KROPE_IK = _H0_IQ + IDX_HEADS * IDX_DIM
_H0_IW = _H0_KROPE_IK + LANES
_H0_WIDTH = _H0_IW + LANES


def _rope_lanes(x, tab_ref, shifts):
    out = x * tab_ref[0]
    for i, shift in enumerate(shifts):
        out = out + pltpu.roll(x, shift, 1) * tab_ref[1 + i]
    return out


def _l0_prep_kernel(h_ref, qn_ref, kvn_ref, wuq_ref, wukv_ref, tm_ref, td_ref, ti_ref, tki_ref,
                    qcat_ref, knope_ref, va_ref, kpe_ref, qb_ref, kb_ref, vb_ref, iq_ref, iklo_ref, ikhi_ref,
                    iw_ref):
    def rms(x, g_ref):
        return (x * lax.rsqrt(jnp.mean(x * x, axis=-1, keepdims=True) + 1e-6) * g_ref[...]).astype(BF16)

    def slab(col, j=0):
        return h_ref[:, col + j * LANES:col + (j + 1) * LANES]

    half_m, half_d, half_i = MLA_ROPE // 2, DSA_ROT // 2, IDX_ROT // 2
    shifts_m = (half_m, LANES - half_m)
    shifts_d = (half_d, LANES - half_d)
    shifts_i = (half_i, LANES - half_i)

    qa = jnp.dot(rms(h_ref[:, _H0_CQ:_H0_CQ + MLA_Q_LORA], qn_ref), wuq_ref[...], preferred_element_type=F32)
    kva = jnp.dot(rms(h_ref[:, _H0_CKV:_H0_CKV + MLA_KV_LORA], kvn_ref), wukv_ref[...],
                  preferred_element_type=F32)
    n_nope = MLA_HEADS * MLA_NOPE
    knope_ref[...] = kva[:, :n_nope].astype(BF16)
    va_ref[...] = kva[:, n_nope:].astype(BF16)
    q_scale = (MLA_NOPE + MLA_ROPE) ** -0.5 * LOG2_E
    for h in range(MLA_HEADS):
        nope = qa[:, h * LANES:(h + 1) * LANES]
        pe = qa[:, n_nope + h * LANES:n_nope + (h + 1) * LANES]
        qcat_ref[:, 2 * h * LANES:(2 * h + 1) * LANES] = (nope * q_scale).astype(BF16)
        qcat_ref[:, (2 * h + 1) * LANES:(2 * h + 2) * LANES] = (_rope_lanes(pe, tm_ref, shifts_m) * q_scale).astype(BF16)

    d_scale = DSA_HEAD_DIM ** -0.5 * LOG2_E
    for h in range(DSA_HEADS):
        qb_ref[:, h * LANES:(h + 1) * LANES] = (_rope_lanes(slab(_H0_DQ, h), td_ref, shifts_d) * d_scale).astype(BF16)
    for g in range(DSA_KV_HEADS):
        kb_ref[:, g * LANES:(g + 1) * LANES] = _rope_lanes(slab(_H0_DKV, g), td_ref, shifts_d).astype(BF16)
    vb_ref[...] = h_ref[:, _H0_DKV + DSA_KV_HEADS * LANES:_H0_DKV + 2 * DSA_KV_HEADS * LANES].astype(BF16)
    for j in range(IDX_HEADS * IDX_DIM // LANES):
        iq_ref[:, j * LANES:(j + 1) * LANES] = _rope_lanes(slab(_H0_IQ, j), ti_ref, shifts_i).astype(BF16)

    ki = _rope_lanes(slab(_H0_KROPE_IK), tki_ref, shifts_m + shifts_i)
    lane = lax.broadcasted_iota(jnp.int32, ki.shape, 1)
    kpe_ref[...] = jnp.where(lane < MLA_ROPE, ki, 0.0).astype(BF16)
    ik_hi = jnp.where(lane >= MLA_ROPE, ki, 0.0)
    ikhi_ref[...] = ik_hi.astype(BF16)
    iklo_ref[...] = pltpu.roll(ik_hi, LANES - MLA_ROPE, 1).astype(BF16)
    iw_ref[...] = h_ref[:, _H0_IW:_H0_IW + IDX_HEADS] * (IDX_HEADS ** -0.5 * IDX_DIM ** -0.5)


def l0_attention_operands(h0, q_norm, w_uq, kv_norm, w_ukv, tables, tm=256):
    s = h0.shape[0]
    bf = lambda n: jax.ShapeDtypeStruct((s, n), BF16)
    row = lambda n: pl.BlockSpec((tm, n), lambda i: (i, 0))
    whole = lambda a: pl.BlockSpec(a.shape, lambda i: (0,) * a.ndim)
    tab = lambda t: pl.BlockSpec((t.shape[0], tm, LANES), lambda i: (0, i, 0))
    widths = [2 * MLA_HEADS * LANES, MLA_HEADS * MLA_NOPE, MLA_HEADS * MLA_V, LANES, DSA_HEADS * DSA_HEAD_DIM,
              DSA_KV_HEADS * DSA_HEAD_DIM, DSA_KV_HEADS * DSA_HEAD_DIM, IDX_HEADS * IDX_DIM, LANES, LANES]
    w_uq = w_uq.astype(BF16)
    w_ukv = w_ukv.astype(BF16)
    q_norm = q_norm.reshape(1, -1)
    kv_norm = kv_norm.reshape(1, -1)
    return pl.pallas_call(
        _l0_prep_kernel,
        grid=(s // tm,),
        in_specs=[row(_H0_WIDTH), whole(q_norm), whole(kv_norm), whole(w_uq), whole(w_ukv)] + [tab(t) for t in tables],
        out_specs=[row(n) for n in widths] + [row(IDX_HEADS)],
        out_shape=[bf(n) for n in widths] + [jax.ShapeDtypeStruct((s, IDX_HEADS), F32)],
        compiler_params=_params("parallel"),
        name="l0_attention_operands",
    )(h0, q_norm, kv_norm, w_uq, w_ukv, *tables)


def _layer_norm(z, g, b):
    mu = jnp.mean(z, axis=-1, keepdims=True)
    zc = z - mu
    var = jnp.mean(zc * zc, axis=-1, keepdims=True)
    return zc * lax.rsqrt(var + 1e-5) * g + b


def _add_ln_kernel(x_ref, y_ref, g_ref, b_ref, o_ref, obf_ref):
    z = DEEPNORM_ALPHA * x_ref[...] + y_ref[...]
    out = _layer_norm(z, g_ref[...], b_ref[...])
    o_ref[...] = out
    obf_ref[...] = out.astype(BF16)


def add_layer_norm(x, y, g, b, tm=256):
    s, d = x.shape
    row = pl.BlockSpec((tm, d), lambda i: (i, 0))
    vec = pl.BlockSpec((1, d), lambda i: (0, 0))
    return pl.pallas_call(
        _add_ln_kernel,
        grid=(s // tm,),
        in_specs=[row, row, vec, vec],
        out_specs=[row, row],
        out_shape=[jax.ShapeDtypeStruct((s, d), F32), jax.ShapeDtypeStruct((s, d), BF16)],
        compiler_params=_params("parallel"),
        name="add_layer_norm",
    )(x, y, g.reshape(1, d), b.reshape(1, d))


def _lane_tile(x, n):
    return jnp.tile(x, (1, n))


def _softmax_step(s, v_ext, m_ref, acc_ref):
    tk = s.shape[1]
    m_prev = m_ref[...]
    m_new = jnp.maximum(m_prev, jnp.max(s, axis=1)[:, None])
    p = jnp.exp2(s - _lane_tile(m_new, tk // LANES))
    alpha = jnp.exp2(m_prev - m_new)
    pv = jnp.dot(p.astype(BF16), v_ext, preferred_element_type=F32)
    acc_ref[...] = _lane_tile(alpha, 2) * acc_ref[...] + pv
    m_ref[...] = m_new


def _mla_kernel(q_ref, kn_ref, kp_ref, v_ref, o_ref, m_sc, acc_sc, *, tq, tk, hp):
    qi = pl.program_id(1)
    m_sc[...] = jnp.full_like(m_sc, MASKED_SCORE)
    acc_sc[...] = jnp.zeros_like(acc_sc)
    ones = jnp.ones((tk, LANES), BF16)
    per_block = tq // tk

    def attend(c, diagonal_offset):
        kp = kp_ref[c]
        kn = kn_ref[c]
        v = v_ref[c]
        for j in range(hp):
            q = q_ref[:, j * 2 * LANES:(j + 1) * 2 * LANES]
            k = jnp.concatenate([kn[:, j * LANES:(j + 1) * LANES], kp], axis=1)
            s = lax.dot_general(q, k, (((1,), (1,)), ((), ())), preferred_element_type=F32)
            if diagonal_offset is not None:
                row = lax.broadcasted_iota(jnp.int32, (tq, tk), 0)
                col = diagonal_offset * tk + lax.broadcasted_iota(jnp.int32, (tq, tk), 1)
                s = jnp.where(col <= row, s, MASKED_SCORE)
            v_ext = jnp.concatenate([v[:, j * LANES:(j + 1) * LANES], ones], axis=1)
            _softmax_step(s, v_ext, m_sc.at[j], acc_sc.at[j])

    def below_diagonal(c, carry):
        attend(c, None)
        return carry

    lax.fori_loop(0, qi * per_block, below_diagonal, 0)
    for d in range(per_block):
        attend(qi * per_block + d, d)
    for j in range(hp):
        acc = acc_sc[j]
        o_ref[:, j * LANES:(j + 1) * LANES] = (acc[:, :LANES] / acc[:, LANES:]).astype(o_ref.dtype)


def mla_attention(q, k_nope, k_pe, v, tq=512, tk=512, hp=2):
    s = q.shape[0]
    h = MLA_HEADS
    nc = s // tk
    assert tq % tk == 0 and h % hp == 0
    per_head = pl.BlockSpec((nc, tk, hp * LANES), lambda hh, qi: (0, 0, hh))
    return pl.pallas_call(
        functools.partial(_mla_kernel, tq=tq, tk=tk, hp=hp),
        grid=(h // hp, s // tq),
        in_specs=[
            pl.BlockSpec((tq, hp * 2 * LANES), lambda hh, qi: (qi, hh)),
            per_head,
            pl.BlockSpec((nc, tk, LANES), lambda hh, qi: (0, 0, 0)),
            per_head,
        ],
        out_specs=pl.BlockSpec((tq, hp * LANES), lambda hh, qi: (qi, hh)),
        out_shape=jax.ShapeDtypeStruct((s, h * MLA_V), BF16),
        scratch_shapes=[pltpu.VMEM((hp, tq, LANES), F32), pltpu.VMEM((hp, tq, 2 * LANES), F32)],
        compiler_params=_params("parallel", "arbitrary"),
        name="mla_attention",
    )(q, k_nope.reshape(nc, tk, h * MLA_NOPE), k_pe.reshape(nc, tk, LANES), v.reshape(nc, tk, h * MLA_V))


def _sortable_key(x):
    bits = pltpu.bitcast(x, jnp.int32)
    return bits ^ ((bits >> 31) & 0x7FFFFFFF)


def _dsa_kernel(iq_ref, iw_ref, iklo_ref, ikhi_ref, q_ref, k_ref, v_ref, o_ref, key_sc, half_sc, wb_sc, m_sc,
                acc_sc, *, tq, tk, n_sel):
    i = pl.program_id(0)
    n_chunks = ((i + 1) * tq + tk - 1) // tk
    rep = DSA_HEADS // DSA_KV_HEADS
    lane_tiles = tk // LANES
    heads_per_block = LANES // IDX_DIM

    assert heads_per_block == 2
    iq_blocks = jnp.concatenate(
        [iq_ref[:, j * LANES:(j + 1) * LANES] for j in range(IDX_HEADS // heads_per_block)], axis=0)
    iw = iw_ref[...]
    for h in range(IDX_HEADS):
        wb_sc[h] = jnp.broadcast_to(iw[:, h:h + 1], (tq, LANES))

    def index_chunk(c, carry):
        ik = jnp.concatenate([iklo_ref[c], ikhi_ref[c]], axis=0)
        d = lax.dot_general(iq_blocks, ik, (((1,), (1,)), ((), ())), preferred_element_type=F32)
        d = jnp.maximum(d, 0.0)
        score = None
        for h in range(IDX_HEADS):
            j, part = divmod(h, heads_per_block)
            term = _lane_tile(wb_sc[h], lane_tiles) * d[j * tq:(j + 1) * tq, part * tk:(part + 1) * tk]
            score = term if score is None else score + term
        key = _sortable_key(score)
        key_sc[c] = key
        half_sc[c] = (key >> 16).astype(jnp.int16)
        return carry

    lax.fori_loop(0, n_chunks, index_chunk, 0)
    last = n_chunks - 1
    row = i * tq + lax.broadcasted_iota(jnp.int32, (tq, tk), 0)
    col = last * tk + lax.broadcasted_iota(jnp.int32, (tq, tk), 1)
    masked_last = jnp.where(col <= row, key_sc[last], INT32_MIN)
    key_sc[last] = masked_last
    half_sc[last] = (masked_last >> 16).astype(jnp.int16)

    def search16(base_count):
        def search_bit(b, state):
            thr, above = state
            cand = thr + lax.shift_left(jnp.int32(1), 15 - b)
            cand16 = cand.astype(jnp.int16)

            def count_chunk(c, cnt):
                half = half_sc[c]
                for j in range(lane_tiles):
                    cnt = cnt + jnp.where(half[:, j * LANES:(j + 1) * LANES] >= cand16, jnp.int16(1), jnp.int16(0))
                return cnt

            cnt = lax.fori_loop(0, n_chunks, count_chunk, jnp.zeros((tq, LANES), jnp.int16))
            total = base_count + jnp.sum(cnt.astype(jnp.int32).astype(F32), axis=1)[:, None]
            ok = total >= n_sel
            return jnp.where(ok, cand, thr), jnp.where(ok, above, total)

        init = (jnp.full((tq, LANES), INT16_MIN, jnp.int32), jnp.broadcast_to(base_count, (tq, LANES)))
        return lax.fori_loop(0, 16, search_bit, init)

    thr_hi, n_above = search16(jnp.zeros((tq, LANES), F32))
    thr_hi_t = _lane_tile(thr_hi, lane_tiles)

    def low_halves(c, carry):
        key = key_sc[c]
        low = (key & 0xFFFF) + INT16_MIN
        half_sc[c] = jnp.where((key >> 16) == thr_hi_t, low, INT16_MIN).astype(jnp.int16)
        return carry

    lax.fori_loop(0, n_chunks, low_halves, 0)
    thr_lo, _ = search16(n_above)
    thr = (thr_hi << 16) | (thr_lo - INT16_MIN)
    thr = _lane_tile(jnp.maximum(thr, INT32_MIN + 1), lane_tiles)

    m_sc[...] = jnp.full_like(m_sc, MASKED_SCORE)
    acc_sc[...] = jnp.zeros_like(acc_sc)
    ones = jnp.ones((tk, LANES), BF16)

    def attend_chunk(c, carry):
        bias = jnp.where(key_sc[c] >= thr, 0.0, MASKED_SCORE)
        kc = k_ref[c]
        vc = v_ref[c]
        for g in range(DSA_KV_HEADS):
            cols = slice(g * DSA_HEAD_DIM, (g + 1) * DSA_HEAD_DIM)
            qg = jnp.concatenate(
                [q_ref[:, (g * rep + r) * DSA_HEAD_DIM:(g * rep + r + 1) * DSA_HEAD_DIM] for r in range(rep)], axis=0)
            s = lax.dot_general(qg, kc[:, cols], (((1,), (1,)), ((), ())), preferred_element_type=F32)
            s = (s.reshape(rep, tq, tk) + bias[None]).reshape(rep * tq, tk)
            _softmax_step(s, jnp.concatenate([vc[:, cols], ones], axis=1), m_sc.at[g], acc_sc.at[g])
        return carry

    lax.fori_loop(0, n_chunks, attend_chunk, 0)
    for g in range(DSA_KV_HEADS):
        acc = acc_sc[g]
        out = acc[:, :LANES] / acc[:, LANES:]
        for r in range(rep):
            hh = g * rep + r
            o_ref[:, hh * DSA_HEAD_DIM:(hh + 1) * DSA_HEAD_DIM] = out[r * tq:(r + 1) * tq].astype(o_ref.dtype)


def dsa_attention(iq, iw, ik_lo, ik_hi, q, k, v, tq=128, tk=512):
    s = q.shape[0]
    n_sel = min(IDX_TOPK_MAX, s // 4)
    nc = s // tk
    assert tk >= n_sel and s % tk == 0 and tk % tq == 0 and DSA_HEAD_DIM == LANES
    rep = DSA_HEADS // DSA_KV_HEADS
    kvw = DSA_KV_HEADS * DSA_HEAD_DIM
    whole3 = lambda i: (0, 0, 0)
    return pl.pallas_call(
        functools.partial(_dsa_kernel, tq=tq, tk=tk, n_sel=n_sel),
        grid=(s // tq,),
        in_specs=[
            pl.BlockSpec((tq, IDX_HEADS * IDX_DIM), lambda i: (i, 0)),
            pl.BlockSpec((tq, IDX_HEADS), lambda i: (i, 0)),
            pl.BlockSpec((nc, tk, LANES), whole3),
            pl.BlockSpec((nc, tk, LANES), whole3),
            pl.BlockSpec((tq, DSA_HEADS * DSA_HEAD_DIM), lambda i: (i, 0)),
            pl.BlockSpec((nc, tk, kvw), whole3),
            pl.BlockSpec((nc, tk, kvw), whole3),
        ],
        out_specs=pl.BlockSpec((tq, DSA_HEADS * DSA_HEAD_DIM), lambda i: (i, 0)),
        out_shape=jax.ShapeDtypeStruct((s, DSA_HEADS * DSA_HEAD_DIM), BF16),
        scratch_shapes=[
            pltpu.VMEM((nc, tq, tk), jnp.int32),
            pltpu.VMEM((nc, tq, tk), jnp.int16),
            pltpu.VMEM((IDX_HEADS, tq, LANES), F32),
            pltpu.VMEM((DSA_KV_HEADS, rep * tq, LANES), F32),
            pltpu.VMEM((DSA_KV_HEADS, rep * tq, 2 * LANES), F32),
        ],
        compiler_params=_params("parallel"),
        name="dsa_attention",
    )(iq, iw, ik_lo.reshape(nc, tk, LANES), ik_hi.reshape(nc, tk, LANES), q, k.reshape(nc, tk, kvw),
      v.reshape(nc, tk, kvw))


def _retention_tables():
    h, c = RET_HEADS, RET_CHUNK
    log_g = np.log(1.0 - 2.0 ** (-5.0 - np.arange(h, dtype=np.float32))).astype(np.float32).astype(np.float64)
    idx = np.arange(c, dtype=np.float64)
    diff = idx[:, None] - idx[None, :]
    decay_in = np.where(diff[None] >= 0, np.exp(np.maximum(diff, 0.0)[None] * log_g[:, None, None]), 0.0)
    xi = np.exp((idx + 1.0)[None, :] * log_g[:, None])
    zeta = np.exp((c - 1.0 - idx)[None, :] * log_g[:, None])
    chunk_decay = np.exp(c * log_g)
    lane = np.ones((1, 1, LANES))
    return (decay_in.astype(np.float32), (xi[:, :, None] * lane).astype(np.float32),
            (zeta[:, :, None] * lane).astype(np.float32),
            (chunk_decay[:, None, None] * np.ones((1, 8, LANES))).astype(np.float32))


def _retention_kernel(q_ref, k_ref, v_ref, g_ref, cos_ref, sin_ref, din_ref, xi_ref, zeta_ref, cd_ref,
                      gng_ref, gnb_ref, o_ref, r_sc):
    n = pl.program_id(1)
    half = RET_QK_DIM // 2

    @pl.when(n == 0)
    def _():
        r_sc[...] = jnp.zeros_like(r_sc)

    scale = RET_QK_DIM ** -0.5
    xi = jnp.concatenate([xi_ref[0]] * (RET_V_DIM // LANES), axis=1)
    zeta = jnp.concatenate([zeta_ref[0]] * (RET_QK_DIM // LANES), axis=1)

    for ci in range(q_ref.shape[0] // RET_CHUNK):
        rows = slice(ci * RET_CHUNK, (ci + 1) * RET_CHUNK)
        cos = cos_ref[rows]
        sin = sin_ref[rows]

        def rope(x):
            x1 = x[:, :half]
            x2 = x[:, half:]
            return x1 * cos - x2 * sin, x1 * sin + x2 * cos

        q1, q2 = rope(q_ref[rows])
        k1, k2 = rope(k_ref[rows])
        qr = jnp.concatenate([q1, q2], axis=1).astype(BF16)
        kr = jnp.concatenate([k1 * scale, k2 * scale], axis=1)
        v = v_ref[rows].astype(BF16)

        inner = lax.dot_general(qr, kr.astype(BF16), (((1,), (1,)), ((), ())), preferred_element_type=F32)
        inner = inner * din_ref[0]
        r = r_sc[...]
        cross = jnp.dot(qr, r.astype(BF16), preferred_element_type=F32)
        o = jnp.dot(inner.astype(BF16), v, preferred_element_type=F32) + cross * xi

        kz = (kr * zeta).astype(BF16)
        upd = lax.dot_general(kz, v, (((0,), (0,)), ((), ())), preferred_element_type=F32)
        r_sc[...] = cd_ref[0][:1, :1] * r + upd

        mu = jnp.mean(o, axis=-1, keepdims=True)
        oc = o - mu
        var = jnp.mean(oc * oc, axis=-1, keepdims=True)
        y = oc * lax.rsqrt(var + 1e-5) * gng_ref[...] + gnb_ref[...]
        gate = g_ref[rows]
        o_ref[rows] = (gate * jax.nn.sigmoid(gate) * y).astype(o_ref.dtype)


def retention(h1, cos, sin, gn_g, gn_b, chunks_per_step=8):
    s = h1.shape[0]
    hh, dk, dv = RET_HEADS, RET_QK_DIM, RET_V_DIM
    c = RET_CHUNK * chunks_per_step
    assert s % c == 0
    din, xi, zeta, cd = (jnp.asarray(t) for t in _retention_tables())
    k_blk0 = hh * dk // dk
    v_blk0 = 2 * hh * dk // dv
    g_blk0 = v_blk0 + hh
    per_head = lambda r, w: pl.BlockSpec((1, r, w), lambda h, n: (h, 0, 0))
    return pl.pallas_call(
        _retention_kernel,
        grid=(hh, s // c),
        in_specs=[
            pl.BlockSpec((c, dk), lambda h, n: (n, h)),
            pl.BlockSpec((c, dk), lambda h, n: (n, k_blk0 + h)),
            pl.BlockSpec((c, dv), lambda h, n: (n, v_blk0 + h)),
            pl.BlockSpec((c, dv), lambda h, n: (n, g_blk0 + h)),
            pl.BlockSpec((c, dk // 2), lambda h, n: (n, 0)),
            pl.BlockSpec((c, dk // 2), lambda h, n: (n, 0)),
            per_head(RET_CHUNK, RET_CHUNK), per_head(RET_CHUNK, LANES), per_head(RET_CHUNK, LANES),
            per_head(8, LANES),
            pl.BlockSpec((1, dv), lambda h, n: (0, h)),
            pl.BlockSpec((1, dv), lambda h, n: (0, h)),
        ],
        out_specs=pl.BlockSpec((c, dv), lambda h, n: (n, h)),
        out_shape=jax.ShapeDtypeStruct((s, hh * dv), BF16),
        scratch_shapes=[pltpu.VMEM((dk, dv), F32)],
        compiler_params=_params("parallel", "arbitrary"),
        name="retention",
    )(h1, h1, h1, h1, cos, sin, din, xi, zeta, cd, gn_g.reshape(1, -1), gn_b.reshape(1, -1))


def _first_argmax(v, idx, n):
    m = jnp.max(v, axis=0, keepdims=True)
    first = jnp.min(jnp.where(v == m, idx, n), axis=0, keepdims=True)
    return m, first


def _router_kernel(x_ref, rwt_ref, rb_ref, e_ref, g_ref):
    tm = x_ref.shape[0]
    epg = EXPERTS_PER_GROUP
    logits = lax.dot_general(rwt_ref[...], x_ref[...], (((1,), (1,)), ((), ())),
                             precision=lax.Precision.HIGHEST, preferred_element_type=F32)
    scores = jax.nn.sigmoid(logits)
    biased = scores + rb_ref[...]
    idx = lax.broadcasted_iota(jnp.int32, (epg, tm), 0)

    group_scores = []
    for g in range(N_GROUPS):
        v = biased[g * epg:(g + 1) * epg]
        m1, first = _first_argmax(v, idx, epg)
        m2 = jnp.max(jnp.where(idx == first, -jnp.inf, v), axis=0, keepdims=True)
        group_scores.append(m1 + m2)
    gmax = group_scores[0]
    for g in range(1, N_GROUPS):
        gmax = jnp.maximum(gmax, group_scores[g])
    gsel = jnp.full((1, tm), N_GROUPS, jnp.int32)
    for g in range(N_GROUPS - 1, -1, -1):
        gsel = jnp.where(group_scores[g] == gmax, g, gsel)

    in_biased = jnp.zeros((epg, tm), F32)
    in_scores = jnp.zeros((epg, tm), F32)
    for g in range(N_GROUPS):
        pick = gsel == g
        in_biased = jnp.where(pick, biased[g * epg:(g + 1) * epg], in_biased)
        in_scores = jnp.where(pick, scores[g * epg:(g + 1) * epg], in_scores)
    _, loc1 = _first_argmax(in_biased, idx, epg)
    _, loc2 = _first_argmax(jnp.where(idx == loc1, -jnp.inf, in_biased), idx, epg)
    s1 = jnp.sum(jnp.where(idx == loc1, in_scores, 0.0), axis=0, keepdims=True)
    s2 = jnp.sum(jnp.where(idx == loc2, in_scores, 0.0), axis=0, keepdims=True)
    denom = s1 + s2
    e_ref[...] = jnp.concatenate([gsel * epg + loc1, gsel * epg + loc2], axis=0)
    g_ref[...] = jnp.concatenate([s1 / denom, s2 / denom], axis=0)


def moe_router(x, router_w, router_b, tm=512):
    s, d = x.shape
    return pl.pallas_call(
        _router_kernel,
        grid=(s // tm,),
        in_specs=[
            pl.BlockSpec((tm, d), lambda i: (i, 0)),
            pl.BlockSpec((N_EXPERTS, d), lambda i: (0, 0)),
            pl.BlockSpec((N_EXPERTS, 1), lambda i: (0, 0)),
        ],
        out_specs=[pl.BlockSpec((TOP_K, tm), lambda i: (0, i)), pl.BlockSpec((TOP_K, tm), lambda i: (0, i))],
        out_shape=[jax.ShapeDtypeStruct((TOP_K, s), jnp.int32), jax.ShapeDtypeStruct((TOP_K, s), F32)],
        compiler_params=_params("parallel"),
        name="moe_router",
    )(x, router_w.T, router_b.reshape(N_EXPERTS, 1))


def _scatter_rows_kernel(pad_ref, pos_ref, x_ref, xs_hbm, zero_sc, sem, zsem, *, tm, n_pad):
    i = pl.program_id(0)

    def zero_copy(p):
        return pltpu.make_async_copy(zero_sc, xs_hbm.at[pl.ds(pad_ref[p], 1)], zsem)

    @pl.when(i == 0)
    def _():
        zero_sc[...] = jnp.zeros_like(zero_sc)

        def start_zero(p, carry):
            zero_copy(p).start()
            return carry

        lax.fori_loop(0, n_pad, start_zero, 0)

    def start(g, carry):
        for u in range(DMA_ISSUE_UNROLL):
            r = g * DMA_ISSUE_UNROLL + u
            for k in range(TOP_K):
                pltpu.make_async_copy(x_ref.at[pl.ds(r, 1)], xs_hbm.at[pl.ds(pos_ref[0, k, r], 1)],
                                      sem).start(priority=k % 2)
        return carry

    lax.fori_loop(0, tm // DMA_ISSUE_UNROLL, start, 0)
    for k in range(TOP_K):
        pltpu.make_async_copy(x_ref, xs_hbm.at[pl.ds(0, tm)], sem).wait()

    @pl.when(i == 0)
    def _():
        def wait_zero(p, carry):
            zero_copy(p).wait()
            return carry

        lax.fori_loop(0, n_pad, wait_zero, 0)


def scatter_rows(x, pos, pad_pos, n_rows, tm=256):
    s, d = x.shape
    nt = s // tm
    n_pad = pad_pos.shape[0]
    return pl.pallas_call(
        functools.partial(_scatter_rows_kernel, tm=tm, n_pad=n_pad),
        grid_spec=pltpu.PrefetchScalarGridSpec(
            num_scalar_prefetch=1,
            grid=(nt,),
            in_specs=[
                pl.BlockSpec((1, TOP_K, tm), lambda i, pad: (i, 0, 0), memory_space=pltpu.SMEM),
                pl.BlockSpec((tm, d), lambda i, pad: (i, 0)),
            ],
            out_specs=pl.BlockSpec(memory_space=pl.ANY),
            scratch_shapes=[pltpu.VMEM((1, d), x.dtype), pltpu.SemaphoreType.DMA(()), pltpu.SemaphoreType.DMA(())],
        ),
        out_shape=jax.ShapeDtypeStruct((n_rows, d), x.dtype),
        compiler_params=_params("arbitrary"),
        name="scatter_rows",
    )(pad_pos, pos.reshape(TOP_K, nt, tm).transpose(1, 0, 2), x)


def _expert_kernel(te_ref, rs_ref, nv_ref, na_ref, dst_ref, xs_hbm, wg_ref, wu_ref, wd_ref, y_hbm,
                   wg_sc, wu_sc, wd_sc, xbuf, ybuf, lsem, ssem, *, tm):
    j = pl.program_id(0)
    n_active = na_ref[0]
    active = j < n_active
    slot = j % 2
    new_expert = (j == 0) | (te_ref[j] != te_ref[jnp.maximum(j - 1, 0)])
    chunk_sizes = [c for c in (256, 128, 64, 32, 16, 8) if c <= tm]

    def load_tile(t, dst_slot, wait):
        n8 = ((nv_ref[t] + 7) // 8) * 8
        base = rs_ref[t]
        off = jnp.int32(0)
        for c in chunk_sizes:
            take = (n8 & c) != 0

            @pl.when(take)
            def _(off=off, c=c):
                copy = pltpu.make_async_copy(
                    xs_hbm.at[pl.ds(pl.multiple_of(base + off, 8), c)],
                    xbuf.at[dst_slot, pl.ds(pl.multiple_of(off, 8), c)], lsem.at[dst_slot])
                if wait:
                    copy.wait()
                else:
                    copy.start()

            off = off + jnp.where(take, c, 0)

    def wait_scatter(src_slot, n):
        def wait_row(r, carry):
            pltpu.make_async_copy(ybuf.at[src_slot, pl.ds(0, 1)], y_hbm.at[pl.ds(0, 1)], ssem.at[src_slot]).wait()
            return carry

        lax.fori_loop(0, n, wait_row, 0)

    @pl.when(j == 0)
    def _():
        xbuf[...] = jnp.zeros_like(xbuf)
        load_tile(0, 0, False)

    @pl.when(j + 1 < n_active)
    def _():
        load_tile(j + 1, 1 - slot, False)

    @pl.when(active & new_expert)
    def _():
        wg_sc[...] = wg_ref[0].astype(BF16)
        wu_sc[...] = wu_ref[0].astype(BF16)
        wd_sc[...] = wd_ref[0].astype(BF16)

    @pl.when(active)
    def _():
        load_tile(j, slot, True)

        @pl.when(j >= 2)
        def _():
            wait_scatter(slot, nv_ref[jnp.maximum(j - 2, 0)])

        x = xbuf[slot].astype(BF16)
        hg = jnp.dot(x, wg_sc[...], preferred_element_type=F32)
        hu = jnp.dot(x, wu_sc[...], preferred_element_type=F32)
        h = hg * jax.nn.sigmoid(hg) * hu
        ybuf[slot] = jnp.dot(h.astype(BF16), wd_sc[...], preferred_element_type=F32)

        def scatter_row(r, carry):
            pltpu.make_async_copy(ybuf.at[slot, pl.ds(r, 1)], y_hbm.at[pl.ds(dst_ref[0, 0, r], 1)],
                                  ssem.at[slot]).start()
            return carry

        lax.fori_loop(0, nv_ref[j], scatter_row, 0)

        @pl.when(j == n_active - 1)
        def _():
            wait_scatter(slot, nv_ref[j])

            @pl.when(j >= 1)
            def _():
                wait_scatter(1 - slot, nv_ref[jnp.maximum(j - 1, 0)])


def routed_experts(xs, plan, n_out_rows, w_gate, w_up, w_down, tm):
    d = xs.shape[1]
    f = w_gate.shape[2]
    nt = plan.tile_expert.shape[0]

    def dst_blk(j, te, rs, nv, na):
        return (jnp.minimum(j, na[0] - 1), 0, 0)

    def weight_blk(j, te, rs, nv, na):
        return (te[j], 0, 0)

    return pl.pallas_call(
        functools.partial(_expert_kernel, tm=tm),
        grid_spec=pltpu.PrefetchScalarGridSpec(
            num_scalar_prefetch=4,
            grid=(nt,),
            in_specs=[
                pl.BlockSpec((1, 1, tm), dst_blk, memory_space=pltpu.SMEM),
                pl.BlockSpec(memory_space=pl.ANY),
                pl.BlockSpec((1, d, f), weight_blk),
                pl.BlockSpec((1, d, f), weight_blk),
                pl.BlockSpec((1, f, d), weight_blk),
            ],
            out_specs=pl.BlockSpec(memory_space=pl.ANY),
            scratch_shapes=[pltpu.VMEM((d, f), BF16), pltpu.VMEM((d, f), BF16), pltpu.VMEM((f, d), BF16),
                            pltpu.VMEM((2, tm, d), xs.dtype), pltpu.VMEM((2, tm, d), F32),
                            pltpu.SemaphoreType.DMA((2,)), pltpu.SemaphoreType.DMA((2,))],
        ),
        out_shape=jax.ShapeDtypeStruct((n_out_rows, d), F32),
        compiler_params=_params("arbitrary"),
        name="routed_experts",
    )(plan.tile_expert, plan.tile_row_start, plan.n_valid, plan.n_active, plan.dst_tiles.reshape(nt, 1, tm),
      xs, w_gate, w_up, w_down)


def _shared_kernel(x_ref, wg_ref, wu_ref, wd_ref, o_ref):
    x = x_ref[...]
    hg = jnp.dot(x, wg_ref[...], preferred_element_type=F32)
    hu = jnp.dot(x, wu_ref[...], preferred_element_type=F32)
    h = hg * jax.nn.sigmoid(hg) * hu
    o_ref[...] = jnp.dot(h.astype(BF16), wd_ref[...], preferred_element_type=F32)


def shared_expert(x_bf, wg, wu, wd, tm=512):
    s, d = x_bf.shape
    f = wg.shape[1]
    whole = lambda r, c: pl.BlockSpec((r, c), lambda i: (0, 0))
    return pl.pallas_call(
        _shared_kernel,
        grid=(s // tm,),
        in_specs=[pl.BlockSpec((tm, d), lambda i: (i, 0)), whole(d, f), whole(d, f), whole(f, d)],
        out_specs=pl.BlockSpec((tm, d), lambda i: (i, 0)),
        out_shape=jax.ShapeDtypeStruct((s, d), F32),
        compiler_params=_params("parallel"),
        name="shared_expert",
    )(x_bf, wg.astype(BF16), wu.astype(BF16), wd.astype(BF16))


def _combine_kernel(x_ref, sh_ref, gate_ref, y0_ref, y1_ref, g_ref, b_ref, o_ref, obf_ref):
    gate = gate_ref[...]
    routed = gate[:, 0:1] * y0_ref[0] + gate[:, 1:2] * y1_ref[0]
    z = DEEPNORM_ALPHA * x_ref[...] + (routed + sh_ref[...])
    out = _layer_norm(z, g_ref[...], b_ref[...])
    o_ref[...] = out
    obf_ref[...] = out.astype(BF16)


def moe_combine(x, shared, gates, y, g, b, tm=256):
    s, d = x.shape
    assert TOP_K == 2
    row = pl.BlockSpec((tm, d), lambda i: (i, 0))
    vec = pl.BlockSpec((1, d), lambda i: (0, 0))
    return pl.pallas_call(
        _combine_kernel,
        grid=(s // tm,),
        in_specs=[
            row, row,
            pl.BlockSpec((tm, TOP_K), lambda i: (i, 0)),
            pl.BlockSpec((1, tm, d), lambda i: (0, i, 0)),
            pl.BlockSpec((1, tm, d), lambda i: (1, i, 0)),
            vec, vec,
        ],
        out_specs=[row, row],
        out_shape=[jax.ShapeDtypeStruct((s, d), F32), jax.ShapeDtypeStruct((s, d), BF16)],
        compiler_params=_params("parallel"),
        name="moe_combine",
    )(x, shared, gates.T, y, y, g.reshape(1, d), b.reshape(1, d))


def _rope_tables(pos, rot_dim, theta):
    half = rot_dim // 2
    inv = (1.0 / (theta ** (np.arange(half, dtype=np.float32) / half))).astype(np.float32)
    ang = pos.astype(F32)[:, None] * inv
    return jnp.cos(ang), jnp.sin(ang)


def _rope(x, cos, sin, rot_dim):
    half = rot_dim // 2
    x1 = x[..., :half]
    x2 = x[..., half:rot_dim]
    c = cos[:, None, :]
    s = sin[:, None, :]
    return jnp.concatenate([x1 * c - x2 * s, x1 * s + x2 * c, x[..., rot_dim:]], axis=-1)


class _RoutingPlan(NamedTuple):
    pos: jax.Array
    pad_pos: jax.Array
    n_rows: int
    tile_expert: jax.Array
    tile_row_start: jax.Array
    n_valid: jax.Array
    n_active: jax.Array
    dst_tiles: jax.Array


ROW_ALIGN = 8


def _routing_plan(experts, tm):
    k, s = experts.shape
    flat = experts.reshape(-1)
    expert_ids = jnp.arange(N_EXPERTS, dtype=jnp.int32)
    onehot = (flat[:, None] == expert_ids[None, :]).astype(jnp.int32)
    csum = jnp.cumsum(onehot, axis=0)
    rank = jnp.take_along_axis(csum, flat[:, None], axis=1)[:, 0] - 1
    sizes = csum[-1]
    padded = (sizes + ROW_ALIGN - 1) // ROW_ALIGN * ROW_ALIGN
    seg_end = jnp.cumsum(padded)
    seg_start = seg_end - padded
    pos = seg_start[flat] + rank
    n_rows = k * s + N_EXPERTS * (ROW_ALIGN - 1)

    i_pad = jnp.arange(ROW_ALIGN - 1, dtype=jnp.int32)
    is_pad = i_pad[None, :] < (padded - sizes)[:, None]
    spare = jnp.cumsum(jnp.logical_not(is_pad).reshape(-1).astype(jnp.int32)).reshape(is_pad.shape) - 1
    pad_pos = jnp.where(is_pad, (seg_start + sizes)[:, None] + i_pad[None, :], seg_end[-1] + spare).reshape(-1)

    tiles = (sizes + tm - 1) // tm
    tile_end = jnp.cumsum(tiles)
    n_active = tile_end[-1]
    n_tiles = k * s // tm + N_EXPERTS
    tile_ids = jnp.minimum(jnp.arange(n_tiles, dtype=jnp.int32), n_active - 1)
    tile_expert = jnp.searchsorted(tile_end, tile_ids, side="right").astype(jnp.int32)
    local = tile_ids - (tile_end - tiles)[tile_expert]
    tile_row_start = seg_start[tile_expert] + local * tm
    n_valid = jnp.clip(sizes[tile_expert] - local * tm, 0, tm)
    slot_of_row = jnp.zeros((n_rows,), jnp.int32).at[pos].set(jnp.arange(k * s, dtype=jnp.int32))
    rows = jnp.minimum(tile_row_start[:, None] + jnp.arange(tm, dtype=jnp.int32)[None, :], n_rows - 1)
    return _RoutingPlan(pos.reshape(k, s), pad_pos.astype(jnp.int32), n_rows, tile_expert,
                        tile_row_start.astype(jnp.int32), n_valid.astype(jnp.int32),
                        n_active.reshape(1).astype(jnp.int32), slot_of_row[rows])


def _grouped_moe(x, x_bf, router_w, router_b, w_gate, w_up, w_down, sh_gate, sh_up, sh_down, ln_g, ln_b,
                 expert_tm=256):
    s, d = x.shape
    experts, gates = moe_router(x, router_w, router_b)
    plan = _routing_plan(experts, expert_tm)
    xs = scatter_rows(x, plan.pos, plan.pad_pos, plan.n_rows)
    y = routed_experts(xs, plan, TOP_K * s, w_gate, w_up, w_down, expert_tm)
    shared = shared_expert(x_bf, sh_gate, sh_up, sh_down)
    return moe_combine(x, shared, gates, y.reshape(TOP_K, s, d), ln_g, ln_b)


def _rope_lane_tables(pos, half, theta, period, offset=0, limit=None):
    cos, sin = _rope_tables(pos, 2 * half, theta)
    lane = np.arange(LANES)
    p = lane % period - offset
    in_range = np.ones(LANES, bool) if limit is None else lane < limit
    first = (p >= 0) & (p < half) & in_range
    second = (p >= half) & (p < 2 * half) & in_range
    idx = np.where(first | second, p % half, 0)
    cos_l, sin_l = cos[:, idx], sin[:, idx]
    return jnp.stack([jnp.where(first | second, cos_l, 1.0),
                      jnp.where(second, sin_l, 0.0),
                      jnp.where(first, -sin_l, 0.0)])


def _mla_dsa_mixer(x, pos, w_in, q_norm, w_uq, kv_norm, w_ukv, w_out):
    cq, ckv, krope, dq, dkv, iq, ik, iw = jnp.split(w_in, [int(c) for c in np.cumsum(SPLITS_AB)[:-1]], axis=1)
    w_in_l = jnp.concatenate([cq, ckv, dq, dkv, iq, krope, ik, iw,
                              jnp.zeros((w_in.shape[0], LANES - IDX_HEADS), w_in.dtype)], axis=1)
    w_uq3 = w_uq.reshape(MLA_Q_LORA, MLA_HEADS, MLA_NOPE + MLA_ROPE)
    w_uq_l = jnp.concatenate([w_uq3[:, :, :MLA_NOPE].reshape(MLA_Q_LORA, -1),
                              jnp.pad(w_uq3[:, :, MLA_NOPE:], ((0, 0), (0, 0), (0, LANES - MLA_ROPE))
                                      ).reshape(MLA_Q_LORA, -1)], axis=1)
    w_ukv3 = w_ukv.reshape(MLA_KV_LORA, MLA_HEADS, MLA_NOPE + MLA_V)
    w_ukv_l = jnp.concatenate([w_ukv3[:, :, :MLA_NOPE].reshape(MLA_KV_LORA, -1),
                               w_ukv3[:, :, MLA_NOPE:].reshape(MLA_KV_LORA, -1)], axis=1)

    tab_m = _rope_lane_tables(pos, MLA_ROPE // 2, ROPE_THETA, LANES)
    tab_d = _rope_lane_tables(pos, DSA_ROT // 2, ROPE_THETA, LANES)
    tab_i = _rope_lane_tables(pos, IDX_ROT // 2, ROPE_THETA, IDX_DIM)
    tab_k = _rope_lane_tables(pos, MLA_ROPE // 2, ROPE_THETA, LANES, limit=MLA_ROPE)
    tab_ik = _rope_lane_tables(pos, IDX_ROT // 2, ROPE_THETA, LANES, offset=MLA_ROPE)
    tab_ki = jnp.concatenate([(tab_k[0] * tab_ik[0])[None], tab_k[1:], tab_ik[1:]])

    h0 = matmul(x, w_in_l, F32, 1024, 512, name="l0_in_proj")
    q_cat, k_nope, v_a, k_pe, qb, kb, vb, iq_r, ik_lo, ik_hi, iw_s = l0_attention_operands(
        h0, q_norm, w_uq_l, kv_norm, w_ukv_l, (tab_m, tab_d, tab_i, tab_ki))
    out_a = mla_attention(q_cat, k_nope, k_pe, v_a)
    out_b = dsa_attention(iq_r, iw_s, ik_lo, ik_hi, qb, kb, vb)
    return matmul_concat(out_a, out_b, w_out, F32, 1024, 512, name="l0_out_proj")


def _retention_mixer(x_bf, pos, w_in, gn_g, gn_b, w_out):
    h1 = matmul(x_bf, w_in, F32, 1024, 1024, name="l1_in_proj")
    cos, sin = _rope_tables(pos, RET_QK_DIM, RET_THETA)
    y = retention(h1, cos, sin, gn_g, gn_b)
    return matmul(y, w_out, F32, 1024, 512, name="l1_out_proj")


def kernel(x, positions, router_w, router_b, l0_w_in, l0_mla_q_norm, l0_mla_w_uq, l0_mla_kv_norm, l0_mla_w_ukv, l0_w_out, l1_w_in, l1_ret_gn_g, l1_ret_gn_b, l1_w_out, l0_ln_mix_g, l0_ln_mix_b, l0_moe_w_gate, l0_moe_w_up, l0_moe_w_down, l0_sh_gate, l0_sh_up, l0_sh_down, l0_ln_ffn_g, l0_ln_ffn_b, l1_ln_mix_g, l1_ln_mix_b, l1_moe_w_gate, l1_moe_w_up, l1_moe_w_down, l1_sh_gate, l1_sh_up, l1_sh_down, l1_ln_ffn_g, l1_ln_ffn_b):
    assert x.shape[0] == 1
    xt = x[0]
    pos = positions[0]

    mix = _mla_dsa_mixer(xt, pos, l0_w_in, l0_mla_q_norm, l0_mla_w_uq, l0_mla_kv_norm, l0_mla_w_ukv, l0_w_out)
    xt, xt_bf = add_layer_norm(xt, mix, l0_ln_mix_g, l0_ln_mix_b)
    xt, xt_bf = _grouped_moe(xt, xt_bf, router_w, router_b, l0_moe_w_gate, l0_moe_w_up, l0_moe_w_down,
                             l0_sh_gate, l0_sh_up, l0_sh_down, l0_ln_ffn_g, l0_ln_ffn_b)

    mix = _retention_mixer(xt_bf, pos, l1_w_in, l1_ret_gn_g, l1_ret_gn_b, l1_w_out)
    xt, xt_bf = add_layer_norm(xt, mix, l1_ln_mix_g, l1_ln_mix_b)
    xt, _ = _grouped_moe(xt, xt_bf, router_w, router_b, l1_moe_w_gate, l1_moe_w_up, l1_moe_w_down,
                         l1_sh_gate, l1_sh_up, l1_sh_down, l1_ln_ffn_g, l1_ln_ffn_b)
    return xt[None]
```

```python
import functools
from typing import NamedTuple

import numpy as np
import jax
import jax.numpy as jnp
from jax import lax
from jax.experimental import pallas as pl
from jax.experimental.pallas import tpu as pltpu

F32 = jnp.float32
BF16 = jnp.bfloat16

D_MODEL = 2048
DEPTH = 2
ROPE_THETA = 500000.0
MLA_HEADS = 8
MLA_Q_LORA = 512
MLA_KV_LORA = 256
MLA_NOPE = 128
MLA_ROPE = 64
MLA_V = 128
DSA_HEADS = 8
DSA_KV_HEADS = 2
DSA_HEAD_DIM = 128
DSA_ROT = DSA_HEAD_DIM // 4
IDX_HEADS = 16
IDX_DIM = 64
IDX_ROT = IDX_DIM // 4
IDX_TOPK_MAX = 256
RET_HEADS = 8
RET_QK_DIM = 256
RET_V_DIM = 512
RET_CHUNK = 128
RET_THETA = 10000.0
N_EXPERTS = 64
N_GROUPS = 8
EXPERTS_PER_GROUP = N_EXPERTS // N_GROUPS
TOP_K = 2
D_EXPERT = 512
D_SHARED = 1024
DEEPNORM_ALPHA = (2.0 * DEPTH) ** 0.25

SPLITS_AB = (MLA_Q_LORA, MLA_KV_LORA, MLA_ROPE, DSA_HEADS * DSA_HEAD_DIM,
             2 * DSA_KV_HEADS * DSA_HEAD_DIM, IDX_HEADS * IDX_DIM, IDX_DIM, IDX_HEADS)
IN_AB = sum(SPLITS_AB)

VMEM_LIMIT_BYTES = 56 * 1024 * 1024
LANES = 128
MASKED_SCORE = -1e30
LOG2_E = 1.4426950408889634
DMA_ISSUE_UNROLL = 8
INT32_MIN = -2 ** 31


def _params(*sem):
    return pltpu.CompilerParams(dimension_semantics=sem, vmem_limit_bytes=VMEM_LIMIT_BYTES)


def _mm_kernel(*refs, norm_eps):
    if norm_eps is None:
        a_ref, b_ref, o_ref = refs
        a = a_ref[...]
    else:
        a_ref, g_ref, b_ref, o_ref = refs
        af = a_ref[...].astype(F32)
        a = af * lax.rsqrt(jnp.mean(af * af, axis=-1, keepdims=True) + norm_eps) * g_ref[...]
    o_ref[...] = jnp.dot(a.astype(BF16), b_ref[...].astype(BF16),
                         preferred_element_type=F32).astype(o_ref.dtype)


def matmul(a, b, out_dtype, tm, tn, *, name, a_cols=None, norm_gain=None, norm_eps=1e-6):
    m = a.shape[0]
    k, n = b.shape
    col_blk = 0 if a_cols is None else a_cols[0]
    assert (a.shape[1] == k) if a_cols is None else (a_cols[1] == k)
    assert m % tm == 0 and n % tn == 0
    in_specs = [pl.BlockSpec((tm, k), lambda i, j: (i, col_blk))]
    args = [a]
    if norm_gain is not None:
        in_specs.append(pl.BlockSpec((1, k), lambda i, j: (0, 0)))
        args.append(norm_gain.reshape(1, k))
    in_specs.append(pl.BlockSpec((k, tn), lambda i, j: (0, j)))
    args.append(b)
    return pl.pallas_call(
        functools.partial(_mm_kernel, norm_eps=None if norm_gain is None else norm_eps),
        grid=(m // tm, n // tn),
        in_specs=in_specs,
        out_specs=pl.BlockSpec((tm, tn), lambda i, j: (i, j)),
        out_shape=jax.ShapeDtypeStruct((m, n), out_dtype),
        compiler_params=_params("parallel", "parallel"),
        name=name,
    )(*args)


def _mm2_kernel(a1_ref, a2_ref, b1_ref, b2_ref, o_ref):
    acc = jnp.dot(a1_ref[...].astype(BF16), b1_ref[...].astype(BF16), preferred_element_type=F32)
    acc = acc + jnp.dot(a2_ref[...].astype(BF16), b2_ref[...].astype(BF16), preferred_element_type=F32)
    o_ref[...] = acc.astype(o_ref.dtype)


def matmul_concat(a1, a2, b, out_dtype, tm, tn, *, name):
    m, k1 = a1.shape
    n = b.shape[1]
    assert a2.shape == (m, k1) and b.shape[0] == 2 * k1 and m % tm == 0 and n % tn == 0
    a_spec = pl.BlockSpec((tm, k1), lambda i, j: (i, 0))
    return pl.pallas_call(
        _mm2_kernel,
        grid=(m // tm, n // tn),
        in_specs=[a_spec, a_spec, pl.BlockSpec((k1, tn), lambda i, j: (0, j)), pl.BlockSpec((k1, tn), lambda i, j: (1, j))],
        out_specs=pl.BlockSpec((tm, tn), lambda i, j: (i, j)),
        out_shape=jax.ShapeDtypeStruct((m, n), out_dtype),
        compiler_params=_params("parallel", "parallel"),
        name=name,
    )(a1, a2, b, b)


_H0_CQ = 0
_H0_CKV = _H0_CQ + MLA_Q_LORA
_H0_DQ = _H0_CKV + MLA_KV_LORA
_H0_DKV = _H0_DQ + DSA_HEADS * DSA_HEAD_DIM
_H0_IQ = _H0_DKV + 2 * DSA_KV_HEADS * DSA_HEAD_DIM
_H0_KROPE_IK = _H0_IQ + IDX_HEADS * IDX_DIM
_H0_IW = _H0_KROPE_IK + LANES
_H0_WIDTH = _H0_IW + LANES


def _rope_lanes(x, tab_ref, shifts):
    out = x * tab_ref[0]
    for i, shift in enumerate(shifts):
        out = out + pltpu.roll(x, shift, 1) * tab_ref[1 + i]
    return out


def _l0_prep_kernel(h_ref, qn_ref, kvn_ref, wuq_ref, wukv_ref, tm_ref, td_ref, ti_ref, tki_ref,
                    qcat_ref, knope_ref, va_ref, kpe_ref, qb_ref, kb_ref, vb_ref, iq_ref, iklo_ref, ikhi_ref,
                    iw_ref):
    def rms(x, g_ref):
        return (x * lax.rsqrt(jnp.mean(x * x, axis=-1, keepdims=True) + 1e-6) * g_ref[...]).astype(BF16)

    def slab(col, j=0):
        return h_ref[:, col + j * LANES:col + (j + 1) * LANES]

    half_m, half_d, half_i = MLA_ROPE // 2, DSA_ROT // 2, IDX_ROT // 2
    shifts_m = (half_m, LANES - half_m)
    shifts_d = (half_d, LANES - half_d)
    shifts_i = (half_i, LANES - half_i)

    qa = jnp.dot(rms(h_ref[:, _H0_CQ:_H0_CQ + MLA_Q_LORA], qn_ref), wuq_ref[...], preferred_element_type=F32)
    kva = jnp.dot(rms(h_ref[:, _H0_CKV:_H0_CKV + MLA_KV_LORA], kvn_ref), wukv_ref[...],
                  preferred_element_type=F32)
    n_nope = MLA_HEADS * MLA_NOPE
    knope_ref[...] = kva[:, :n_nope].astype(BF16)
    va_ref[...] = kva[:, n_nope:].astype(BF16)
    q_scale = (MLA_NOPE + MLA_ROPE) ** -0.5 * LOG2_E
    for h in range(MLA_HEADS):
        nope = qa[:, h * LANES:(h + 1) * LANES]
        pe = qa[:, n_nope + h * LANES:n_nope + (h + 1) * LANES]
        qcat_ref[:, 2 * h * LANES:(2 * h + 1) * LANES] = (nope * q_scale).astype(BF16)
        qcat_ref[:, (2 * h + 1) * LANES:(2 * h + 2) * LANES] = (_rope_lanes(pe, tm_ref, shifts_m) * q_scale).astype(BF16)

    d_scale = DSA_HEAD_DIM ** -0.5 * LOG2_E
    for h in range(DSA_HEADS):
        qb_ref[:, h * LANES:(h + 1) * LANES] = (_rope_lanes(slab(_H0_DQ, h), td_ref, shifts_d) * d_scale).astype(BF16)
    for g in range(DSA_KV_HEADS):
        kb_ref[:, g * LANES:(g + 1) * LANES] = _rope_lanes(slab(_H0_DKV, g), td_ref, shifts_d).astype(BF16)
    vb_ref[...] = h_ref[:, _H0_DKV + DSA_KV_HEADS * LANES:_H0_DKV + 2 * DSA_KV_HEADS * LANES].astype(BF16)
    for j in range(IDX_HEADS * IDX_DIM // LANES):
        iq_ref[:, j * LANES:(j + 1) * LANES] = _rope_lanes(slab(_H0_IQ, j), ti_ref, shifts_i).astype(BF16)

    ki = _rope_lanes(slab(_H0_KROPE_IK), tki_ref, shifts_m + shifts_i)
    lane = lax.broadcasted_iota(jnp.int32, ki.shape, 1)
    kpe_ref[...] = jnp.where(lane < MLA_ROPE, ki, 0.0).astype(BF16)
    ik_hi = jnp.where(lane >= MLA_ROPE, ki, 0.0)
    ikhi_ref[...] = ik_hi.astype(BF16)
    iklo_ref[...] = pltpu.roll(ik_hi, LANES - MLA_ROPE, 1).astype(BF16)
    iw_ref[...] = h_ref[:, _H0_IW:_H0_IW + IDX_HEADS] * (IDX_HEADS ** -0.5 * IDX_DIM ** -0.5)


def l0_attention_operands(h0, q_norm, w_uq, kv_norm, w_ukv, tables, tm=256):
    s = h0.shape[0]
    bf = lambda n: jax.ShapeDtypeStruct((s, n), BF16)
    row = lambda n: pl.BlockSpec((tm, n), lambda i: (i, 0))
    whole = lambda a: pl.BlockSpec(a.shape, lambda i: (0,) * a.ndim)
    tab = lambda t: pl.BlockSpec((t.shape[0], tm, LANES), lambda i: (0, i, 0))
    widths = [2 * MLA_HEADS * LANES, MLA_HEADS * MLA_NOPE, MLA_HEADS * MLA_V, LANES, DSA_HEADS * DSA_HEAD_DIM,
              DSA_KV_HEADS * DSA_HEAD_DIM, DSA_KV_HEADS * DSA_HEAD_DIM, IDX_HEADS * IDX_DIM, LANES, LANES]
    w_uq = w_uq.astype(BF16)
    w_ukv = w_ukv.astype(BF16)
    q_norm = q_norm.reshape(1, -1)
    kv_norm = kv_norm.reshape(1, -1)
    return pl.pallas_call(
        _l0_prep_kernel,
        grid=(s // tm,),
        in_specs=[row(_H0_WIDTH), whole(q_norm), whole(kv_norm), whole(w_uq), whole(w_ukv)] + [tab(t) for t in tables],
        out_specs=[row(n) for n in widths] + [row(IDX_HEADS)],
        out_shape=[bf(n) for n in widths] + [jax.ShapeDtypeStruct((s, IDX_HEADS), F32)],
        compiler_params=_params("parallel"),
        name="l0_attention_operands",
    )(h0, q_norm, kv_norm, w_uq, w_ukv, *tables)


def _layer_norm(z, g, b):
    mu = jnp.mean(z, axis=-1, keepdims=True)
    zc = z - mu
    var = jnp.mean(zc * zc, axis=-1, keepdims=True)
    return zc * lax.rsqrt(var + 1e-5) * g + b


def _add_ln_route_kernel(x_ref, y_ref, g_ref, b_ref, rwt_ref, rb_ref, o_ref, obf_ref, e_ref, gate_ref):
    z = DEEPNORM_ALPHA * x_ref[...] + y_ref[...]
    out = _layer_norm(z, g_ref[...], b_ref[...])
    out_bf = out.astype(BF16)
    o_ref[...] = out
    obf_ref[...] = out_bf
    e_ref[...], gate_ref[...] = _route_tokens(out_bf, rwt_ref[...], rb_ref[...])


def add_layer_norm_route(x, y, g, b, router_w, router_b, tm=256):
    s, d = x.shape
    row = pl.BlockSpec((tm, d), lambda i: (i, 0))
    vec = pl.BlockSpec((1, d), lambda i: (0, 0))
    slots = pl.BlockSpec((TOP_K, tm), lambda i: (0, i))
    return pl.pallas_call(
        _add_ln_route_kernel,
        grid=(s // tm,),
        in_specs=[row, row, vec, vec, pl.BlockSpec((N_EXPERTS, d), lambda i: (0, 0)),
                  pl.BlockSpec((N_EXPERTS, 1), lambda i: (0, 0))],
        out_specs=[row, row, slots, slots],
        out_shape=[jax.ShapeDtypeStruct((s, d), F32), jax.ShapeDtypeStruct((s, d), BF16),
                   jax.ShapeDtypeStruct((TOP_K, s), jnp.int32), jax.ShapeDtypeStruct((TOP_K, s), F32)],
        compiler_params=_params("parallel"),
        name="add_layer_norm_route",
    )(x, y, g.reshape(1, d), b.reshape(1, d), router_w.T.astype(BF16), router_b.reshape(N_EXPERTS, 1))


def _lane_tile(x, n):
    return jnp.tile(x, (1, n))


def _softmax_step(s, v_ext, m_ref, acc_ref):
    tk = s.shape[1]
    m_prev = m_ref[...]
    m_new = jnp.maximum(m_prev, jnp.max(s, axis=1)[:, None])
    p = jnp.exp2(s - _lane_tile(m_new, tk // LANES))
    alpha = jnp.exp2(m_prev - m_new)
    pv = jnp.dot(p.astype(BF16), v_ext, preferred_element_type=F32)
    acc_ref[...] = _lane_tile(alpha, 2) * acc_ref[...] + pv
    m_ref[...] = m_new


def _mla_kernel(q_ref, kn_ref, kp_ref, v_ref, o_ref, m_sc, acc_sc, *, tq, tk, hp):
    qi = pl.program_id(1)
    m_sc[...] = jnp.full_like(m_sc, MASKED_SCORE)
    acc_sc[...] = jnp.zeros_like(acc_sc)
    ones = jnp.ones((tk, LANES), BF16)
    per_block = tq // tk

    def attend(c, diagonal_offset):
        kp = kp_ref[c]
        kn = kn_ref[c]
        v = v_ref[c]
        for j in range(hp):
            q = q_ref[:, j * 2 * LANES:(j + 1) * 2 * LANES]
            k = jnp.concatenate([kn[:, j * LANES:(j + 1) * LANES], kp], axis=1)
            s = lax.dot_general(q, k, (((1,), (1,)), ((), ())), preferred_element_type=F32)
            if diagonal_offset is not None:
                row = lax.broadcasted_iota(jnp.int32, (tq, tk), 0)
                col = diagonal_offset * tk + lax.broadcasted_iota(jnp.int32, (tq, tk), 1)
                s = jnp.where(col <= row, s, MASKED_SCORE)
            v_ext = jnp.concatenate([v[:, j * LANES:(j + 1) * LANES], ones], axis=1)
            _softmax_step(s, v_ext, m_sc.at[j], acc_sc.at[j])

    def below_diagonal(c, carry):
        attend(c, None)
        return carry

    lax.fori_loop(0, qi * per_block, below_diagonal, 0)
    for d in range(per_block):
        attend(qi * per_block + d, d)
    for j in range(hp):
        acc = acc_sc[j]
        o_ref[:, j * LANES:(j + 1) * LANES] = (acc[:, :LANES] / acc[:, LANES:]).astype(o_ref.dtype)


def mla_attention(q, k_nope, k_pe, v, tq=512, tk=512, hp=2):
    s = q.shape[0]
    h = MLA_HEADS
    nc = s // tk
    assert tq % tk == 0 and h % hp == 0
    per_head = pl.BlockSpec((nc, tk, hp * LANES), lambda hh, qi: (0, 0, hh))
    return pl.pallas_call(
        functools.partial(_mla_kernel, tq=tq, tk=tk, hp=hp),
        grid=(h // hp, s // tq),
        in_specs=[
            pl.BlockSpec((tq, hp * 2 * LANES), lambda hh, qi: (qi, hh)),
            per_head,
            pl.BlockSpec((nc, tk, LANES), lambda hh, qi: (0, 0, 0)),
            per_head,
        ],
        out_specs=pl.BlockSpec((tq, hp * LANES), lambda hh, qi: (qi, hh)),
        out_shape=jax.ShapeDtypeStruct((s, h * MLA_V), BF16),
        scratch_shapes=[pltpu.VMEM((hp, tq, LANES), F32), pltpu.VMEM((hp, tq, 2 * LANES), F32)],
        compiler_params=_params("parallel", "arbitrary"),
        name="mla_attention",
    )(q, k_nope.reshape(nc, tk, h * MLA_NOPE), k_pe.reshape(nc, tk, LANES), v.reshape(nc, tk, h * MLA_V))


def _sortable_key(x):
    bits = pltpu.bitcast(x, jnp.int32)
    return bits ^ ((bits >> 31) & 0x7FFFFFFF)


def _dsa_kernel(iq_ref, iw_ref, iklo_ref, ikhi_ref, q_ref, k_ref, v_ref, o_ref, key_sc, wb_sc, m_sc, acc_sc,
                *, tq, tk, n_sel):
    i = pl.program_id(0)
    n_chunks = ((i + 1) * tq + tk - 1) // tk
    rep = DSA_HEADS // DSA_KV_HEADS
    lane_tiles = tk // LANES
    heads_per_block = LANES // IDX_DIM

    assert heads_per_block == 2
    iq_blocks = jnp.concatenate(
        [iq_ref[:, j * LANES:(j + 1) * LANES] for j in range(IDX_HEADS // heads_per_block)], axis=0)
    iw = iw_ref[...]
    for h in range(IDX_HEADS):
        wb_sc[h] = jnp.broadcast_to(iw[:, h:h + 1], (tq, LANES))

    def index_chunk(c, carry):
        ik = jnp.concatenate([iklo_ref[c], ikhi_ref[c]], axis=0)
        d = lax.dot_general(iq_blocks, ik, (((1,), (1,)), ((), ())), preferred_element_type=F32)
        d = jnp.maximum(d, 0.0)
        score = None
        for h in range(IDX_HEADS):
            j, part = divmod(h, heads_per_block)
            term = _lane_tile(wb_sc[h], lane_tiles) * d[j * tq:(j + 1) * tq, part * tk:(part + 1) * tk]
            score = term if score is None else score + term
        key_sc[c] = _sortable_key(score)
        return carry

    lax.fori_loop(0, n_chunks, index_chunk, 0)
    last = n_chunks - 1
    row = i * tq + lax.broadcasted_iota(jnp.int32, (tq, tk), 0)
    col = last * tk + lax.broadcasted_iota(jnp.int32, (tq, tk), 1)
    key_sc[last] = jnp.where(col <= row, key_sc[last], INT32_MIN)

    def search_bit(b, thr):
        cand = thr + lax.shift_left(jnp.int32(1), 31 - b)

        def count_chunk(c, cnt):
            key = key_sc[c]
            for j in range(lane_tiles):
                cnt = cnt + jnp.where(key[:, j * LANES:(j + 1) * LANES] >= cand, 1.0, 0.0)
            return cnt

        cnt = lax.fori_loop(0, n_chunks, count_chunk, jnp.zeros((tq, LANES), F32))
        return jnp.where(jnp.sum(cnt, axis=1)[:, None] >= n_sel, cand, thr)

    thr = lax.fori_loop(0, 32, search_bit, jnp.full((tq, LANES), INT32_MIN, jnp.int32))
    thr = _lane_tile(jnp.maximum(thr, INT32_MIN + 1), lane_tiles)

    m_sc[...] = jnp.full_like(m_sc, MASKED_SCORE)
    acc_sc[...] = jnp.zeros_like(acc_sc)
    ones = jnp.ones((tk, LANES), BF16)

    def attend_chunk(c, carry):
        bias = jnp.where(key_sc[c] >= thr, 0.0, MASKED_SCORE)
        kc = k_ref[c]
        vc = v_ref[c]
        for g in range(DSA_KV_HEADS):
            cols = slice(g * DSA_HEAD_DIM, (g + 1) * DSA_HEAD_DIM)
            qg = jnp.concatenate(
                [q_ref[:, (g * rep + r) * DSA_HEAD_DIM:(g * rep + r + 1) * DSA_HEAD_DIM] for r in range(rep)], axis=0)
            s = lax.dot_general(qg, kc[:, cols], (((1,), (1,)), ((), ())), preferred_element_type=F32)
            s = (s.reshape(rep, tq, tk) + bias[None]).reshape(rep * tq, tk)
            _softmax_step(s, jnp.concatenate([vc[:, cols], ones], axis=1), m_sc.at[g], acc_sc.at[g])
        return carry

    lax.fori_loop(0, n_chunks, attend_chunk, 0)
    for g in range(DSA_KV_HEADS):
        acc = acc_sc[g]
        out = acc[:, :LANES] / acc[:, LANES:]
        for r in range(rep):
            hh = g * rep + r
            o_ref[:, hh * DSA_HEAD_DIM:(hh + 1) * DSA_HEAD_DIM] = out[r * tq:(r + 1) * tq].astype(o_ref.dtype)


def dsa_attention(iq, iw, ik_lo, ik_hi, q, k, v, tq=128, tk=512):
    s = q.shape[0]
    n_sel = min(IDX_TOPK_MAX, s // 4)
    nc = s // tk
    assert tk >= n_sel and s % tk == 0 and tk % tq == 0 and DSA_HEAD_DIM == LANES
    rep = DSA_HEADS // DSA_KV_HEADS
    kvw = DSA_KV_HEADS * DSA_HEAD_DIM
    whole3 = lambda i: (0, 0, 0)
    return pl.pallas_call(
        functools.partial(_dsa_kernel, tq=tq, tk=tk, n_sel=n_sel),
        grid=(s // tq,),
        in_specs=[
            pl.BlockSpec((tq, IDX_HEADS * IDX_DIM), lambda i: (i, 0)),
            pl.BlockSpec((tq, IDX_HEADS), lambda i: (i, 0)),
            pl.BlockSpec((nc, tk, LANES), whole3),
            pl.BlockSpec((nc, tk, LANES), whole3),
            pl.BlockSpec((tq, DSA_HEADS * DSA_HEAD_DIM), lambda i: (i, 0)),
            pl.BlockSpec((nc, tk, kvw), whole3),
            pl.BlockSpec((nc, tk, kvw), whole3),
        ],
        out_specs=pl.BlockSpec((tq, DSA_HEADS * DSA_HEAD_DIM), lambda i: (i, 0)),
        out_shape=jax.ShapeDtypeStruct((s, DSA_HEADS * DSA_HEAD_DIM), BF16),
        scratch_shapes=[
            pltpu.VMEM((nc, tq, tk), jnp.int32),
            pltpu.VMEM((IDX_HEADS, tq, LANES), F32),
            pltpu.VMEM((DSA_KV_HEADS, rep * tq, LANES), F32),
            pltpu.VMEM((DSA_KV_HEADS, rep * tq, 2 * LANES), F32),
        ],
        compiler_params=_params("parallel"),
        name="dsa_attention",
    )(iq, iw, ik_lo.reshape(nc, tk, LANES), ik_hi.reshape(nc, tk, LANES), q, k.reshape(nc, tk, kvw),
      v.reshape(nc, tk, kvw))


def _retention_tables():
    h, c = RET_HEADS, RET_CHUNK
    log_g = np.log(1.0 - 2.0 ** (-5.0 - np.arange(h, dtype=np.float32))).astype(np.float32).astype(np.float64)
    idx = np.arange(c, dtype=np.float64)
    diff = idx[:, None] - idx[None, :]
    decay_in = np.where(diff[None] >= 0, np.exp(np.maximum(diff, 0.0)[None] * log_g[:, None, None]), 0.0)
    xi = np.exp((idx + 1.0)[None, :] * log_g[:, None])
    zeta = np.exp((c - 1.0 - idx)[None, :] * log_g[:, None])
    chunk_decay = np.exp(c * log_g)
    lane = np.ones((1, 1, LANES))
    return (decay_in.astype(np.float32), (xi[:, :, None] * lane).astype(np.float32),
            (zeta[:, :, None] * lane).astype(np.float32),
            (chunk_decay[:, None, None] * np.ones((1, 8, LANES))).astype(np.float32))


def _retention_kernel(q_ref, k_ref, v_ref, g_ref, cos_ref, sin_ref, din_ref, xi_ref, zeta_ref, cd_ref,
                      gng_ref, gnb_ref, o_ref, r_sc):
    n = pl.program_id(1)
    half = RET_QK_DIM // 2

    @pl.when(n == 0)
    def _():
        r_sc[...] = jnp.zeros_like(r_sc)

    scale = RET_QK_DIM ** -0.5
    xi = jnp.concatenate([xi_ref[0]] * (RET_V_DIM // LANES), axis=1)
    zeta = jnp.concatenate([zeta_ref[0]] * (RET_QK_DIM // LANES), axis=1)

    for ci in range(q_ref.shape[0] // RET_CHUNK):
        rows = slice(ci * RET_CHUNK, (ci + 1) * RET_CHUNK)
        cos = cos_ref[rows]
        sin = sin_ref[rows]

        def rope(x):
            x1 = x[:, :half]
            x2 = x[:, half:]
            return x1 * cos - x2 * sin, x1 * sin + x2 * cos

        q1, q2 = rope(q_ref[rows])
        k1, k2 = rope(k_ref[rows])
        qr = jnp.concatenate([q1, q2], axis=1).astype(BF16)
        kr = jnp.concatenate([k1 * scale, k2 * scale], axis=1)
        v = v_ref[rows].astype(BF16)

        inner = lax.dot_general(qr, kr.astype(BF16), (((1,), (1,)), ((), ())), preferred_element_type=F32)
        inner = inner * din_ref[0]
        r = r_sc[...]
        cross = jnp.dot(qr, r.astype(BF16), preferred_element_type=F32)
        o = jnp.dot(inner.astype(BF16), v, preferred_element_type=F32) + cross * xi

        kz = (kr * zeta).astype(BF16)
        upd = lax.dot_general(kz, v, (((0,), (0,)), ((), ())), preferred_element_type=F32)
        r_sc[...] = cd_ref[0][:1, :1] * r + upd

        mu = jnp.mean(o, axis=-1, keepdims=True)
        oc = o - mu
        var = jnp.mean(oc * oc, axis=-1, keepdims=True)
        y = oc * lax.rsqrt(var + 1e-5) * gng_ref[...] + gnb_ref[...]
        gate = g_ref[rows]
        o_ref[rows] = (gate * jax.nn.sigmoid(gate) * y).astype(o_ref.dtype)


def retention(h1, cos, sin, gn_g, gn_b, chunks_per_step=8):
    s = h1.shape[0]
    hh, dk, dv = RET_HEADS, RET_QK_DIM, RET_V_DIM
    c = RET_CHUNK * chunks_per_step
    assert s % c == 0
    din, xi, zeta, cd = (jnp.asarray(t) for t in _retention_tables())
    k_blk0 = hh * dk // dk
    v_blk0 = 2 * hh * dk // dv
    g_blk0 = v_blk0 + hh
    per_head = lambda r, w: pl.BlockSpec((1, r, w), lambda h, n: (h, 0, 0))
    return pl.pallas_call(
        _retention_kernel,
        grid=(hh, s // c),
        in_specs=[
            pl.BlockSpec((c, dk), lambda h, n: (n, h)),
            pl.BlockSpec((c, dk), lambda h, n: (n, k_blk0 + h)),
            pl.BlockSpec((c, dv), lambda h, n: (n, v_blk0 + h)),
            pl.BlockSpec((c, dv), lambda h, n: (n, g_blk0 + h)),
            pl.BlockSpec((c, dk // 2), lambda h, n: (n, 0)),
            pl.BlockSpec((c, dk // 2), lambda h, n: (n, 0)),
            per_head(RET_CHUNK, RET_CHUNK), per_head(RET_CHUNK, LANES), per_head(RET_CHUNK, LANES),
            per_head(8, LANES),
            pl.BlockSpec((1, dv), lambda h, n: (0, h)),
            pl.BlockSpec((1, dv), lambda h, n: (0, h)),
        ],
        out_specs=pl.BlockSpec((c, dv), lambda h, n: (n, h)),
        out_shape=jax.ShapeDtypeStruct((s, hh * dv), BF16),
        scratch_shapes=[pltpu.VMEM((dk, dv), F32)],
        compiler_params=_params("parallel", "arbitrary"),
        name="retention",
    )(h1, h1, h1, h1, cos, sin, din, xi, zeta, cd, gn_g.reshape(1, -1), gn_b.reshape(1, -1))


def _first_argmax(v, idx, n):
    m = jnp.max(v, axis=0, keepdims=True)
    first = jnp.min(jnp.where(v == m, idx, n), axis=0, keepdims=True)
    return m, first


def _route_tokens(x, rwt, rb):
    tm = x.shape[0]
    epg = EXPERTS_PER_GROUP
    logits = lax.dot_general(rwt, x, (((1,), (1,)), ((), ())), preferred_element_type=F32)
    scores = jax.nn.sigmoid(logits)
    biased = scores + rb
    idx = lax.broadcasted_iota(jnp.int32, (epg, tm), 0)

    group_scores = []
    for g in range(N_GROUPS):
        v = biased[g * epg:(g + 1) * epg]
        m1, first = _first_argmax(v, idx, epg)
        m2 = jnp.max(jnp.where(idx == first, -jnp.inf, v), axis=0, keepdims=True)
        group_scores.append(m1 + m2)
    gmax = group_scores[0]
    for g in range(1, N_GROUPS):
        gmax = jnp.maximum(gmax, group_scores[g])
    gsel = jnp.full((1, tm), N_GROUPS, jnp.int32)
    for g in range(N_GROUPS - 1, -1, -1):
        gsel = jnp.where(group_scores[g] == gmax, g, gsel)

    in_biased = jnp.zeros((epg, tm), F32)
    in_scores = jnp.zeros((epg, tm), F32)
    for g in range(N_GROUPS):
        pick = gsel == g
        in_biased = jnp.where(pick, biased[g * epg:(g + 1) * epg], in_biased)
        in_scores = jnp.where(pick, scores[g * epg:(g + 1) * epg], in_scores)
    _, loc1 = _first_argmax(in_biased, idx, epg)
    _, loc2 = _first_argmax(jnp.where(idx == loc1, -jnp.inf, in_biased), idx, epg)
    s1 = jnp.sum(jnp.where(idx == loc1, in_scores, 0.0), axis=0, keepdims=True)
    s2 = jnp.sum(jnp.where(idx == loc2, in_scores, 0.0), axis=0, keepdims=True)
    denom = s1 + s2
    experts = jnp.concatenate([gsel * epg + loc1, gsel * epg + loc2], axis=0)
    return experts, jnp.concatenate([s1 / denom, s2 / denom], axis=0)


def _scatter_rows_kernel(pad_ref, pos_ref, x_ref, xs_hbm, zero_sc, sem, zsem, *, tm, n_pad):
    i = pl.program_id(0)

    def zero_copy(p):
        return pltpu.make_async_copy(zero_sc, xs_hbm.at[pl.ds(pad_ref[p], 1)], zsem)

    @pl.when(i == 0)
    def _():
        zero_sc[...] = jnp.zeros_like(zero_sc)

        def start_zero(p, carry):
            zero_copy(p).start()
            return carry

        lax.fori_loop(0, n_pad, start_zero, 0)

    def start(g, carry):
        for u in range(DMA_ISSUE_UNROLL):
            r = g * DMA_ISSUE_UNROLL + u
            for k in range(TOP_K):
                pltpu.make_async_copy(x_ref.at[pl.ds(r, 1)], xs_hbm.at[pl.ds(pos_ref[0, k, r], 1)],
                                      sem).start(priority=k % 2)
        return carry

    lax.fori_loop(0, tm // DMA_ISSUE_UNROLL, start, 0)
    for k in range(TOP_K):
        pltpu.make_async_copy(x_ref, xs_hbm.at[pl.ds(0, tm)], sem).wait()

    @pl.when(i == 0)
    def _():
        def wait_zero(p, carry):
            zero_copy(p).wait()
            return carry

        lax.fori_loop(0, n_pad, wait_zero, 0)


def scatter_rows(x, pos, pad_pos, n_rows, tm=256):
    s, d = x.shape
    nt = s // tm
    n_pad = pad_pos.shape[0]
    return pl.pallas_call(
        functools.partial(_scatter_rows_kernel, tm=tm, n_pad=n_pad),
        grid_spec=pltpu.PrefetchScalarGridSpec(
            num_scalar_prefetch=1,
            grid=(nt,),
            in_specs=[
                pl.BlockSpec((1, TOP_K, tm), lambda i, pad: (i, 0, 0), memory_space=pltpu.SMEM),
                pl.BlockSpec((tm, d), lambda i, pad: (i, 0)),
            ],
            out_specs=pl.BlockSpec(memory_space=pl.ANY),
            scratch_shapes=[pltpu.VMEM((1, d), x.dtype), pltpu.SemaphoreType.DMA(()), pltpu.SemaphoreType.DMA(())],
        ),
        out_shape=jax.ShapeDtypeStruct((n_rows, d), x.dtype),
        compiler_params=_params("arbitrary"),
        name="scatter_rows",
    )(pad_pos, pos.reshape(TOP_K, nt, tm).transpose(1, 0, 2), x)


def _expert_kernel(te_ref, rs_ref, nv_ref, na_ref, dst_ref, xs_hbm, wg_ref, wu_ref, wd_ref, y_hbm,
                   wg_sc, wu_sc, wd_sc, xbuf, ybuf, lsem, ssem, *, tm):
    j = pl.program_id(0)
    n_active = na_ref[0]
    active = j < n_active
    slot = j % 2
    new_expert = (j == 0) | (te_ref[j] != te_ref[jnp.maximum(j - 1, 0)])
    chunk_sizes = [c for c in (256, 128, 64, 32, 16, 8) if c <= tm]

    def load_tile(t, dst_slot, wait):
        n8 = ((nv_ref[t] + 7) // 8) * 8
        base = rs_ref[t]
        off = jnp.int32(0)
        for c in chunk_sizes:
            take = (n8 & c) != 0

            @pl.when(take)
            def _(off=off, c=c):
                copy = pltpu.make_async_copy(
                    xs_hbm.at[pl.ds(pl.multiple_of(base + off, 8), c)],
                    xbuf.at[dst_slot, pl.ds(pl.multiple_of(off, 8), c)], lsem.at[dst_slot])
                if wait:
                    copy.wait()
                else:
                    copy.start()

            off = off + jnp.where(take, c, 0)

    def wait_scatter(src_slot, n):
        def wait_row(r, carry):
            pltpu.make_async_copy(ybuf.at[src_slot, pl.ds(0, 1)], y_hbm.at[pl.ds(0, 1)], ssem.at[src_slot]).wait()
            return carry

        lax.fori_loop(0, n, wait_row, 0)

    @pl.when(j == 0)
    def _():
        xbuf[...] = jnp.zeros_like(xbuf)
        load_tile(0, 0, False)

    @pl.when(j + 1 < n_active)
    def _():
        load_tile(j + 1, 1 - slot, False)

    @pl.when(active & new_expert)
    def _():
        wg_sc[...] = wg_ref[0].astype(BF16)
        wu_sc[...] = wu_ref[0].astype(BF16)
        wd_sc[...] = wd_ref[0].astype(BF16)

    @pl.when(active)
    def _():
        load_tile(j, slot, True)

        @pl.when(j >= 2)
        def _():
            wait_scatter(slot, nv_ref[jnp.maximum(j - 2, 0)])

        x = xbuf[slot].astype(BF16)
        hg = jnp.dot(x, wg_sc[...], preferred_element_type=F32)
        hu = jnp.dot(x, wu_sc[...], preferred_element_type=F32)
        h = hg * jax.nn.sigmoid(hg) * hu
        ybuf[slot] = jnp.dot(h.astype(BF16), wd_sc[...], preferred_element_type=F32)

        def scatter_row(r, carry):
            pltpu.make_async_copy(ybuf.at[slot, pl.ds(r, 1)], y_hbm.at[pl.ds(dst_ref[0, 0, r], 1)],
                                  ssem.at[slot]).start()
            return carry

        lax.fori_loop(0, nv_ref[j], scatter_row, 0)

        @pl.when(j == n_active - 1)
        def _():
            wait_scatter(slot, nv_ref[j])

            @pl.when(j >= 1)
            def _():
                wait_scatter(1 - slot, nv_ref[jnp.maximum(j - 1, 0)])


def routed_experts(xs, plan, n_out_rows, w_gate, w_up, w_down, tm):
    d = xs.shape[1]
    f = w_gate.shape[2]
    nt = plan.tile_expert.shape[0]

    def dst_blk(j, te, rs, nv, na):
        return (jnp.minimum(j, na[0] - 1), 0, 0)

    def weight_blk(j, te, rs, nv, na):
        return (te[j], 0, 0)

    return pl.pallas_call(
        functools.partial(_expert_kernel, tm=tm),
        grid_spec=pltpu.PrefetchScalarGridSpec(
            num_scalar_prefetch=4,
            grid=(nt,),
            in_specs=[
                pl.BlockSpec((1, 1, tm), dst_blk, memory_space=pltpu.SMEM),
                pl.BlockSpec(memory_space=pl.ANY),
                pl.BlockSpec((1, d, f), weight_blk),
                pl.BlockSpec((1, d, f), weight_blk),
                pl.BlockSpec((1, f, d), weight_blk),
            ],
            out_specs=pl.BlockSpec(memory_space=pl.ANY),
            scratch_shapes=[pltpu.VMEM((d, f), BF16), pltpu.VMEM((d, f), BF16), pltpu.VMEM((f, d), BF16),
                            pltpu.VMEM((2, tm, d), xs.dtype), pltpu.VMEM((2, tm, d), F32),
                            pltpu.SemaphoreType.DMA((2,)), pltpu.SemaphoreType.DMA((2,))],
        ),
        out_shape=jax.ShapeDtypeStruct((n_out_rows, d), F32),
        compiler_params=_params("arbitrary"),
        name="routed_experts",
    )(plan.tile_expert, plan.tile_row_start, plan.n_valid, plan.n_active, plan.dst_tiles.reshape(nt, 1, tm),
      xs, w_gate, w_up, w_down)


def _shared_kernel(x_ref, wg_ref, wu_ref, wd_ref, o_ref):
    x = x_ref[...]
    hg = jnp.dot(x, wg_ref[...], preferred_element_type=F32)
    hu = jnp.dot(x, wu_ref[...], preferred_element_type=F32)
    h = hg * jax.nn.sigmoid(hg) * hu
    o_ref[...] = jnp.dot(h.astype(BF16), wd_ref[...], preferred_element_type=F32)


def shared_expert(x_bf, wg, wu, wd, tm=512):
    s, d = x_bf.shape
    f = wg.shape[1]
    whole = lambda r, c: pl.BlockSpec((r, c), lambda i: (0, 0))
    return pl.pallas_call(
        _shared_kernel,
        grid=(s // tm,),
        in_specs=[pl.BlockSpec((tm, d), lambda i: (i, 0)), whole(d, f), whole(d, f), whole(f, d)],
        out_specs=pl.BlockSpec((tm, d), lambda i: (i, 0)),
        out_shape=jax.ShapeDtypeStruct((s, d), F32),
        compiler_params=_params("parallel"),
        name="shared_expert",
    )(x_bf, wg.astype(BF16), wu.astype(BF16), wd.astype(BF16))


def _combine_kernel(x_ref, sh_ref, gate_ref, y0_ref, y1_ref, g_ref, b_ref, o_ref, obf_ref):
    gate = gate_ref[...]
    routed = gate[:, 0:1] * y0_ref[0] + gate[:, 1:2] * y1_ref[0]
    z = DEEPNORM_ALPHA * x_ref[...] + (routed + sh_ref[...])
    out = _layer_norm(z, g_ref[...], b_ref[...])
    o_ref[...] = out
    obf_ref[...] = out.astype(BF16)


def moe_combine(x, shared, gates, y, g, b, tm=256):
    s, d = x.shape
    assert TOP_K == 2
    row = pl.BlockSpec((tm, d), lambda i: (i, 0))
    vec = pl.BlockSpec((1, d), lambda i: (0, 0))
    return pl.pallas_call(
        _combine_kernel,
        grid=(s // tm,),
        in_specs=[
            row, row,
            pl.BlockSpec((tm, TOP_K), lambda i: (i, 0)),
            pl.BlockSpec((1, tm, d), lambda i: (0, i, 0)),
            pl.BlockSpec((1, tm, d), lambda i: (1, i, 0)),
            vec, vec,
        ],
        out_specs=[row, row],
        out_shape=[jax.ShapeDtypeStruct((s, d), F32), jax.ShapeDtypeStruct((s, d), BF16)],
        compiler_params=_params("parallel"),
        name="moe_combine",
    )(x, shared, gates.T, y, y, g.reshape(1, d), b.reshape(1, d))


def _rope_tables(pos, rot_dim, theta):
    half = rot_dim // 2
    inv = (1.0 / (theta ** (np.arange(half, dtype=np.float32) / half))).astype(np.float32)
    ang = pos.astype(F32)[:, None] * inv
    return jnp.cos(ang), jnp.sin(ang)


def _rope(x, cos, sin, rot_dim):
    half = rot_dim // 2
    x1 = x[..., :half]
    x2 = x[..., half:rot_dim]
    c = cos[:, None, :]
    s = sin[:, None, :]
    return jnp.concatenate([x1 * c - x2 * s, x1 * s + x2 * c, x[..., rot_dim:]], axis=-1)


class _RoutingPlan(NamedTuple):
    pos: jax.Array
    pad_pos: jax.Array
    n_rows: int
    tile_expert: jax.Array
    tile_row_start: jax.Array
    n_valid: jax.Array
    n_active: jax.Array
    dst_tiles: jax.Array


ROW_ALIGN = 8


def _routing_plan(experts, tm):
    k, s = experts.shape
    flat = experts.reshape(-1)
    expert_ids = jnp.arange(N_EXPERTS, dtype=jnp.int32)
    onehot = (flat[:, None] == expert_ids[None, :]).astype(jnp.int32)
    csum = jnp.cumsum(onehot, axis=0)
    rank = jnp.take_along_axis(csum, flat[:, None], axis=1)[:, 0] - 1
    sizes = csum[-1]
    padded = (sizes + ROW_ALIGN - 1) // ROW_ALIGN * ROW_ALIGN
    seg_end = jnp.cumsum(padded)
    seg_start = seg_end - padded
    pos = seg_start[flat] + rank
    n_rows = k * s + N_EXPERTS * (ROW_ALIGN - 1)

    i_pad = jnp.arange(ROW_ALIGN - 1, dtype=jnp.int32)
    is_pad = i_pad[None, :] < (padded - sizes)[:, None]
    spare = jnp.cumsum(jnp.logical_not(is_pad).reshape(-1).astype(jnp.int32)).reshape(is_pad.shape) - 1
    pad_pos = jnp.where(is_pad, (seg_start + sizes)[:, None] + i_pad[None, :], seg_end[-1] + spare).reshape(-1)

    tiles = (sizes + tm - 1) // tm
    tile_end = jnp.cumsum(tiles)
    n_active = tile_end[-1]
    n_tiles = k * s // tm + N_EXPERTS
    tile_ids = jnp.minimum(jnp.arange(n_tiles, dtype=jnp.int32), n_active - 1)
    tile_expert = jnp.searchsorted(tile_end, tile_ids, side="right").astype(jnp.int32)
    local = tile_ids - (tile_end - tiles)[tile_expert]
    tile_row_start = seg_start[tile_expert] + local * tm
    n_valid = jnp.clip(sizes[tile_expert] - local * tm, 0, tm)
    slot_of_row = jnp.zeros((n_rows,), jnp.int32).at[pos].set(jnp.arange(k * s, dtype=jnp.int32))
    rows = jnp.minimum(tile_row_start[:, None] + jnp.arange(tm, dtype=jnp.int32)[None, :], n_rows - 1)
    return _RoutingPlan(pos.reshape(k, s), pad_pos.astype(jnp.int32), n_rows, tile_expert,
                        tile_row_start.astype(jnp.int32), n_valid.astype(jnp.int32),
                        n_active.reshape(1).astype(jnp.int32), slot_of_row[rows])


def _grouped_moe(x, x_bf, experts, gates, w_gate, w_up, w_down, sh_gate, sh_up, sh_down, ln_g, ln_b,
                 expert_tm=256):
    s, d = x.shape
    plan = _routing_plan(experts, expert_tm)
    xs = scatter_rows(x, plan.pos, plan.pad_pos, plan.n_rows)
    y = routed_experts(xs, plan, TOP_K * s, w_gate, w_up, w_down, expert_tm)
    shared = shared_expert(x_bf, sh_gate, sh_up, sh_down)
    return moe_combine(x, shared, gates, y.reshape(TOP_K, s, d), ln_g, ln_b)


def _rope_lane_tables(pos, half, theta, period, offset=0, limit=None):
    cos, sin = _rope_tables(pos, 2 * half, theta)
    lane = np.arange(LANES)
    p = lane % period - offset
    in_range = np.ones(LANES, bool) if limit is None else lane < limit
    first = (p >= 0) & (p < half) & in_range
    second = (p >= half) & (p < 2 * half) & in_range
    idx = np.where(first | second, p % half, 0)
    cos_l, sin_l = cos[:, idx], sin[:, idx]
    return jnp.stack([jnp.where(first | second, cos_l, 1.0),
                      jnp.where(second, sin_l, 0.0),
                      jnp.where(first, -sin_l, 0.0)])


def _mla_dsa_mixer(x, pos, w_in, q_norm, w_uq, kv_norm, w_ukv, w_out):
    cq, ckv, krope, dq, dkv, iq, ik, iw = jnp.split(w_in, [int(c) for c in np.cumsum(SPLITS_AB)[:-1]], axis=1)
    w_in_l = jnp.concatenate([cq, ckv, dq, dkv, iq, krope, ik, iw,
                              jnp.zeros((w_in.shape[0], LANES - IDX_HEADS), w_in.dtype)], axis=1)
    w_uq3 = w_uq.reshape(MLA_Q_LORA, MLA_HEADS, MLA_NOPE + MLA_ROPE)
    w_uq_l = jnp.concatenate([w_uq3[:, :, :MLA_NOPE].reshape(MLA_Q_LORA, -1),
                              jnp.pad(w_uq3[:, :, MLA_NOPE:], ((0, 0), (0, 0), (0, LANES - MLA_ROPE))
                                      ).reshape(MLA_Q_LORA, -1)], axis=1)
    w_ukv3 = w_ukv.reshape(MLA_KV_LORA, MLA_HEADS, MLA_NOPE + MLA_V)
    w_ukv_l = jnp.concatenate([w_ukv3[:, :, :MLA_NOPE].reshape(MLA_KV_LORA, -1),
                               w_ukv3[:, :, MLA_NOPE:].reshape(MLA_KV_LORA, -1)], axis=1)

    tab_m = _rope_lane_tables(pos, MLA_ROPE // 2, ROPE_THETA, LANES)
    tab_d = _rope_lane_tables(pos, DSA_ROT // 2, ROPE_THETA, LANES)
    tab_i = _rope_lane_tables(pos, IDX_ROT // 2, ROPE_THETA, IDX_DIM)
    tab_k = _rope_lane_tables(pos, MLA_ROPE // 2, ROPE_THETA, LANES, limit=MLA_ROPE)
    tab_ik = _rope_lane_tables(pos, IDX_ROT // 2, ROPE_THETA, LANES, offset=MLA_ROPE)
    tab_ki = jnp.concatenate([(tab_k[0] * tab_ik[0])[None], tab_k[1:], tab_ik[1:]])

    h0 = matmul(x, w_in_l, F32, 1024, 512, name="l0_in_proj")
    q_cat, k_nope, v_a, k_pe, qb, kb, vb, iq_r, ik_lo, ik_hi, iw_s = l0_attention_operands(
        h0, q_norm, w_uq_l, kv_norm, w_ukv_l, (tab_m, tab_d, tab_i, tab_ki))
    out_a = mla_attention(q_cat, k_nope, k_pe, v_a)
    out_b = dsa_attention(iq_r, iw_s, ik_lo, ik_hi, qb, kb, vb)
    return matmul_concat(out_a, out_b, w_out, F32, 1024, 512, name="l0_out_proj")


def _retention_mixer(x_bf, pos, w_in, gn_g, gn_b, w_out):
    h1 = matmul(x_bf, w_in, F32, 1024, 1024, name="l1_in_proj")
    cos, sin = _rope_tables(pos, RET_QK_DIM, RET_THETA)
    y = retention(h1, cos, sin, gn_g, gn_b)
    return matmul(y, w_out, F32, 1024, 512, name="l1_out_proj")


def kernel(x, positions, router_w, router_b, l0_w_in, l0_mla_q_norm, l0_mla_w_uq, l0_mla_kv_norm, l0_mla_w_ukv, l0_w_out, l1_w_in, l1_ret_gn_g, l1_ret_gn_b, l1_w_out, l0_ln_mix_g, l0_ln_mix_b, l0_moe_w_gate, l0_moe_w_up, l0_moe_w_down, l0_sh_gate, l0_sh_up, l0_sh_down, l0_ln_ffn_g, l0_ln_ffn_b, l1_ln_mix_g, l1_ln_mix_b, l1_moe_w_gate, l1_moe_w_up, l1_moe_w_down, l1_sh_gate, l1_sh_up, l1_sh_down, l1_ln_ffn_g, l1_ln_ffn_b):
    assert x.shape[0] == 1
    xt = x[0]
    pos = positions[0]

    mix = _mla_dsa_mixer(xt, pos, l0_w_in, l0_mla_q_norm, l0_mla_w_uq, l0_mla_kv_norm, l0_mla_w_ukv, l0_w_out)
    xt, xt_bf, experts, gates = add_layer_norm_route(xt, mix, l0_ln_mix_g, l0_ln_mix_b, router_w, router_b)
    xt, xt_bf = _grouped_moe(xt, xt_bf, experts, gates, l0_moe_w_gate, l0_moe_w_up, l0_moe_w_down,
                             l0_sh_gate, l0_sh_up, l0_sh_down, l0_ln_ffn_g, l0_ln_ffn_b)

    mix = _retention_mixer(xt_bf, pos, l1_w_in, l1_ret_gn_g, l1_ret_gn_b, l1_w_out)
    xt, xt_bf, experts, gates = add_layer_norm_route(xt, mix, l1_ln_mix_g, l1_ln_mix_b, router_w, router_b)
    xt, _ = _grouped_moe(xt, xt_bf, experts, gates, l1_moe_w_gate, l1_moe_w_up, l1_moe_w_down,
                         l1_sh_gate, l1_sh_up, l1_sh_down, l1_ln_ffn_g, l1_ln_ffn_b)
    return xt[None]
```

```python
import functools
from typing import NamedTuple

import numpy as np
import jax
import jax.numpy as jnp
from jax import lax
from jax.experimental import pallas as pl
from jax.experimental.pallas import tpu as pltpu

F32 = jnp.float32
BF16 = jnp.bfloat16

D_MODEL = 2048
DEPTH = 2
ROPE_THETA = 500000.0
MLA_HEADS = 8
MLA_Q_LORA = 512
MLA_KV_LORA = 256
MLA_NOPE = 128
MLA_ROPE = 64
MLA_V = 128
DSA_HEADS = 8
DSA_KV_HEADS = 2
DSA_HEAD_DIM = 128
DSA_ROT = DSA_HEAD_DIM // 4
IDX_HEADS = 16
IDX_DIM = 64
IDX_ROT = IDX_DIM // 4
IDX_TOPK_MAX = 256
RET_HEADS = 8
RET_QK_DIM = 256
RET_V_DIM = 512
RET_CHUNK = 128
RET_THETA = 10000.0
N_EXPERTS = 64
N_GROUPS = 8
EXPERTS_PER_GROUP = N_EXPERTS // N_GROUPS
TOP_K = 2
D_EXPERT = 512
D_SHARED = 1024
DEEPNORM_ALPHA = (2.0 * DEPTH) ** 0.25

SPLITS_AB = (MLA_Q_LORA, MLA_KV_LORA, MLA_ROPE, DSA_HEADS * DSA_HEAD_DIM,
             2 * DSA_KV_HEADS * DSA_HEAD_DIM, IDX_HEADS * IDX_DIM, IDX_DIM, IDX_HEADS)
IN_AB = sum(SPLITS_AB)

VMEM_LIMIT_BYTES = 56 * 1024 * 1024
LANES = 128
MASKED_SCORE = -1e30
LOG2_E = 1.4426950408889634
DMA_ISSUE_UNROLL = 8
INT32_MIN = -2 ** 31


def _params(*sem):
    return pltpu.CompilerParams(dimension_semantics=sem, vmem_limit_bytes=VMEM_LIMIT_BYTES)


def _mm_kernel(*refs, norm_eps):
    if norm_eps is None:
        a_ref, b_ref, o_ref = refs
        a = a_ref[...]
    else:
        a_ref, g_ref, b_ref, o_ref = refs
        af = a_ref[...].astype(F32)
        a = af * lax.rsqrt(jnp.mean(af * af, axis=-1, keepdims=True) + norm_eps) * g_ref[...]
    o_ref[...] = jnp.dot(a.astype(BF16), b_ref[...].astype(BF16),
                         preferred_element_type=F32).astype(o_ref.dtype)


def matmul(a, b, out_dtype, tm, tn, *, name, a_cols=None, norm_gain=None, norm_eps=1e-6):
    m = a.shape[0]
    k, n = b.shape
    col_blk = 0 if a_cols is None else a_cols[0]
    assert (a.shape[1] == k) if a_cols is None else (a_cols[1] == k)
    assert m % tm == 0 and n % tn == 0
    in_specs = [pl.BlockSpec((tm, k), lambda i, j: (i, col_blk))]
    args = [a]
    if norm_gain is not None:
        in_specs.append(pl.BlockSpec((1, k), lambda i, j: (0, 0)))
        args.append(norm_gain.reshape(1, k))
    in_specs.append(pl.BlockSpec((k, tn), lambda i, j: (0, j)))
    args.append(b)
    return pl.pallas_call(
        functools.partial(_mm_kernel, norm_eps=None if norm_gain is None else norm_eps),
        grid=(m // tm, n // tn),
        in_specs=in_specs,
        out_specs=pl.BlockSpec((tm, tn), lambda i, j: (i, j)),
        out_shape=jax.ShapeDtypeStruct((m, n), out_dtype),
        compiler_params=_params("parallel", "parallel"),
        name=name,
    )(*args)


def _mm2_kernel(a1_ref, a2_ref, b1_ref, b2_ref, o_ref):
    acc = jnp.dot(a1_ref[...].astype(BF16), b1_ref[...].astype(BF16), preferred_element_type=F32)
    acc = acc + jnp.dot(a2_ref[...].astype(BF16), b2_ref[...].astype(BF16), preferred_element_type=F32)
    o_ref[...] = acc.astype(o_ref.dtype)


def matmul_concat(a1, a2, b, out_dtype, tm, tn, *, name):
    m, k1 = a1.shape
    n = b.shape[1]
    assert a2.shape == (m, k1) and b.shape[0] == 2 * k1 and m % tm == 0 and n % tn == 0
    a_spec = pl.BlockSpec((tm, k1), lambda i, j: (i, 0))
    return pl.pallas_call(
        _mm2_kernel,
        grid=(m // tm, n // tn),
        in_specs=[a_spec, a_spec, pl.BlockSpec((k1, tn), lambda i, j: (0, j)), pl.BlockSpec((k1, tn), lambda i, j: (1, j))],
        out_specs=pl.BlockSpec((tm, tn), lambda i, j: (i, j)),
        out_shape=jax.ShapeDtypeStruct((m, n), out_dtype),
        compiler_params=_params("parallel", "parallel"),
        name=name,
    )(a1, a2, b, b)


_H0_CQ = 0
_H0_CKV = _H0_CQ + MLA_Q_LORA
_H0_DQ = _H0_CKV + MLA_KV_LORA
_H0_DKV = _H0_DQ + DSA_HEADS * DSA_HEAD_DIM
_H0_IQ = _H0_DKV + 2 * DSA_KV_HEADS * DSA_HEAD_DIM
_H0_KROPE_IK = _H0_IQ + IDX_HEADS * IDX_DIM
_H0_IW = _H0_KROPE_IK + LANES
_H0_WIDTH = _H0_IW + LANES


def _rope_lanes(x, tab_ref, shifts):
    out = x * tab_ref[0]
    for i, shift in enumerate(shifts):
        out = out + pltpu.roll(x, shift, 1) * tab_ref[1 + i]
    return out


def _l0_prep_kernel(h_ref, qn_ref, kvn_ref, wuq_ref, wukv_ref, tm_ref, td_ref, ti_ref, tki_ref,
                    qcat_ref, knope_ref, va_ref, kpe_ref, qb_ref, kb_ref, vb_ref, iq_ref, iklo_ref, ikhi_ref,
                    iw_ref):
    def rms(x, g_ref):
        return (x * lax.rsqrt(jnp.mean(x * x, axis=-1, keepdims=True) + 1e-6) * g_ref[...]).astype(BF16)

    def slab(col, j=0):
        return h_ref[:, col + j * LANES:col + (j + 1) * LANES]

    half_m, half_d, half_i = MLA_ROPE // 2, DSA_ROT // 2, IDX_ROT // 2
    shifts_m = (half_m, LANES - half_m)
    shifts_d = (half_d, LANES - half_d)
    shifts_i = (half_i, LANES - half_i)

    qa = jnp.dot(rms(h_ref[:, _H0_CQ:_H0_CQ + MLA_Q_LORA], qn_ref), wuq_ref[...], preferred_element_type=F32)
    kva = jnp.dot(rms(h_ref[:, _H0_CKV:_H0_CKV + MLA_KV_LORA], kvn_ref), wukv_ref[...],
                  preferred_element_type=F32)
    n_nope = MLA_HEADS * MLA_NOPE
    knope_ref[...] = kva[:, :n_nope].astype(BF16)
    va_ref[...] = kva[:, n_nope:].astype(BF16)
    q_scale = (MLA_NOPE + MLA_ROPE) ** -0.5 * LOG2_E
    for h in range(MLA_HEADS):
        nope = qa[:, h * LANES:(h + 1) * LANES]
        pe = qa[:, n_nope + h * LANES:n_nope + (h + 1) * LANES]
        qcat_ref[:, 2 * h * LANES:(2 * h + 1) * LANES] = (nope * q_scale).astype(BF16)
        qcat_ref[:, (2 * h + 1) * LANES:(2 * h + 2) * LANES] = (_rope_lanes(pe, tm_ref, shifts_m) * q_scale).astype(BF16)

    d_scale = DSA_HEAD_DIM ** -0.5 * LOG2_E
    for h in range(DSA_HEADS):
        qb_ref[:, h * LANES:(h + 1) * LANES] = (_rope_lanes(slab(_H0_DQ, h), td_ref, shifts_d) * d_scale).astype(BF16)
    for g in range(DSA_KV_HEADS):
        kb_ref[:, g * LANES:(g + 1) * LANES] = _rope_lanes(slab(_H0_DKV, g), td_ref, shifts_d).astype(BF16)
    vb_ref[...] = h_ref[:, _H0_DKV + DSA_KV_HEADS * LANES:_H0_DKV + 2 * DSA_KV_HEADS * LANES].astype(BF16)
    for j in range(IDX_HEADS * IDX_DIM // LANES):
        iq_ref[:, j * LANES:(j + 1) * LANES] = _rope_lanes(slab(_H0_IQ, j), ti_ref, shifts_i).astype(BF16)

    ki = _rope_lanes(slab(_H0_KROPE_IK), tki_ref, shifts_m + shifts_i)
    lane = lax.broadcasted_iota(jnp.int32, ki.shape, 1)
    kpe_ref[...] = jnp.where(lane < MLA_ROPE, ki, 0.0).astype(BF16)
    ik_hi = jnp.where(lane >= MLA_ROPE, ki, 0.0)
    ikhi_ref[...] = ik_hi.astype(BF16)
    iklo_ref[...] = pltpu.roll(ik_hi, LANES - MLA_ROPE, 1).astype(BF16)
    iw_ref[...] = h_ref[:, _H0_IW:_H0_IW + IDX_HEADS] * (IDX_HEADS ** -0.5 * IDX_DIM ** -0.5)


def l0_attention_operands(h0, q_norm, w_uq, kv_norm, w_ukv, tables, tm=256):
    s = h0.shape[0]
    bf = lambda n: jax.ShapeDtypeStruct((s, n), BF16)
    row = lambda n: pl.BlockSpec((tm, n), lambda i: (i, 0))
    whole = lambda a: pl.BlockSpec(a.shape, lambda i: (0,) * a.ndim)
    tab = lambda t: pl.BlockSpec((t.shape[0], tm, LANES), lambda i: (0, i, 0))
    widths = [2 * MLA_HEADS * LANES, MLA_HEADS * MLA_NOPE, MLA_HEADS * MLA_V, LANES, DSA_HEADS * DSA_HEAD_DIM,
              DSA_KV_HEADS * DSA_HEAD_DIM, DSA_KV_HEADS * DSA_HEAD_DIM, IDX_HEADS * IDX_DIM, LANES, LANES]
    w_uq = w_uq.astype(BF16)
    w_ukv = w_ukv.astype(BF16)
    q_norm = q_norm.reshape(1, -1)
    kv_norm = kv_norm.reshape(1, -1)
    return pl.pallas_call(
        _l0_prep_kernel,
        grid=(s // tm,),
        in_specs=[row(_H0_WIDTH), whole(q_norm), whole(kv_norm), whole(w_uq), whole(w_ukv)] + [tab(t) for t in tables],
        out_specs=[row(n) for n in widths] + [row(IDX_HEADS)],
        out_shape=[bf(n) for n in widths] + [jax.ShapeDtypeStruct((s, IDX_HEADS), F32)],
        compiler_params=_params("parallel"),
        name="l0_attention_operands",
    )(h0, q_norm, kv_norm, w_uq, w_ukv, *tables)


def _layer_norm(z, g, b):
    mu = jnp.mean(z, axis=-1, keepdims=True)
    zc = z - mu
    var = jnp.mean(zc * zc, axis=-1, keepdims=True)
    return zc * lax.rsqrt(var + 1e-5) * g + b


def _add_ln_route_kernel(x_ref, y_ref, g_ref, b_ref, rwt_ref, rb_ref, o_ref, obf_ref, e_ref, gate_ref):
    z = DEEPNORM_ALPHA * x_ref[...] + y_ref[...]
    out = _layer_norm(z, g_ref[...], b_ref[...])
    out_bf = out.astype(BF16)
    o_ref[...] = out
    obf_ref[...] = out_bf
    e_ref[...], gate_ref[...] = _route_tokens(out_bf, rwt_ref[...], rb_ref[...])


def add_layer_norm_route(x, y, g, b, router_w, router_b, tm=256):
    s, d = x.shape
    row = pl.BlockSpec((tm, d), lambda i: (i, 0))
    vec = pl.BlockSpec((1, d), lambda i: (0, 0))
    slots = pl.BlockSpec((TOP_K, tm), lambda i: (0, i))
    return pl.pallas_call(
        _add_ln_route_kernel,
        grid=(s // tm,),
        in_specs=[row, row, vec, vec, pl.BlockSpec((N_EXPERTS, d), lambda i: (0, 0)),
                  pl.BlockSpec((N_EXPERTS, 1), lambda i: (0, 0))],
        out_specs=[row, row, slots, slots],
        out_shape=[jax.ShapeDtypeStruct((s, d), F32), jax.ShapeDtypeStruct((s, d), BF16),
                   jax.ShapeDtypeStruct((TOP_K, s), jnp.int32), jax.ShapeDtypeStruct((TOP_K, s), F32)],
        compiler_params=_params("parallel"),
        name="add_layer_norm_route",
    )(x, y, g.reshape(1, d), b.reshape(1, d), router_w.T.astype(BF16), router_b.reshape(N_EXPERTS, 1))


def _lane_tile(x, n):
    return jnp.tile(x, (1, n))


def _softmax_step(s, v_ext, m_ref, acc_ref):
    tk = s.shape[1]
    m_prev = m_ref[...]
    m_new = jnp.maximum(m_prev, jnp.max(s, axis=1)[:, None])
    p = jnp.exp2(s - _lane_tile(m_new, tk // LANES))
    alpha = jnp.exp2(m_prev - m_new)
    pv = jnp.dot(p.astype(BF16), v_ext, preferred_element_type=F32)
    acc_ref[...] = _lane_tile(alpha, 2) * acc_ref[...] + pv
    m_ref[...] = m_new


def _mla_kernel(q_ref, kn_ref, kp_ref, v_ref, o_ref, m_sc, acc_sc, *, tq, tk, hp):
    qi = pl.program_id(1)
    m_sc[...] = jnp.full_like(m_sc, MASKED_SCORE)
    acc_sc[...] = jnp.zeros_like(acc_sc)
    ones = jnp.ones((tk, LANES), BF16)
    per_block = tq // tk

    def attend(c, diagonal_offset):
        kp = kp_ref[c]
        kn = kn_ref[c]
        v = v_ref[c]
        for j in range(hp):
            q = q_ref[:, j * 2 * LANES:(j + 1) * 2 * LANES]
            k = jnp.concatenate([kn[:, j * LANES:(j + 1) * LANES], kp], axis=1)
            s = lax.dot_general(q, k, (((1,), (1,)), ((), ())), preferred_element_type=F32)
            if diagonal_offset is not None:
                row = lax.broadcasted_iota(jnp.int32, (tq, tk), 0)
                col = diagonal_offset * tk + lax.broadcasted_iota(jnp.int32, (tq, tk), 1)
                s = jnp.where(col <= row, s, MASKED_SCORE)
            v_ext = jnp.concatenate([v[:, j * LANES:(j + 1) * LANES], ones], axis=1)
            _softmax_step(s, v_ext, m_sc.at[j], acc_sc.at[j])

    def below_diagonal(c, carry):
        attend(c, None)
        return carry

    lax.fori_loop(0, qi * per_block, below_diagonal, 0)
    for d in range(per_block):
        attend(qi * per_block + d, d)
    for j in range(hp):
        acc = acc_sc[j]
        o_ref[:, j * LANES:(j + 1) * LANES] = (acc[:, :LANES] / acc[:, LANES:]).astype(o_ref.dtype)


def mla_attention(q, k_nope, k_pe, v, tq=1024, tk=512, hp=2):
    s = q.shape[0]
    h = MLA_HEADS
    nc = s // tk
    assert tq % tk == 0 and h % hp == 0
    per_head = pl.BlockSpec((nc, tk, hp * LANES), lambda hh, qi: (0, 0, hh))
    return pl.pallas_call(
        functools.partial(_mla_kernel, tq=tq, tk=tk, hp=hp),
        grid=(h // hp, s // tq),
        in_specs=[
            pl.BlockSpec((tq, hp * 2 * LANES), lambda hh, qi: (qi, hh)),
            per_head,
            pl.BlockSpec((nc, tk, LANES), lambda hh, qi: (0, 0, 0)),
            per_head,
        ],
        out_specs=pl.BlockSpec((tq, hp * LANES), lambda hh, qi: (qi, hh)),
        out_shape=jax.ShapeDtypeStruct((s, h * MLA_V), BF16),
        scratch_shapes=[pltpu.VMEM((hp, tq, LANES), F32), pltpu.VMEM((hp, tq, 2 * LANES), F32)],
        compiler_params=_params("parallel", "arbitrary"),
        name="mla_attention",
    )(q, k_nope.reshape(nc, tk, h * MLA_NOPE), k_pe.reshape(nc, tk, LANES), v.reshape(nc, tk, h * MLA_V))


def _sortable_key(x):
    bits = pltpu.bitcast(x, jnp.int32)
    return bits ^ ((bits >> 31) & 0x7FFFFFFF)


def _dsa_kernel(iq_ref, iw_ref, iklo_ref, ikhi_ref, q_ref, k_ref, v_ref, o_ref, key_sc, wb_sc, m_sc, acc_sc,
                *, tq, tk, n_sel):
    i = pl.program_id(0)
    n_chunks = ((i + 1) * tq + tk - 1) // tk
    rep = DSA_HEADS // DSA_KV_HEADS
    lane_tiles = tk // LANES
    heads_per_block = LANES // IDX_DIM

    assert heads_per_block == 2
    iq_blocks = jnp.concatenate(
        [iq_ref[:, j * LANES:(j + 1) * LANES] for j in range(IDX_HEADS // heads_per_block)], axis=0)
    iw = iw_ref[...]
    for h in range(IDX_HEADS):
        wb_sc[h] = jnp.broadcast_to(iw[:, h:h + 1], (tq, LANES))

    def index_chunk(c, carry):
        ik = jnp.concatenate([iklo_ref[c], ikhi_ref[c]], axis=0)
        d = lax.dot_general(iq_blocks, ik, (((1,), (1,)), ((), ())), preferred_element_type=F32)
        d = jnp.maximum(d, 0.0)
        score = None
        for h in range(IDX_HEADS):
            j, part = divmod(h, heads_per_block)
            term = _lane_tile(wb_sc[h], lane_tiles) * d[j * tq:(j + 1) * tq, part * tk:(part + 1) * tk]
            score = term if score is None else score + term
        key_sc[c] = _sortable_key(score)
        return carry

    lax.fori_loop(0, n_chunks, index_chunk, 0)
    last = n_chunks - 1
    row = i * tq + lax.broadcasted_iota(jnp.int32, (tq, tk), 0)
    col = last * tk + lax.broadcasted_iota(jnp.int32, (tq, tk), 1)
    key_sc[last] = jnp.where(col <= row, key_sc[last], INT32_MIN)

    def search_bit(b, thr):
        cand = thr + lax.shift_left(jnp.int32(1), 31 - b)

        def count_chunk(c, cnt):
            key = key_sc[c]
            for j in range(lane_tiles):
                cnt = cnt + jnp.where(key[:, j * LANES:(j + 1) * LANES] >= cand, 1.0, 0.0)
            return cnt

        cnt = lax.fori_loop(0, n_chunks, count_chunk, jnp.zeros((tq, LANES), F32))
        return jnp.where(jnp.sum(cnt, axis=1)[:, None] >= n_sel, cand, thr)

    thr = lax.fori_loop(0, 32, search_bit, jnp.full((tq, LANES), INT32_MIN, jnp.int32))
    thr = _lane_tile(jnp.maximum(thr, INT32_MIN + 1), lane_tiles)

    m_sc[...] = jnp.full_like(m_sc, MASKED_SCORE)
    acc_sc[...] = jnp.zeros_like(acc_sc)
    ones = jnp.ones((tk, LANES), BF16)

    def attend_chunk(c, carry):
        bias = jnp.where(key_sc[c] >= thr, 0.0, MASKED_SCORE)
        kc = k_ref[c]
        vc = v_ref[c]
        for g in range(DSA_KV_HEADS):
            cols = slice(g * DSA_HEAD_DIM, (g + 1) * DSA_HEAD_DIM)
            qg = jnp.concatenate(
                [q_ref[:, (g * rep + r) * DSA_HEAD_DIM:(g * rep + r + 1) * DSA_HEAD_DIM] for r in range(rep)], axis=0)
            s = lax.dot_general(qg, kc[:, cols], (((1,), (1,)), ((), ())), preferred_element_type=F32)
            s = (s.reshape(rep, tq, tk) + bias[None]).reshape(rep * tq, tk)
            _softmax_step(s, jnp.concatenate([vc[:, cols], ones], axis=1), m_sc.at[g], acc_sc.at[g])
        return carry

    lax.fori_loop(0, n_chunks, attend_chunk, 0)
    for g in range(DSA_KV_HEADS):
        acc = acc_sc[g]
        out = acc[:, :LANES] / acc[:, LANES:]
        for r in range(rep):
            hh = g * rep + r
            o_ref[:, hh * DSA_HEAD_DIM:(hh + 1) * DSA_HEAD_DIM] = out[r * tq:(r + 1) * tq].astype(o_ref.dtype)


def dsa_attention(iq, iw, ik_lo, ik_hi, q, k, v, tq=128, tk=512):
    s = q.shape[0]
    n_sel = min(IDX_TOPK_MAX, s // 4)
    nc = s // tk
    assert tk >= n_sel and s % tk == 0 and tk % tq == 0 and DSA_HEAD_DIM == LANES
    rep = DSA_HEADS // DSA_KV_HEADS
    kvw = DSA_KV_HEADS * DSA_HEAD_DIM
    whole3 = lambda i: (0, 0, 0)
    return pl.pallas_call(
        functools.partial(_dsa_kernel, tq=tq, tk=tk, n_sel=n_sel),
        grid=(s // tq,),
        in_specs=[
            pl.BlockSpec((tq, IDX_HEADS * IDX_DIM), lambda i: (i, 0)),
            pl.BlockSpec((tq, IDX_HEADS), lambda i: (i, 0)),
            pl.BlockSpec((nc, tk, LANES), whole3),
            pl.BlockSpec((nc, tk, LANES), whole3),
            pl.BlockSpec((tq, DSA_HEADS * DSA_HEAD_DIM), lambda i: (i, 0)),
            pl.BlockSpec((nc, tk, kvw), whole3),
            pl.BlockSpec((nc, tk, kvw), whole3),
        ],
        out_specs=pl.BlockSpec((tq, DSA_HEADS * DSA_HEAD_DIM), lambda i: (i, 0)),
        out_shape=jax.ShapeDtypeStruct((s, DSA_HEADS * DSA_HEAD_DIM), BF16),
        scratch_shapes=[
            pltpu.VMEM((nc, tq, tk), jnp.int32),
            pltpu.VMEM((IDX_HEADS, tq, LANES), F32),
            pltpu.VMEM((DSA_KV_HEADS, rep * tq, LANES), F32),
            pltpu.VMEM((DSA_KV_HEADS, rep * tq, 2 * LANES), F32),
        ],
        compiler_params=_params("parallel"),
        name="dsa_attention",
    )(iq, iw, ik_lo.reshape(nc, tk, LANES), ik_hi.reshape(nc, tk, LANES), q, k.reshape(nc, tk, kvw),
      v.reshape(nc, tk, kvw))


def _retention_tables():
    h, c = RET_HEADS, RET_CHUNK
    log_g = np.log(1.0 - 2.0 ** (-5.0 - np.arange(h, dtype=np.float32))).astype(np.float32).astype(np.float64)
    idx = np.arange(c, dtype=np.float64)
    diff = idx[:, None] - idx[None, :]
    decay_in = np.where(diff[None] >= 0, np.exp(np.maximum(diff, 0.0)[None] * log_g[:, None, None]), 0.0)
    xi = np.exp((idx + 1.0)[None, :] * log_g[:, None])
    zeta = np.exp((c - 1.0 - idx)[None, :] * log_g[:, None])
    chunk_decay = np.exp(c * log_g)
    lane = np.ones((1, 1, LANES))
    return (decay_in.astype(np.float32), (xi[:, :, None] * lane).astype(np.float32),
            (zeta[:, :, None] * lane).astype(np.float32),
            (chunk_decay[:, None, None] * np.ones((1, 8, LANES))).astype(np.float32))


def _retention_kernel(q_ref, k_ref, v_ref, g_ref, cos_ref, sin_ref, din_ref, xi_ref, zeta_ref, cd_ref,
                      gng_ref, gnb_ref, o_ref, r_sc):
    n = pl.program_id(1)
    half = RET_QK_DIM // 2

    @pl.when(n == 0)
    def _():
        r_sc[...] = jnp.zeros_like(r_sc)

    scale = RET_QK_DIM ** -0.5
    xi = jnp.concatenate([xi_ref[0]] * (RET_V_DIM // LANES), axis=1)
    zeta = jnp.concatenate([zeta_ref[0]] * (RET_QK_DIM // LANES), axis=1)

    for ci in range(q_ref.shape[0] // RET_CHUNK):
        rows = slice(ci * RET_CHUNK, (ci + 1) * RET_CHUNK)
        cos = cos_ref[rows]
        sin = sin_ref[rows]

        def rope(x):
            x1 = x[:, :half]
            x2 = x[:, half:]
            return x1 * cos - x2 * sin, x1 * sin + x2 * cos

        q1, q2 = rope(q_ref[rows])
        k1, k2 = rope(k_ref[rows])
        qr = jnp.concatenate([q1, q2], axis=1).astype(BF16)
        kr = jnp.concatenate([k1 * scale, k2 * scale], axis=1)
        v = v_ref[rows].astype(BF16)

        inner = lax.dot_general(qr, kr.astype(BF16), (((1,), (1,)), ((), ())), preferred_element_type=F32)
        inner = inner * din_ref[0]
        r = r_sc[...]
        cross = jnp.dot(qr, r.astype(BF16), preferred_element_type=F32)
        o = jnp.dot(inner.astype(BF16), v, preferred_element_type=F32) + cross * xi

        kz = (kr * zeta).astype(BF16)
        upd = lax.dot_general(kz, v, (((0,), (0,)), ((), ())), preferred_element_type=F32)
        r_sc[...] = cd_ref[0][:1, :1] * r + upd

        mu = jnp.mean(o, axis=-1, keepdims=True)
        oc = o - mu
        var = jnp.mean(oc * oc, axis=-1, keepdims=True)
        y = oc * lax.rsqrt(var + 1e-5) * gng_ref[...] + gnb_ref[...]
        gate = g_ref[rows]
        o_ref[rows] = (gate * jax.nn.sigmoid(gate) * y).astype(o_ref.dtype)


def retention(h1, cos, sin, gn_g, gn_b, chunks_per_step=8):
    s = h1.shape[0]
    hh, dk, dv = RET_HEADS, RET_QK_DIM, RET_V_DIM
    c = RET_CHUNK * chunks_per_step
    assert s % c == 0
    din, xi, zeta, cd = (jnp.asarray(t) for t in _retention_tables())
    k_blk0 = hh * dk // dk
    v_blk0 = 2 * hh * dk // dv
    g_blk0 = v_blk0 + hh
    per_head = lambda r, w: pl.BlockSpec((1, r, w), lambda h, n: (h, 0, 0))
    return pl.pallas_call(
        _retention_kernel,
        grid=(hh, s // c),
        in_specs=[
            pl.BlockSpec((c, dk), lambda h, n: (n, h)),
            pl.BlockSpec((c, dk), lambda h, n: (n, k_blk0 + h)),
            pl.BlockSpec((c, dv), lambda h, n: (n, v_blk0 + h)),
            pl.BlockSpec((c, dv), lambda h, n: (n, g_blk0 + h)),
            pl.BlockSpec((c, dk // 2), lambda h, n: (n, 0)),
            pl.BlockSpec((c, dk // 2), lambda h, n: (n, 0)),
            per_head(RET_CHUNK, RET_CHUNK), per_head(RET_CHUNK, LANES), per_head(RET_CHUNK, LANES),
            per_head(8, LANES),
            pl.BlockSpec((1, dv), lambda h, n: (0, h)),
            pl.BlockSpec((1, dv), lambda h, n: (0, h)),
        ],
        out_specs=pl.BlockSpec((c, dv), lambda h, n: (n, h)),
        out_shape=jax.ShapeDtypeStruct((s, hh * dv), BF16),
        scratch_shapes=[pltpu.VMEM((dk, dv), F32)],
        compiler_params=_params("parallel", "arbitrary"),
        name="retention",
    )(h1, h1, h1, h1, cos, sin, din, xi, zeta, cd, gn_g.reshape(1, -1), gn_b.reshape(1, -1))


def _first_argmax(v, idx, n):
    m = jnp.max(v, axis=0, keepdims=True)
    first = jnp.min(jnp.where(v == m, idx, n), axis=0, keepdims=True)
    return m, first


def _route_tokens(x, rwt, rb):
    tm = x.shape[0]
    epg = EXPERTS_PER_GROUP
    logits = lax.dot_general(rwt, x, (((1,), (1,)), ((), ())), preferred_element_type=F32)
    scores = jax.nn.sigmoid(logits)
    biased = scores + rb
    idx = lax.broadcasted_iota(jnp.int32, (epg, tm), 0)

    group_scores = []
    for g in range(N_GROUPS):
        v = biased[g * epg:(g + 1) * epg]
        m1, first = _first_argmax(v, idx, epg)
        m2 = jnp.max(jnp.where(idx == first, -jnp.inf, v), axis=0, keepdims=True)
        group_scores.append(m1 + m2)
    gmax = group_scores[0]
    for g in range(1, N_GROUPS):
        gmax = jnp.maximum(gmax, group_scores[g])
    gsel = jnp.full((1, tm), N_GROUPS, jnp.int32)
    for g in range(N_GROUPS - 1, -1, -1):
        gsel = jnp.where(group_scores[g] == gmax, g, gsel)

    in_biased = jnp.zeros((epg, tm), F32)
    in_scores = jnp.zeros((epg, tm), F32)
    for g in range(N_GROUPS):
        pick = gsel == g
        in_biased = jnp.where(pick, biased[g * epg:(g + 1) * epg], in_biased)
        in_scores = jnp.where(pick, scores[g * epg:(g + 1) * epg], in_scores)
    _, loc1 = _first_argmax(in_biased, idx, epg)
    _, loc2 = _first_argmax(jnp.where(idx == loc1, -jnp.inf, in_biased), idx, epg)
    s1 = jnp.sum(jnp.where(idx == loc1, in_scores, 0.0), axis=0, keepdims=True)
    s2 = jnp.sum(jnp.where(idx == loc2, in_scores, 0.0), axis=0, keepdims=True)
    denom = s1 + s2
    experts = jnp.concatenate([gsel * epg + loc1, gsel * epg + loc2], axis=0)
    return experts, jnp.concatenate([s1 / denom, s2 / denom], axis=0)


def _scatter_rows_kernel(pad_ref, pos_ref, x_ref, xs_hbm, zero_sc, sem, zsem, *, tm, n_pad):
    i = pl.program_id(0)

    def zero_copy(p):
        return pltpu.make_async_copy(zero_sc, xs_hbm.at[pl.ds(pad_ref[p], 1)], zsem)

    @pl.when(i == 0)
    def _():
        zero_sc[...] = jnp.zeros_like(zero_sc)

        def start_zero(p, carry):
            zero_copy(p).start()
            return carry

        lax.fori_loop(0, n_pad, start_zero, 0)

    def start(g, carry):
        for u in range(DMA_ISSUE_UNROLL):
            r = g * DMA_ISSUE_UNROLL + u
            for k in range(TOP_K):
                pltpu.make_async_copy(x_ref.at[pl.ds(r, 1)], xs_hbm.at[pl.ds(pos_ref[0, k, r], 1)],
                                      sem).start(priority=k % 2)
        return carry

    lax.fori_loop(0, tm // DMA_ISSUE_UNROLL, start, 0)
    for k in range(TOP_K):
        pltpu.make_async_copy(x_ref, xs_hbm.at[pl.ds(0, tm)], sem).wait()

    @pl.when(i == 0)
    def _():
        def wait_zero(p, carry):
            zero_copy(p).wait()
            return carry

        lax.fori_loop(0, n_pad, wait_zero, 0)


def scatter_rows(x, pos, pad_pos, n_rows, tm=256):
    s, d = x.shape
    nt = s // tm
    n_pad = pad_pos.shape[0]
    return pl.pallas_call(
        functools.partial(_scatter_rows_kernel, tm=tm, n_pad=n_pad),
        grid_spec=pltpu.PrefetchScalarGridSpec(
            num_scalar_prefetch=1,
            grid=(nt,),
            in_specs=[
                pl.BlockSpec((1, TOP_K, tm), lambda i, pad: (i, 0, 0), memory_space=pltpu.SMEM),
                pl.BlockSpec((tm, d), lambda i, pad: (i, 0)),
            ],
            out_specs=pl.BlockSpec(memory_space=pl.ANY),
            scratch_shapes=[pltpu.VMEM((1, d), x.dtype), pltpu.SemaphoreType.DMA(()), pltpu.SemaphoreType.DMA(())],
        ),
        out_shape=jax.ShapeDtypeStruct((n_rows, d), x.dtype),
        compiler_params=_params("arbitrary"),
        name="scatter_rows",
    )(pad_pos, pos.reshape(TOP_K, nt, tm).transpose(1, 0, 2), x)


def _expert_kernel(te_ref, rs_ref, nv_ref, na_ref, dst_ref, xs_hbm, wg_ref, wu_ref, wd_ref, y_hbm,
                   wg_sc, wu_sc, wd_sc, xbuf, ybuf, lsem, ssem, *, tm):
    j = pl.program_id(0)
    n_active = na_ref[0]
    active = j < n_active
    slot = j % 2
    new_expert = (j == 0) | (te_ref[j] != te_ref[jnp.maximum(j - 1, 0)])
    chunk_sizes = [c for c in (256, 128, 64, 32, 16, 8) if c <= tm]

    def load_tile(t, dst_slot, wait):
        n8 = ((nv_ref[t] + 7) // 8) * 8
        base = rs_ref[t]
        off = jnp.int32(0)
        for c in chunk_sizes:
            take = (n8 & c) != 0

            @pl.when(take)
            def _(off=off, c=c):
                copy = pltpu.make_async_copy(
                    xs_hbm.at[pl.ds(pl.multiple_of(base + off, 8), c)],
                    xbuf.at[dst_slot, pl.ds(pl.multiple_of(off, 8), c)], lsem.at[dst_slot])
                if wait:
                    copy.wait()
                else:
                    copy.start()

            off = off + jnp.where(take, c, 0)

    def wait_scatter(src_slot, n):
        c = tm
        while c >= 1:
            @pl.when((n & c) != 0)
            def _(c=c):
                pltpu.make_async_copy(ybuf.at[src_slot, pl.ds(0, c)], y_hbm.at[pl.ds(0, c)],
                                      ssem.at[src_slot]).wait()
            c //= 2

    @pl.when(j == 0)
    def _():
        xbuf[...] = jnp.zeros_like(xbuf)
        load_tile(0, 0, False)

    @pl.when(j + 1 < n_active)
    def _():
        load_tile(j + 1, 1 - slot, False)

    @pl.when(active & new_expert)
    def _():
        wg_sc[...] = wg_ref[0].astype(BF16)
        wu_sc[...] = wu_ref[0].astype(BF16)
        wd_sc[...] = wd_ref[0].astype(BF16)

    @pl.when(active)
    def _():
        load_tile(j, slot, True)

        @pl.when(j >= 2)
        def _():
            wait_scatter(slot, nv_ref[jnp.maximum(j - 2, 0)])

        x = xbuf[slot].astype(BF16)
        hg = jnp.dot(x, wg_sc[...], preferred_element_type=F32)
        hu = jnp.dot(x, wu_sc[...], preferred_element_type=F32)
        h = hg * jax.nn.sigmoid(hg) * hu
        ybuf[slot] = jnp.dot(h.astype(BF16), wd_sc[...], preferred_element_type=F32)

        def scatter_row(r):
            pltpu.make_async_copy(ybuf.at[slot, pl.ds(r, 1)], y_hbm.at[pl.ds(dst_ref[0, 0, r], 1)],
                                  ssem.at[slot]).start()

        def scatter_group(i, carry):
            for u in range(DMA_ISSUE_UNROLL):
                scatter_row(i * DMA_ISSUE_UNROLL + u)
            return carry

        def scatter_one(r, carry):
            scatter_row(r)
            return carry

        n_groups = lax.shift_right_logical(nv_ref[j], DMA_ISSUE_UNROLL.bit_length() - 1)
        lax.fori_loop(0, n_groups, scatter_group, 0)
        lax.fori_loop(n_groups * DMA_ISSUE_UNROLL, nv_ref[j], scatter_one, 0)

        @pl.when(j == n_active - 1)
        def _():
            wait_scatter(slot, nv_ref[j])

            @pl.when(j >= 1)
            def _():
                wait_scatter(1 - slot, nv_ref[jnp.maximum(j - 1, 0)])


def routed_experts(xs, plan, n_out_rows, w_gate, w_up, w_down, tm):
    d = xs.shape[1]
    f = w_gate.shape[2]
    nt = plan.tile_expert.shape[0]

    def dst_blk(j, te, rs, nv, na):
        return (jnp.minimum(j, na[0] - 1), 0, 0)

    def weight_blk(j, te, rs, nv, na):
        return (te[j], 0, 0)

    return pl.pallas_call(
        functools.partial(_expert_kernel, tm=tm),
        grid_spec=pltpu.PrefetchScalarGridSpec(
            num_scalar_prefetch=4,
            grid=(nt,),
            in_specs=[
                pl.BlockSpec((1, 1, tm), dst_blk, memory_space=pltpu.SMEM),
                pl.BlockSpec(memory_space=pl.ANY),
                pl.BlockSpec((1, d, f), weight_blk),
                pl.BlockSpec((1, d, f), weight_blk),
                pl.BlockSpec((1, f, d), weight_blk),
            ],
            out_specs=pl.BlockSpec(memory_space=pl.ANY),
            scratch_shapes=[pltpu.VMEM((d, f), BF16), pltpu.VMEM((d, f), BF16), pltpu.VMEM((f, d), BF16),
                            pltpu.VMEM((2, tm, d), xs.dtype), pltpu.VMEM((2, tm, d), F32),
                            pltpu.SemaphoreType.DMA((2,)), pltpu.SemaphoreType.DMA((2,))],
        ),
        out_shape=jax.ShapeDtypeStruct((n_out_rows, d), F32),
        compiler_params=_params("arbitrary"),
        name="routed_experts",
    )(plan.tile_expert, plan.tile_row_start, plan.n_valid, plan.n_active, plan.dst_tiles.reshape(nt, 1, tm),
      xs, w_gate, w_up, w_down)


def _shared_kernel(x_ref, wg_ref, wu_ref, wd_ref, o_ref):
    x = x_ref[...]
    hg = jnp.dot(x, wg_ref[...], preferred_element_type=F32)
    hu = jnp.dot(x, wu_ref[...], preferred_element_type=F32)
    h = hg * jax.nn.sigmoid(hg) * hu
    o_ref[...] = jnp.dot(h.astype(BF16), wd_ref[...], preferred_element_type=F32)


def shared_expert(x_bf, wg, wu, wd, tm=512):
    s, d = x_bf.shape
    f = wg.shape[1]
    whole = lambda r, c: pl.BlockSpec((r, c), lambda i: (0, 0))
    return pl.pallas_call(
        _shared_kernel,
        grid=(s // tm,),
        in_specs=[pl.BlockSpec((tm, d), lambda i: (i, 0)), whole(d, f), whole(d, f), whole(f, d)],
        out_specs=pl.BlockSpec((tm, d), lambda i: (i, 0)),
        out_shape=jax.ShapeDtypeStruct((s, d), F32),
        compiler_params=_params("parallel"),
        name="shared_expert",
    )(x_bf, wg.astype(BF16), wu.astype(BF16), wd.astype(BF16))


def _combine_kernel(x_ref, sh_ref, gate_ref, y0_ref, y1_ref, g_ref, b_ref, o_ref, obf_ref):
    gate = gate_ref[...]
    routed = gate[:, 0:1] * y0_ref[0] + gate[:, 1:2] * y1_ref[0]
    z = DEEPNORM_ALPHA * x_ref[...] + (routed + sh_ref[...])
    out = _layer_norm(z, g_ref[...], b_ref[...])
    o_ref[...] = out
    obf_ref[...] = out.astype(BF16)


def moe_combine(x, shared, gates, y, g, b, tm=256):
    s, d = x.shape
    assert TOP_K == 2
    row = pl.BlockSpec((tm, d), lambda i: (i, 0))
    vec = pl.BlockSpec((1, d), lambda i: (0, 0))
    return pl.pallas_call(
        _combine_kernel,
        grid=(s // tm,),
        in_specs=[
            row, row,
            pl.BlockSpec((tm, TOP_K), lambda i: (i, 0)),
            pl.BlockSpec((1, tm, d), lambda i: (0, i, 0)),
            pl.BlockSpec((1, tm, d), lambda i: (1, i, 0)),
            vec, vec,
        ],
        out_specs=[row, row],
        out_shape=[jax.ShapeDtypeStruct((s, d), F32), jax.ShapeDtypeStruct((s, d), BF16)],
        compiler_params=_params("parallel"),
        name="moe_combine",
    )(x, shared, gates.T, y, y, g.reshape(1, d), b.reshape(1, d))


def _rope_tables(pos, rot_dim, theta):
    half = rot_dim // 2
    inv = (1.0 / (theta ** (np.arange(half, dtype=np.float32) / half))).astype(np.float32)
    ang = pos.astype(F32)[:, None] * inv
    return jnp.cos(ang), jnp.sin(ang)


def _rope(x, cos, sin, rot_dim):
    half = rot_dim // 2
    x1 = x[..., :half]
    x2 = x[..., half:rot_dim]
    c = cos[:, None, :]
    s = sin[:, None, :]
    return jnp.concatenate([x1 * c - x2 * s, x1 * s + x2 * c, x[..., rot_dim:]], axis=-1)


class _RoutingPlan(NamedTuple):
    pos: jax.Array
    pad_pos: jax.Array
    n_rows: int
    tile_expert: jax.Array
    tile_row_start: jax.Array
    n_valid: jax.Array
    n_active: jax.Array
    dst_tiles: jax.Array


ROW_ALIGN = 8


def _routing_plan(experts, tm):
    k, s = experts.shape
    flat = experts.reshape(-1)
    expert_ids = jnp.arange(N_EXPERTS, dtype=jnp.int32)
    onehot = (flat[:, None] == expert_ids[None, :]).astype(jnp.int32)
    csum = jnp.cumsum(onehot, axis=0)
    rank = jnp.take_along_axis(csum, flat[:, None], axis=1)[:, 0] - 1
    sizes = csum[-1]
    padded = (sizes + ROW_ALIGN - 1) // ROW_ALIGN * ROW_ALIGN
    seg_end = jnp.cumsum(padded)
    seg_start = seg_end - padded
    pos = seg_start[flat] + rank
    n_rows = k * s + N_EXPERTS * (ROW_ALIGN - 1)

    i_pad = jnp.arange(ROW_ALIGN - 1, dtype=jnp.int32)
    is_pad = i_pad[None, :] < (padded - sizes)[:, None]
    spare = jnp.cumsum(jnp.logical_not(is_pad).reshape(-1).astype(jnp.int32)).reshape(is_pad.shape) - 1
    pad_pos = jnp.where(is_pad, (seg_start + sizes)[:, None] + i_pad[None, :], seg_end[-1] + spare).reshape(-1)

    tiles = (sizes + tm - 1) // tm
    tile_end = jnp.cumsum(tiles)
    n_active = tile_end[-1]
    n_tiles = k * s // tm + N_EXPERTS
    tile_ids = jnp.minimum(jnp.arange(n_tiles, dtype=jnp.int32), n_active - 1)
    tile_expert = jnp.searchsorted(tile_end, tile_ids, side="right").astype(jnp.int32)
    local = tile_ids - (tile_end - tiles)[tile_expert]
    tile_row_start = seg_start[tile_expert] + local * tm
    n_valid = jnp.clip(sizes[tile_expert] - local * tm, 0, tm)
    slot_of_row = jnp.zeros((n_rows,), jnp.int32).at[pos].set(jnp.arange(k * s, dtype=jnp.int32))
    rows = jnp.minimum(tile_row_start[:, None] + jnp.arange(tm, dtype=jnp.int32)[None, :], n_rows - 1)
    return _RoutingPlan(pos.reshape(k, s), pad_pos.astype(jnp.int32), n_rows, tile_expert,
                        tile_row_start.astype(jnp.int32), n_valid.astype(jnp.int32),
                        n_active.reshape(1).astype(jnp.int32), slot_of_row[rows])


def _grouped_moe(x, x_bf, experts, gates, w_gate, w_up, w_down, sh_gate, sh_up, sh_down, ln_g, ln_b,
                 expert_tm=256):
    s, d = x.shape
    plan = _routing_plan(experts, expert_tm)
    xs = scatter_rows(x, plan.pos, plan.pad_pos, plan.n_rows)
    y = routed_experts(xs, plan, TOP_K * s, w_gate, w_up, w_down, expert_tm)
    shared = shared_expert(x_bf, sh_gate, sh_up, sh_down)
    return moe_combine(x, shared, gates, y.reshape(TOP_K, s, d), ln_g, ln_b)


def _rope_lane_tables(pos, half, theta, period, offset=0, limit=None):
    cos, sin = _rope_tables(pos, 2 * half, theta)
    lane = np.arange(LANES)
    p = lane % period - offset
    in_range = np.ones(LANES, bool) if limit is None else lane < limit
    first = (p >= 0) & (p < half) & in_range
    second = (p >= half) & (p < 2 * half) & in_range
    idx = np.where(first | second, p % half, 0)
    cos_l, sin_l = cos[:, idx], sin[:, idx]
    return jnp.stack([jnp.where(first | second, cos_l, 1.0),
                      jnp.where(second, sin_l, 0.0),
                      jnp.where(first, -sin_l, 0.0)])


def _mla_dsa_mixer(x, pos, w_in, q_norm, w_uq, kv_norm, w_ukv, w_out):
    cq, ckv, krope, dq, dkv, iq, ik, iw = jnp.split(w_in, [int(c) for c in np.cumsum(SPLITS_AB)[:-1]], axis=1)
    w_in_l = jnp.concatenate([cq, ckv, dq, dkv, iq, krope, ik, iw,
                              jnp.zeros((w_in.shape[0], LANES - IDX_HEADS), w_in.dtype)], axis=1)
    w_uq3 = w_uq.reshape(MLA_Q_LORA, MLA_HEADS, MLA_NOPE + MLA_ROPE)
    w_uq_l = jnp.concatenate([w_uq3[:, :, :MLA_NOPE].reshape(MLA_Q_LORA, -1),
                              jnp.pad(w_uq3[:, :, MLA_NOPE:], ((0, 0), (0, 0), (0, LANES - MLA_ROPE))
                                      ).reshape(MLA_Q_LORA, -1)], axis=1)
    w_ukv3 = w_ukv.reshape(MLA_KV_LORA, MLA_HEADS, MLA_NOPE + MLA_V)
    w_ukv_l = jnp.concatenate([w_ukv3[:, :, :MLA_NOPE].reshape(MLA_KV_LORA, -1),
                               w_ukv3[:, :, MLA_NOPE:].reshape(MLA_KV_LORA, -1)], axis=1)

    tab_m = _rope_lane_tables(pos, MLA_ROPE // 2, ROPE_THETA, LANES)
    tab_d = _rope_lane_tables(pos, DSA_ROT // 2, ROPE_THETA, LANES)
    tab_i = _rope_lane_tables(pos, IDX_ROT // 2, ROPE_THETA, IDX_DIM)
    tab_k = _rope_lane_tables(pos, MLA_ROPE // 2, ROPE_THETA, LANES, limit=MLA_ROPE)
    tab_ik = _rope_lane_tables(pos, IDX_ROT // 2, ROPE_THETA, LANES, offset=MLA_ROPE)
    tab_ki = jnp.concatenate([(tab_k[0] * tab_ik[0])[None], tab_k[1:], tab_ik[1:]])

    h0 = matmul(x, w_in_l, F32, 1024, 512, name="l0_in_proj")
    q_cat, k_nope, v_a, k_pe, qb, kb, vb, iq_r, ik_lo, ik_hi, iw_s = l0_attention_operands(
        h0, q_norm, w_uq_l, kv_norm, w_ukv_l, (tab_m, tab_d, tab_i, tab_ki))
    out_a = mla_attention(q_cat, k_nope, k_pe, v_a)
    out_b = dsa_attention(iq_r, iw_s, ik_lo, ik_hi, qb, kb, vb)
    return matmul_concat(out_a, out_b, w_out, F32, 1024, 512, name="l0_out_proj")


def _retention_mixer(x_bf, pos, w_in, gn_g, gn_b, w_out):
    h1 = matmul(x_bf, w_in, F32, 1024, 1024, name="l1_in_proj")
    cos, sin = _rope_tables(pos, RET_QK_DIM, RET_THETA)
    y = retention(h1, cos, sin, gn_g, gn_b)
    return matmul(y, w_out, F32, 1024, 512, name="l1_out_proj")


def kernel(x, positions, router_w, router_b, l0_w_in, l0_mla_q_norm, l0_mla_w_uq, l0_mla_kv_norm, l0_mla_w_ukv, l0_w_out, l1_w_in, l1_ret_gn_g, l1_ret_gn_b, l1_w_out, l0_ln_mix_g, l0_ln_mix_b, l0_moe_w_gate, l0_moe_w_up, l0_moe_w_down, l0_sh_gate, l0_sh_up, l0_sh_down, l0_ln_ffn_g, l0_ln_ffn_b, l1_ln_mix_g, l1_ln_mix_b, l1_moe_w_gate, l1_moe_w_up, l1_moe_w_down, l1_sh_gate, l1_sh_up, l1_sh_down, l1_ln_ffn_g, l1_ln_ffn_b):
    assert x.shape[0] == 1
    xt = x[0]
    pos = positions[0]

    mix = _mla_dsa_mixer(xt, pos, l0_w_in, l0_mla_q_norm, l0_mla_w_uq, l0_mla_kv_norm, l0_mla_w_ukv, l0_w_out)
    xt, xt_bf, experts, gates = add_layer_norm_route(xt, mix, l0_ln_mix_g, l0_ln_mix_b, router_w, router_b)
    xt, xt_bf = _grouped_moe(xt, xt_bf, experts, gates, l0_moe_w_gate, l0_moe_w_up, l0_moe_w_down,
                             l0_sh_gate, l0_sh_up, l0_sh_down, l0_ln_ffn_g, l0_ln_ffn_b)

    mix = _retention_mixer(xt_bf, pos, l1_w_in, l1_ret_gn_g, l1_ret_gn_b, l1_w_out)
    xt, xt_bf, experts, gates = add_layer_norm_route(xt, mix, l1_ln_mix_g, l1_ln_mix_b, router_w, router_b)
    xt, _ = _grouped_moe(xt, xt_bf, experts, gates, l1_moe_w_gate, l1_moe_w_up, l1_moe_w_down,
                         l1_sh_gate, l1_sh_up, l1_sh_down, l1_ln_ffn_g, l1_ln_ffn_b)
    return xt[None]
```

```python
import functools
from typing import NamedTuple

import numpy as np
import jax
import jax.numpy as jnp
from jax import lax
from jax.experimental import pallas as pl
from jax.experimental.pallas import tpu as pltpu

F32 = jnp.float32
BF16 = jnp.bfloat16

D_MODEL = 2048
DEPTH = 2
ROPE_THETA = 500000.0
MLA_HEADS = 8
MLA_Q_LORA = 512
MLA_KV_LORA = 256
MLA_NOPE = 128
MLA_ROPE = 64
MLA_V = 128
DSA_HEADS = 8
DSA_KV_HEADS = 2
DSA_HEAD_DIM = 128
DSA_ROT = DSA_HEAD_DIM // 4
IDX_HEADS = 16
IDX_DIM = 64
IDX_ROT = IDX_DIM // 4
IDX_TOPK_MAX = 256
RET_HEADS = 8
RET_QK_DIM = 256
RET_V_DIM = 512
RET_CHUNK = 128
RET_THETA = 10000.0
N_EXPERTS = 64
N_GROUPS = 8
EXPERTS_PER_GROUP = N_EXPERTS // N_GROUPS
TOP_K = 2
D_EXPERT = 512
D_SHARED = 1024
DEEPNORM_ALPHA = (2.0 * DEPTH) ** 0.25

SPLITS_AB = (MLA_Q_LORA, MLA_KV_LORA, MLA_ROPE, DSA_HEADS * DSA_HEAD_DIM,
             2 * DSA_KV_HEADS * DSA_HEAD_DIM, IDX_HEADS * IDX_DIM, IDX_DIM, IDX_HEADS)
IN_AB = sum(SPLITS_AB)

VMEM_LIMIT_BYTES = 56 * 1024 * 1024
LANES = 128
MASKED_SCORE = -1e30
LOG2_E = 1.4426950408889634
DMA_ISSUE_UNROLL = 8
INT32_MIN = -2 ** 31


def _params(*sem):
    return pltpu.CompilerParams(dimension_semantics=sem, vmem_limit_bytes=VMEM_LIMIT_BYTES)


def _mm_kernel(*refs, norm_eps):
    if norm_eps is None:
        a_ref, b_ref, o_ref = refs
        a = a_ref[...]
    else:
        a_ref, g_ref, b_ref, o_ref = refs
        af = a_ref[...].astype(F32)
        a = af * lax.rsqrt(jnp.mean(af * af, axis=-1, keepdims=True) + norm_eps) * g_ref[...]
    o_ref[...] = jnp.dot(a.astype(BF16), b_ref[...].astype(BF16),
                         preferred_element_type=F32).astype(o_ref.dtype)


def matmul(a, b, out_dtype, tm, tn, *, name, a_cols=None, norm_gain=None, norm_eps=1e-6):
    m = a.shape[0]
    k, n = b.shape
    col_blk = 0 if a_cols is None else a_cols[0]
    assert (a.shape[1] == k) if a_cols is None else (a_cols[1] == k)
    assert m % tm == 0 and n % tn == 0
    in_specs = [pl.BlockSpec((tm, k), lambda i, j: (i, col_blk))]
    args = [a]
    if norm_gain is not None:
        in_specs.append(pl.BlockSpec((1, k), lambda i, j: (0, 0)))
        args.append(norm_gain.reshape(1, k))
    in_specs.append(pl.BlockSpec((k, tn), lambda i, j: (0, j)))
    args.append(b)
    return pl.pallas_call(
        functools.partial(_mm_kernel, norm_eps=None if norm_gain is None else norm_eps),
        grid=(m // tm, n // tn),
        in_specs=in_specs,
        out_specs=pl.BlockSpec((tm, tn), lambda i, j: (i, j)),
        out_shape=jax.ShapeDtypeStruct((m, n), out_dtype),
        compiler_params=_params("parallel", "parallel"),
        name=name,
    )(*args)


_H0_CQ = 0
_H0_CKV = _H0_CQ + MLA_Q_LORA
_H0_DQ = _H0_CKV + MLA_KV_LORA
_H0_DKV = _H0_DQ + DSA_HEADS * DSA_HEAD_DIM
_H0_IQ = _H0_DKV + 2 * DSA_KV_HEADS * DSA_HEAD_DIM
_H0_KROPE_IK = _H0_IQ + IDX_HEADS * IDX_DIM
_H0_IW = _H0_KROPE_IK + LANES
_H0_WIDTH = _H0_IW + LANES


def _rope_lanes(x, tab_ref, shifts):
    out = x * tab_ref[0]
    for i, shift in enumerate(shifts):
        out = out + pltpu.roll(x, shift, 1) * tab_ref[1 + i]
    return out


def _l0_prep_kernel(h_ref, qn_ref, kvn_ref, wuq_ref, wukv_ref, tm_ref, td_ref, ti_ref, tki_ref,
                    qcat_ref, knope_ref, va_ref, kpe_ref, qb_ref, kb_ref, vb_ref, iq_ref, iklo_ref, ikhi_ref,
                    iw_ref):
    def rms(x, g_ref):
        return (x * lax.rsqrt(jnp.mean(x * x, axis=-1, keepdims=True) + 1e-6) * g_ref[...]).astype(BF16)

    def slab(col, j=0):
        return h_ref[:, col + j * LANES:col + (j + 1) * LANES]

    half_m, half_d, half_i = MLA_ROPE // 2, DSA_ROT // 2, IDX_ROT // 2
    shifts_m = (half_m, LANES - half_m)
    shifts_d = (half_d, LANES - half_d)
    shifts_i = (half_i, LANES - half_i)

    qa = jnp.dot(rms(h_ref[:, _H0_CQ:_H0_CQ + MLA_Q_LORA], qn_ref), wuq_ref[...], preferred_element_type=F32)
    kva = jnp.dot(rms(h_ref[:, _H0_CKV:_H0_CKV + MLA_KV_LORA], kvn_ref), wukv_ref[...],
                  preferred_element_type=F32)
    n_nope = MLA_HEADS * MLA_NOPE
    knope_ref[...] = kva[:, :n_nope].astype(BF16)
    va_ref[...] = kva[:, n_nope:].astype(BF16)
    q_scale = (MLA_NOPE + MLA_ROPE) ** -0.5 * LOG2_E
    for h in range(MLA_HEADS):
        nope = qa[:, h * LANES:(h + 1) * LANES]
        pe = qa[:, n_nope + h * LANES:n_nope + (h + 1) * LANES]
        qcat_ref[:, 2 * h * LANES:(2 * h + 1) * LANES] = (nope * q_scale).astype(BF16)
        qcat_ref[:, (2 * h + 1) * LANES:(2 * h + 2) * LANES] = (_rope_lanes(pe, tm_ref, shifts_m) * q_scale).astype(BF16)

    d_scale = DSA_HEAD_DIM ** -0.5 * LOG2_E
    for h in range(DSA_HEADS):
        qb_ref[:, h * LANES:(h + 1) * LANES] = (_rope_lanes(slab(_H0_DQ, h), td_ref, shifts_d) * d_scale).astype(BF16)
    for g in range(DSA_KV_HEADS):
        kb_ref[:, g * LANES:(g + 1) * LANES] = _rope_lanes(slab(_H0_DKV, g), td_ref, shifts_d).astype(BF16)
    vb_ref[...] = h_ref[:, _H0_DKV + DSA_KV_HEADS * LANES:_H0_DKV + 2 * DSA_KV_HEADS * LANES].astype(BF16)
    for j in range(IDX_HEADS * IDX_DIM // LANES):
        iq_ref[:, j * LANES:(j + 1) * LANES] = _rope_lanes(slab(_H0_IQ, j), ti_ref, shifts_i).astype(BF16)

    ki = _rope_lanes(slab(_H0_KROPE_IK), tki_ref, shifts_m + shifts_i)
    lane = lax.broadcasted_iota(jnp.int32, ki.shape, 1)
    kpe_ref[...] = jnp.where(lane < MLA_ROPE, ki, 0.0).astype(BF16)
    ik_hi = jnp.where(lane >= MLA_ROPE, ki, 0.0)
    ikhi_ref[...] = ik_hi.astype(BF16)
    iklo_ref[...] = pltpu.roll(ik_hi, LANES - MLA_ROPE, 1).astype(BF16)
    iw_ref[...] = h_ref[:, _H0_IW:_H0_IW + IDX_HEADS] * (IDX_HEADS ** -0.5 * IDX_DIM ** -0.5)


def l0_attention_operands(h0, q_norm, w_uq, kv_norm, w_ukv, tables, tm=256):
    s = h0.shape[0]
    bf = lambda n: jax.ShapeDtypeStruct((s, n), BF16)
    row = lambda n: pl.BlockSpec((tm, n), lambda i: (i, 0))
    whole = lambda a: pl.BlockSpec(a.shape, lambda i: (0,) * a.ndim)
    tab = lambda t: pl.BlockSpec((t.shape[0], tm, LANES), lambda i: (0, i, 0))
    widths = [2 * MLA_HEADS * LANES, MLA_HEADS * MLA_NOPE, MLA_HEADS * MLA_V, LANES, DSA_HEADS * DSA_HEAD_DIM,
              DSA_KV_HEADS * DSA_HEAD_DIM, DSA_KV_HEADS * DSA_HEAD_DIM, IDX_HEADS * IDX_DIM, LANES, LANES]
    w_uq = w_uq.astype(BF16)
    w_ukv = w_ukv.astype(BF16)
    q_norm = q_norm.reshape(1, -1)
    kv_norm = kv_norm.reshape(1, -1)
    return pl.pallas_call(
        _l0_prep_kernel,
        grid=(s // tm,),
        in_specs=[row(_H0_WIDTH), whole(q_norm), whole(kv_norm), whole(w_uq), whole(w_ukv)] + [tab(t) for t in tables],
        out_specs=[row(n) for n in widths] + [row(IDX_HEADS)],
        out_shape=[bf(n) for n in widths] + [jax.ShapeDtypeStruct((s, IDX_HEADS), F32)],
        compiler_params=_params("parallel"),
        name="l0_attention_operands",
    )(h0, q_norm, kv_norm, w_uq, w_ukv, *tables)


def _layer_norm(z, g, b):
    mu = jnp.mean(z, axis=-1, keepdims=True)
    zc = z - mu
    var = jnp.mean(zc * zc, axis=-1, keepdims=True)
    return zc * lax.rsqrt(var + 1e-5) * g + b


def _proj_ln_route_kernel(*refs, n_parts):
    a_refs = refs[:n_parts]
    w_ref, x_ref, g_ref, b_ref, rwt_ref, rb_ref, o_ref, obf_ref, e_ref, gate_ref = refs[n_parts:]
    mix = None
    k0 = 0
    for a_ref in a_refs:
        k = a_ref.shape[1]
        part = jnp.dot(a_ref[...], w_ref[k0:k0 + k, :], preferred_element_type=F32)
        mix = part if mix is None else mix + part
        k0 += k
    out = _layer_norm(DEEPNORM_ALPHA * x_ref[...] + mix, g_ref[...], b_ref[...])
    out_bf = out.astype(BF16)
    o_ref[...] = out
    obf_ref[...] = out_bf
    e_ref[...], gate_ref[...] = _route_tokens(out_bf, rwt_ref[...], rb_ref[...])


def out_proj_ln_route(a_parts, w, x, g, b, router_w, router_b, *, name, tm=512):
    s, d = x.shape
    w = w.astype(BF16)
    assert sum(a.shape[1] for a in a_parts) == w.shape[0] and w.shape[1] == d
    row = pl.BlockSpec((tm, d), lambda i: (i, 0))
    vec = pl.BlockSpec((1, d), lambda i: (0, 0))
    slots = pl.BlockSpec((TOP_K, tm), lambda i: (0, i))
    resident = lambda shape: pl.BlockSpec(shape, lambda i: (0, 0), pipeline_mode=pl.Buffered(1))
    return pl.pallas_call(
        functools.partial(_proj_ln_route_kernel, n_parts=len(a_parts)),
        grid=(s // tm,),
        in_specs=[pl.BlockSpec((tm, a.shape[1]), lambda i: (i, 0)) for a in a_parts]
        + [resident(w.shape), row, vec, vec, resident((N_EXPERTS, d)), resident((N_EXPERTS, 1))],
        out_specs=[row, row, slots, slots],
        out_shape=[jax.ShapeDtypeStruct((s, d), F32), jax.ShapeDtypeStruct((s, d), BF16),
                   jax.ShapeDtypeStruct((TOP_K, s), jnp.int32), jax.ShapeDtypeStruct((TOP_K, s), F32)],
        compiler_params=_params("parallel"),
        name=name,
    )(*a_parts, w, x, g.reshape(1, d), b.reshape(1, d), router_w.T.astype(BF16), router_b.reshape(N_EXPERTS, 1))


def _lane_tile(x, n):
    return jnp.tile(x, (1, n))


def _softmax_step(s, v_ext, m_ref, acc_ref):
    tk = s.shape[1]
    m_prev = m_ref[...]
    m_new = jnp.maximum(m_prev, jnp.max(s, axis=1)[:, None])
    p = jnp.exp2(s - _lane_tile(m_new, tk // LANES))
    alpha = jnp.exp2(m_prev - m_new)
    pv = jnp.dot(p.astype(BF16), v_ext, preferred_element_type=F32)
    acc_ref[...] = _lane_tile(alpha, 2) * acc_ref[...] + pv
    m_ref[...] = m_new


def _mla_kernel(q_ref, kn_ref, kp_ref, v_ref, o_ref, m_sc, acc_sc, *, tq, tk, hp):
    qi = pl.program_id(1)
    m_sc[...] = jnp.full_like(m_sc, MASKED_SCORE)
    acc_sc[...] = jnp.zeros_like(acc_sc)
    ones = jnp.ones((tk, LANES), BF16)
    per_block = tq // tk

    def attend(c, diagonal_offset):
        kp = kp_ref[c]
        kn = kn_ref[c]
        v = v_ref[c]
        for j in range(hp):
            q = q_ref[:, j * 2 * LANES:(j + 1) * 2 * LANES]
            k = jnp.concatenate([kn[:, j * LANES:(j + 1) * LANES], kp], axis=1)
            s = lax.dot_general(q, k, (((1,), (1,)), ((), ())), preferred_element_type=F32)
            if diagonal_offset is not None:
                row = lax.broadcasted_iota(jnp.int32, (tq, tk), 0)
                col = diagonal_offset * tk + lax.broadcasted_iota(jnp.int32, (tq, tk), 1)
                s = jnp.where(col <= row, s, MASKED_SCORE)
            v_ext = jnp.concatenate([v[:, j * LANES:(j + 1) * LANES], ones], axis=1)
            _softmax_step(s, v_ext, m_sc.at[j], acc_sc.at[j])

    def below_diagonal(c, carry):
        attend(c, None)
        return carry

    lax.fori_loop(0, qi * per_block, below_diagonal, 0)
    for d in range(per_block):
        attend(qi * per_block + d, d)
    for j in range(hp):
        acc = acc_sc[j]
        o_ref[:, j * LANES:(j + 1) * LANES] = (acc[:, :LANES] / acc[:, LANES:]).astype(o_ref.dtype)


def mla_attention(q, k_nope, k_pe, v, tq=1024, tk=512, hp=2):
    s = q.shape[0]
    h = MLA_HEADS
    nc = s // tk
    assert tq % tk == 0 and h % hp == 0
    per_head = pl.BlockSpec((nc, tk, hp * LANES), lambda hh, qi: (0, 0, hh))
    return pl.pallas_call(
        functools.partial(_mla_kernel, tq=tq, tk=tk, hp=hp),
        grid=(h // hp, s // tq),
        in_specs=[
            pl.BlockSpec((tq, hp * 2 * LANES), lambda hh, qi: (qi, hh)),
            per_head,
            pl.BlockSpec((nc, tk, LANES), lambda hh, qi: (0, 0, 0)),
            per_head,
        ],
        out_specs=pl.BlockSpec((tq, hp * LANES), lambda hh, qi: (qi, hh)),
        out_shape=jax.ShapeDtypeStruct((s, h * MLA_V), BF16),
        scratch_shapes=[pltpu.VMEM((hp, tq, LANES), F32), pltpu.VMEM((hp, tq, 2 * LANES), F32)],
        compiler_params=_params("parallel", "arbitrary"),
        name="mla_attention",
    )(q, k_nope.reshape(nc, tk, h * MLA_NOPE), k_pe.reshape(nc, tk, LANES), v.reshape(nc, tk, h * MLA_V))


def _sortable_key(x):
    bits = pltpu.bitcast(x, jnp.int32)
    return bits ^ ((bits >> 31) & 0x7FFFFFFF)


def _dsa_kernel(iq_ref, iw_ref, iklo_ref, ikhi_ref, q_ref, k_ref, v_ref, o_ref, key_sc, wb_sc, m_sc, acc_sc,
                *, tq, tk, n_sel):
    i = pl.program_id(0)
    n_chunks = ((i + 1) * tq + tk - 1) // tk
    rep = DSA_HEADS // DSA_KV_HEADS
    lane_tiles = tk // LANES
    heads_per_block = LANES // IDX_DIM

    assert heads_per_block == 2
    iq_blocks = jnp.concatenate(
        [iq_ref[:, j * LANES:(j + 1) * LANES] for j in range(IDX_HEADS // heads_per_block)], axis=0)
    iw = iw_ref[...]
    for h in range(IDX_HEADS):
        wb_sc[h] = jnp.broadcast_to(iw[:, h:h + 1], (tq, LANES))

    def index_chunk(c, carry):
        ik = jnp.concatenate([iklo_ref[c], ikhi_ref[c]], axis=0)
        d = lax.dot_general(iq_blocks, ik, (((1,), (1,)), ((), ())), preferred_element_type=F32)
        d = jnp.maximum(d, 0.0)
        score = None
        for h in range(IDX_HEADS):
            j, part = divmod(h, heads_per_block)
            term = _lane_tile(wb_sc[h], lane_tiles) * d[j * tq:(j + 1) * tq, part * tk:(part + 1) * tk]
            score = term if score is None else score + term
        key_sc[c] = _sortable_key(score)
        return carry

    lax.fori_loop(0, n_chunks, index_chunk, 0)
    last = n_chunks - 1
    row = i * tq + lax.broadcasted_iota(jnp.int32, (tq, tk), 0)
    col = last * tk + lax.broadcasted_iota(jnp.int32, (tq, tk), 1)
    key_sc[last] = jnp.where(col <= row, key_sc[last], INT32_MIN)

    def search_bit(b, thr):
        cand = thr + lax.shift_left(jnp.int32(1), 31 - b)

        def count_chunk(c, cnt):
            key = key_sc[c]
            for j in range(lane_tiles):
                cnt = cnt + jnp.where(key[:, j * LANES:(j + 1) * LANES] >= cand, 1.0, 0.0)
            return cnt

        cnt = lax.fori_loop(0, n_chunks, count_chunk, jnp.zeros((tq, LANES), F32))
        return jnp.where(jnp.sum(cnt, axis=1)[:, None] >= n_sel, cand, thr)

    thr = lax.fori_loop(0, 32, search_bit, jnp.full((tq, LANES), INT32_MIN, jnp.int32))
    thr = _lane_tile(jnp.maximum(thr, INT32_MIN + 1), lane_tiles)

    m_sc[...] = jnp.full_like(m_sc, MASKED_SCORE)
    acc_sc[...] = jnp.zeros_like(acc_sc)
    ones = jnp.ones((tk, LANES), BF16)

    def attend_chunk(c, carry):
        bias = jnp.where(key_sc[c] >= thr, 0.0, MASKED_SCORE)
        kc = k_ref[c]
        vc = v_ref[c]
        for g in range(DSA_KV_HEADS):
            cols = slice(g * DSA_HEAD_DIM, (g + 1) * DSA_HEAD_DIM)
            qg = jnp.concatenate(
                [q_ref[:, (g * rep + r) * DSA_HEAD_DIM:(g * rep + r + 1) * DSA_HEAD_DIM] for r in range(rep)], axis=0)
            s = lax.dot_general(qg, kc[:, cols], (((1,), (1,)), ((), ())), preferred_element_type=F32)
            s = (s.reshape(rep, tq, tk) + bias[None]).reshape(rep * tq, tk)
            _softmax_step(s, jnp.concatenate([vc[:, cols], ones], axis=1), m_sc.at[g], acc_sc.at[g])
        return carry

    lax.fori_loop(0, n_chunks, attend_chunk, 0)
    for g in range(DSA_KV_HEADS):
        acc = acc_sc[g]
        out = acc[:, :LANES] / acc[:, LANES:]
        for r in range(rep):
            hh = g * rep + r
            o_ref[:, hh * DSA_HEAD_DIM:(hh + 1) * DSA_HEAD_DIM] = out[r * tq:(r + 1) * tq].astype(o_ref.dtype)


def dsa_attention(iq, iw, ik_lo, ik_hi, q, k, v, tq=128, tk=512):
    s = q.shape[0]
    n_sel = min(IDX_TOPK_MAX, s // 4)
    nc = s // tk
    assert tk >= n_sel and s % tk == 0 and tk % tq == 0 and DSA_HEAD_DIM == LANES
    rep = DSA_HEADS // DSA_KV_HEADS
    kvw = DSA_KV_HEADS * DSA_HEAD_DIM
    whole3 = lambda i: (0, 0, 0)
    return pl.pallas_call(
        functools.partial(_dsa_kernel, tq=tq, tk=tk, n_sel=n_sel),
        grid=(s // tq,),
        in_specs=[
            pl.BlockSpec((tq, IDX_HEADS * IDX_DIM), lambda i: (i, 0)),
            pl.BlockSpec((tq, IDX_HEADS), lambda i: (i, 0)),
            pl.BlockSpec((nc, tk, LANES), whole3),
            pl.BlockSpec((nc, tk, LANES), whole3),
            pl.BlockSpec((tq, DSA_HEADS * DSA_HEAD_DIM), lambda i: (i, 0)),
            pl.BlockSpec((nc, tk, kvw), whole3),
            pl.BlockSpec((nc, tk, kvw), whole3),
        ],
        out_specs=pl.BlockSpec((tq, DSA_HEADS * DSA_HEAD_DIM), lambda i: (i, 0)),
        out_shape=jax.ShapeDtypeStruct((s, DSA_HEADS * DSA_HEAD_DIM), BF16),
        scratch_shapes=[
            pltpu.VMEM((nc, tq, tk), jnp.int32),
            pltpu.VMEM((IDX_HEADS, tq, LANES), F32),
            pltpu.VMEM((DSA_KV_HEADS, rep * tq, LANES), F32),
            pltpu.VMEM((DSA_KV_HEADS, rep * tq, 2 * LANES), F32),
        ],
        compiler_params=_params("parallel"),
        name="dsa_attention",
    )(iq, iw, ik_lo.reshape(nc, tk, LANES), ik_hi.reshape(nc, tk, LANES), q, k.reshape(nc, tk, kvw),
      v.reshape(nc, tk, kvw))


def _retention_tables():
    h, c = RET_HEADS, RET_CHUNK
    log_g = np.log(1.0 - 2.0 ** (-5.0 - np.arange(h, dtype=np.float32))).astype(np.float32).astype(np.float64)
    idx = np.arange(c, dtype=np.float64)
    diff = idx[:, None] - idx[None, :]
    decay_in = np.where(diff[None] >= 0, np.exp(np.maximum(diff, 0.0)[None] * log_g[:, None, None]), 0.0)
    xi = np.exp((idx + 1.0)[None, :] * log_g[:, None])
    zeta = np.exp((c - 1.0 - idx)[None, :] * log_g[:, None])
    chunk_decay = np.exp(c * log_g)
    lane = np.ones((1, 1, LANES))
    return (decay_in.astype(np.float32), (xi[:, :, None] * lane).astype(np.float32),
            (zeta[:, :, None] * lane).astype(np.float32),
            (chunk_decay[:, None, None] * np.ones((1, 8, LANES))).astype(np.float32))


def _retention_kernel(q_ref, k_ref, v_ref, g_ref, cos_ref, sin_ref, din_ref, xi_ref, zeta_ref, cd_ref,
                      gng_ref, gnb_ref, o_ref, r_sc):
    n = pl.program_id(1)
    half = RET_QK_DIM // 2

    @pl.when(n == 0)
    def _():
        r_sc[...] = jnp.zeros_like(r_sc)

    scale = RET_QK_DIM ** -0.5
    xi = jnp.concatenate([xi_ref[0]] * (RET_V_DIM // LANES), axis=1)
    zeta = jnp.concatenate([zeta_ref[0]] * (RET_QK_DIM // LANES), axis=1)

    for ci in range(q_ref.shape[0] // RET_CHUNK):
        rows = slice(ci * RET_CHUNK, (ci + 1) * RET_CHUNK)
        cos = cos_ref[rows]
        sin = sin_ref[rows]

        def rope(x):
            x1 = x[:, :half]
            x2 = x[:, half:]
            return x1 * cos - x2 * sin, x1 * sin + x2 * cos

        q1, q2 = rope(q_ref[rows].astype(F32))
        k1, k2 = rope(k_ref[rows].astype(F32))
        qr = jnp.concatenate([q1, q2], axis=1).astype(BF16)
        kr = jnp.concatenate([k1 * scale, k2 * scale], axis=1)
        v = v_ref[rows].astype(BF16)

        inner = lax.dot_general(qr, kr.astype(BF16), (((1,), (1,)), ((), ())), preferred_element_type=F32)
        inner = inner * din_ref[0]
        r = r_sc[...]
        cross = jnp.dot(qr, r.astype(BF16), preferred_element_type=F32)
        o = jnp.dot(inner.astype(BF16), v, preferred_element_type=F32) + cross * xi

        kz = (kr * zeta).astype(BF16)
        upd = lax.dot_general(kz, v, (((0,), (0,)), ((), ())), preferred_element_type=F32)
        r_sc[...] = cd_ref[0][:1, :1] * r + upd

        mu = jnp.mean(o, axis=-1, keepdims=True)
        oc = o - mu
        var = jnp.mean(oc * oc, axis=-1, keepdims=True)
        y = oc * lax.rsqrt(var + 1e-5) * gng_ref[...] + gnb_ref[...]
        gate = g_ref[rows].astype(F32)
        o_ref[rows] = (gate * jax.nn.sigmoid(gate) * y).astype(o_ref.dtype)


def retention(h1, cos, sin, gn_g, gn_b, chunks_per_step=8):
    s = h1.shape[0]
    hh, dk, dv = RET_HEADS, RET_QK_DIM, RET_V_DIM
    c = RET_CHUNK * chunks_per_step
    assert s % c == 0
    din, xi, zeta, cd = (jnp.asarray(t) for t in _retention_tables())
    k_blk0 = hh * dk // dk
    v_blk0 = 2 * hh * dk // dv
    g_blk0 = v_blk0 + hh
    per_head = lambda r, w: pl.BlockSpec((1, r, w), lambda h, n: (h, 0, 0))
    return pl.pallas_call(
        _retention_kernel,
        grid=(hh, s // c),
        in_specs=[
            pl.BlockSpec((c, dk), lambda h, n: (n, h)),
            pl.BlockSpec((c, dk), lambda h, n: (n, k_blk0 + h)),
            pl.BlockSpec((c, dv), lambda h, n: (n, v_blk0 + h)),
            pl.BlockSpec((c, dv), lambda h, n: (n, g_blk0 + h)),
            pl.BlockSpec((c, dk // 2), lambda h, n: (n, 0)),
            pl.BlockSpec((c, dk // 2), lambda h, n: (n, 0)),
            per_head(RET_CHUNK, RET_CHUNK), per_head(RET_CHUNK, LANES), per_head(RET_CHUNK, LANES),
            per_head(8, LANES),
            pl.BlockSpec((1, dv), lambda h, n: (0, h)),
            pl.BlockSpec((1, dv), lambda h, n: (0, h)),
        ],
        out_specs=pl.BlockSpec((c, dv), lambda h, n: (n, h)),
        out_shape=jax.ShapeDtypeStruct((s, hh * dv), BF16),
        scratch_shapes=[pltpu.VMEM((dk, dv), F32)],
        compiler_params=_params("parallel", "arbitrary"),
        name="retention",
    )(h1, h1, h1, h1, cos, sin, din, xi, zeta, cd, gn_g.reshape(1, -1), gn_b.reshape(1, -1))


def _first_argmax(v, idx, n):
    m = jnp.max(v, axis=0, keepdims=True)
    first = jnp.min(jnp.where(v == m, idx, n), axis=0, keepdims=True)
    return m, first


def _route_tokens(x, rwt, rb):
    tm = x.shape[0]
    epg = EXPERTS_PER_GROUP
    logits = lax.dot_general(rwt, x, (((1,), (1,)), ((), ())), preferred_element_type=F32)
    scores = jax.nn.sigmoid(logits)
    biased = scores + rb
    idx = lax.broadcasted_iota(jnp.int32, (epg, tm), 0)

    group_scores = []
    for g in range(N_GROUPS):
        v = biased[g * epg:(g + 1) * epg]
        m1, first = _first_argmax(v, idx, epg)
        m2 = jnp.max(jnp.where(idx == first, -jnp.inf, v), axis=0, keepdims=True)
        group_scores.append(m1 + m2)
    gmax = group_scores[0]
    for g in range(1, N_GROUPS):
        gmax = jnp.maximum(gmax, group_scores[g])
    gsel = jnp.full((1, tm), N_GROUPS, jnp.int32)
    for g in range(N_GROUPS - 1, -1, -1):
        gsel = jnp.where(group_scores[g] == gmax, g, gsel)

    in_biased = jnp.zeros((epg, tm), F32)
    in_scores = jnp.zeros((epg, tm), F32)
    for g in range(N_GROUPS):
        pick = gsel == g
        in_biased = jnp.where(pick, biased[g * epg:(g + 1) * epg], in_biased)
        in_scores = jnp.where(pick, scores[g * epg:(g + 1) * epg], in_scores)
    _, loc1 = _first_argmax(in_biased, idx, epg)
    _, loc2 = _first_argmax(jnp.where(idx == loc1, -jnp.inf, in_biased), idx, epg)
    s1 = jnp.sum(jnp.where(idx == loc1, in_scores, 0.0), axis=0, keepdims=True)
    s2 = jnp.sum(jnp.where(idx == loc2, in_scores, 0.0), axis=0, keepdims=True)
    denom = s1 + s2
    experts = jnp.concatenate([gsel * epg + loc1, gsel * epg + loc2], axis=0)
    return experts, jnp.concatenate([s1 / denom, s2 / denom], axis=0)


def _scatter_rows_kernel(pad_ref, pos_ref, x_ref, xs_hbm, zero_sc, sem, zsem, *, tm, n_pad):
    i = pl.program_id(0)

    def zero_copy(p):
        return pltpu.make_async_copy(zero_sc, xs_hbm.at[pl.ds(pad_ref[p], 1)], zsem)

    @pl.when(i == 0)
    def _():
        zero_sc[...] = jnp.zeros_like(zero_sc)

        def start_zero(p, carry):
            zero_copy(p).start()
            return carry

        lax.fori_loop(0, n_pad, start_zero, 0)

    def start(g, carry):
        for u in range(DMA_ISSUE_UNROLL):
            r = g * DMA_ISSUE_UNROLL + u
            for k in range(TOP_K):
                pltpu.make_async_copy(x_ref.at[pl.ds(r, 1)], xs_hbm.at[pl.ds(pos_ref[0, k, r], 1)],
                                      sem).start(priority=k % 2)
        return carry

    lax.fori_loop(0, tm // DMA_ISSUE_UNROLL, start, 0)
    for k in range(TOP_K):
        pltpu.make_async_copy(x_ref, xs_hbm.at[pl.ds(0, tm)], sem).wait()

    @pl.when(i == 0)
    def _():
        def wait_zero(p, carry):
            zero_copy(p).wait()
            return carry

        lax.fori_loop(0, n_pad, wait_zero, 0)


def scatter_rows(x, pos, pad_pos, n_rows, tm=256):
    s, d = x.shape
    nt = s // tm
    n_pad = pad_pos.shape[0]
    return pl.pallas_call(
        functools.partial(_scatter_rows_kernel, tm=tm, n_pad=n_pad),
        grid_spec=pltpu.PrefetchScalarGridSpec(
            num_scalar_prefetch=1,
            grid=(nt,),
            in_specs=[
                pl.BlockSpec((1, TOP_K, tm), lambda i, pad: (i, 0, 0), memory_space=pltpu.SMEM),
                pl.BlockSpec((tm, d), lambda i, pad: (i, 0)),
            ],
            out_specs=pl.BlockSpec(memory_space=pl.ANY),
            scratch_shapes=[pltpu.VMEM((1, d), x.dtype), pltpu.SemaphoreType.DMA(()), pltpu.SemaphoreType.DMA(())],
        ),
        out_shape=jax.ShapeDtypeStruct((n_rows, d), x.dtype),
        compiler_params=_params("arbitrary"),
        name="scatter_rows",
    )(pad_pos, pos.reshape(TOP_K, nt, tm).transpose(1, 0, 2), x)


def _expert_kernel(te_ref, rs_ref, nv_ref, na_ref, dst_ref, xs_hbm, wg_ref, wu_ref, wd_ref, y_hbm,
                   wg_sc, wu_sc, wd_sc, xbuf, ybuf, lsem, ssem, *, tm):
    j = pl.program_id(0)
    n_active = na_ref[0]
    active = j < n_active
    slot = j % 2
    new_expert = (j == 0) | (te_ref[j] != te_ref[jnp.maximum(j - 1, 0)])
    chunk_sizes = [c for c in (256, 128, 64, 32, 16, 8) if c <= tm]

    def load_tile(t, dst_slot, wait):
        n8 = ((nv_ref[t] + 7) // 8) * 8
        base = rs_ref[t]
        off = jnp.int32(0)
        for c in chunk_sizes:
            take = (n8 & c) != 0

            @pl.when(take)
            def _(off=off, c=c):
                copy = pltpu.make_async_copy(
                    xs_hbm.at[pl.ds(pl.multiple_of(base + off, 8), c)],
                    xbuf.at[dst_slot, pl.ds(pl.multiple_of(off, 8), c)], lsem.at[dst_slot])
                if wait:
                    copy.wait()
                else:
                    copy.start()

            off = off + jnp.where(take, c, 0)

    def wait_scatter(src_slot, n):
        c = tm
        while c >= 1:
            @pl.when((n & c) != 0)
            def _(c=c):
                pltpu.make_async_copy(ybuf.at[src_slot, pl.ds(0, c)], y_hbm.at[pl.ds(0, c)],
                                      ssem.at[src_slot]).wait()
            c //= 2

    @pl.when(j == 0)
    def _():
        xbuf[...] = jnp.zeros_like(xbuf)
        load_tile(0, 0, False)

    @pl.when(j + 1 < n_active)
    def _():
        load_tile(j + 1, 1 - slot, False)

    @pl.when(active & new_expert)
    def _():
        wg_sc[...] = wg_ref[0].astype(BF16)
        wu_sc[...] = wu_ref[0].astype(BF16)
        wd_sc[...] = wd_ref[0].astype(BF16)

    @pl.when(active)
    def _():
        load_tile(j, slot, True)

        @pl.when(j >= 2)
        def _():
            wait_scatter(slot, nv_ref[jnp.maximum(j - 2, 0)])

        x = xbuf[slot].astype(BF16)
        hg = jnp.dot(x, wg_sc[...], preferred_element_type=F32)
        hu = jnp.dot(x, wu_sc[...], preferred_element_type=F32)
        h = hg * jax.nn.sigmoid(hg) * hu
        ybuf[slot] = jnp.dot(h.astype(BF16), wd_sc[...], preferred_element_type=F32)

        def scatter_row(r):
            pltpu.make_async_copy(ybuf.at[slot, pl.ds(r, 1)], y_hbm.at[pl.ds(dst_ref[0, 0, r], 1)],
                                  ssem.at[slot]).start()

        def scatter_group(i, carry):
            for u in range(DMA_ISSUE_UNROLL):
                scatter_row(i * DMA_ISSUE_UNROLL + u)
            return carry

        def scatter_one(r, carry):
            scatter_row(r)
            return carry

        n_groups = lax.shift_right_logical(nv_ref[j], DMA_ISSUE_UNROLL.bit_length() - 1)
        lax.fori_loop(0, n_groups, scatter_group, 0)
        lax.fori_loop(n_groups * DMA_ISSUE_UNROLL, nv_ref[j], scatter_one, 0)

        @pl.when(j == n_active - 1)
        def _():
            wait_scatter(slot, nv_ref[j])

            @pl.when(j >= 1)
            def _():
                wait_scatter(1 - slot, nv_ref[jnp.maximum(j - 1, 0)])


def routed_experts(xs, plan, n_out_rows, w_gate, w_up, w_down, tm):
    d = xs.shape[1]
    f = w_gate.shape[2]
    nt = plan.tile_expert.shape[0]

    def dst_blk(j, te, rs, nv, na):
        return (jnp.minimum(j, na[0] - 1), 0, 0)

    def weight_blk(j, te, rs, nv, na):
        return (te[j], 0, 0)

    return pl.pallas_call(
        functools.partial(_expert_kernel, tm=tm),
        grid_spec=pltpu.PrefetchScalarGridSpec(
            num_scalar_prefetch=4,
            grid=(nt,),
            in_specs=[
                pl.BlockSpec((1, 1, tm), dst_blk, memory_space=pltpu.SMEM),
                pl.BlockSpec(memory_space=pl.ANY),
                pl.BlockSpec((1, d, f), weight_blk),
                pl.BlockSpec((1, d, f), weight_blk),
                pl.BlockSpec((1, f, d), weight_blk),
            ],
            out_specs=pl.BlockSpec(memory_space=pl.ANY),
            scratch_shapes=[pltpu.VMEM((d, f), BF16), pltpu.VMEM((d, f), BF16), pltpu.VMEM((f, d), BF16),
                            pltpu.VMEM((2, tm, d), xs.dtype), pltpu.VMEM((2, tm, d), F32),
                            pltpu.SemaphoreType.DMA((2,)), pltpu.SemaphoreType.DMA((2,))],
        ),
        out_shape=jax.ShapeDtypeStruct((n_out_rows, d), F32),
        compiler_params=_params("arbitrary"),
        name="routed_experts",
    )(plan.tile_expert, plan.tile_row_start, plan.n_valid, plan.n_active, plan.dst_tiles.reshape(nt, 1, tm),
      xs, w_gate, w_up, w_down)


def _shared_kernel(x_ref, wg_ref, wu_ref, wd_ref, o_ref):
    x = x_ref[...]
    hg = jnp.dot(x, wg_ref[...], preferred_element_type=F32)
    hu = jnp.dot(x, wu_ref[...], preferred_element_type=F32)
    h = hg * jax.nn.sigmoid(hg) * hu
    o_ref[...] = jnp.dot(h.astype(BF16), wd_ref[...], preferred_element_type=F32)


def shared_expert(x_bf, wg, wu, wd, tm=512):
    s, d = x_bf.shape
    f = wg.shape[1]
    whole = lambda r, c: pl.BlockSpec((r, c), lambda i: (0, 0))
    return pl.pallas_call(
        _shared_kernel,
        grid=(s // tm,),
        in_specs=[pl.BlockSpec((tm, d), lambda i: (i, 0)), whole(d, f), whole(d, f), whole(f, d)],
        out_specs=pl.BlockSpec((tm, d), lambda i: (i, 0)),
        out_shape=jax.ShapeDtypeStruct((s, d), F32),
        compiler_params=_params("parallel"),
        name="shared_expert",
    )(x_bf, wg.astype(BF16), wu.astype(BF16), wd.astype(BF16))


def _combine_kernel(x_ref, sh_ref, gate_ref, y0_ref, y1_ref, g_ref, b_ref, o_ref, obf_ref):
    gate = gate_ref[...]
    routed = gate[:, 0:1] * y0_ref[0] + gate[:, 1:2] * y1_ref[0]
    z = DEEPNORM_ALPHA * x_ref[...] + (routed + sh_ref[...])
    out = _layer_norm(z, g_ref[...], b_ref[...])
    o_ref[...] = out
    obf_ref[...] = out.astype(BF16)


def moe_combine(x, shared, gates, y, g, b, tm=256):
    s, d = x.shape
    assert TOP_K == 2
    row = pl.BlockSpec((tm, d), lambda i: (i, 0))
    vec = pl.BlockSpec((1, d), lambda i: (0, 0))
    return pl.pallas_call(
        _combine_kernel,
        grid=(s // tm,),
        in_specs=[
            row, row,
            pl.BlockSpec((tm, TOP_K), lambda i: (i, 0)),
            pl.BlockSpec((1, tm, d), lambda i: (0, i, 0)),
            pl.BlockSpec((1, tm, d), lambda i: (1, i, 0)),
            vec, vec,
        ],
        out_specs=[row, row],
        out_shape=[jax.ShapeDtypeStruct((s, d), F32), jax.ShapeDtypeStruct((s, d), BF16)],
        compiler_params=_params("parallel"),
        name="moe_combine",
    )(x, shared, gates.T, y, y, g.reshape(1, d), b.reshape(1, d))


def _rope_tables(pos, rot_dim, theta):
    half = rot_dim // 2
    inv = (1.0 / (theta ** (np.arange(half, dtype=np.float32) / half))).astype(np.float32)
    ang = pos.astype(F32)[:, None] * inv
    return jnp.cos(ang), jnp.sin(ang)


def _rope(x, cos, sin, rot_dim):
    half = rot_dim // 2
    x1 = x[..., :half]
    x2 = x[..., half:rot_dim]
    c = cos[:, None, :]
    s = sin[:, None, :]
    return jnp.concatenate([x1 * c - x2 * s, x1 * s + x2 * c, x[..., rot_dim:]], axis=-1)


class _RoutingPlan(NamedTuple):
    pos: jax.Array
    pad_pos: jax.Array
    n_rows: int
    tile_expert: jax.Array
    tile_row_start: jax.Array
    n_valid: jax.Array
    n_active: jax.Array
    dst_tiles: jax.Array


ROW_ALIGN = 8


def _routing_plan(experts, tm):
    k, s = experts.shape
    flat = experts.reshape(-1)
    expert_ids = jnp.arange(N_EXPERTS, dtype=jnp.int32)
    onehot = (flat[:, None] == expert_ids[None, :]).astype(jnp.int32)
    csum = jnp.cumsum(onehot, axis=0)
    rank = jnp.take_along_axis(csum, flat[:, None], axis=1)[:, 0] - 1
    sizes = csum[-1]
    padded = (sizes + ROW_ALIGN - 1) // ROW_ALIGN * ROW_ALIGN
    seg_end = jnp.cumsum(padded)
    seg_start = seg_end - padded
    pos = seg_start[flat] + rank
    n_rows = k * s + N_EXPERTS * (ROW_ALIGN - 1)

    i_pad = jnp.arange(ROW_ALIGN - 1, dtype=jnp.int32)
    is_pad = i_pad[None, :] < (padded - sizes)[:, None]
    spare = jnp.cumsum(jnp.logical_not(is_pad).reshape(-1).astype(jnp.int32)).reshape(is_pad.shape) - 1
    pad_pos = jnp.where(is_pad, (seg_start + sizes)[:, None] + i_pad[None, :], seg_end[-1] + spare).reshape(-1)

    tiles = (sizes + tm - 1) // tm
    tile_end = jnp.cumsum(tiles)
    n_active = tile_end[-1]
    n_tiles = k * s // tm + N_EXPERTS
    tile_ids = jnp.minimum(jnp.arange(n_tiles, dtype=jnp.int32), n_active - 1)
    tile_expert = jnp.searchsorted(tile_end, tile_ids, side="right").astype(jnp.int32)
    local = tile_ids - (tile_end - tiles)[tile_expert]
    tile_row_start = seg_start[tile_expert] + local * tm
    n_valid = jnp.clip(sizes[tile_expert] - local * tm, 0, tm)
    slot_of_row = jnp.zeros((n_rows,), jnp.int32).at[pos].set(jnp.arange(k * s, dtype=jnp.int32))
    rows = jnp.minimum(tile_row_start[:, None] + jnp.arange(tm, dtype=jnp.int32)[None, :], n_rows - 1)
    return _RoutingPlan(pos.reshape(k, s), pad_pos.astype(jnp.int32), n_rows, tile_expert,
                        tile_row_start.astype(jnp.int32), n_valid.astype(jnp.int32),
                        n_active.reshape(1).astype(jnp.int32), slot_of_row[rows])


def _grouped_moe(x, x_bf, experts, gates, w_gate, w_up, w_down, sh_gate, sh_up, sh_down, ln_g, ln_b,
                 expert_tm=256):
    s, d = x.shape
    plan = _routing_plan(experts, expert_tm)
    xs = scatter_rows(x, plan.pos, plan.pad_pos, plan.n_rows)
    y = routed_experts(xs, plan, TOP_K * s, w_gate, w_up, w_down, expert_tm)
    shared = shared_expert(x_bf, sh_gate, sh_up, sh_down)
    return moe_combine(x, shared, gates, y.reshape(TOP_K, s, d), ln_g, ln_b)


def _rope_lane_tables(pos, half, theta, period, offset=0, limit=None):
    cos, sin = _rope_tables(pos, 2 * half, theta)
    lane = np.arange(LANES)
    p = lane % period - offset
    in_range = np.ones(LANES, bool) if limit is None else lane < limit
    first = (p >= 0) & (p < half) & in_range
    second = (p >= half) & (p < 2 * half) & in_range
    idx = np.where(first | second, p % half, 0)
    cos_l, sin_l = cos[:, idx], sin[:, idx]
    return jnp.stack([jnp.where(first | second, cos_l, 1.0),
                      jnp.where(second, sin_l, 0.0),
                      jnp.where(first, -sin_l, 0.0)])


def _mla_dsa_mixer(x, pos, w_in, q_norm, w_uq, kv_norm, w_ukv):
    cq, ckv, krope, dq, dkv, iq, ik, iw = jnp.split(w_in, [int(c) for c in np.cumsum(SPLITS_AB)[:-1]], axis=1)
    w_in_l = jnp.concatenate([cq, ckv, dq, dkv, iq, krope, ik, iw,
                              jnp.zeros((w_in.shape[0], LANES - IDX_HEADS), w_in.dtype)], axis=1)
    w_uq3 = w_uq.reshape(MLA_Q_LORA, MLA_HEADS, MLA_NOPE + MLA_ROPE)
    w_uq_l = jnp.concatenate([w_uq3[:, :, :MLA_NOPE].reshape(MLA_Q_LORA, -1),
                              jnp.pad(w_uq3[:, :, MLA_NOPE:], ((0, 0), (0, 0), (0, LANES - MLA_ROPE))
                                      ).reshape(MLA_Q_LORA, -1)], axis=1)
    w_ukv3 = w_ukv.reshape(MLA_KV_LORA, MLA_HEADS, MLA_NOPE + MLA_V)
    w_ukv_l = jnp.concatenate([w_ukv3[:, :, :MLA_NOPE].reshape(MLA_KV_LORA, -1),
                               w_ukv3[:, :, MLA_NOPE:].reshape(MLA_KV_LORA, -1)], axis=1)

    tab_m = _rope_lane_tables(pos, MLA_ROPE // 2, ROPE_THETA, LANES)
    tab_d = _rope_lane_tables(pos, DSA_ROT // 2, ROPE_THETA, LANES)
    tab_i = _rope_lane_tables(pos, IDX_ROT // 2, ROPE_THETA, IDX_DIM)
    tab_k = _rope_lane_tables(pos, MLA_ROPE // 2, ROPE_THETA, LANES, limit=MLA_ROPE)
    tab_ik = _rope_lane_tables(pos, IDX_ROT // 2, ROPE_THETA, LANES, offset=MLA_ROPE)
    tab_ki = jnp.concatenate([(tab_k[0] * tab_ik[0])[None], tab_k[1:], tab_ik[1:]])

    h0 = matmul(x, w_in_l, F32, 1024, 512, name="l0_in_proj")
    q_cat, k_nope, v_a, k_pe, qb, kb, vb, iq_r, ik_lo, ik_hi, iw_s = l0_attention_operands(
        h0, q_norm, w_uq_l, kv_norm, w_ukv_l, (tab_m, tab_d, tab_i, tab_ki))
    out_a = mla_attention(q_cat, k_nope, k_pe, v_a)
    out_b = dsa_attention(iq_r, iw_s, ik_lo, ik_hi, qb, kb, vb)
    return out_a, out_b


def _retention_mixer(x_bf, pos, w_in, gn_g, gn_b):
    h1 = matmul(x_bf, w_in, BF16, 1024, 1024, name="l1_in_proj")
    cos, sin = _rope_tables(pos, RET_QK_DIM, RET_THETA)
    return retention(h1, cos, sin, gn_g, gn_b)


def kernel(x, positions, router_w, router_b, l0_w_in, l0_mla_q_norm, l0_mla_w_uq, l0_mla_kv_norm, l0_mla_w_ukv, l0_w_out, l1_w_in, l1_ret_gn_g, l1_ret_gn_b, l1_w_out, l0_ln_mix_g, l0_ln_mix_b, l0_moe_w_gate, l0_moe_w_up, l0_moe_w_down, l0_sh_gate, l0_sh_up, l0_sh_down, l0_ln_ffn_g, l0_ln_ffn_b, l1_ln_mix_g, l1_ln_mix_b, l1_moe_w_gate, l1_moe_w_up, l1_moe_w_down, l1_sh_gate, l1_sh_up, l1_sh_down, l1_ln_ffn_g, l1_ln_ffn_b):
    assert x.shape[0] == 1
    xt = x[0]
    pos = positions[0]

    out_a, out_b = _mla_dsa_mixer(xt, pos, l0_w_in, l0_mla_q_norm, l0_mla_w_uq, l0_mla_kv_norm, l0_mla_w_ukv)
    xt, xt_bf, experts, gates = out_proj_ln_route((out_a, out_b), l0_w_out, xt, l0_ln_mix_g, l0_ln_mix_b,
                                                  router_w, router_b, name="l0_out_proj_ln_route")
    xt, xt_bf = _grouped_moe(xt, xt_bf, experts, gates, l0_moe_w_gate, l0_moe_w_up, l0_moe_w_down,
                             l0_sh_gate, l0_sh_up, l0_sh_down, l0_ln_ffn_g, l0_ln_ffn_b)

    ret = _retention_mixer(xt_bf, pos, l1_w_in, l1_ret_gn_g, l1_ret_gn_b)
    xt, xt_bf, experts, gates = out_proj_ln_route((ret,), l1_w_out, xt, l1_ln_mix_g, l1_ln_mix_b,
                                                  router_w, router_b, name="l1_out_proj_ln_route")
    xt, _ = _grouped_moe(xt, xt_bf, experts, gates, l1_moe_w_gate, l1_moe_w_up, l1_moe_w_down,
                         l1_sh_gate, l1_sh_up, l1_sh_down, l1_ln_ffn_g, l1_ln_ffn_b)
    return xt[None]
```

```python
import functools
from typing import NamedTuple

import numpy as np
import jax
import jax.numpy as jnp
from jax import lax
from jax.experimental import pallas as pl
from jax.experimental.pallas import tpu as pltpu

F32 = jnp.float32
BF16 = jnp.bfloat16

D_MODEL = 2048
DEPTH = 2
ROPE_THETA = 500000.0
MLA_HEADS = 8
MLA_Q_LORA = 512
MLA_KV_LORA = 256
MLA_NOPE = 128
MLA_ROPE = 64
MLA_V = 128
DSA_HEADS = 8
DSA_KV_HEADS = 2
DSA_HEAD_DIM = 128
DSA_ROT = DSA_HEAD_DIM // 4
IDX_HEADS = 16
IDX_DIM = 64
IDX_ROT = IDX_DIM // 4
IDX_TOPK_MAX = 256
RET_HEADS = 8
RET_QK_DIM = 256
RET_V_DIM = 512
RET_CHUNK = 128
RET_THETA = 10000.0
N_EXPERTS = 64
N_GROUPS = 8
EXPERTS_PER_GROUP = N_EXPERTS // N_GROUPS
TOP_K = 2
D_EXPERT = 512
D_SHARED = 1024
DEEPNORM_ALPHA = (2.0 * DEPTH) ** 0.25

SPLITS_AB = (MLA_Q_LORA, MLA_KV_LORA, MLA_ROPE, DSA_HEADS * DSA_HEAD_DIM,
             2 * DSA_KV_HEADS * DSA_HEAD_DIM, IDX_HEADS * IDX_DIM, IDX_DIM, IDX_HEADS)
IN_AB = sum(SPLITS_AB)

VMEM_LIMIT_BYTES = 56 * 1024 * 1024
LANES = 128
MASKED_SCORE = -1e30
LOG2_E = 1.4426950408889634
DMA_ISSUE_UNROLL = 8
INT32_MIN = -2 ** 31


def _params(*sem):
    return pltpu.CompilerParams(dimension_semantics=sem, vmem_limit_bytes=VMEM_LIMIT_BYTES)


def _mm_kernel(*refs, norm_eps):
    if norm_eps is None:
        a_ref, b_ref, o_ref = refs
        a = a_ref[...]
    else:
        a_ref, g_ref, b_ref, o_ref = refs
        af = a_ref[...].astype(F32)
        a = af * lax.rsqrt(jnp.mean(af * af, axis=-1, keepdims=True) + norm_eps) * g_ref[...]
    o_ref[...] = jnp.dot(a.astype(BF16), b_ref[...].astype(BF16),
                         preferred_element_type=F32).astype(o_ref.dtype)


def matmul(a, b, out_dtype, tm, tn, *, name, a_cols=None, norm_gain=None, norm_eps=1e-6):
    m = a.shape[0]
    k, n = b.shape
    col_blk = 0 if a_cols is None else a_cols[0]
    assert (a.shape[1] == k) if a_cols is None else (a_cols[1] == k)
    assert m % tm == 0 and n % tn == 0
    in_specs = [pl.BlockSpec((tm, k), lambda i, j: (i, col_blk))]
    args = [a]
    if norm_gain is not None:
        in_specs.append(pl.BlockSpec((1, k), lambda i, j: (0, 0)))
        args.append(norm_gain.reshape(1, k))
    in_specs.append(pl.BlockSpec((k, tn), lambda i, j: (0, j)))
    args.append(b)
    return pl.pallas_call(
        functools.partial(_mm_kernel, norm_eps=None if norm_gain is None else norm_eps),
        grid=(m // tm, n // tn),
        in_specs=in_specs,
        out_specs=pl.BlockSpec((tm, tn), lambda i, j: (i, j)),
        out_shape=jax.ShapeDtypeStruct((m, n), out_dtype),
        compiler_params=_params("parallel", "parallel"),
        name=name,
    )(*args)


_H0_CQ = 0
_H0_CKV = _H0_CQ + MLA_Q_LORA
_H0_DQ = _H0_CKV + MLA_KV_LORA
_H0_DKV = _H0_DQ + DSA_HEADS * DSA_HEAD_DIM
_H0_IQ = _H0_DKV + 2 * DSA_KV_HEADS * DSA_HEAD_DIM
_H0_KROPE_IK = _H0_IQ + IDX_HEADS * IDX_DIM
_H0_IW = _H0_KROPE_IK + LANES
_H0_WIDTH = _H0_IW + LANES


def _rope_lanes(x, tab_ref, shifts):
    out = x * tab_ref[0]
    for i, shift in enumerate(shifts):
        out = out + pltpu.roll(x, shift, 1) * tab_ref[1 + i]
    return out


def _l0_prep_kernel(h_ref, qn_ref, kvn_ref, wuq_ref, wukv_ref, tm_ref, td_ref, ti_ref, tki_ref,
                    qcat_ref, knope_ref, va_ref, kpe_ref, qb_ref, kb_ref, vb_ref, iq_ref, iklo_ref, ikhi_ref,
                    iw_ref):
    def rms(x, g_ref):
        return (x * lax.rsqrt(jnp.mean(x * x, axis=-1, keepdims=True) + 1e-6) * g_ref[...]).astype(BF16)

    def slab(col, j=0):
        return h_ref[:, col + j * LANES:col + (j + 1) * LANES]

    half_m, half_d, half_i = MLA_ROPE // 2, DSA_ROT // 2, IDX_ROT // 2
    shifts_m = (half_m, LANES - half_m)
    shifts_d = (half_d, LANES - half_d)
    shifts_i = (half_i, LANES - half_i)

    qa = jnp.dot(rms(h_ref[:, _H0_CQ:_H0_CQ + MLA_Q_LORA], qn_ref), wuq_ref[...], preferred_element_type=F32)
    kva = jnp.dot(rms(h_ref[:, _H0_CKV:_H0_CKV + MLA_KV_LORA], kvn_ref), wukv_ref[...],
                  preferred_element_type=F32)
    n_nope = MLA_HEADS * MLA_NOPE
    knope_ref[...] = kva[:, :n_nope].astype(BF16)
    va_ref[...] = kva[:, n_nope:].astype(BF16)
    q_scale = (MLA_NOPE + MLA_ROPE) ** -0.5 * LOG2_E
    for h in range(MLA_HEADS):
        nope = qa[:, h * LANES:(h + 1) * LANES]
        pe = qa[:, n_nope + h * LANES:n_nope + (h + 1) * LANES]
        qcat_ref[:, 2 * h * LANES:(2 * h + 1) * LANES] = (nope * q_scale).astype(BF16)
        qcat_ref[:, (2 * h + 1) * LANES:(2 * h + 2) * LANES] = (_rope_lanes(pe, tm_ref, shifts_m) * q_scale).astype(BF16)

    d_scale = DSA_HEAD_DIM ** -0.5 * LOG2_E
    for h in range(DSA_HEADS):
        qb_ref[:, h * LANES:(h + 1) * LANES] = (_rope_lanes(slab(_H0_DQ, h), td_ref, shifts_d) * d_scale).astype(BF16)
    for g in range(DSA_KV_HEADS):
        kb_ref[:, g * LANES:(g + 1) * LANES] = _rope_lanes(slab(_H0_DKV, g), td_ref, shifts_d).astype(BF16)
    vb_ref[...] = h_ref[:, _H0_DKV + DSA_KV_HEADS * LANES:_H0_DKV + 2 * DSA_KV_HEADS * LANES].astype(BF16)
    for j in range(IDX_HEADS * IDX_DIM // LANES):
        iq_ref[:, j * LANES:(j + 1) * LANES] = _rope_lanes(slab(_H0_IQ, j), ti_ref, shifts_i).astype(BF16)

    ki = _rope_lanes(slab(_H0_KROPE_IK), tki_ref, shifts_m + shifts_i)
    lane = lax.broadcasted_iota(jnp.int32, ki.shape, 1)
    kpe_ref[...] = jnp.where(lane < MLA_ROPE, ki, 0.0).astype(BF16)
    ik_hi = jnp.where(lane >= MLA_ROPE, ki, 0.0)
    ikhi_ref[...] = ik_hi.astype(BF16)
    iklo_ref[...] = pltpu.roll(ik_hi, LANES - MLA_ROPE, 1).astype(BF16)
    iw_ref[...] = h_ref[:, _H0_IW:_H0_IW + IDX_HEADS] * (IDX_HEADS ** -0.5 * IDX_DIM ** -0.5)


def l0_attention_operands(h0, q_norm, w_uq, kv_norm, w_ukv, tables, tm=256):
    s = h0.shape[0]
    bf = lambda n: jax.ShapeDtypeStruct((s, n), BF16)
    row = lambda n: pl.BlockSpec((tm, n), lambda i: (i, 0))
    whole = lambda a: pl.BlockSpec(a.shape, lambda i: (0,) * a.ndim)
    tab = lambda t: pl.BlockSpec((t.shape[0], tm, LANES), lambda i: (0, i, 0))
    widths = [2 * MLA_HEADS * LANES, MLA_HEADS * MLA_NOPE, MLA_HEADS * MLA_V, LANES, DSA_HEADS * DSA_HEAD_DIM,
              DSA_KV_HEADS * DSA_HEAD_DIM, DSA_KV_HEADS * DSA_HEAD_DIM, IDX_HEADS * IDX_DIM, LANES, LANES]
    w_uq = w_uq.astype(BF16)
    w_ukv = w_ukv.astype(BF16)
    q_norm = q_norm.reshape(1, -1)
    kv_norm = kv_norm.reshape(1, -1)
    return pl.pallas_call(
        _l0_prep_kernel,
        grid=(s // tm,),
        in_specs=[row(_H0_WIDTH), whole(q_norm), whole(kv_norm), whole(w_uq), whole(w_ukv)] + [tab(t) for t in tables],
        out_specs=[row(n) for n in widths] + [row(IDX_HEADS)],
        out_shape=[bf(n) for n in widths] + [jax.ShapeDtypeStruct((s, IDX_HEADS), F32)],
        compiler_params=_params("parallel"),
        name="l0_attention_operands",
    )(h0, q_norm, kv_norm, w_uq, w_ukv, *tables)


def _layer_norm(z, g, b):
    mu = jnp.mean(z, axis=-1, keepdims=True)
    zc = z - mu
    var = jnp.mean(zc * zc, axis=-1, keepdims=True)
    return zc * lax.rsqrt(var + 1e-5) * g + b


def _proj_ln_route_kernel(*refs, n_parts):
    a_refs = refs[:n_parts]
    (w_ref, x_ref, g_ref, b_ref, rwt_ref, rb_ref,
     o_ref, obf_ref, e_ref, gate_ref, rank_ref, count_ref, run_sc) = refs[n_parts:]
    mix = None
    k0 = 0
    for a_ref in a_refs:
        k = a_ref.shape[1]
        part = jnp.dot(a_ref[...], w_ref[k0:k0 + k, :], preferred_element_type=F32)
        mix = part if mix is None else mix + part
        k0 += k
    out = _layer_norm(DEEPNORM_ALPHA * x_ref[...] + mix, g_ref[...], b_ref[...])
    out_bf = out.astype(BF16)
    o_ref[...] = out
    obf_ref[...] = out_bf
    experts, gate_ref[...] = _route_tokens(out_bf, rwt_ref[...], rb_ref[...])
    e_ref[...] = experts

    @pl.when(pl.program_id(0) == 0)
    def _():
        run_sc[...] = jnp.zeros_like(run_sc)

    tm = out.shape[0]
    expert_id = lax.broadcasted_iota(jnp.int32, (N_EXPERTS, tm), 0)
    before = (lax.broadcasted_iota(jnp.int32, (tm, tm), 0) < lax.broadcasted_iota(jnp.int32, (tm, tm), 1))
    before = jnp.where(before, 1.0, 0.0).astype(BF16)
    base = run_sc[...][:, :1]
    ranks = []
    for k in range(TOP_K):
        onehot = expert_id == experts[k:k + 1, :]
        prefix = jnp.dot(jnp.where(onehot, 1.0, 0.0).astype(BF16), before, preferred_element_type=F32)
        ranks.append(jnp.sum(jnp.where(onehot, prefix + base, 0.0), axis=0, keepdims=True))
        base = base + jnp.sum(jnp.where(onehot, 1.0, 0.0), axis=1, keepdims=True)
    rank_ref[...] = jnp.concatenate(ranks, axis=0).astype(jnp.int32)
    run_sc[...] = jnp.broadcast_to(base, run_sc.shape)
    count_ref[...] = run_sc[...]


def out_proj_ln_route(a_parts, w, x, g, b, router_w, router_b, *, name, tm=512):
    s, d = x.shape
    w = w.astype(BF16)
    assert sum(a.shape[1] for a in a_parts) == w.shape[0] and w.shape[1] == d
    row = pl.BlockSpec((tm, d), lambda i: (i, 0))
    vec = pl.BlockSpec((1, d), lambda i: (0, 0))
    slots = pl.BlockSpec((TOP_K, tm), lambda i: (0, i))
    resident = lambda shape: pl.BlockSpec(shape, lambda i: (0, 0), pipeline_mode=pl.Buffered(1))
    return pl.pallas_call(
        functools.partial(_proj_ln_route_kernel, n_parts=len(a_parts)),
        grid=(s // tm,),
        in_specs=[pl.BlockSpec((tm, a.shape[1]), lambda i: (i, 0)) for a in a_parts]
        + [resident(w.shape), row, vec, vec, resident((N_EXPERTS, d)), resident((N_EXPERTS, 1))],
        out_specs=[row, row, slots, slots, slots, pl.BlockSpec((N_EXPERTS, LANES), lambda i: (0, 0))],
        out_shape=[jax.ShapeDtypeStruct((s, d), F32), jax.ShapeDtypeStruct((s, d), BF16),
                   jax.ShapeDtypeStruct((TOP_K, s), jnp.int32), jax.ShapeDtypeStruct((TOP_K, s), F32),
                   jax.ShapeDtypeStruct((TOP_K, s), jnp.int32), jax.ShapeDtypeStruct((N_EXPERTS, LANES), F32)],
        scratch_shapes=[pltpu.VMEM((N_EXPERTS, LANES), F32)],
        compiler_params=_params("arbitrary"),
        name=name,
    )(*a_parts, w, x, g.reshape(1, d), b.reshape(1, d), router_w.T.astype(BF16), router_b.reshape(N_EXPERTS, 1))


def _lane_tile(x, n):
    return jnp.tile(x, (1, n))


def _softmax_step(s, v_ext, m_ref, acc_ref):
    tk = s.shape[1]
    m_prev = m_ref[...]
    m_new = jnp.maximum(m_prev, jnp.max(s, axis=1)[:, None])
    p = jnp.exp2(s - _lane_tile(m_new, tk // LANES))
    alpha = jnp.exp2(m_prev - m_new)
    pv = jnp.dot(p.astype(BF16), v_ext, preferred_element_type=F32)
    acc_ref[...] = _lane_tile(alpha, 2) * acc_ref[...] + pv
    m_ref[...] = m_new


def _mla_kernel(q_ref, kn_ref, kp_ref, v_ref, o_ref, m_sc, acc_sc, *, tq, tk, hp):
    qi = pl.program_id(1)
    m_sc[...] = jnp.full_like(m_sc, MASKED_SCORE)
    acc_sc[...] = jnp.zeros_like(acc_sc)
    ones = jnp.ones((tk, LANES), BF16)
    per_block = tq // tk

    def attend(c, diagonal_offset):
        kp = kp_ref[c]
        kn = kn_ref[c]
        v = v_ref[c]
        for j in range(hp):
            q = q_ref[:, j * 2 * LANES:(j + 1) * 2 * LANES]
            k = jnp.concatenate([kn[:, j * LANES:(j + 1) * LANES], kp], axis=1)
            s = lax.dot_general(q, k, (((1,), (1,)), ((), ())), preferred_element_type=F32)
            if diagonal_offset is not None:
                row = lax.broadcasted_iota(jnp.int32, (tq, tk), 0)
                col = diagonal_offset * tk + lax.broadcasted_iota(jnp.int32, (tq, tk), 1)
                s = jnp.where(col <= row, s, MASKED_SCORE)
            v_ext = jnp.concatenate([v[:, j * LANES:(j + 1) * LANES], ones], axis=1)
            _softmax_step(s, v_ext, m_sc.at[j], acc_sc.at[j])

    def below_diagonal(c, carry):
        attend(c, None)
        return carry

    lax.fori_loop(0, qi * per_block, below_diagonal, 0)
    for d in range(per_block):
        attend(qi * per_block + d, d)
    for j in range(hp):
        acc = acc_sc[j]
        o_ref[:, j * LANES:(j + 1) * LANES] = (acc[:, :LANES] / acc[:, LANES:]).astype(o_ref.dtype)


def mla_attention(q, k_nope, k_pe, v, tq=1024, tk=512, hp=2):
    s = q.shape[0]
    h = MLA_HEADS
    nc = s // tk
    assert tq % tk == 0 and h % hp == 0
    per_head = pl.BlockSpec((nc, tk, hp * LANES), lambda hh, qi: (0, 0, hh))
    return pl.pallas_call(
        functools.partial(_mla_kernel, tq=tq, tk=tk, hp=hp),
        grid=(h // hp, s // tq),
        in_specs=[
            pl.BlockSpec((tq, hp * 2 * LANES), lambda hh, qi: (qi, hh)),
            per_head,
            pl.BlockSpec((nc, tk, LANES), lambda hh, qi: (0, 0, 0)),
            per_head,
        ],
        out_specs=pl.BlockSpec((tq, hp * LANES), lambda hh, qi: (qi, hh)),
        out_shape=jax.ShapeDtypeStruct((s, h * MLA_V), BF16),
        scratch_shapes=[pltpu.VMEM((hp, tq, LANES), F32), pltpu.VMEM((hp, tq, 2 * LANES), F32)],
        compiler_params=_params("parallel", "arbitrary"),
        name="mla_attention",
    )(q, k_nope.reshape(nc, tk, h * MLA_NOPE), k_pe.reshape(nc, tk, LANES), v.reshape(nc, tk, h * MLA_V))


def _sortable_key(x):
    bits = pltpu.bitcast(x, jnp.int32)
    return bits ^ ((bits >> 31) & 0x7FFFFFFF)


def _dsa_kernel(iq_ref, iw_ref, iklo_ref, ikhi_ref, q_ref, k_ref, v_ref, o_ref, key_sc, wb_sc, m_sc, acc_sc,
                *, tq, tk, n_sel):
    i = pl.program_id(0)
    n_chunks = ((i + 1) * tq + tk - 1) // tk
    rep = DSA_HEADS // DSA_KV_HEADS
    lane_tiles = tk // LANES
    heads_per_block = LANES // IDX_DIM

    assert heads_per_block == 2
    iq_blocks = jnp.concatenate(
        [iq_ref[:, j * LANES:(j + 1) * LANES] for j in range(IDX_HEADS // heads_per_block)], axis=0)
    iw = iw_ref[...]
    for h in range(IDX_HEADS):
        wb_sc[h] = jnp.broadcast_to(iw[:, h:h + 1], (tq, LANES))

    def index_chunk(c, carry):
        ik = jnp.concatenate([iklo_ref[c], ikhi_ref[c]], axis=0)
        d = lax.dot_general(iq_blocks, ik, (((1,), (1,)), ((), ())), preferred_element_type=F32)
        d = jnp.maximum(d, 0.0)
        score = None
        for h in range(IDX_HEADS):
            j, part = divmod(h, heads_per_block)
            term = _lane_tile(wb_sc[h], lane_tiles) * d[j * tq:(j + 1) * tq, part * tk:(part + 1) * tk]
            score = term if score is None else score + term
        key_sc[c] = _sortable_key(score)
        return carry

    lax.fori_loop(0, n_chunks, index_chunk, 0)
    last = n_chunks - 1
    row = i * tq + lax.broadcasted_iota(jnp.int32, (tq, tk), 0)
    col = last * tk + lax.broadcasted_iota(jnp.int32, (tq, tk), 1)
    key_sc[last] = jnp.where(col <= row, key_sc[last], INT32_MIN)

    def search_bit(b, thr):
        cand = thr + lax.shift_left(jnp.int32(1), 31 - b)

        def count_chunk(c, cnt):
            key = key_sc[c]
            for j in range(lane_tiles):
                cnt = cnt + jnp.where(key[:, j * LANES:(j + 1) * LANES] >= cand, 1.0, 0.0)
            return cnt

        cnt = lax.fori_loop(0, n_chunks, count_chunk, jnp.zeros((tq, LANES), F32))
        return jnp.where(jnp.sum(cnt, axis=1)[:, None] >= n_sel, cand, thr)

    thr = lax.fori_loop(0, 32, search_bit, jnp.full((tq, LANES), INT32_MIN, jnp.int32))
    thr = _lane_tile(jnp.maximum(thr, INT32_MIN + 1), lane_tiles)

    m_sc[...] = jnp.full_like(m_sc, MASKED_SCORE)
    acc_sc[...] = jnp.zeros_like(acc_sc)
    ones = jnp.ones((tk, LANES), BF16)

    def attend_chunk(c, carry):
        bias = jnp.where(key_sc[c] >= thr, 0.0, MASKED_SCORE)
        kc = k_ref[c]
        vc = v_ref[c]
        for g in range(DSA_KV_HEADS):
            cols = slice(g * DSA_HEAD_DIM, (g + 1) * DSA_HEAD_DIM)
            qg = jnp.concatenate(
                [q_ref[:, (g * rep + r) * DSA_HEAD_DIM:(g * rep + r + 1) * DSA_HEAD_DIM] for r in range(rep)], axis=0)
            s = lax.dot_general(qg, kc[:, cols], (((1,), (1,)), ((), ())), preferred_element_type=F32)
            s = (s.reshape(rep, tq, tk) + bias[None]).reshape(rep * tq, tk)
            _softmax_step(s, jnp.concatenate([vc[:, cols], ones], axis=1), m_sc.at[g], acc_sc.at[g])
        return carry

    lax.fori_loop(0, n_chunks, attend_chunk, 0)
    for g in range(DSA_KV_HEADS):
        acc = acc_sc[g]
        out = acc[:, :LANES] / acc[:, LANES:]
        for r in range(rep):
            hh = g * rep + r
            o_ref[:, hh * DSA_HEAD_DIM:(hh + 1) * DSA_HEAD_DIM] = out[r * tq:(r + 1) * tq].astype(o_ref.dtype)


def dsa_attention(iq, iw, ik_lo, ik_hi, q, k, v, tq=128, tk=512):
    s = q.shape[0]
    n_sel = min(IDX_TOPK_MAX, s // 4)
    nc = s // tk
    assert tk >= n_sel and s % tk == 0 and tk % tq == 0 and DSA_HEAD_DIM == LANES
    rep = DSA_HEADS // DSA_KV_HEADS
    kvw = DSA_KV_HEADS * DSA_HEAD_DIM
    whole3 = lambda i: (0, 0, 0)
    return pl.pallas_call(
        functools.partial(_dsa_kernel, tq=tq, tk=tk, n_sel=n_sel),
        grid=(s // tq,),
        in_specs=[
            pl.BlockSpec((tq, IDX_HEADS * IDX_DIM), lambda i: (i, 0)),
            pl.BlockSpec((tq, IDX_HEADS), lambda i: (i, 0)),
            pl.BlockSpec((nc, tk, LANES), whole3),
            pl.BlockSpec((nc, tk, LANES), whole3),
            pl.BlockSpec((tq, DSA_HEADS * DSA_HEAD_DIM), lambda i: (i, 0)),
            pl.BlockSpec((nc, tk, kvw), whole3),
            pl.BlockSpec((nc, tk, kvw), whole3),
        ],
        out_specs=pl.BlockSpec((tq, DSA_HEADS * DSA_HEAD_DIM), lambda i: (i, 0)),
        out_shape=jax.ShapeDtypeStruct((s, DSA_HEADS * DSA_HEAD_DIM), BF16),
        scratch_shapes=[
            pltpu.VMEM((nc, tq, tk), jnp.int32),
            pltpu.VMEM((IDX_HEADS, tq, LANES), F32),
            pltpu.VMEM((DSA_KV_HEADS, rep * tq, LANES), F32),
            pltpu.VMEM((DSA_KV_HEADS, rep * tq, 2 * LANES), F32),
        ],
        compiler_params=_params("parallel"),
        name="dsa_attention",
    )(iq, iw, ik_lo.reshape(nc, tk, LANES), ik_hi.reshape(nc, tk, LANES), q, k.reshape(nc, tk, kvw),
      v.reshape(nc, tk, kvw))


def _retention_tables():
    h, c = RET_HEADS, RET_CHUNK
    log_g = np.log(1.0 - 2.0 ** (-5.0 - np.arange(h, dtype=np.float32))).astype(np.float32).astype(np.float64)
    idx = np.arange(c, dtype=np.float64)
    diff = idx[:, None] - idx[None, :]
    decay_in = np.where(diff[None] >= 0, np.exp(np.maximum(diff, 0.0)[None] * log_g[:, None, None]), 0.0)
    xi = np.exp((idx + 1.0)[None, :] * log_g[:, None])
    zeta = np.exp((c - 1.0 - idx)[None, :] * log_g[:, None])
    chunk_decay = np.exp(c * log_g)
    lane = np.ones((1, 1, LANES))
    return (decay_in.astype(np.float32), (xi[:, :, None] * lane).astype(np.float32),
            (zeta[:, :, None] * lane).astype(np.float32),
            (chunk_decay[:, None, None] * np.ones((1, 8, LANES))).astype(np.float32))


def _retention_kernel(q_ref, k_ref, v_ref, g_ref, cos_ref, sin_ref, din_ref, xi_ref, zeta_ref, cd_ref,
                      gng_ref, gnb_ref, o_ref, r_sc):
    n = pl.program_id(1)
    half = RET_QK_DIM // 2

    @pl.when(n == 0)
    def _():
        r_sc[...] = jnp.zeros_like(r_sc)

    scale = RET_QK_DIM ** -0.5
    xi = jnp.concatenate([xi_ref[0]] * (RET_V_DIM // LANES), axis=1)
    zeta = jnp.concatenate([zeta_ref[0]] * (RET_QK_DIM // LANES), axis=1)

    for ci in range(q_ref.shape[0] // RET_CHUNK):
        rows = slice(ci * RET_CHUNK, (ci + 1) * RET_CHUNK)
        cos = cos_ref[rows]
        sin = sin_ref[rows]

        def rope(x):
            x1 = x[:, :half]
            x2 = x[:, half:]
            return x1 * cos - x2 * sin, x1 * sin + x2 * cos

        q1, q2 = rope(q_ref[rows].astype(F32))
        k1, k2 = rope(k_ref[rows].astype(F32))
        qr = jnp.concatenate([q1, q2], axis=1).astype(BF16)
        kr = jnp.concatenate([k1 * scale, k2 * scale], axis=1)
        v = v_ref[rows].astype(BF16)

        inner = lax.dot_general(qr, kr.astype(BF16), (((1,), (1,)), ((), ())), preferred_element_type=F32)
        inner = inner * din_ref[0]
        r = r_sc[...]
        cross = jnp.dot(qr, r.astype(BF16), preferred_element_type=F32)
        o = jnp.dot(inner.astype(BF16), v, preferred_element_type=F32) + cross * xi

        kz = (kr * zeta).astype(BF16)
        upd = lax.dot_general(kz, v, (((0,), (0,)), ((), ())), preferred_element_type=F32)
        r_sc[...] = cd_ref[0][:1, :1] * r + upd

        mu = jnp.mean(o, axis=-1, keepdims=True)
        oc = o - mu
        var = jnp.mean(oc * oc, axis=-1, keepdims=True)
        y = oc * lax.rsqrt(var + 1e-5) * gng_ref[...] + gnb_ref[...]
        gate = g_ref[rows].astype(F32)
        o_ref[rows] = (gate * jax.nn.sigmoid(gate) * y).astype(o_ref.dtype)


def retention(h1, cos, sin, gn_g, gn_b, chunks_per_step=8):
    s = h1.shape[0]
    hh, dk, dv = RET_HEADS, RET_QK_DIM, RET_V_DIM
    c = RET_CHUNK * chunks_per_step
    assert s % c == 0
    din, xi, zeta, cd = (jnp.asarray(t) for t in _retention_tables())
    k_blk0 = hh * dk // dk
    v_blk0 = 2 * hh * dk // dv
    g_blk0 = v_blk0 + hh
    per_head = lambda r, w: pl.BlockSpec((1, r, w), lambda h, n: (h, 0, 0))
    return pl.pallas_call(
        _retention_kernel,
        grid=(hh, s // c),
        in_specs=[
            pl.BlockSpec((c, dk), lambda h, n: (n, h)),
            pl.BlockSpec((c, dk), lambda h, n: (n, k_blk0 + h)),
            pl.BlockSpec((c, dv), lambda h, n: (n, v_blk0 + h)),
            pl.BlockSpec((c, dv), lambda h, n: (n, g_blk0 + h)),
            pl.BlockSpec((c, dk // 2), lambda h, n: (n, 0)),
            pl.BlockSpec((c, dk // 2), lambda h, n: (n, 0)),
            per_head(RET_CHUNK, RET_CHUNK), per_head(RET_CHUNK, LANES), per_head(RET_CHUNK, LANES),
            per_head(8, LANES),
            pl.BlockSpec((1, dv), lambda h, n: (0, h)),
            pl.BlockSpec((1, dv), lambda h, n: (0, h)),
        ],
        out_specs=pl.BlockSpec((c, dv), lambda h, n: (n, h)),
        out_shape=jax.ShapeDtypeStruct((s, hh * dv), BF16),
        scratch_shapes=[pltpu.VMEM((dk, dv), F32)],
        compiler_params=_params("parallel", "arbitrary"),
        name="retention",
    )(h1, h1, h1, h1, cos, sin, din, xi, zeta, cd, gn_g.reshape(1, -1), gn_b.reshape(1, -1))


def _first_argmax(v, idx, n):
    m = jnp.max(v, axis=0, keepdims=True)
    first = jnp.min(jnp.where(v == m, idx, n), axis=0, keepdims=True)
    return m, first


def _route_tokens(x, rwt, rb):
    tm = x.shape[0]
    epg = EXPERTS_PER_GROUP
    logits = lax.dot_general(rwt, x, (((1,), (1,)), ((), ())), preferred_element_type=F32)
    scores = jax.nn.sigmoid(logits)
    biased = scores + rb
    idx = lax.broadcasted_iota(jnp.int32, (epg, tm), 0)

    group_scores = []
    for g in range(N_GROUPS):
        v = biased[g * epg:(g + 1) * epg]
        m1, first = _first_argmax(v, idx, epg)
        m2 = jnp.max(jnp.where(idx == first, -jnp.inf, v), axis=0, keepdims=True)
        group_scores.append(m1 + m2)
    gmax = group_scores[0]
    for g in range(1, N_GROUPS):
        gmax = jnp.maximum(gmax, group_scores[g])
    gsel = jnp.full((1, tm), N_GROUPS, jnp.int32)
    for g in range(N_GROUPS - 1, -1, -1):
        gsel = jnp.where(group_scores[g] == gmax, g, gsel)

    in_biased = jnp.zeros((epg, tm), F32)
    in_scores = jnp.zeros((epg, tm), F32)
    for g in range(N_GROUPS):
        pick = gsel == g
        in_biased = jnp.where(pick, biased[g * epg:(g + 1) * epg], in_biased)
        in_scores = jnp.where(pick, scores[g * epg:(g + 1) * epg], in_scores)
    _, loc1 = _first_argmax(in_biased, idx, epg)
    _, loc2 = _first_argmax(jnp.where(idx == loc1, -jnp.inf, in_biased), idx, epg)
    s1 = jnp.sum(jnp.where(idx == loc1, in_scores, 0.0), axis=0, keepdims=True)
    s2 = jnp.sum(jnp.where(idx == loc2, in_scores, 0.0), axis=0, keepdims=True)
    denom = s1 + s2
    experts = jnp.concatenate([gsel * epg + loc1, gsel * epg + loc2], axis=0)
    return experts, jnp.concatenate([s1 / denom, s2 / denom], axis=0)


def _scatter_rows_kernel(pad_ref, pos_ref, x_ref, xs_hbm, zero_sc, sem, zsem, *, tm, n_pad):
    i = pl.program_id(0)

    def zero_copy(p):
        return pltpu.make_async_copy(zero_sc, xs_hbm.at[pl.ds(pad_ref[p], 1)], zsem)

    @pl.when(i == 0)
    def _():
        zero_sc[...] = jnp.zeros_like(zero_sc)

        def start_zero(p, carry):
            zero_copy(p).start()
            return carry

        lax.fori_loop(0, n_pad, start_zero, 0)

    def start(g, carry):
        for u in range(DMA_ISSUE_UNROLL):
            r = g * DMA_ISSUE_UNROLL + u
            for k in range(TOP_K):
                pltpu.make_async_copy(x_ref.at[pl.ds(r, 1)], xs_hbm.at[pl.ds(pos_ref[0, k, r], 1)],
                                      sem).start(priority=k % 2)
        return carry

    lax.fori_loop(0, tm // DMA_ISSUE_UNROLL, start, 0)
    for k in range(TOP_K):
        pltpu.make_async_copy(x_ref, xs_hbm.at[pl.ds(0, tm)], sem).wait()

    @pl.when(i == 0)
    def _():
        def wait_zero(p, carry):
            zero_copy(p).wait()
            return carry

        lax.fori_loop(0, n_pad, wait_zero, 0)


def scatter_rows(x, pos, pad_pos, n_rows, tm=256):
    s, d = x.shape
    nt = s // tm
    n_pad = pad_pos.shape[0]
    return pl.pallas_call(
        functools.partial(_scatter_rows_kernel, tm=tm, n_pad=n_pad),
        grid_spec=pltpu.PrefetchScalarGridSpec(
            num_scalar_prefetch=1,
            grid=(nt,),
            in_specs=[
                pl.BlockSpec((1, TOP_K, tm), lambda i, pad: (i, 0, 0), memory_space=pltpu.SMEM),
                pl.BlockSpec((tm, d), lambda i, pad: (i, 0)),
            ],
            out_specs=pl.BlockSpec(memory_space=pl.ANY),
            scratch_shapes=[pltpu.VMEM((1, d), x.dtype), pltpu.SemaphoreType.DMA(()), pltpu.SemaphoreType.DMA(())],
        ),
        out_shape=jax.ShapeDtypeStruct((n_rows, d), x.dtype),
        compiler_params=_params("arbitrary"),
        name="scatter_rows",
    )(pad_pos, pos.reshape(TOP_K, nt, tm).transpose(1, 0, 2), x)


def _expert_kernel(te_ref, rs_ref, nv_ref, na_ref, dst_ref, xs_hbm, wg_ref, wu_ref, wd_ref, y_hbm,
                   wg_sc, wu_sc, wd_sc, xbuf, ybuf, lsem, ssem, *, tm):
    j = pl.program_id(0)
    n_active = na_ref[0]
    active = j < n_active
    slot = j % 2
    new_expert = (j == 0) | (te_ref[j] != te_ref[jnp.maximum(j - 1, 0)])
    chunk_sizes = [c for c in (256, 128, 64, 32, 16, 8) if c <= tm]

    def load_tile(t, dst_slot, wait):
        n8 = ((nv_ref[t] + 7) // 8) * 8
        base = rs_ref[t]
        off = jnp.int32(0)
        for c in chunk_sizes:
            take = (n8 & c) != 0

            @pl.when(take)
            def _(off=off, c=c):
                copy = pltpu.make_async_copy(
                    xs_hbm.at[pl.ds(pl.multiple_of(base + off, 8), c)],
                    xbuf.at[dst_slot, pl.ds(pl.multiple_of(off, 8), c)], lsem.at[dst_slot])
                if wait:
                    copy.wait()
                else:
                    copy.start()

            off = off + jnp.where(take, c, 0)

    def wait_scatter(src_slot, n):
        c = tm
        while c >= 1:
            @pl.when((n & c) != 0)
            def _(c=c):
                pltpu.make_async_copy(ybuf.at[src_slot, pl.ds(0, c)], y_hbm.at[pl.ds(0, c)],
                                      ssem.at[src_slot]).wait()
            c //= 2

    @pl.when(j == 0)
    def _():
        xbuf[...] = jnp.zeros_like(xbuf)
        load_tile(0, 0, False)

    @pl.when(j + 1 < n_active)
    def _():
        load_tile(j + 1, 1 - slot, False)

    @pl.when(active & new_expert)
    def _():
        wg_sc[...] = wg_ref[0].astype(BF16)
        wu_sc[...] = wu_ref[0].astype(BF16)
        wd_sc[...] = wd_ref[0].astype(BF16)

    @pl.when(active)
    def _():
        load_tile(j, slot, True)

        @pl.when(j >= 2)
        def _():
            wait_scatter(slot, nv_ref[jnp.maximum(j - 2, 0)])

        x = xbuf[slot].astype(BF16)
        hg = jnp.dot(x, wg_sc[...], preferred_element_type=F32)
        hu = jnp.dot(x, wu_sc[...], preferred_element_type=F32)
        h = hg * jax.nn.sigmoid(hg) * hu
        ybuf[slot] = jnp.dot(h.astype(BF16), wd_sc[...], preferred_element_type=F32)

        def scatter_row(r):
            pltpu.make_async_copy(ybuf.at[slot, pl.ds(r, 1)], y_hbm.at[pl.ds(dst_ref[0, 0, r], 1)],
                                  ssem.at[slot]).start()

        def scatter_group(i, carry):
            for u in range(DMA_ISSUE_UNROLL):
                scatter_row(i * DMA_ISSUE_UNROLL + u)
            return carry

        def scatter_one(r, carry):
            scatter_row(r)
            return carry

        n_groups = lax.shift_right_logical(nv_ref[j], DMA_ISSUE_UNROLL.bit_length() - 1)
        lax.fori_loop(0, n_groups, scatter_group, 0)
        lax.fori_loop(n_groups * DMA_ISSUE_UNROLL, nv_ref[j], scatter_one, 0)

        @pl.when(j == n_active - 1)
        def _():
            wait_scatter(slot, nv_ref[j])

            @pl.when(j >= 1)
            def _():
                wait_scatter(1 - slot, nv_ref[jnp.maximum(j - 1, 0)])


def routed_experts(xs, plan, n_out_rows, w_gate, w_up, w_down, tm):
    d = xs.shape[1]
    f = w_gate.shape[2]
    nt = plan.tile_expert.shape[0]

    def dst_blk(j, te, rs, nv, na):
        return (jnp.minimum(j, na[0] - 1), 0, 0)

    def weight_blk(j, te, rs, nv, na):
        return (te[j], 0, 0)

    return pl.pallas_call(
        functools.partial(_expert_kernel, tm=tm),
        grid_spec=pltpu.PrefetchScalarGridSpec(
            num_scalar_prefetch=4,
            grid=(nt,),
            in_specs=[
                pl.BlockSpec((1, 1, tm), dst_blk, memory_space=pltpu.SMEM),
                pl.BlockSpec(memory_space=pl.ANY),
                pl.BlockSpec((1, d, f), weight_blk),
                pl.BlockSpec((1, d, f), weight_blk),
                pl.BlockSpec((1, f, d), weight_blk),
            ],
            out_specs=pl.BlockSpec(memory_space=pl.ANY),
            scratch_shapes=[pltpu.VMEM((d, f), BF16), pltpu.VMEM((d, f), BF16), pltpu.VMEM((f, d), BF16),
                            pltpu.VMEM((2, tm, d), xs.dtype), pltpu.VMEM((2, tm, d), F32),
                            pltpu.SemaphoreType.DMA((2,)), pltpu.SemaphoreType.DMA((2,))],
        ),
        out_shape=jax.ShapeDtypeStruct((n_out_rows, d), F32),
        compiler_params=_params("arbitrary"),
        name="routed_experts",
    )(plan.tile_expert, plan.tile_row_start, plan.n_valid, plan.n_active, plan.dst_tiles.reshape(nt, 1, tm),
      xs, w_gate, w_up, w_down)


def _shared_kernel(x_ref, wg_ref, wu_ref, wd_ref, o_ref):
    x = x_ref[...]
    hg = jnp.dot(x, wg_ref[...], preferred_element_type=F32)
    hu = jnp.dot(x, wu_ref[...], preferred_element_type=F32)
    h = hg * jax.nn.sigmoid(hg) * hu
    o_ref[...] = jnp.dot(h.astype(BF16), wd_ref[...], preferred_element_type=F32)


def shared_expert(x_bf, wg, wu, wd, tm=512):
    s, d = x_bf.shape
    f = wg.shape[1]
    whole = lambda r, c: pl.BlockSpec((r, c), lambda i: (0, 0))
    return pl.pallas_call(
        _shared_kernel,
        grid=(s // tm,),
        in_specs=[pl.BlockSpec((tm, d), lambda i: (i, 0)), whole(d, f), whole(d, f), whole(f, d)],
        out_specs=pl.BlockSpec((tm, d), lambda i: (i, 0)),
        out_shape=jax.ShapeDtypeStruct((s, d), F32),
        compiler_params=_params("parallel"),
        name="shared_expert",
    )(x_bf, wg.astype(BF16), wu.astype(BF16), wd.astype(BF16))


def _combine_kernel(x_ref, sh_ref, gate_ref, y0_ref, y1_ref, g_ref, b_ref, o_ref, obf_ref):
    gate = gate_ref[...]
    routed = gate[:, 0:1] * y0_ref[0] + gate[:, 1:2] * y1_ref[0]
    z = DEEPNORM_ALPHA * x_ref[...] + (routed + sh_ref[...])
    out = _layer_norm(z, g_ref[...], b_ref[...])
    o_ref[...] = out
    obf_ref[...] = out.astype(BF16)


def moe_combine(x, shared, gates, y, g, b, tm=256):
    s, d = x.shape
    assert TOP_K == 2
    row = pl.BlockSpec((tm, d), lambda i: (i, 0))
    vec = pl.BlockSpec((1, d), lambda i: (0, 0))
    return pl.pallas_call(
        _combine_kernel,
        grid=(s // tm,),
        in_specs=[
            row, row,
            pl.BlockSpec((tm, TOP_K), lambda i: (i, 0)),
            pl.BlockSpec((1, tm, d), lambda i: (0, i, 0)),
            pl.BlockSpec((1, tm, d), lambda i: (1, i, 0)),
            vec, vec,
        ],
        out_specs=[row, row],
        out_shape=[jax.ShapeDtypeStruct((s, d), F32), jax.ShapeDtypeStruct((s, d), BF16)],
        compiler_params=_params("parallel"),
        name="moe_combine",
    )(x, shared, gates.T, y, y, g.reshape(1, d), b.reshape(1, d))


def _rope_tables(pos, rot_dim, theta):
    half = rot_dim // 2
    inv = (1.0 / (theta ** (np.arange(half, dtype=np.float32) / half))).astype(np.float32)
    ang = pos.astype(F32)[:, None] * inv
    return jnp.cos(ang), jnp.sin(ang)


def _rope(x, cos, sin, rot_dim):
    half = rot_dim // 2
    x1 = x[..., :half]
    x2 = x[..., half:rot_dim]
    c = cos[:, None, :]
    s = sin[:, None, :]
    return jnp.concatenate([x1 * c - x2 * s, x1 * s + x2 * c, x[..., rot_dim:]], axis=-1)


class _RoutingPlan(NamedTuple):
    pos: jax.Array
    pad_pos: jax.Array
    n_rows: int
    tile_expert: jax.Array
    tile_row_start: jax.Array
    n_valid: jax.Array
    n_active: jax.Array
    dst_tiles: jax.Array


ROW_ALIGN = 8


def _routing_plan(experts, ranks, sizes, tm):
    k, s = experts.shape
    flat = experts.reshape(-1)
    rank = ranks.reshape(-1)
    padded = (sizes + ROW_ALIGN - 1) // ROW_ALIGN * ROW_ALIGN
    seg_end = jnp.cumsum(padded)
    seg_start = seg_end - padded
    pos = seg_start[flat] + rank
    n_rows = k * s + N_EXPERTS * (ROW_ALIGN - 1)

    i_pad = jnp.arange(ROW_ALIGN - 1, dtype=jnp.int32)
    is_pad = i_pad[None, :] < (padded - sizes)[:, None]
    spare = jnp.cumsum(jnp.logical_not(is_pad).reshape(-1).astype(jnp.int32)).reshape(is_pad.shape) - 1
    pad_pos = jnp.where(is_pad, (seg_start + sizes)[:, None] + i_pad[None, :], seg_end[-1] + spare).reshape(-1)

    tiles = (sizes + tm - 1) // tm
    tile_end = jnp.cumsum(tiles)
    n_active = tile_end[-1]
    n_tiles = k * s // tm + N_EXPERTS
    tile_ids = jnp.minimum(jnp.arange(n_tiles, dtype=jnp.int32), n_active - 1)
    tile_expert = jnp.sum((tile_end[None, :] <= tile_ids[:, None]).astype(jnp.int32), axis=1)
    local = tile_ids - (tile_end - tiles)[tile_expert]
    tile_row_start = seg_start[tile_expert] + local * tm
    n_valid = jnp.clip(sizes[tile_expert] - local * tm, 0, tm)
    slot_of_row = jnp.zeros((n_rows,), jnp.int32).at[pos].set(jnp.arange(k * s, dtype=jnp.int32))
    rows = jnp.minimum(tile_row_start[:, None] + jnp.arange(tm, dtype=jnp.int32)[None, :], n_rows - 1)
    return _RoutingPlan(pos.reshape(k, s), pad_pos.astype(jnp.int32), n_rows, tile_expert,
                        tile_row_start.astype(jnp.int32), n_valid.astype(jnp.int32),
                        n_active.reshape(1).astype(jnp.int32), slot_of_row[rows])


def _grouped_moe(x, x_bf, routing, w_gate, w_up, w_down, sh_gate, sh_up, sh_down, ln_g, ln_b, expert_tm=256):
    s, d = x.shape
    experts, gates, ranks, counts = routing
    plan = _routing_plan(experts, ranks, counts[:, 0].astype(jnp.int32), expert_tm)
    xs = scatter_rows(x, plan.pos, plan.pad_pos, plan.n_rows)
    y = routed_experts(xs, plan, TOP_K * s, w_gate, w_up, w_down, expert_tm)
    shared = shared_expert(x_bf, sh_gate, sh_up, sh_down)
    return moe_combine(x, shared, gates, y.reshape(TOP_K, s, d), ln_g, ln_b)


def _rope_lane_tables(pos, half, theta, period, offset=0, limit=None):
    cos, sin = _rope_tables(pos, 2 * half, theta)
    lane = np.arange(LANES)
    p = lane % period - offset
    in_range = np.ones(LANES, bool) if limit is None else lane < limit
    first = (p >= 0) & (p < half) & in_range
    second = (p >= half) & (p < 2 * half) & in_range
    idx = np.where(first | second, p % half, 0)
    cos_l, sin_l = cos[:, idx], sin[:, idx]
    return jnp.stack([jnp.where(first | second, cos_l, 1.0),
                      jnp.where(second, sin_l, 0.0),
                      jnp.where(first, -sin_l, 0.0)])


def _mla_dsa_mixer(x, pos, w_in, q_norm, w_uq, kv_norm, w_ukv):
    cq, ckv, krope, dq, dkv, iq, ik, iw = jnp.split(w_in, [int(c) for c in np.cumsum(SPLITS_AB)[:-1]], axis=1)
    w_in_l = jnp.concatenate([cq, ckv, dq, dkv, iq, krope, ik, iw,
                              jnp.zeros((w_in.shape[0], LANES - IDX_HEADS), w_in.dtype)], axis=1)
    w_uq3 = w_uq.reshape(MLA_Q_LORA, MLA_HEADS, MLA_NOPE + MLA_ROPE)
    w_uq_l = jnp.concatenate([w_uq3[:, :, :MLA_NOPE].reshape(MLA_Q_LORA, -1),
                              jnp.pad(w_uq3[:, :, MLA_NOPE:], ((0, 0), (0, 0), (0, LANES - MLA_ROPE))
                                      ).reshape(MLA_Q_LORA, -1)], axis=1)
    w_ukv3 = w_ukv.reshape(MLA_KV_LORA, MLA_HEADS, MLA_NOPE + MLA_V)
    w_ukv_l = jnp.concatenate([w_ukv3[:, :, :MLA_NOPE].reshape(MLA_KV_LORA, -1),
                               w_ukv3[:, :, MLA_NOPE:].reshape(MLA_KV_LORA, -1)], axis=1)

    tab_m = _rope_lane_tables(pos, MLA_ROPE // 2, ROPE_THETA, LANES)
    tab_d = _rope_lane_tables(pos, DSA_ROT // 2, ROPE_THETA, LANES)
    tab_i = _rope_lane_tables(pos, IDX_ROT // 2, ROPE_THETA, IDX_DIM)
    tab_k = _rope_lane_tables(pos, MLA_ROPE // 2, ROPE_THETA, LANES, limit=MLA_ROPE)
    tab_ik = _rope_lane_tables(pos, IDX_ROT // 2, ROPE_THETA, LANES, offset=MLA_ROPE)
    tab_ki = jnp.concatenate([(tab_k[0] * tab_ik[0])[None], tab_k[1:], tab_ik[1:]])

    h0 = matmul(x, w_in_l, F32, 1024, 512, name="l0_in_proj")
    q_cat, k_nope, v_a, k_pe, qb, kb, vb, iq_r, ik_lo, ik_hi, iw_s = l0_attention_operands(
        h0, q_norm, w_uq_l, kv_norm, w_ukv_l, (tab_m, tab_d, tab_i, tab_ki))
    out_a = mla_attention(q_cat, k_nope, k_pe, v_a)
    out_b = dsa_attention(iq_r, iw_s, ik_lo, ik_hi, qb, kb, vb)
    return out_a, out_b


def _retention_mixer(x_bf, pos, w_in, gn_g, gn_b):
    h1 = matmul(x_bf, w_in, BF16, 1024, 1024, name="l1_in_proj")
    cos, sin = _rope_tables(pos, RET_QK_DIM, RET_THETA)
    return retention(h1, cos, sin, gn_g, gn_b)


def kernel(x, positions, router_w, router_b, l0_w_in, l0_mla_q_norm, l0_mla_w_uq, l0_mla_kv_norm, l0_mla_w_ukv, l0_w_out, l1_w_in, l1_ret_gn_g, l1_ret_gn_b, l1_w_out, l0_ln_mix_g, l0_ln_mix_b, l0_moe_w_gate, l0_moe_w_up, l0_moe_w_down, l0_sh_gate, l0_sh_up, l0_sh_down, l0_ln_ffn_g, l0_ln_ffn_b, l1_ln_mix_g, l1_ln_mix_b, l1_moe_w_gate, l1_moe_w_up, l1_moe_w_down, l1_sh_gate, l1_sh_up, l1_sh_down, l1_ln_ffn_g, l1_ln_ffn_b):
    assert x.shape[0] == 1
    xt = x[0]
    pos = positions[0]

    out_a, out_b = _mla_dsa_mixer(xt, pos, l0_w_in, l0_mla_q_norm, l0_mla_w_uq, l0_mla_kv_norm, l0_mla_w_ukv)
    xt, xt_bf, *routing = out_proj_ln_route((out_a, out_b), l0_w_out, xt, l0_ln_mix_g, l0_ln_mix_b,
                                            router_w, router_b, name="l0_out_proj_ln_route")
    xt, xt_bf = _grouped_moe(xt, xt_bf, routing, l0_moe_w_gate, l0_moe_w_up, l0_moe_w_down,
                             l0_sh_gate, l0_sh_up, l0_sh_down, l0_ln_ffn_g, l0_ln_ffn_b)

    ret = _retention_mixer(xt_bf, pos, l1_w_in, l1_ret_gn_g, l1_ret_gn_b)
    xt, xt_bf, *routing = out_proj_ln_route((ret,), l1_w_out, xt, l1_ln_mix_g, l1_ln_mix_b,
                                            router_w, router_b, name="l1_out_proj_ln_route")
    xt, _ = _grouped_moe(xt, xt_bf, routing, l1_moe_w_gate, l1_moe_w_up, l1_moe_w_down,
                         l1_sh_gate, l1_sh_up, l1_sh_down, l1_ln_ffn_g, l1_ln_ffn_b)
    return xt[None]
```

```python
import functools
from typing import NamedTuple

import numpy as np
import jax
import jax.numpy as jnp
from jax import lax
from jax.experimental import pallas as pl
from jax.experimental.pallas import tpu as pltpu

F32 = jnp.float32
BF16 = jnp.bfloat16

DEPTH = 2
ROPE_THETA = 500000.0
MLA_HEADS = 8
MLA_Q_LORA = 512
MLA_KV_LORA = 256
MLA_NOPE = 128
MLA_ROPE = 64
MLA_V = 128
DSA_HEADS = 8
DSA_KV_HEADS = 2
DSA_HEAD_DIM = 128
DSA_ROT = DSA_HEAD_DIM // 4
IDX_HEADS = 16
IDX_DIM = 64
IDX_ROT = IDX_DIM // 4
IDX_TOPK_MAX = 256
RET_HEADS = 8
RET_QK_DIM = 256
RET_V_DIM = 512
RET_CHUNK = 128
RET_THETA = 10000.0
N_EXPERTS = 64
N_GROUPS = 8
EXPERTS_PER_GROUP = N_EXPERTS // N_GROUPS
TOP_K = 2
DEEPNORM_ALPHA = (2.0 * DEPTH) ** 0.25

SPLITS_AB = (MLA_Q_LORA, MLA_KV_LORA, MLA_ROPE, DSA_HEADS * DSA_HEAD_DIM,
             2 * DSA_KV_HEADS * DSA_HEAD_DIM, IDX_HEADS * IDX_DIM, IDX_DIM, IDX_HEADS)

VMEM_LIMIT_BYTES = 56 * 1024 * 1024
LANES = 128
MASKED_SCORE = -1e30
LOG2_E = 1.4426950408889634
DMA_ISSUE_UNROLL = 8
INT32_MIN = -2 ** 31


def _params(*sem):
    return pltpu.CompilerParams(dimension_semantics=sem, vmem_limit_bytes=VMEM_LIMIT_BYTES)


def _mm_kernel(a_ref, b_ref, o_ref):
    o_ref[...] = jnp.dot(a_ref[...].astype(BF16), b_ref[...].astype(BF16),
                         preferred_element_type=F32).astype(o_ref.dtype)


def matmul(a, b, out_dtype, tm, tn, *, name):
    m, k = a.shape
    n = b.shape[1]
    assert b.shape[0] == k and m % tm == 0 and n % tn == 0
    return pl.pallas_call(
        _mm_kernel,
        grid=(m // tm, n // tn),
        in_specs=[pl.BlockSpec((tm, k), lambda i, j: (i, 0)), pl.BlockSpec((k, tn), lambda i, j: (0, j))],
        out_specs=pl.BlockSpec((tm, tn), lambda i, j: (i, j)),
        out_shape=jax.ShapeDtypeStruct((m, n), out_dtype),
        compiler_params=_params("parallel", "parallel"),
        name=name,
    )(a, b)


_H0_CQ = 0
_H0_CKV = _H0_CQ + MLA_Q_LORA
_H0_DQ = _H0_CKV + MLA_KV_LORA
_H0_DKV = _H0_DQ + DSA_HEADS * DSA_HEAD_DIM
_H0_IQ = _H0_DKV + 2 * DSA_KV_HEADS * DSA_HEAD_DIM
_H0_KROPE_IK = _H0_IQ + IDX_HEADS * IDX_DIM
_H0_IW = _H0_KROPE_IK + LANES
_H0_WIDTH = _H0_IW + LANES


def _rope_lanes(x, tab_ref, shifts):
    out = x * tab_ref[0]
    for i, shift in enumerate(shifts):
        out = out + pltpu.roll(x, shift, 1) * tab_ref[1 + i]
    return out


def _l0_prep_kernel(h_ref, qn_ref, kvn_ref, wuq_ref, wukv_ref, tm_ref, td_ref, ti_ref, tki_ref,
                    qcat_ref, knope_ref, va_ref, kpe_ref, qb_ref, kb_ref, vb_ref, iq_ref, iklo_ref, ikhi_ref,
                    iw_ref):
    def rms(x, g_ref):
        return (x * lax.rsqrt(jnp.mean(x * x, axis=-1, keepdims=True) + 1e-6) * g_ref[...]).astype(BF16)

    def slab(col, j=0):
        return h_ref[:, col + j * LANES:col + (j + 1) * LANES]

    half_m, half_d, half_i = MLA_ROPE // 2, DSA_ROT // 2, IDX_ROT // 2
    shifts_m = (half_m, LANES - half_m)
    shifts_d = (half_d, LANES - half_d)
    shifts_i = (half_i, LANES - half_i)

    qa = jnp.dot(rms(h_ref[:, _H0_CQ:_H0_CQ + MLA_Q_LORA], qn_ref), wuq_ref[...], preferred_element_type=F32)
    kva = jnp.dot(rms(h_ref[:, _H0_CKV:_H0_CKV + MLA_KV_LORA], kvn_ref), wukv_ref[...],
                  preferred_element_type=F32)
    n_nope = MLA_HEADS * MLA_NOPE
    knope_ref[...] = kva[:, :n_nope].astype(BF16)
    va_ref[...] = kva[:, n_nope:].astype(BF16)
    q_scale = (MLA_NOPE + MLA_ROPE) ** -0.5 * LOG2_E
    for h in range(MLA_HEADS):
        nope = qa[:, h * LANES:(h + 1) * LANES]
        pe = qa[:, n_nope + h * LANES:n_nope + (h + 1) * LANES]
        qcat_ref[:, 2 * h * LANES:(2 * h + 1) * LANES] = (nope * q_scale).astype(BF16)
        qcat_ref[:, (2 * h + 1) * LANES:(2 * h + 2) * LANES] = (_rope_lanes(pe, tm_ref, shifts_m) * q_scale).astype(BF16)

    d_scale = DSA_HEAD_DIM ** -0.5 * LOG2_E
    for h in range(DSA_HEADS):
        qb_ref[:, h * LANES:(h + 1) * LANES] = (_rope_lanes(slab(_H0_DQ, h), td_ref, shifts_d) * d_scale).astype(BF16)
    for g in range(DSA_KV_HEADS):
        kb_ref[:, g * LANES:(g + 1) * LANES] = _rope_lanes(slab(_H0_DKV, g), td_ref, shifts_d).astype(BF16)
    vb_ref[...] = h_ref[:, _H0_DKV + DSA_KV_HEADS * LANES:_H0_DKV + 2 * DSA_KV_HEADS * LANES].astype(BF16)
    for j in range(IDX_HEADS * IDX_DIM // LANES):
        iq_ref[:, j * LANES:(j + 1) * LANES] = _rope_lanes(slab(_H0_IQ, j), ti_ref, shifts_i).astype(BF16)

    ki = _rope_lanes(slab(_H0_KROPE_IK), tki_ref, shifts_m + shifts_i)
    lane = lax.broadcasted_iota(jnp.int32, ki.shape, 1)
    kpe_ref[...] = jnp.where(lane < MLA_ROPE, ki, 0.0).astype(BF16)
    ik_hi = jnp.where(lane >= MLA_ROPE, ki, 0.0)
    ikhi_ref[...] = ik_hi.astype(BF16)
    iklo_ref[...] = pltpu.roll(ik_hi, LANES - MLA_ROPE, 1).astype(BF16)
    iw_ref[...] = h_ref[:, _H0_IW:_H0_IW + IDX_HEADS] * (IDX_HEADS ** -0.5 * IDX_DIM ** -0.5)


def l0_attention_operands(h0, q_norm, w_uq, kv_norm, w_ukv, tables, tm=256):
    s = h0.shape[0]
    bf = lambda n: jax.ShapeDtypeStruct((s, n), BF16)
    row = lambda n: pl.BlockSpec((tm, n), lambda i: (i, 0))
    whole = lambda a: pl.BlockSpec(a.shape, lambda i: (0,) * a.ndim)
    tab = lambda t: pl.BlockSpec((t.shape[0], tm, LANES), lambda i: (0, i, 0))
    widths = [2 * MLA_HEADS * LANES, MLA_HEADS * MLA_NOPE, MLA_HEADS * MLA_V, LANES, DSA_HEADS * DSA_HEAD_DIM,
              DSA_KV_HEADS * DSA_HEAD_DIM, DSA_KV_HEADS * DSA_HEAD_DIM, IDX_HEADS * IDX_DIM, LANES, LANES]
    w_uq = w_uq.astype(BF16)
    w_ukv = w_ukv.astype(BF16)
    q_norm = q_norm.reshape(1, -1)
    kv_norm = kv_norm.reshape(1, -1)
    return pl.pallas_call(
        _l0_prep_kernel,
        grid=(s // tm,),
        in_specs=[row(_H0_WIDTH), whole(q_norm), whole(kv_norm), whole(w_uq), whole(w_ukv)] + [tab(t) for t in tables],
        out_specs=[row(n) for n in widths] + [row(IDX_HEADS)],
        out_shape=[bf(n) for n in widths] + [jax.ShapeDtypeStruct((s, IDX_HEADS), F32)],
        compiler_params=_params("parallel"),
        name="l0_attention_operands",
    )(h0, q_norm, kv_norm, w_uq, w_ukv, *tables)


def _layer_norm(z, g, b):
    mu = jnp.mean(z, axis=-1, keepdims=True)
    zc = z - mu
    var = jnp.mean(zc * zc, axis=-1, keepdims=True)
    return zc * lax.rsqrt(var + 1e-5) * g + b


def _proj_ln_route_kernel(*refs, n_parts):
    a_refs = refs[:n_parts]
    (w_ref, x_ref, g_ref, b_ref, rwt_ref, rb_ref,
     o_ref, obf_ref, e_ref, gate_ref, rank_ref, count_ref, run_sc) = refs[n_parts:]
    mix = None
    k0 = 0
    for a_ref in a_refs:
        k = a_ref.shape[1]
        part = jnp.dot(a_ref[...], w_ref[k0:k0 + k, :], preferred_element_type=F32)
        mix = part if mix is None else mix + part
        k0 += k
    out = _layer_norm(DEEPNORM_ALPHA * x_ref[...] + mix, g_ref[...], b_ref[...])
    out_bf = out.astype(BF16)
    o_ref[...] = out
    obf_ref[...] = out_bf
    experts, gate_ref[...] = _route_tokens(out_bf, rwt_ref[...], rb_ref[...])
    e_ref[...] = experts

    @pl.when(pl.program_id(0) == 0)
    def _():
        run_sc[...] = jnp.zeros_like(run_sc)

    tm = out.shape[0]
    expert_id = lax.broadcasted_iota(jnp.int32, (N_EXPERTS, tm), 0)
    before = (lax.broadcasted_iota(jnp.int32, (tm, tm), 0) < lax.broadcasted_iota(jnp.int32, (tm, tm), 1))
    before = jnp.where(before, 1.0, 0.0).astype(BF16)
    base = run_sc[...][:, :1]
    ranks = []
    for k in range(TOP_K):
        onehot = expert_id == experts[k:k + 1, :]
        prefix = jnp.dot(jnp.where(onehot, 1.0, 0.0).astype(BF16), before, preferred_element_type=F32)
        ranks.append(jnp.sum(jnp.where(onehot, prefix + base, 0.0), axis=0, keepdims=True))
        base = base + jnp.sum(jnp.where(onehot, 1.0, 0.0), axis=1, keepdims=True)
    rank_ref[...] = jnp.concatenate(ranks, axis=0).astype(jnp.int32)
    run_sc[...] = jnp.broadcast_to(base, run_sc.shape)
    count_ref[...] = run_sc[...]


def out_proj_ln_route(a_parts, w, x, g, b, router_w, router_b, *, name, tm=512):
    s, d = x.shape
    w = w.astype(BF16)
    assert sum(a.shape[1] for a in a_parts) == w.shape[0] and w.shape[1] == d
    row = pl.BlockSpec((tm, d), lambda i: (i, 0))
    vec = pl.BlockSpec((1, d), lambda i: (0, 0))
    slots = pl.BlockSpec((TOP_K, tm), lambda i: (0, i))
    resident = lambda shape: pl.BlockSpec(shape, lambda i: (0, 0), pipeline_mode=pl.Buffered(1))
    return pl.pallas_call(
        functools.partial(_proj_ln_route_kernel, n_parts=len(a_parts)),
        grid=(s // tm,),
        in_specs=[pl.BlockSpec((tm, a.shape[1]), lambda i: (i, 0)) for a in a_parts]
        + [resident(w.shape), row, vec, vec, resident((N_EXPERTS, d)), resident((N_EXPERTS, 1))],
        out_specs=[row, row, slots, slots, slots, pl.BlockSpec((N_EXPERTS, LANES), lambda i: (0, 0))],
        out_shape=[jax.ShapeDtypeStruct((s, d), F32), jax.ShapeDtypeStruct((s, d), BF16),
                   jax.ShapeDtypeStruct((TOP_K, s), jnp.int32), jax.ShapeDtypeStruct((TOP_K, s), F32),
                   jax.ShapeDtypeStruct((TOP_K, s), jnp.int32), jax.ShapeDtypeStruct((N_EXPERTS, LANES), F32)],
        scratch_shapes=[pltpu.VMEM((N_EXPERTS, LANES), F32)],
        compiler_params=_params("arbitrary"),
        name=name,
    )(*a_parts, w, x, g.reshape(1, d), b.reshape(1, d), router_w.T.astype(BF16), router_b.reshape(N_EXPERTS, 1))


def _lane_tile(x, n):
    return jnp.tile(x, (1, n))


def _softmax_step(s, v_ext, m_ref, acc_ref):
    tk = s.shape[1]
    m_prev = m_ref[...]
    m_new = jnp.maximum(m_prev, jnp.max(s, axis=1)[:, None])
    p = jnp.exp2(s - _lane_tile(m_new, tk // LANES))
    alpha = jnp.exp2(m_prev - m_new)
    pv = jnp.dot(p.astype(BF16), v_ext, preferred_element_type=F32)
    acc_ref[...] = _lane_tile(alpha, 2) * acc_ref[...] + pv
    m_ref[...] = m_new


def _mla_kernel(q_ref, kn_ref, kp_ref, v_ref, o_ref, m_sc, acc_sc, *, tq, tk, hp):
    qi = pl.program_id(1)
    m_sc[...] = jnp.full_like(m_sc, MASKED_SCORE)
    acc_sc[...] = jnp.zeros_like(acc_sc)
    ones = jnp.ones((tk, LANES), BF16)
    per_block = tq // tk

    def attend(c, diagonal_offset):
        kp = kp_ref[c]
        kn = kn_ref[c]
        v = v_ref[c]
        for j in range(hp):
            q = q_ref[:, j * 2 * LANES:(j + 1) * 2 * LANES]
            k = jnp.concatenate([kn[:, j * LANES:(j + 1) * LANES], kp], axis=1)
            s = lax.dot_general(q, k, (((1,), (1,)), ((), ())), preferred_element_type=F32)
            if diagonal_offset is not None:
                row = lax.broadcasted_iota(jnp.int32, (tq, tk), 0)
                col = diagonal_offset * tk + lax.broadcasted_iota(jnp.int32, (tq, tk), 1)
                s = jnp.where(col <= row, s, MASKED_SCORE)
            v_ext = jnp.concatenate([v[:, j * LANES:(j + 1) * LANES], ones], axis=1)
            _softmax_step(s, v_ext, m_sc.at[j], acc_sc.at[j])

    def below_diagonal(c, carry):
        attend(c, None)
        return carry

    lax.fori_loop(0, qi * per_block, below_diagonal, 0)
    for d in range(per_block):
        attend(qi * per_block + d, d)
    for j in range(hp):
        acc = acc_sc[j]
        o_ref[:, j * LANES:(j + 1) * LANES] = (acc[:, :LANES] / acc[:, LANES:]).astype(o_ref.dtype)


def mla_attention(q, k_nope, k_pe, v, tq=1024, tk=512, hp=2):
    s = q.shape[0]
    h = MLA_HEADS
    nc = s // tk
    assert tq % tk == 0 and h % hp == 0
    per_head = pl.BlockSpec((nc, tk, hp * LANES), lambda hh, qi: (0, 0, hh))
    return pl.pallas_call(
        functools.partial(_mla_kernel, tq=tq, tk=tk, hp=hp),
        grid=(h // hp, s // tq),
        in_specs=[
            pl.BlockSpec((tq, hp * 2 * LANES), lambda hh, qi: (qi, hh)),
            per_head,
            pl.BlockSpec((nc, tk, LANES), lambda hh, qi: (0, 0, 0)),
            per_head,
        ],
        out_specs=pl.BlockSpec((tq, hp * LANES), lambda hh, qi: (qi, hh)),
        out_shape=jax.ShapeDtypeStruct((s, h * MLA_V), BF16),
        scratch_shapes=[pltpu.VMEM((hp, tq, LANES), F32), pltpu.VMEM((hp, tq, 2 * LANES), F32)],
        compiler_params=_params("parallel", "arbitrary"),
        name="mla_attention",
    )(q, k_nope.reshape(nc, tk, h * MLA_NOPE), k_pe.reshape(nc, tk, LANES), v.reshape(nc, tk, h * MLA_V))


def _sortable_key(x):
    bits = pltpu.bitcast(x, jnp.int32)
    return bits ^ ((bits >> 31) & 0x7FFFFFFF)


def _dsa_kernel(iq_ref, iw_ref, iklo_ref, ikhi_ref, q_ref, k_ref, v_ref, o_ref, key_sc, wb_sc, m_sc, acc_sc,
                *, tq, tk, n_sel):
    i = pl.program_id(0)
    n_chunks = ((i + 1) * tq + tk - 1) // tk
    rep = DSA_HEADS // DSA_KV_HEADS
    lane_tiles = tk // LANES
    heads_per_block = LANES // IDX_DIM

    assert heads_per_block == 2
    iq_blocks = jnp.concatenate(
        [iq_ref[:, j * LANES:(j + 1) * LANES] for j in range(IDX_HEADS // heads_per_block)], axis=0)
    iw = iw_ref[...]
    for h in range(IDX_HEADS):
        wb_sc[h] = jnp.broadcast_to(iw[:, h:h + 1], (tq, LANES))

    def index_chunk(c, carry):
        ik = jnp.concatenate([iklo_ref[c], ikhi_ref[c]], axis=0)
        d = lax.dot_general(iq_blocks, ik, (((1,), (1,)), ((), ())), preferred_element_type=F32)
        d = jnp.maximum(d, 0.0)
        score = None
        for h in range(IDX_HEADS):
            j, part = divmod(h, heads_per_block)
            term = _lane_tile(wb_sc[h], lane_tiles) * d[j * tq:(j + 1) * tq, part * tk:(part + 1) * tk]
            score = term if score is None else score + term
        key_sc[c] = _sortable_key(score)
        return carry

    lax.fori_loop(0, n_chunks, index_chunk, 0)
    last = n_chunks - 1
    row = i * tq + lax.broadcasted_iota(jnp.int32, (tq, tk), 0)
    col = last * tk + lax.broadcasted_iota(jnp.int32, (tq, tk), 1)
    key_sc[last] = jnp.where(col <= row, key_sc[last], INT32_MIN)

    def search_bit(b, thr):
        cand = thr + lax.shift_left(jnp.int32(1), 31 - b)

        def count_chunk(c, cnt):
            key = key_sc[c]
            for j in range(lane_tiles):
                cnt = cnt + jnp.where(key[:, j * LANES:(j + 1) * LANES] >= cand, 1.0, 0.0)
            return cnt

        cnt = lax.fori_loop(0, n_chunks, count_chunk, jnp.zeros((tq, LANES), F32))
        return jnp.where(jnp.sum(cnt, axis=1)[:, None] >= n_sel, cand, thr)

    thr = lax.fori_loop(0, 32, search_bit, jnp.full((tq, LANES), INT32_MIN, jnp.int32))
    thr = _lane_tile(jnp.maximum(thr, INT32_MIN + 1), lane_tiles)

    m_sc[...] = jnp.full_like(m_sc, MASKED_SCORE)
    acc_sc[...] = jnp.zeros_like(acc_sc)
    ones = jnp.ones((tk, LANES), BF16)

    def attend_chunk(c, carry):
        bias = jnp.where(key_sc[c] >= thr, 0.0, MASKED_SCORE)
        kc = k_ref[c]
        vc = v_ref[c]
        for g in range(DSA_KV_HEADS):
            cols = slice(g * DSA_HEAD_DIM, (g + 1) * DSA_HEAD_DIM)
            qg = jnp.concatenate(
                [q_ref[:, (g * rep + r) * DSA_HEAD_DIM:(g * rep + r + 1) * DSA_HEAD_DIM] for r in range(rep)], axis=0)
            s = lax.dot_general(qg, kc[:, cols], (((1,), (1,)), ((), ())), preferred_element_type=F32)
            s = (s.reshape(rep, tq, tk) + bias[None]).reshape(rep * tq, tk)
            _softmax_step(s, jnp.concatenate([vc[:, cols], ones], axis=1), m_sc.at[g], acc_sc.at[g])
        return carry

    lax.fori_loop(0, n_chunks, attend_chunk, 0)
    for g in range(DSA_KV_HEADS):
        acc = acc_sc[g]
        out = acc[:, :LANES] / acc[:, LANES:]
        for r in range(rep):
            hh = g * rep + r
            o_ref[:, hh * DSA_HEAD_DIM:(hh + 1) * DSA_HEAD_DIM] = out[r * tq:(r + 1) * tq].astype(o_ref.dtype)


def dsa_attention(iq, iw, ik_lo, ik_hi, q, k, v, tq=128, tk=512):
    s = q.shape[0]
    n_sel = min(IDX_TOPK_MAX, s // 4)
    nc = s // tk
    assert tk >= n_sel and s % tk == 0 and tk % tq == 0 and DSA_HEAD_DIM == LANES
    rep = DSA_HEADS // DSA_KV_HEADS
    kvw = DSA_KV_HEADS * DSA_HEAD_DIM
    whole3 = lambda i: (0, 0, 0)
    return pl.pallas_call(
        functools.partial(_dsa_kernel, tq=tq, tk=tk, n_sel=n_sel),
        grid=(s // tq,),
        in_specs=[
            pl.BlockSpec((tq, IDX_HEADS * IDX_DIM), lambda i: (i, 0)),
            pl.BlockSpec((tq, IDX_HEADS), lambda i: (i, 0)),
            pl.BlockSpec((nc, tk, LANES), whole3),
            pl.BlockSpec((nc, tk, LANES), whole3),
            pl.BlockSpec((tq, DSA_HEADS * DSA_HEAD_DIM), lambda i: (i, 0)),
            pl.BlockSpec((nc, tk, kvw), whole3),
            pl.BlockSpec((nc, tk, kvw), whole3),
        ],
        out_specs=pl.BlockSpec((tq, DSA_HEADS * DSA_HEAD_DIM), lambda i: (i, 0)),
        out_shape=jax.ShapeDtypeStruct((s, DSA_HEADS * DSA_HEAD_DIM), BF16),
        scratch_shapes=[
            pltpu.VMEM((nc, tq, tk), jnp.int32),
            pltpu.VMEM((IDX_HEADS, tq, LANES), F32),
            pltpu.VMEM((DSA_KV_HEADS, rep * tq, LANES), F32),
            pltpu.VMEM((DSA_KV_HEADS, rep * tq, 2 * LANES), F32),
        ],
        compiler_params=_params("parallel"),
        name="dsa_attention",
    )(iq, iw, ik_lo.reshape(nc, tk, LANES), ik_hi.reshape(nc, tk, LANES), q, k.reshape(nc, tk, kvw),
      v.reshape(nc, tk, kvw))


def _retention_tables():
    h, c = RET_HEADS, RET_CHUNK
    log_g = np.log(1.0 - 2.0 ** (-5.0 - np.arange(h, dtype=np.float32))).astype(np.float32).astype(np.float64)
    idx = np.arange(c, dtype=np.float64)
    diff = idx[:, None] - idx[None, :]
    decay_in = np.where(diff[None] >= 0, np.exp(np.maximum(diff, 0.0)[None] * log_g[:, None, None]), 0.0)
    xi = np.exp((idx + 1.0)[None, :] * log_g[:, None])
    zeta = np.exp((c - 1.0 - idx)[None, :] * log_g[:, None])
    chunk_decay = np.exp(c * log_g)
    lane = np.ones((1, 1, LANES))
    return (decay_in.astype(np.float32), (xi[:, :, None] * lane).astype(np.float32),
            (zeta[:, :, None] * lane).astype(np.float32),
            (chunk_decay[:, None, None] * np.ones((1, 8, LANES))).astype(np.float32))


def _retention_kernel(q_ref, k_ref, v_ref, g_ref, cos_ref, sin_ref, din_ref, xi_ref, zeta_ref, cd_ref,
                      gng_ref, gnb_ref, o_ref, r_sc):
    n = pl.program_id(1)
    half = RET_QK_DIM // 2

    @pl.when(n == 0)
    def _():
        r_sc[...] = jnp.zeros_like(r_sc)

    scale = RET_QK_DIM ** -0.5
    xi = jnp.concatenate([xi_ref[0]] * (RET_V_DIM // LANES), axis=1)
    zeta = jnp.concatenate([zeta_ref[0]] * (RET_QK_DIM // LANES), axis=1)

    for ci in range(q_ref.shape[0] // RET_CHUNK):
        rows = slice(ci * RET_CHUNK, (ci + 1) * RET_CHUNK)
        cos = cos_ref[rows]
        sin = sin_ref[rows]

        def rope(x):
            x1 = x[:, :half]
            x2 = x[:, half:]
            return x1 * cos - x2 * sin, x1 * sin + x2 * cos

        q1, q2 = rope(q_ref[rows].astype(F32))
        k1, k2 = rope(k_ref[rows].astype(F32))
        qr = jnp.concatenate([q1, q2], axis=1).astype(BF16)
        kr = jnp.concatenate([k1 * scale, k2 * scale], axis=1)
        v = v_ref[rows].astype(BF16)

        inner = lax.dot_general(qr, kr.astype(BF16), (((1,), (1,)), ((), ())), preferred_element_type=F32)
        inner = inner * din_ref[0]
        r = r_sc[...]
        cross = jnp.dot(qr, r.astype(BF16), preferred_element_type=F32)
        o = jnp.dot(inner.astype(BF16), v, preferred_element_type=F32) + cross * xi

        kz = (kr * zeta).astype(BF16)
        upd = lax.dot_general(kz, v, (((0,), (0,)), ((), ())), preferred_element_type=F32)
        r_sc[...] = cd_ref[0][:1, :1] * r + upd

        mu = jnp.mean(o, axis=-1, keepdims=True)
        oc = o - mu
        var = jnp.mean(oc * oc, axis=-1, keepdims=True)
        y = oc * lax.rsqrt(var + 1e-5) * gng_ref[...] + gnb_ref[...]
        gate = g_ref[rows].astype(F32)
        o_ref[rows] = (gate * jax.nn.sigmoid(gate) * y).astype(o_ref.dtype)


def retention(h1, cos, sin, gn_g, gn_b, chunks_per_step=8):
    s = h1.shape[0]
    hh, dk, dv = RET_HEADS, RET_QK_DIM, RET_V_DIM
    c = RET_CHUNK * chunks_per_step
    assert s % c == 0
    din, xi, zeta, cd = (jnp.asarray(t) for t in _retention_tables())
    k_blk0 = hh * dk // dk
    v_blk0 = 2 * hh * dk // dv
    g_blk0 = v_blk0 + hh
    per_head = lambda r, w: pl.BlockSpec((1, r, w), lambda h, n: (h, 0, 0))
    return pl.pallas_call(
        _retention_kernel,
        grid=(hh, s // c),
        in_specs=[
            pl.BlockSpec((c, dk), lambda h, n: (n, h)),
            pl.BlockSpec((c, dk), lambda h, n: (n, k_blk0 + h)),
            pl.BlockSpec((c, dv), lambda h, n: (n, v_blk0 + h)),
            pl.BlockSpec((c, dv), lambda h, n: (n, g_blk0 + h)),
            pl.BlockSpec((c, dk // 2), lambda h, n: (n, 0)),
            pl.BlockSpec((c, dk // 2), lambda h, n: (n, 0)),
            per_head(RET_CHUNK, RET_CHUNK), per_head(RET_CHUNK, LANES), per_head(RET_CHUNK, LANES),
            per_head(8, LANES),
            pl.BlockSpec((1, dv), lambda h, n: (0, h)),
            pl.BlockSpec((1, dv), lambda h, n: (0, h)),
        ],
        out_specs=pl.BlockSpec((c, dv), lambda h, n: (n, h)),
        out_shape=jax.ShapeDtypeStruct((s, hh * dv), BF16),
        scratch_shapes=[pltpu.VMEM((dk, dv), F32)],
        compiler_params=_params("parallel", "arbitrary"),
        name="retention",
    )(h1, h1, h1, h1, cos, sin, din, xi, zeta, cd, gn_g.reshape(1, -1), gn_b.reshape(1, -1))


def _first_argmax(v, idx, n):
    m = jnp.max(v, axis=0, keepdims=True)
    first = jnp.min(jnp.where(v == m, idx, n), axis=0, keepdims=True)
    return m, first


def _route_tokens(x, rwt, rb):
    tm = x.shape[0]
    epg = EXPERTS_PER_GROUP
    logits = lax.dot_general(rwt, x, (((1,), (1,)), ((), ())), preferred_element_type=F32)
    scores = jax.nn.sigmoid(logits)
    biased = scores + rb
    idx = lax.broadcasted_iota(jnp.int32, (epg, tm), 0)

    group_scores = []
    for g in range(N_GROUPS):
        v = biased[g * epg:(g + 1) * epg]
        m1, first = _first_argmax(v, idx, epg)
        m2 = jnp.max(jnp.where(idx == first, -jnp.inf, v), axis=0, keepdims=True)
        group_scores.append(m1 + m2)
    gmax = group_scores[0]
    for g in range(1, N_GROUPS):
        gmax = jnp.maximum(gmax, group_scores[g])
    gsel = jnp.full((1, tm), N_GROUPS, jnp.int32)
    for g in range(N_GROUPS - 1, -1, -1):
        gsel = jnp.where(group_scores[g] == gmax, g, gsel)

    in_biased = jnp.zeros((epg, tm), F32)
    in_scores = jnp.zeros((epg, tm), F32)
    for g in range(N_GROUPS):
        pick = gsel == g
        in_biased = jnp.where(pick, biased[g * epg:(g + 1) * epg], in_biased)
        in_scores = jnp.where(pick, scores[g * epg:(g + 1) * epg], in_scores)
    _, loc1 = _first_argmax(in_biased, idx, epg)
    _, loc2 = _first_argmax(jnp.where(idx == loc1, -jnp.inf, in_biased), idx, epg)
    s1 = jnp.sum(jnp.where(idx == loc1, in_scores, 0.0), axis=0, keepdims=True)
    s2 = jnp.sum(jnp.where(idx == loc2, in_scores, 0.0), axis=0, keepdims=True)
    denom = s1 + s2
    experts = jnp.concatenate([gsel * epg + loc1, gsel * epg + loc2], axis=0)
    return experts, jnp.concatenate([s1 / denom, s2 / denom], axis=0)


def _scatter_rows_kernel(pad_ref, pos_ref, x_ref, xs_hbm, zero_sc, sem, zsem, *, tm, n_pad):
    i = pl.program_id(0)

    def zero_copy(p):
        return pltpu.make_async_copy(zero_sc, xs_hbm.at[pl.ds(pad_ref[p], 1)], zsem)

    @pl.when(i == 0)
    def _():
        zero_sc[...] = jnp.zeros_like(zero_sc)

        def start_zero(p, carry):
            zero_copy(p).start()
            return carry

        lax.fori_loop(0, n_pad, start_zero, 0)

    def start(g, carry):
        for u in range(DMA_ISSUE_UNROLL):
            r = g * DMA_ISSUE_UNROLL + u
            for k in range(TOP_K):
                pltpu.make_async_copy(x_ref.at[pl.ds(r, 1)], xs_hbm.at[pl.ds(pos_ref[0, k, r], 1)],
                                      sem).start(priority=k % 2)
        return carry

    lax.fori_loop(0, tm // DMA_ISSUE_UNROLL, start, 0)
    for k in range(TOP_K):
        pltpu.make_async_copy(x_ref, xs_hbm.at[pl.ds(0, tm)], sem).wait()

    @pl.when(i == 0)
    def _():
        def wait_zero(p, carry):
            zero_copy(p).wait()
            return carry

        lax.fori_loop(0, n_pad, wait_zero, 0)


def scatter_rows(x, pos, pad_pos, n_rows, tm=256):
    s, d = x.shape
    nt = s // tm
    n_pad = pad_pos.shape[0]
    return pl.pallas_call(
        functools.partial(_scatter_rows_kernel, tm=tm, n_pad=n_pad),
        grid_spec=pltpu.PrefetchScalarGridSpec(
            num_scalar_prefetch=1,
            grid=(nt,),
            in_specs=[
                pl.BlockSpec((1, TOP_K, tm), lambda i, pad: (i, 0, 0), memory_space=pltpu.SMEM),
                pl.BlockSpec((tm, d), lambda i, pad: (i, 0)),
            ],
            out_specs=pl.BlockSpec(memory_space=pl.ANY),
            scratch_shapes=[pltpu.VMEM((1, d), x.dtype), pltpu.SemaphoreType.DMA(()), pltpu.SemaphoreType.DMA(())],
        ),
        out_shape=jax.ShapeDtypeStruct((n_rows, d), x.dtype),
        compiler_params=_params("arbitrary"),
        name="scatter_rows",
    )(pad_pos, pos.reshape(TOP_K, nt, tm).transpose(1, 0, 2), x)


def _expert_kernel(te_ref, rs_ref, nv_ref, na_ref, dst_ref, xs_hbm, wg_ref, wu_ref, wd_ref, y_hbm,
                   wg_sc, wu_sc, wd_sc, xbuf, ybuf, lsem, ssem, *, tm):
    j = pl.program_id(0)
    n_active = na_ref[0]
    active = j < n_active
    slot = j % 2
    new_expert = (j == 0) | (te_ref[j] != te_ref[jnp.maximum(j - 1, 0)])
    chunk_sizes = [c for c in (256, 128, 64, 32, 16, 8) if c <= tm]

    def load_tile(t, dst_slot, wait):
        n8 = ((nv_ref[t] + 7) // 8) * 8
        base = rs_ref[t]
        off = jnp.int32(0)
        for c in chunk_sizes:
            take = (n8 & c) != 0

            @pl.when(take)
            def _(off=off, c=c):
                copy = pltpu.make_async_copy(
                    xs_hbm.at[pl.ds(pl.multiple_of(base + off, 8), c)],
                    xbuf.at[dst_slot, pl.ds(pl.multiple_of(off, 8), c)], lsem.at[dst_slot])
                if wait:
                    copy.wait()
                else:
                    copy.start()

            off = off + jnp.where(take, c, 0)

    def wait_scatter(src_slot, n):
        c = tm
        while c >= 1:
            @pl.when((n & c) != 0)
            def _(c=c):
                pltpu.make_async_copy(ybuf.at[src_slot, pl.ds(0, c)], y_hbm.at[pl.ds(0, c)],
                                      ssem.at[src_slot]).wait()
            c //= 2

    @pl.when(j == 0)
    def _():
        xbuf[...] = jnp.zeros_like(xbuf)
        load_tile(0, 0, False)

    @pl.when(j + 1 < n_active)
    def _():
        load_tile(j + 1, 1 - slot, False)

    @pl.when(active & new_expert)
    def _():
        wg_sc[...] = wg_ref[0].astype(BF16)
        wu_sc[...] = wu_ref[0].astype(BF16)
        wd_sc[...] = wd_ref[0].astype(BF16)

    @pl.when(active)
    def _():
        load_tile(j, slot, True)

        @pl.when(j >= 2)
        def _():
            wait_scatter(slot, nv_ref[jnp.maximum(j - 2, 0)])

        x = xbuf[slot].astype(BF16)
        hg = jnp.dot(x, wg_sc[...], preferred_element_type=F32)
        hu = jnp.dot(x, wu_sc[...], preferred_element_type=F32)
        h = hg * jax.nn.sigmoid(hg) * hu
        ybuf[slot] = jnp.dot(h.astype(BF16), wd_sc[...], preferred_element_type=F32)

        def scatter_row(r):
            pltpu.make_async_copy(ybuf.at[slot, pl.ds(r, 1)], y_hbm.at[pl.ds(dst_ref[0, 0, r], 1)],
                                  ssem.at[slot]).start()

        def scatter_group(i, carry):
            for u in range(DMA_ISSUE_UNROLL):
                scatter_row(i * DMA_ISSUE_UNROLL + u)
            return carry

        def scatter_one(r, carry):
            scatter_row(r)
            return carry

        n_groups = lax.shift_right_logical(nv_ref[j], DMA_ISSUE_UNROLL.bit_length() - 1)
        lax.fori_loop(0, n_groups, scatter_group, 0)
        lax.fori_loop(n_groups * DMA_ISSUE_UNROLL, nv_ref[j], scatter_one, 0)

        @pl.when(j == n_active - 1)
        def _():
            wait_scatter(slot, nv_ref[j])

            @pl.when(j >= 1)
            def _():
                wait_scatter(1 - slot, nv_ref[jnp.maximum(j - 1, 0)])


def routed_experts(xs, plan, n_out_rows, w_gate, w_up, w_down, tm):
    d = xs.shape[1]
    f = w_gate.shape[2]
    nt = plan.tile_expert.shape[0]

    def dst_blk(j, te, rs, nv, na):
        return (jnp.minimum(j, na[0] - 1), 0, 0)

    def weight_blk(j, te, rs, nv, na):
        return (te[j], 0, 0)

    return pl.pallas_call(
        functools.partial(_expert_kernel, tm=tm),
        grid_spec=pltpu.PrefetchScalarGridSpec(
            num_scalar_prefetch=4,
            grid=(nt,),
            in_specs=[
                pl.BlockSpec((1, 1, tm), dst_blk, memory_space=pltpu.SMEM),
                pl.BlockSpec(memory_space=pl.ANY),
                pl.BlockSpec((1, d, f), weight_blk),
                pl.BlockSpec((1, d, f), weight_blk),
                pl.BlockSpec((1, f, d), weight_blk),
            ],
            out_specs=pl.BlockSpec(memory_space=pl.ANY),
            scratch_shapes=[pltpu.VMEM((d, f), BF16), pltpu.VMEM((d, f), BF16), pltpu.VMEM((f, d), BF16),
                            pltpu.VMEM((2, tm, d), xs.dtype), pltpu.VMEM((2, tm, d), F32),
                            pltpu.SemaphoreType.DMA((2,)), pltpu.SemaphoreType.DMA((2,))],
        ),
        out_shape=jax.ShapeDtypeStruct((n_out_rows, d), F32),
        compiler_params=_params("arbitrary"),
        name="routed_experts",
    )(plan.tile_expert, plan.tile_row_start, plan.n_valid, plan.n_active, plan.dst_tiles.reshape(nt, 1, tm),
      xs, w_gate, w_up, w_down)


def _shared_kernel(x_ref, wg_ref, wu_ref, wd_ref, o_ref):
    x = x_ref[...]
    hg = jnp.dot(x, wg_ref[...], preferred_element_type=F32)
    hu = jnp.dot(x, wu_ref[...], preferred_element_type=F32)
    h = hg * jax.nn.sigmoid(hg) * hu
    o_ref[...] = jnp.dot(h.astype(BF16), wd_ref[...], preferred_element_type=F32)


def shared_expert(x_bf, wg, wu, wd, tm=512):
    s, d = x_bf.shape
    f = wg.shape[1]
    whole = lambda r, c: pl.BlockSpec((r, c), lambda i: (0, 0))
    return pl.pallas_call(
        _shared_kernel,
        grid=(s // tm,),
        in_specs=[pl.BlockSpec((tm, d), lambda i: (i, 0)), whole(d, f), whole(d, f), whole(f, d)],
        out_specs=pl.BlockSpec((tm, d), lambda i: (i, 0)),
        out_shape=jax.ShapeDtypeStruct((s, d), F32),
        compiler_params=_params("parallel"),
        name="shared_expert",
    )(x_bf, wg.astype(BF16), wu.astype(BF16), wd.astype(BF16))


def _combine_kernel(x_ref, sh_ref, gate_ref, y0_ref, y1_ref, g_ref, b_ref, o_ref, obf_ref):
    gate = gate_ref[...]
    routed = gate[:, 0:1] * y0_ref[0] + gate[:, 1:2] * y1_ref[0]
    z = DEEPNORM_ALPHA * x_ref[...] + (routed + sh_ref[...])
    out = _layer_norm(z, g_ref[...], b_ref[...])
    o_ref[...] = out
    obf_ref[...] = out.astype(BF16)


def moe_combine(x, shared, gates, y, g, b, tm=256):
    s, d = x.shape
    assert TOP_K == 2
    row = pl.BlockSpec((tm, d), lambda i: (i, 0))
    vec = pl.BlockSpec((1, d), lambda i: (0, 0))
    return pl.pallas_call(
        _combine_kernel,
        grid=(s // tm,),
        in_specs=[
            row, row,
            pl.BlockSpec((tm, TOP_K), lambda i: (i, 0)),
            pl.BlockSpec((1, tm, d), lambda i: (0, i, 0)),
            pl.BlockSpec((1, tm, d), lambda i: (1, i, 0)),
            vec, vec,
        ],
        out_specs=[row, row],
        out_shape=[jax.ShapeDtypeStruct((s, d), F32), jax.ShapeDtypeStruct((s, d), BF16)],
        compiler_params=_params("parallel"),
        name="moe_combine",
    )(x, shared, gates.T, y, y, g.reshape(1, d), b.reshape(1, d))


def _rope_tables(pos, rot_dim, theta):
    half = rot_dim // 2
    inv = (1.0 / (theta ** (np.arange(half, dtype=np.float32) / half))).astype(np.float32)
    ang = pos.astype(F32)[:, None] * inv
    return jnp.cos(ang), jnp.sin(ang)


class _RoutingPlan(NamedTuple):
    pos: jax.Array
    pad_pos: jax.Array
    n_rows: int
    tile_expert: jax.Array
    tile_row_start: jax.Array
    n_valid: jax.Array
    n_active: jax.Array
    dst_tiles: jax.Array


ROW_ALIGN = 8


def _routing_plan(experts, ranks, sizes, tm):
    k, s = experts.shape
    flat = experts.reshape(-1)
    rank = ranks.reshape(-1)
    padded = (sizes + ROW_ALIGN - 1) // ROW_ALIGN * ROW_ALIGN
    seg_end = jnp.cumsum(padded)
    seg_start = seg_end - padded
    pos = seg_start[flat] + rank
    n_rows = k * s + N_EXPERTS * (ROW_ALIGN - 1)

    i_pad = jnp.arange(ROW_ALIGN - 1, dtype=jnp.int32)
    is_pad = i_pad[None, :] < (padded - sizes)[:, None]
    spare = jnp.cumsum(jnp.logical_not(is_pad).reshape(-1).astype(jnp.int32)).reshape(is_pad.shape) - 1
    pad_pos = jnp.where(is_pad, (seg_start + sizes)[:, None] + i_pad[None, :], seg_end[-1] + spare).reshape(-1)

    tiles = (sizes + tm - 1) // tm
    tile_end = jnp.cumsum(tiles)
    n_active = tile_end[-1]
    n_tiles = k * s // tm + N_EXPERTS
    tile_ids = jnp.minimum(jnp.arange(n_tiles, dtype=jnp.int32), n_active - 1)
    tile_expert = jnp.sum((tile_end[None, :] <= tile_ids[:, None]).astype(jnp.int32), axis=1)
    local = tile_ids - (tile_end - tiles)[tile_expert]
    tile_row_start = seg_start[tile_expert] + local * tm
    n_valid = jnp.clip(sizes[tile_expert] - local * tm, 0, tm)
    slot_of_row = jnp.zeros((n_rows,), jnp.int32).at[pos].set(jnp.arange(k * s, dtype=jnp.int32))
    rows = jnp.minimum(tile_row_start[:, None] + jnp.arange(tm, dtype=jnp.int32)[None, :], n_rows - 1)
    return _RoutingPlan(pos.reshape(k, s), pad_pos.astype(jnp.int32), n_rows, tile_expert,
                        tile_row_start.astype(jnp.int32), n_valid.astype(jnp.int32),
                        n_active.reshape(1).astype(jnp.int32), slot_of_row[rows])


def _grouped_moe(x, x_bf, routing, w_gate, w_up, w_down, sh_gate, sh_up, sh_down, ln_g, ln_b, expert_tm=256):
    s, d = x.shape
    experts, gates, ranks, counts = routing
    plan = _routing_plan(experts, ranks, counts[:, 0].astype(jnp.int32), expert_tm)
    xs = scatter_rows(x, plan.pos, plan.pad_pos, plan.n_rows)
    y = routed_experts(xs, plan, TOP_K * s, w_gate, w_up, w_down, expert_tm)
    shared = shared_expert(x_bf, sh_gate, sh_up, sh_down)
    return moe_combine(x, shared, gates, y.reshape(TOP_K, s, d), ln_g, ln_b)


def _rope_lane_tables(pos, half, theta, period, offset=0, limit=None):
    cos, sin = _rope_tables(pos, 2 * half, theta)
    lane = np.arange(LANES)
    p = lane % period - offset
    in_range = np.ones(LANES, bool) if limit is None else lane < limit
    first = (p >= 0) & (p < half) & in_range
    second = (p >= half) & (p < 2 * half) & in_range
    idx = np.where(first | second, p % half, 0)
    cos_l, sin_l = cos[:, idx], sin[:, idx]
    return jnp.stack([jnp.where(first | second, cos_l, 1.0),
                      jnp.where(second, sin_l, 0.0),
                      jnp.where(first, -sin_l, 0.0)])


def _mla_dsa_mixer(x, pos, w_in, q_norm, w_uq, kv_norm, w_ukv):
    cq, ckv, krope, dq, dkv, iq, ik, iw = jnp.split(w_in, [int(c) for c in np.cumsum(SPLITS_AB)[:-1]], axis=1)
    w_in_l = jnp.concatenate([cq, ckv, dq, dkv, iq, krope, ik, iw,
                              jnp.zeros((w_in.shape[0], LANES - IDX_HEADS), w_in.dtype)], axis=1)
    w_uq3 = w_uq.reshape(MLA_Q_LORA, MLA_HEADS, MLA_NOPE + MLA_ROPE)
    w_uq_l = jnp.concatenate([w_uq3[:, :, :MLA_NOPE].reshape(MLA_Q_LORA, -1),
                              jnp.pad(w_uq3[:, :, MLA_NOPE:], ((0, 0), (0, 0), (0, LANES - MLA_ROPE))
                                      ).reshape(MLA_Q_LORA, -1)], axis=1)
    w_ukv3 = w_ukv.reshape(MLA_KV_LORA, MLA_HEADS, MLA_NOPE + MLA_V)
    w_ukv_l = jnp.concatenate([w_ukv3[:, :, :MLA_NOPE].reshape(MLA_KV_LORA, -1),
                               w_ukv3[:, :, MLA_NOPE:].reshape(MLA_KV_LORA, -1)], axis=1)

    tab_m = _rope_lane_tables(pos, MLA_ROPE // 2, ROPE_THETA, LANES)
    tab_d = _rope_lane_tables(pos, DSA_ROT // 2, ROPE_THETA, LANES)
    tab_i = _rope_lane_tables(pos, IDX_ROT // 2, ROPE_THETA, IDX_DIM)
    tab_k = _rope_lane_tables(pos, MLA_ROPE // 2, ROPE_THETA, LANES, limit=MLA_ROPE)
    tab_ik = _rope_lane_tables(pos, IDX_ROT // 2, ROPE_THETA, LANES, offset=MLA_ROPE)
    tab_ki = jnp.concatenate([(tab_k[0] * tab_ik[0])[None], tab_k[1:], tab_ik[1:]])

    h0 = matmul(x, w_in_l, F32, 1024, 512, name="l0_in_proj")
    q_cat, k_nope, v_a, k_pe, qb, kb, vb, iq_r, ik_lo, ik_hi, iw_s = l0_attention_operands(
        h0, q_norm, w_uq_l, kv_norm, w_ukv_l, (tab_m, tab_d, tab_i, tab_ki))
    out_a = mla_attention(q_cat, k_nope, k_pe, v_a)
    out_b = dsa_attention(iq_r, iw_s, ik_lo, ik_hi, qb, kb, vb)
    return out_a, out_b


def _retention_mixer(x_bf, pos, w_in, gn_g, gn_b):
    h1 = matmul(x_bf, w_in, BF16, 2048, 512, name="l1_in_proj")
    cos, sin = _rope_tables(pos, RET_QK_DIM, RET_THETA)
    return retention(h1, cos, sin, gn_g, gn_b)


def kernel(x, positions, router_w, router_b, l0_w_in, l0_mla_q_norm, l0_mla_w_uq, l0_mla_kv_norm, l0_mla_w_ukv, l0_w_out, l1_w_in, l1_ret_gn_g, l1_ret_gn_b, l1_w_out, l0_ln_mix_g, l0_ln_mix_b, l0_moe_w_gate, l0_moe_w_up, l0_moe_w_down, l0_sh_gate, l0_sh_up, l0_sh_down, l0_ln_ffn_g, l0_ln_ffn_b, l1_ln_mix_g, l1_ln_mix_b, l1_moe_w_gate, l1_moe_w_up, l1_moe_w_down, l1_sh_gate, l1_sh_up, l1_sh_down, l1_ln_ffn_g, l1_ln_ffn_b):
    assert x.shape[0] == 1
    xt = x[0]
    pos = positions[0]

    out_a, out_b = _mla_dsa_mixer(xt, pos, l0_w_in, l0_mla_q_norm, l0_mla_w_uq, l0_mla_kv_norm, l0_mla_w_ukv)
    xt, xt_bf, *routing = out_proj_ln_route((out_a, out_b), l0_w_out, xt, l0_ln_mix_g, l0_ln_mix_b,
                                            router_w, router_b, name="l0_out_proj_ln_route")
    xt, xt_bf = _grouped_moe(xt, xt_bf, routing, l0_moe_w_gate, l0_moe_w_up, l0_moe_w_down,
                             l0_sh_gate, l0_sh_up, l0_sh_down, l0_ln_ffn_g, l0_ln_ffn_b)

    ret = _retention_mixer(xt_bf, pos, l1_w_in, l1_ret_gn_g, l1_ret_gn_b)
    xt, xt_bf, *routing = out_proj_ln_route((ret,), l1_w_out, xt, l1_ln_mix_g, l1_ln_mix_b,
                                            router_w, router_b, name="l1_out_proj_ln_route")
    xt, _ = _grouped_moe(xt, xt_bf, routing, l1_moe_w_gate, l1_moe_w_up, l1_moe_w_down,
                         l1_sh_gate, l1_sh_up, l1_sh_down, l1_ln_ffn_g, l1_ln_ffn_b)
    return xt[None]
```

```python
import functools
from typing import NamedTuple

import numpy as np
import jax
import jax.numpy as jnp
from jax import lax
from jax.experimental import pallas as pl
from jax.experimental.pallas import tpu as pltpu

F32 = jnp.float32
BF16 = jnp.bfloat16

DEPTH = 2
ROPE_THETA = 500000.0
MLA_HEADS = 8
MLA_Q_LORA = 512
MLA_KV_LORA = 256
MLA_NOPE = 128
MLA_ROPE = 64
MLA_V = 128
DSA_HEADS = 8
DSA_KV_HEADS = 2
DSA_HEAD_DIM = 128
DSA_ROT = DSA_HEAD_DIM // 4
IDX_HEADS = 16
IDX_DIM = 64
IDX_ROT = IDX_DIM // 4
IDX_TOPK_MAX = 256
RET_HEADS = 8
RET_QK_DIM = 256
RET_V_DIM = 512
RET_CHUNK = 128
RET_THETA = 10000.0
N_EXPERTS = 64
N_GROUPS = 8
EXPERTS_PER_GROUP = N_EXPERTS // N_GROUPS
TOP_K = 2
DEEPNORM_ALPHA = (2.0 * DEPTH) ** 0.25

SPLITS_AB = (MLA_Q_LORA, MLA_KV_LORA, MLA_ROPE, DSA_HEADS * DSA_HEAD_DIM,
             2 * DSA_KV_HEADS * DSA_HEAD_DIM, IDX_HEADS * IDX_DIM, IDX_DIM, IDX_HEADS)

VMEM_LIMIT_BYTES = 56 * 1024 * 1024
LANES = 128
MASKED_SCORE = -1e30
LOG2_E = 1.4426950408889634
DMA_ISSUE_UNROLL = 8
INT32_MIN = -2 ** 31


def _params(*sem):
    return pltpu.CompilerParams(dimension_semantics=sem, vmem_limit_bytes=VMEM_LIMIT_BYTES)


def _mm_kernel(a_ref, b_ref, o_ref):
    o_ref[...] = jnp.dot(a_ref[...].astype(BF16), b_ref[...].astype(BF16),
                         preferred_element_type=F32).astype(o_ref.dtype)


def matmul(a, b, out_dtype, tm, tn, *, name):
    m, k = a.shape
    n = b.shape[1]
    assert b.shape[0] == k and m % tm == 0 and n % tn == 0
    return pl.pallas_call(
        _mm_kernel,
        grid=(m // tm, n // tn),
        in_specs=[pl.BlockSpec((tm, k), lambda i, j: (i, 0)), pl.BlockSpec((k, tn), lambda i, j: (0, j))],
        out_specs=pl.BlockSpec((tm, tn), lambda i, j: (i, j)),
        out_shape=jax.ShapeDtypeStruct((m, n), out_dtype),
        compiler_params=_params("parallel", "parallel"),
        name=name,
    )(a, b)


_H0_CQ = 0
_H0_CKV = _H0_CQ + MLA_Q_LORA
_H0_DQ = _H0_CKV + MLA_KV_LORA
_H0_DKV = _H0_DQ + DSA_HEADS * DSA_HEAD_DIM
_H0_IQ = _H0_DKV + 2 * DSA_KV_HEADS * DSA_HEAD_DIM
_H0_KROPE_IK = _H0_IQ + IDX_HEADS * IDX_DIM
_H0_IW = _H0_KROPE_IK + LANES
_H0_WIDTH = _H0_IW + LANES


def _rope_lanes(x, tab_ref, shifts):
    out = x * tab_ref[0]
    for i, shift in enumerate(shifts):
        out = out + pltpu.roll(x, shift, 1) * tab_ref[1 + i]
    return out


def _l0_prep_kernel(h_ref, qn_ref, kvn_ref, wuq_ref, wukv_ref, tm_ref, td_ref, ti_ref, tki_ref,
                    qcat_ref, knope_ref, va_ref, kpe_ref, qb_ref, kb_ref, vb_ref, iq_ref, iklo_ref, ikhi_ref,
                    iw_ref):
    def rms(x, g_ref):
        return (x * lax.rsqrt(jnp.mean(x * x, axis=-1, keepdims=True) + 1e-6) * g_ref[...]).astype(BF16)

    def slab(col, j=0):
        return h_ref[:, col + j * LANES:col + (j + 1) * LANES]

    half_m, half_d, half_i = MLA_ROPE // 2, DSA_ROT // 2, IDX_ROT // 2
    shifts_m = (half_m, LANES - half_m)
    shifts_d = (half_d, LANES - half_d)
    shifts_i = (half_i, LANES - half_i)

    qa = jnp.dot(rms(h_ref[:, _H0_CQ:_H0_CQ + MLA_Q_LORA], qn_ref), wuq_ref[...], preferred_element_type=F32)
    kva = jnp.dot(rms(h_ref[:, _H0_CKV:_H0_CKV + MLA_KV_LORA], kvn_ref), wukv_ref[...],
                  preferred_element_type=F32)
    n_nope = MLA_HEADS * MLA_NOPE
    knope_ref[...] = kva[:, :n_nope].astype(BF16)
    va_ref[...] = kva[:, n_nope:].astype(BF16)
    q_scale = (MLA_NOPE + MLA_ROPE) ** -0.5 * LOG2_E
    for h in range(MLA_HEADS):
        nope = qa[:, h * LANES:(h + 1) * LANES]
        pe = qa[:, n_nope + h * LANES:n_nope + (h + 1) * LANES]
        qcat_ref[:, 2 * h * LANES:(2 * h + 1) * LANES] = (nope * q_scale).astype(BF16)
        qcat_ref[:, (2 * h + 1) * LANES:(2 * h + 2) * LANES] = (_rope_lanes(pe, tm_ref, shifts_m) * q_scale).astype(BF16)

    d_scale = DSA_HEAD_DIM ** -0.5 * LOG2_E
    for h in range(DSA_HEADS):
        qb_ref[:, h * LANES:(h + 1) * LANES] = (_rope_lanes(slab(_H0_DQ, h), td_ref, shifts_d) * d_scale).astype(BF16)
    for g in range(DSA_KV_HEADS):
        kb_ref[:, g * LANES:(g + 1) * LANES] = _rope_lanes(slab(_H0_DKV, g), td_ref, shifts_d).astype(BF16)
    vb_ref[...] = h_ref[:, _H0_DKV + DSA_KV_HEADS * LANES:_H0_DKV + 2 * DSA_KV_HEADS * LANES].astype(BF16)
    for j in range(IDX_HEADS * IDX_DIM // LANES):
        iq_ref[:, j * LANES:(j + 1) * LANES] = _rope_lanes(slab(_H0_IQ, j), ti_ref, shifts_i).astype(BF16)

    ki = _rope_lanes(slab(_H0_KROPE_IK), tki_ref, shifts_m + shifts_i)
    lane = lax.broadcasted_iota(jnp.int32, ki.shape, 1)
    kpe_ref[...] = jnp.where(lane < MLA_ROPE, ki, 0.0).astype(BF16)
    ik_hi = jnp.where(lane >= MLA_ROPE, ki, 0.0)
    ikhi_ref[...] = ik_hi.astype(BF16)
    iklo_ref[...] = pltpu.roll(ik_hi, LANES - MLA_ROPE, 1).astype(BF16)
    iw_ref[...] = h_ref[:, _H0_IW:_H0_IW + IDX_HEADS] * (IDX_HEADS ** -0.5 * IDX_DIM ** -0.5)


def l0_attention_operands(h0, q_norm, w_uq, kv_norm, w_ukv, tables, tm=256):
    s = h0.shape[0]
    bf = lambda n: jax.ShapeDtypeStruct((s, n), BF16)
    row = lambda n: pl.BlockSpec((tm, n), lambda i: (i, 0))
    whole = lambda a: pl.BlockSpec(a.shape, lambda i: (0,) * a.ndim)
    tab = lambda t: pl.BlockSpec((t.shape[0], tm, LANES), lambda i: (0, i, 0))
    widths = [2 * MLA_HEADS * LANES, MLA_HEADS * MLA_NOPE, MLA_HEADS * MLA_V, LANES, DSA_HEADS * DSA_HEAD_DIM,
              DSA_KV_HEADS * DSA_HEAD_DIM, DSA_KV_HEADS * DSA_HEAD_DIM, IDX_HEADS * IDX_DIM, LANES, LANES]
    w_uq = w_uq.astype(BF16)
    w_ukv = w_ukv.astype(BF16)
    q_norm = q_norm.reshape(1, -1)
    kv_norm = kv_norm.reshape(1, -1)
    return pl.pallas_call(
        _l0_prep_kernel,
        grid=(s // tm,),
        in_specs=[row(_H0_WIDTH), whole(q_norm), whole(kv_norm), whole(w_uq), whole(w_ukv)] + [tab(t) for t in tables],
        out_specs=[row(n) for n in widths] + [row(IDX_HEADS)],
        out_shape=[bf(n) for n in widths] + [jax.ShapeDtypeStruct((s, IDX_HEADS), F32)],
        compiler_params=_params("parallel"),
        name="l0_attention_operands",
    )(h0, q_norm, kv_norm, w_uq, w_ukv, *tables)


def _layer_norm(z, g, b):
    mu = jnp.mean(z, axis=-1, keepdims=True)
    zc = z - mu
    var = jnp.mean(zc * zc, axis=-1, keepdims=True)
    return zc * lax.rsqrt(var + 1e-5) * g + b


def _proj_ln_route_kernel(*refs, n_parts):
    a_refs = refs[:n_parts]
    (w_ref, x_ref, g_ref, b_ref, rwt_ref, rb_ref,
     o_ref, obf_ref, e_ref, gate_ref, rank_ref, count_ref, run_sc) = refs[n_parts:]
    mix = None
    k0 = 0
    for a_ref in a_refs:
        k = a_ref.shape[1]
        part = jnp.dot(a_ref[...], w_ref[k0:k0 + k, :], preferred_element_type=F32)
        mix = part if mix is None else mix + part
        k0 += k
    out = _layer_norm(DEEPNORM_ALPHA * x_ref[...] + mix, g_ref[...], b_ref[...])
    out_bf = out.astype(BF16)
    o_ref[...] = out
    obf_ref[...] = out_bf
    experts, gate_ref[...] = _route_tokens(out_bf, rwt_ref[...], rb_ref[...])
    e_ref[...] = experts

    @pl.when(pl.program_id(0) == 0)
    def _():
        run_sc[...] = jnp.zeros_like(run_sc)

    tm = out.shape[0]
    expert_id = lax.broadcasted_iota(jnp.int32, (N_EXPERTS, tm), 0)
    before = (lax.broadcasted_iota(jnp.int32, (tm, tm), 0) < lax.broadcasted_iota(jnp.int32, (tm, tm), 1))
    before = jnp.where(before, 1.0, 0.0).astype(BF16)
    base = run_sc[...][:, :1]
    ranks = []
    for k in range(TOP_K):
        onehot = expert_id == experts[k:k + 1, :]
        prefix = jnp.dot(jnp.where(onehot, 1.0, 0.0).astype(BF16), before, preferred_element_type=F32)
        ranks.append(jnp.sum(jnp.where(onehot, prefix + base, 0.0), axis=0, keepdims=True))
        base = base + jnp.sum(jnp.where(onehot, 1.0, 0.0), axis=1, keepdims=True)
    rank_ref[...] = jnp.concatenate(ranks, axis=0).astype(jnp.int32)
    run_sc[...] = jnp.broadcast_to(base, run_sc.shape)
    count_ref[...] = run_sc[...]


def out_proj_ln_route(a_parts, w, x, g, b, router_w, router_b, *, name, tm=512):
    s, d = x.shape
    w = w.astype(BF16)
    assert sum(a.shape[1] for a in a_parts) == w.shape[0] and w.shape[1] == d
    row = pl.BlockSpec((tm, d), lambda i: (i, 0))
    vec = pl.BlockSpec((1, d), lambda i: (0, 0))
    slots = pl.BlockSpec((TOP_K, tm), lambda i: (0, i))
    resident = lambda shape: pl.BlockSpec(shape, lambda i: (0, 0), pipeline_mode=pl.Buffered(1))
    return pl.pallas_call(
        functools.partial(_proj_ln_route_kernel, n_parts=len(a_parts)),
        grid=(s // tm,),
        in_specs=[pl.BlockSpec((tm, a.shape[1]), lambda i: (i, 0)) for a in a_parts]
        + [resident(w.shape), row, vec, vec, resident((N_EXPERTS, d)), resident((N_EXPERTS, 1))],
        out_specs=[row, row, slots, slots, slots, pl.BlockSpec((N_EXPERTS, LANES), lambda i: (0, 0))],
        out_shape=[jax.ShapeDtypeStruct((s, d), F32), jax.ShapeDtypeStruct((s, d), BF16),
                   jax.ShapeDtypeStruct((TOP_K, s), jnp.int32), jax.ShapeDtypeStruct((TOP_K, s), F32),
                   jax.ShapeDtypeStruct((TOP_K, s), jnp.int32), jax.ShapeDtypeStruct((N_EXPERTS, LANES), F32)],
        scratch_shapes=[pltpu.VMEM((N_EXPERTS, LANES), F32)],
        compiler_params=_params("arbitrary"),
        name=name,
    )(*a_parts, w, x, g.reshape(1, d), b.reshape(1, d), router_w.T.astype(BF16), router_b.reshape(N_EXPERTS, 1))


def _lane_tile(x, n):
    return jnp.tile(x, (1, n))


def _softmax_step(s, v_ext, m_ref, acc_ref):
    tk = s.shape[1]
    m_prev = m_ref[...]
    m_new = jnp.maximum(m_prev, jnp.max(s, axis=1)[:, None])
    p = jnp.exp2(s - _lane_tile(m_new, tk // LANES))
    alpha = jnp.exp2(m_prev - m_new)
    pv = jnp.dot(p.astype(BF16), v_ext, preferred_element_type=F32)
    acc_ref[...] = _lane_tile(alpha, 2) * acc_ref[...] + pv
    m_ref[...] = m_new


def _mla_kernel(q_ref, kn_ref, kp_ref, v_ref, o_ref, m_sc, acc_sc, *, tq, tk, hp):
    qi = pl.program_id(1)
    m_sc[...] = jnp.full_like(m_sc, MASKED_SCORE)
    acc_sc[...] = jnp.zeros_like(acc_sc)
    ones = jnp.ones((tk, LANES), BF16)
    per_block = tq // tk

    def attend(c, diagonal_offset):
        kp = kp_ref[c]
        kn = kn_ref[c]
        v = v_ref[c]
        for j in range(hp):
            q = q_ref[:, j * 2 * LANES:(j + 1) * 2 * LANES]
            k = jnp.concatenate([kn[:, j * LANES:(j + 1) * LANES], kp], axis=1)
            s = lax.dot_general(q, k, (((1,), (1,)), ((), ())), preferred_element_type=F32)
            if diagonal_offset is not None:
                row = lax.broadcasted_iota(jnp.int32, (tq, tk), 0)
                col = diagonal_offset * tk + lax.broadcasted_iota(jnp.int32, (tq, tk), 1)
                s = jnp.where(col <= row, s, MASKED_SCORE)
            v_ext = jnp.concatenate([v[:, j * LANES:(j + 1) * LANES], ones], axis=1)
            _softmax_step(s, v_ext, m_sc.at[j], acc_sc.at[j])

    def below_diagonal(c, carry):
        attend(c, None)
        return carry

    lax.fori_loop(0, qi * per_block, below_diagonal, 0)
    for d in range(per_block):
        attend(qi * per_block + d, d)
    for j in range(hp):
        acc = acc_sc[j]
        o_ref[:, j * LANES:(j + 1) * LANES] = (acc[:, :LANES] / acc[:, LANES:]).astype(o_ref.dtype)


def mla_attention(q, k_nope, k_pe, v, tq=1024, tk=512, hp=2):
    s = q.shape[0]
    h = MLA_HEADS
    nc = s // tk
    assert tq % tk == 0 and h % hp == 0
    per_head = pl.BlockSpec((nc, tk, hp * LANES), lambda hh, qi: (0, 0, hh))
    return pl.pallas_call(
        functools.partial(_mla_kernel, tq=tq, tk=tk, hp=hp),
        grid=(h // hp, s // tq),
        in_specs=[
            pl.BlockSpec((tq, hp * 2 * LANES), lambda hh, qi: (qi, hh)),
            per_head,
            pl.BlockSpec((nc, tk, LANES), lambda hh, qi: (0, 0, 0)),
            per_head,
        ],
        out_specs=pl.BlockSpec((tq, hp * LANES), lambda hh, qi: (qi, hh)),
        out_shape=jax.ShapeDtypeStruct((s, h * MLA_V), BF16),
        scratch_shapes=[pltpu.VMEM((hp, tq, LANES), F32), pltpu.VMEM((hp, tq, 2 * LANES), F32)],
        compiler_params=_params("parallel", "arbitrary"),
        name="mla_attention",
    )(q, k_nope.reshape(nc, tk, h * MLA_NOPE), k_pe.reshape(nc, tk, LANES), v.reshape(nc, tk, h * MLA_V))


def _sortable_key(x):
    bits = pltpu.bitcast(x, jnp.int32)
    return bits ^ ((bits >> 31) & 0x7FFFFFFF)


def _dsa_kernel(iq_ref, iw_ref, iklo_ref, ikhi_ref, q_ref, k_ref, v_ref, o_ref, key_sc, wb_sc, m_sc, acc_sc,
                *, tq, tk, n_sel):
    i = pl.program_id(0)
    n_chunks = ((i + 1) * tq + tk - 1) // tk
    rep = DSA_HEADS // DSA_KV_HEADS
    lane_tiles = tk // LANES
    heads_per_block = LANES // IDX_DIM

    assert heads_per_block == 2
    iq_blocks = jnp.concatenate(
        [iq_ref[:, j * LANES:(j + 1) * LANES] for j in range(IDX_HEADS // heads_per_block)], axis=0)
    iw = iw_ref[...]
    for h in range(IDX_HEADS):
        wb_sc[h] = jnp.broadcast_to(iw[:, h:h + 1], (tq, LANES))

    def index_chunk(c, carry):
        ik = jnp.concatenate([iklo_ref[c], ikhi_ref[c]], axis=0)
        d = lax.dot_general(iq_blocks, ik, (((1,), (1,)), ((), ())), preferred_element_type=F32)
        d = jnp.maximum(d, 0.0)
        score = None
        for h in range(IDX_HEADS):
            j, part = divmod(h, heads_per_block)
            term = _lane_tile(wb_sc[h], lane_tiles) * d[j * tq:(j + 1) * tq, part * tk:(part + 1) * tk]
            score = term if score is None else score + term
        key_sc[c] = _sortable_key(score)
        return carry

    lax.fori_loop(0, n_chunks, index_chunk, 0)
    last = n_chunks - 1
    row = i * tq + lax.broadcasted_iota(jnp.int32, (tq, tk), 0)
    col = last * tk + lax.broadcasted_iota(jnp.int32, (tq, tk), 1)
    key_sc[last] = jnp.where(col <= row, key_sc[last], INT32_MIN)

    def search_bit(b, thr):
        cand = thr + lax.shift_left(jnp.int32(1), 31 - b)

        def count_chunk(c, cnt):
            key = key_sc[c]
            for j in range(lane_tiles):
                cnt = cnt + jnp.where(key[:, j * LANES:(j + 1) * LANES] >= cand, 1.0, 0.0)
            return cnt

        cnt = lax.fori_loop(0, n_chunks, count_chunk, jnp.zeros((tq, LANES), F32))
        return jnp.where(jnp.sum(cnt, axis=1)[:, None] >= n_sel, cand, thr)

    thr = lax.fori_loop(0, 32, search_bit, jnp.full((tq, LANES), INT32_MIN, jnp.int32))
    thr = _lane_tile(jnp.maximum(thr, INT32_MIN + 1), lane_tiles)

    m_sc[...] = jnp.full_like(m_sc, MASKED_SCORE)
    acc_sc[...] = jnp.zeros_like(acc_sc)
    ones = jnp.ones((tk, LANES), BF16)

    def attend_chunk(c, carry):
        bias = jnp.where(key_sc[c] >= thr, 0.0, MASKED_SCORE)
        kc = k_ref[c]
        vc = v_ref[c]
        for g in range(DSA_KV_HEADS):
            cols = slice(g * DSA_HEAD_DIM, (g + 1) * DSA_HEAD_DIM)
            qg = jnp.concatenate(
                [q_ref[:, (g * rep + r) * DSA_HEAD_DIM:(g * rep + r + 1) * DSA_HEAD_DIM] for r in range(rep)], axis=0)
            s = lax.dot_general(qg, kc[:, cols], (((1,), (1,)), ((), ())), preferred_element_type=F32)
            s = (s.reshape(rep, tq, tk) + bias[None]).reshape(rep * tq, tk)
            _softmax_step(s, jnp.concatenate([vc[:, cols], ones], axis=1), m_sc.at[g], acc_sc.at[g])
        return carry

    lax.fori_loop(0, n_chunks, attend_chunk, 0)
    for g in range(DSA_KV_HEADS):
        acc = acc_sc[g]
        out = acc[:, :LANES] / acc[:, LANES:]
        for r in range(rep):
            hh = g * rep + r
            o_ref[:, hh * DSA_HEAD_DIM:(hh + 1) * DSA_HEAD_DIM] = out[r * tq:(r + 1) * tq].astype(o_ref.dtype)


def dsa_attention(iq, iw, ik_lo, ik_hi, q, k, v, tq=128, tk=512):
    s = q.shape[0]
    n_sel = min(IDX_TOPK_MAX, s // 4)
    nc = s // tk
    assert tk >= n_sel and s % tk == 0 and tk % tq == 0 and DSA_HEAD_DIM == LANES
    rep = DSA_HEADS // DSA_KV_HEADS
    kvw = DSA_KV_HEADS * DSA_HEAD_DIM
    whole3 = lambda i: (0, 0, 0)
    return pl.pallas_call(
        functools.partial(_dsa_kernel, tq=tq, tk=tk, n_sel=n_sel),
        grid=(s // tq,),
        in_specs=[
            pl.BlockSpec((tq, IDX_HEADS * IDX_DIM), lambda i: (i, 0)),
            pl.BlockSpec((tq, IDX_HEADS), lambda i: (i, 0)),
            pl.BlockSpec((nc, tk, LANES), whole3),
            pl.BlockSpec((nc, tk, LANES), whole3),
            pl.BlockSpec((tq, DSA_HEADS * DSA_HEAD_DIM), lambda i: (i, 0)),
            pl.BlockSpec((nc, tk, kvw), whole3),
            pl.BlockSpec((nc, tk, kvw), whole3),
        ],
        out_specs=pl.BlockSpec((tq, DSA_HEADS * DSA_HEAD_DIM), lambda i: (i, 0)),
        out_shape=jax.ShapeDtypeStruct((s, DSA_HEADS * DSA_HEAD_DIM), BF16),
        scratch_shapes=[
            pltpu.VMEM((nc, tq, tk), jnp.int32),
            pltpu.VMEM((IDX_HEADS, tq, LANES), F32),
            pltpu.VMEM((DSA_KV_HEADS, rep * tq, LANES), F32),
            pltpu.VMEM((DSA_KV_HEADS, rep * tq, 2 * LANES), F32),
        ],
        compiler_params=_params("parallel"),
        name="dsa_attention",
    )(iq, iw, ik_lo.reshape(nc, tk, LANES), ik_hi.reshape(nc, tk, LANES), q, k.reshape(nc, tk, kvw),
      v.reshape(nc, tk, kvw))


def _retention_tables():
    h, c = RET_HEADS, RET_CHUNK
    log_g = np.log(1.0 - 2.0 ** (-5.0 - np.arange(h, dtype=np.float32))).astype(np.float32).astype(np.float64)
    idx = np.arange(c, dtype=np.float64)
    diff = idx[:, None] - idx[None, :]
    decay_in = np.where(diff[None] >= 0, np.exp(np.maximum(diff, 0.0)[None] * log_g[:, None, None]), 0.0)
    xi = np.exp((idx + 1.0)[None, :] * log_g[:, None])
    zeta = np.exp((c - 1.0 - idx)[None, :] * log_g[:, None])
    chunk_decay = np.exp(c * log_g)
    lane = np.ones((1, 1, LANES))
    return (decay_in.astype(np.float32), (xi[:, :, None] * lane).astype(np.float32),
            (zeta[:, :, None] * lane).astype(np.float32),
            (chunk_decay[:, None, None] * np.ones((1, 8, LANES))).astype(np.float32))


def _retention_kernel(q_ref, k_ref, v_ref, g_ref, cos_ref, sin_ref, din_ref, xi_ref, zeta_ref, cd_ref,
                      gng_ref, gnb_ref, o_ref, r_sc):
    n = pl.program_id(1)
    half = RET_QK_DIM // 2

    @pl.when(n == 0)
    def _():
        r_sc[...] = jnp.zeros_like(r_sc)

    scale = RET_QK_DIM ** -0.5
    xi = jnp.concatenate([xi_ref[0]] * (RET_V_DIM // LANES), axis=1)
    zeta = jnp.concatenate([zeta_ref[0]] * (RET_QK_DIM // LANES), axis=1)

    for ci in range(q_ref.shape[0] // RET_CHUNK):
        rows = slice(ci * RET_CHUNK, (ci + 1) * RET_CHUNK)
        cos = cos_ref[rows]
        sin = sin_ref[rows]

        def rope(x):
            x1 = x[:, :half]
            x2 = x[:, half:]
            return x1 * cos - x2 * sin, x1 * sin + x2 * cos

        q1, q2 = rope(q_ref[rows].astype(F32))
        k1, k2 = rope(k_ref[rows].astype(F32))
        qr = jnp.concatenate([q1, q2], axis=1).astype(BF16)
        kr = jnp.concatenate([k1 * scale, k2 * scale], axis=1)
        v = v_ref[rows].astype(BF16)

        inner = lax.dot_general(qr, kr.astype(BF16), (((1,), (1,)), ((), ())), preferred_element_type=F32)
        inner = inner * din_ref[0]
        r = r_sc[...]
        cross = jnp.dot(qr, r.astype(BF16), preferred_element_type=F32)
        o = jnp.dot(inner.astype(BF16), v, preferred_element_type=F32) + cross * xi

        kz = (kr * zeta).astype(BF16)
        upd = lax.dot_general(kz, v, (((0,), (0,)), ((), ())), preferred_element_type=F32)
        r_sc[...] = cd_ref[0][:1, :1] * r + upd

        mu = jnp.mean(o, axis=-1, keepdims=True)
        oc = o - mu
        var = jnp.mean(oc * oc, axis=-1, keepdims=True)
        y = oc * lax.rsqrt(var + 1e-5) * gng_ref[...] + gnb_ref[...]
        gate = g_ref[rows].astype(F32)
        o_ref[rows] = (gate * jax.nn.sigmoid(gate) * y).astype(o_ref.dtype)


def retention(h1, cos, sin, gn_g, gn_b, chunks_per_step=8):
    s = h1.shape[0]
    hh, dk, dv = RET_HEADS, RET_QK_DIM, RET_V_DIM
    c = RET_CHUNK * chunks_per_step
    assert s % c == 0
    din, xi, zeta, cd = (jnp.asarray(t) for t in _retention_tables())
    k_blk0 = hh * dk // dk
    v_blk0 = 2 * hh * dk // dv
    g_blk0 = v_blk0 + hh
    per_head = lambda r, w: pl.BlockSpec((1, r, w), lambda h, n: (h, 0, 0))
    return pl.pallas_call(
        _retention_kernel,
        grid=(hh, s // c),
        in_specs=[
            pl.BlockSpec((c, dk), lambda h, n: (n, h)),
            pl.BlockSpec((c, dk), lambda h, n: (n, k_blk0 + h)),
            pl.BlockSpec((c, dv), lambda h, n: (n, v_blk0 + h)),
            pl.BlockSpec((c, dv), lambda h, n: (n, g_blk0 + h)),
            pl.BlockSpec((c, dk // 2), lambda h, n: (n, 0)),
            pl.BlockSpec((c, dk // 2), lambda h, n: (n, 0)),
            per_head(RET_CHUNK, RET_CHUNK), per_head(RET_CHUNK, LANES), per_head(RET_CHUNK, LANES),
            per_head(8, LANES),
            pl.BlockSpec((1, dv), lambda h, n: (0, h)),
            pl.BlockSpec((1, dv), lambda h, n: (0, h)),
        ],
        out_specs=pl.BlockSpec((c, dv), lambda h, n: (n, h)),
        out_shape=jax.ShapeDtypeStruct((s, hh * dv), BF16),
        scratch_shapes=[pltpu.VMEM((dk, dv), F32)],
        compiler_params=_params("parallel", "arbitrary"),
        name="retention",
    )(h1, h1, h1, h1, cos, sin, din, xi, zeta, cd, gn_g.reshape(1, -1), gn_b.reshape(1, -1))


def _first_argmax(v, idx, n):
    m = jnp.max(v, axis=0, keepdims=True)
    first = jnp.min(jnp.where(v == m, idx, n), axis=0, keepdims=True)
    return m, first


def _route_tokens(x, rwt, rb):
    tm = x.shape[0]
    epg = EXPERTS_PER_GROUP
    logits = lax.dot_general(rwt, x, (((1,), (1,)), ((), ())), preferred_element_type=F32)
    scores = jax.nn.sigmoid(logits)
    biased = scores + rb
    idx = lax.broadcasted_iota(jnp.int32, (epg, tm), 0)

    group_scores = []
    for g in range(N_GROUPS):
        v = biased[g * epg:(g + 1) * epg]
        m1, first = _first_argmax(v, idx, epg)
        m2 = jnp.max(jnp.where(idx == first, -jnp.inf, v), axis=0, keepdims=True)
        group_scores.append(m1 + m2)
    gmax = group_scores[0]
    for g in range(1, N_GROUPS):
        gmax = jnp.maximum(gmax, group_scores[g])
    gsel = jnp.full((1, tm), N_GROUPS, jnp.int32)
    for g in range(N_GROUPS - 1, -1, -1):
        gsel = jnp.where(group_scores[g] == gmax, g, gsel)

    in_biased = jnp.zeros((epg, tm), F32)
    in_scores = jnp.zeros((epg, tm), F32)
    for g in range(N_GROUPS):
        pick = gsel == g
        in_biased = jnp.where(pick, biased[g * epg:(g + 1) * epg], in_biased)
        in_scores = jnp.where(pick, scores[g * epg:(g + 1) * epg], in_scores)
    _, loc1 = _first_argmax(in_biased, idx, epg)
    _, loc2 = _first_argmax(jnp.where(idx == loc1, -jnp.inf, in_biased), idx, epg)
    s1 = jnp.sum(jnp.where(idx == loc1, in_scores, 0.0), axis=0, keepdims=True)
    s2 = jnp.sum(jnp.where(idx == loc2, in_scores, 0.0), axis=0, keepdims=True)
    denom = s1 + s2
    experts = jnp.concatenate([gsel * epg + loc1, gsel * epg + loc2], axis=0)
    return experts, jnp.concatenate([s1 / denom, s2 / denom], axis=0)


def _scatter_rows_kernel(pad_ref, pos_ref, x_ref, xs_hbm, zero_sc, sem, zsem, *, tm, n_pad):
    i = pl.program_id(0)

    def zero_copy(p):
        return pltpu.make_async_copy(zero_sc, xs_hbm.at[pl.ds(pad_ref[p], 1)], zsem)

    @pl.when(i == 0)
    def _():
        zero_sc[...] = jnp.zeros_like(zero_sc)

        def start_zero(p, carry):
            zero_copy(p).start()
            return carry

        lax.fori_loop(0, n_pad, start_zero, 0)

    def start(g, carry):
        for u in range(DMA_ISSUE_UNROLL):
            r = g * DMA_ISSUE_UNROLL + u
            for k in range(TOP_K):
                pltpu.make_async_copy(x_ref.at[pl.ds(r, 1)], xs_hbm.at[pl.ds(pos_ref[0, k, r], 1)],
                                      sem).start(priority=k % 2)
        return carry

    lax.fori_loop(0, tm // DMA_ISSUE_UNROLL, start, 0)
    for k in range(TOP_K):
        pltpu.make_async_copy(x_ref, xs_hbm.at[pl.ds(0, tm)], sem).wait()

    @pl.when(i == 0)
    def _():
        def wait_zero(p, carry):
            zero_copy(p).wait()
            return carry

        lax.fori_loop(0, n_pad, wait_zero, 0)


def scatter_rows(x, pos, pad_pos, n_rows, tm=512):
    s, d = x.shape
    nt = s // tm
    n_pad = pad_pos.shape[0]
    return pl.pallas_call(
        functools.partial(_scatter_rows_kernel, tm=tm, n_pad=n_pad),
        grid_spec=pltpu.PrefetchScalarGridSpec(
            num_scalar_prefetch=1,
            grid=(nt,),
            in_specs=[
                pl.BlockSpec((1, TOP_K, tm), lambda i, pad: (i, 0, 0), memory_space=pltpu.SMEM),
                pl.BlockSpec((tm, d), lambda i, pad: (i, 0)),
            ],
            out_specs=pl.BlockSpec(memory_space=pl.ANY),
            scratch_shapes=[pltpu.VMEM((1, d), x.dtype), pltpu.SemaphoreType.DMA(()), pltpu.SemaphoreType.DMA(())],
        ),
        out_shape=jax.ShapeDtypeStruct((n_rows, d), x.dtype),
        compiler_params=_params("arbitrary"),
        name="scatter_rows",
    )(pad_pos, pos.reshape(TOP_K, nt, tm).transpose(1, 0, 2), x)


def _expert_kernel(te_ref, rs_ref, nv_ref, na_ref, dst_ref, xs_hbm, wg_ref, wu_ref, wd_ref, y_hbm,
                   wg_sc, wu_sc, wd_sc, xbuf, ybuf, lsem, ssem, *, tm):
    j = pl.program_id(0)
    n_active = na_ref[0]
    active = j < n_active
    slot = j % 2
    new_expert = (j == 0) | (te_ref[j] != te_ref[jnp.maximum(j - 1, 0)])
    chunk_sizes = [c for c in (256, 128, 64, 32, 16, 8) if c <= tm]

    def load_tile(t, dst_slot, wait):
        n8 = ((nv_ref[t] + 7) // 8) * 8
        base = rs_ref[t]
        off = jnp.int32(0)
        for c in chunk_sizes:
            take = (n8 & c) != 0

            @pl.when(take)
            def _(off=off, c=c):
                copy = pltpu.make_async_copy(
                    xs_hbm.at[pl.ds(pl.multiple_of(base + off, 8), c)],
                    xbuf.at[dst_slot, pl.ds(pl.multiple_of(off, 8), c)], lsem.at[dst_slot])
                if wait:
                    copy.wait()
                else:
                    copy.start()

            off = off + jnp.where(take, c, 0)

    def wait_scatter(src_slot, n):
        c = tm
        while c >= 1:
            @pl.when((n & c) != 0)
            def _(c=c):
                pltpu.make_async_copy(ybuf.at[src_slot, pl.ds(0, c)], y_hbm.at[pl.ds(0, c)],
                                      ssem.at[src_slot]).wait()
            c //= 2

    @pl.when(j == 0)
    def _():
        xbuf[...] = jnp.zeros_like(xbuf)
        load_tile(0, 0, False)

    @pl.when(j + 1 < n_active)
    def _():
        load_tile(j + 1, 1 - slot, False)

    @pl.when(active & new_expert)
    def _():
        wg_sc[...] = wg_ref[0].astype(BF16)
        wu_sc[...] = wu_ref[0].astype(BF16)
        wd_sc[...] = wd_ref[0].astype(BF16)

    @pl.when(active)
    def _():
        load_tile(j, slot, True)

        @pl.when(j >= 2)
        def _():
            wait_scatter(slot, nv_ref[jnp.maximum(j - 2, 0)])

        x = xbuf[slot].astype(BF16)
        hg = jnp.dot(x, wg_sc[...], preferred_element_type=F32)
        hu = jnp.dot(x, wu_sc[...], preferred_element_type=F32)
        h = hg * jax.nn.sigmoid(hg) * hu
        ybuf[slot] = jnp.dot(h.astype(BF16), wd_sc[...], preferred_element_type=F32)

        def scatter_row(r):
            pltpu.make_async_copy(ybuf.at[slot, pl.ds(r, 1)], y_hbm.at[pl.ds(dst_ref[0, 0, r], 1)],
                                  ssem.at[slot]).start()

        def scatter_group(i, carry):
            for u in range(DMA_ISSUE_UNROLL):
                scatter_row(i * DMA_ISSUE_UNROLL + u)
            return carry

        def scatter_one(r, carry):
            scatter_row(r)
            return carry

        n_groups = lax.shift_right_logical(nv_ref[j], DMA_ISSUE_UNROLL.bit_length() - 1)
        lax.fori_loop(0, n_groups, scatter_group, 0)
        lax.fori_loop(n_groups * DMA_ISSUE_UNROLL, nv_ref[j], scatter_one, 0)

        @pl.when(j == n_active - 1)
        def _():
            wait_scatter(slot, nv_ref[j])

            @pl.when(j >= 1)
            def _():
                wait_scatter(1 - slot, nv_ref[jnp.maximum(j - 1, 0)])


def routed_experts(xs, plan, n_out_rows, w_gate, w_up, w_down, tm):
    d = xs.shape[1]
    f = w_gate.shape[2]
    nt = plan.tile_expert.shape[0]

    def dst_blk(j, te, rs, nv, na):
        return (jnp.minimum(j, na[0] - 1), 0, 0)

    def weight_blk(j, te, rs, nv, na):
        return (te[j], 0, 0)

    return pl.pallas_call(
        functools.partial(_expert_kernel, tm=tm),
        grid_spec=pltpu.PrefetchScalarGridSpec(
            num_scalar_prefetch=4,
            grid=(nt,),
            in_specs=[
                pl.BlockSpec((1, 1, tm), dst_blk, memory_space=pltpu.SMEM),
                pl.BlockSpec(memory_space=pl.ANY),
                pl.BlockSpec((1, d, f), weight_blk),
                pl.BlockSpec((1, d, f), weight_blk),
                pl.BlockSpec((1, f, d), weight_blk),
            ],
            out_specs=pl.BlockSpec(memory_space=pl.ANY),
            scratch_shapes=[pltpu.VMEM((d, f), BF16), pltpu.VMEM((d, f), BF16), pltpu.VMEM((f, d), BF16),
                            pltpu.VMEM((2, tm, d), xs.dtype), pltpu.VMEM((2, tm, d), F32),
                            pltpu.SemaphoreType.DMA((2,)), pltpu.SemaphoreType.DMA((2,))],
        ),
        out_shape=jax.ShapeDtypeStruct((n_out_rows, d), F32),
        compiler_params=_params("arbitrary"),
        name="routed_experts",
    )(plan.tile_expert, plan.tile_row_start, plan.n_valid, plan.n_active, plan.dst_tiles.reshape(nt, 1, tm),
      xs, w_gate, w_up, w_down)


def _shared_kernel(x_ref, wg_ref, wu_ref, wd_ref, o_ref):
    x = x_ref[...]
    hg = jnp.dot(x, wg_ref[...], preferred_element_type=F32)
    hu = jnp.dot(x, wu_ref[...], preferred_element_type=F32)
    h = hg * jax.nn.sigmoid(hg) * hu
    o_ref[...] = jnp.dot(h.astype(BF16), wd_ref[...], preferred_element_type=F32)


def shared_expert(x_bf, wg, wu, wd, tm=512):
    s, d = x_bf.shape
    f = wg.shape[1]
    whole = lambda r, c: pl.BlockSpec((r, c), lambda i: (0, 0))
    return pl.pallas_call(
        _shared_kernel,
        grid=(s // tm,),
        in_specs=[pl.BlockSpec((tm, d), lambda i: (i, 0)), whole(d, f), whole(d, f), whole(f, d)],
        out_specs=pl.BlockSpec((tm, d), lambda i: (i, 0)),
        out_shape=jax.ShapeDtypeStruct((s, d), F32),
        compiler_params=_params("parallel"),
        name="shared_expert",
    )(x_bf, wg.astype(BF16), wu.astype(BF16), wd.astype(BF16))


def _combine_kernel(x_ref, sh_ref, gate_ref, y0_ref, y1_ref, g_ref, b_ref, o_ref, obf_ref):
    gate = gate_ref[...]
    routed = gate[:, 0:1] * y0_ref[0] + gate[:, 1:2] * y1_ref[0]
    z = DEEPNORM_ALPHA * x_ref[...] + (routed + sh_ref[...])
    out = _layer_norm(z, g_ref[...], b_ref[...])
    o_ref[...] = out
    obf_ref[...] = out.astype(BF16)


def moe_combine(x, shared, gates, y, g, b, tm=256):
    s, d = x.shape
    assert TOP_K == 2
    row = pl.BlockSpec((tm, d), lambda i: (i, 0))
    vec = pl.BlockSpec((1, d), lambda i: (0, 0))
    return pl.pallas_call(
        _combine_kernel,
        grid=(s // tm,),
        in_specs=[
            row, row,
            pl.BlockSpec((tm, TOP_K), lambda i: (i, 0)),
            pl.BlockSpec((1, tm, d), lambda i: (0, i, 0)),
            pl.BlockSpec((1, tm, d), lambda i: (1, i, 0)),
            vec, vec,
        ],
        out_specs=[row, row],
        out_shape=[jax.ShapeDtypeStruct((s, d), F32), jax.ShapeDtypeStruct((s, d), BF16)],
        compiler_params=_params("parallel"),
        name="moe_combine",
    )(x, shared, gates.T, y, y, g.reshape(1, d), b.reshape(1, d))


def _rope_tables(pos, rot_dim, theta):
    half = rot_dim // 2
    inv = (1.0 / (theta ** (np.arange(half, dtype=np.float32) / half))).astype(np.float32)
    ang = pos.astype(F32)[:, None] * inv
    return jnp.cos(ang), jnp.sin(ang)


class _RoutingPlan(NamedTuple):
    pos: jax.Array
    pad_pos: jax.Array
    n_rows: int
    tile_expert: jax.Array
    tile_row_start: jax.Array
    n_valid: jax.Array
    n_active: jax.Array
    dst_tiles: jax.Array


ROW_ALIGN = 8


def _routing_plan(experts, ranks, sizes, tm):
    k, s = experts.shape
    flat = experts.reshape(-1)
    rank = ranks.reshape(-1)
    padded = (sizes + ROW_ALIGN - 1) // ROW_ALIGN * ROW_ALIGN
    seg_end = jnp.cumsum(padded)
    seg_start = seg_end - padded
    pos = seg_start[flat] + rank
    n_rows = k * s + N_EXPERTS * (ROW_ALIGN - 1)

    i_pad = jnp.arange(ROW_ALIGN - 1, dtype=jnp.int32)
    is_pad = i_pad[None, :] < (padded - sizes)[:, None]
    spare = jnp.cumsum(jnp.logical_not(is_pad).reshape(-1).astype(jnp.int32)).reshape(is_pad.shape) - 1
    pad_pos = jnp.where(is_pad, (seg_start + sizes)[:, None] + i_pad[None, :], seg_end[-1] + spare).reshape(-1)

    tiles = (sizes + tm - 1) // tm
    tile_end = jnp.cumsum(tiles)
    n_active = tile_end[-1]
    n_tiles = k * s // tm + N_EXPERTS
    tile_ids = jnp.minimum(jnp.arange(n_tiles, dtype=jnp.int32), n_active - 1)
    tile_expert = jnp.sum((tile_end[None, :] <= tile_ids[:, None]).astype(jnp.int32), axis=1)
    local = tile_ids - (tile_end - tiles)[tile_expert]
    tile_row_start = seg_start[tile_expert] + local * tm
    n_valid = jnp.clip(sizes[tile_expert] - local * tm, 0, tm)
    slot_of_row = jnp.zeros((n_rows,), jnp.int32).at[pos].set(jnp.arange(k * s, dtype=jnp.int32))
    rows = jnp.minimum(tile_row_start[:, None] + jnp.arange(tm, dtype=jnp.int32)[None, :], n_rows - 1)
    return _RoutingPlan(pos.reshape(k, s), pad_pos.astype(jnp.int32), n_rows, tile_expert,
                        tile_row_start.astype(jnp.int32), n_valid.astype(jnp.int32),
                        n_active.reshape(1).astype(jnp.int32), slot_of_row[rows])


def _grouped_moe(x, x_bf, routing, w_gate, w_up, w_down, sh_gate, sh_up, sh_down, ln_g, ln_b, expert_tm=256):
    s, d = x.shape
    experts, gates, ranks, counts = routing
    plan = _routing_plan(experts, ranks, counts[:, 0].astype(jnp.int32), expert_tm)
    xs = scatter_rows(x, plan.pos, plan.pad_pos, plan.n_rows)
    y = routed_experts(xs, plan, TOP_K * s, w_gate, w_up, w_down, expert_tm)
    shared = shared_expert(x_bf, sh_gate, sh_up, sh_down)
    return moe_combine(x, shared, gates, y.reshape(TOP_K, s, d), ln_g, ln_b)


def _rope_lane_tables(pos, half, theta, period, offset=0, limit=None):
    cos, sin = _rope_tables(pos, 2 * half, theta)
    lane = np.arange(LANES)
    p = lane % period - offset
    in_range = np.ones(LANES, bool) if limit is None else lane < limit
    first = (p >= 0) & (p < half) & in_range
    second = (p >= half) & (p < 2 * half) & in_range
    idx = np.where(first | second, p % half, 0)
    cos_l, sin_l = cos[:, idx], sin[:, idx]
    return jnp.stack([jnp.where(first | second, cos_l, 1.0),
                      jnp.where(second, sin_l, 0.0),
                      jnp.where(first, -sin_l, 0.0)])


def _mla_dsa_mixer(x, pos, w_in, q_norm, w_uq, kv_norm, w_ukv):
    cq, ckv, krope, dq, dkv, iq, ik, iw = jnp.split(w_in, [int(c) for c in np.cumsum(SPLITS_AB)[:-1]], axis=1)
    w_in_l = jnp.concatenate([cq, ckv, dq, dkv, iq, krope, ik, iw,
                              jnp.zeros((w_in.shape[0], LANES - IDX_HEADS), w_in.dtype)], axis=1)
    w_uq3 = w_uq.reshape(MLA_Q_LORA, MLA_HEADS, MLA_NOPE + MLA_ROPE)
    w_uq_l = jnp.concatenate([w_uq3[:, :, :MLA_NOPE].reshape(MLA_Q_LORA, -1),
                              jnp.pad(w_uq3[:, :, MLA_NOPE:], ((0, 0), (0, 0), (0, LANES - MLA_ROPE))
                                      ).reshape(MLA_Q_LORA, -1)], axis=1)
    w_ukv3 = w_ukv.reshape(MLA_KV_LORA, MLA_HEADS, MLA_NOPE + MLA_V)
    w_ukv_l = jnp.concatenate([w_ukv3[:, :, :MLA_NOPE].reshape(MLA_KV_LORA, -1),
                               w_ukv3[:, :, MLA_NOPE:].reshape(MLA_KV_LORA, -1)], axis=1)

    tab_m = _rope_lane_tables(pos, MLA_ROPE // 2, ROPE_THETA, LANES)
    tab_d = _rope_lane_tables(pos, DSA_ROT // 2, ROPE_THETA, LANES)
    tab_i = _rope_lane_tables(pos, IDX_ROT // 2, ROPE_THETA, IDX_DIM)
    tab_k = _rope_lane_tables(pos, MLA_ROPE // 2, ROPE_THETA, LANES, limit=MLA_ROPE)
    tab_ik = _rope_lane_tables(pos, IDX_ROT // 2, ROPE_THETA, LANES, offset=MLA_ROPE)
    tab_ki = jnp.concatenate([(tab_k[0] * tab_ik[0])[None], tab_k[1:], tab_ik[1:]])

    h0 = matmul(x, w_in_l, F32, 1024, 512, name="l0_in_proj")
    q_cat, k_nope, v_a, k_pe, qb, kb, vb, iq_r, ik_lo, ik_hi, iw_s = l0_attention_operands(
        h0, q_norm, w_uq_l, kv_norm, w_ukv_l, (tab_m, tab_d, tab_i, tab_ki))
    out_a = mla_attention(q_cat, k_nope, k_pe, v_a)
    out_b = dsa_attention(iq_r, iw_s, ik_lo, ik_hi, qb, kb, vb)
    return out_a, out_b


def _retention_mixer(x_bf, pos, w_in, gn_g, gn_b):
    h1 = matmul(x_bf, w_in, BF16, 2048, 512, name="l1_in_proj")
    cos, sin = _rope_tables(pos, RET_QK_DIM, RET_THETA)
    return retention(h1, cos, sin, gn_g, gn_b)


def kernel(x, positions, router_w, router_b, l0_w_in, l0_mla_q_norm, l0_mla_w_uq, l0_mla_kv_norm, l0_mla_w_ukv, l0_w_out, l1_w_in, l1_ret_gn_g, l1_ret_gn_b, l1_w_out, l0_ln_mix_g, l0_ln_mix_b, l0_moe_w_gate, l0_moe_w_up, l0_moe_w_down, l0_sh_gate, l0_sh_up, l0_sh_down, l0_ln_ffn_g, l0_ln_ffn_b, l1_ln_mix_g, l1_ln_mix_b, l1_moe_w_gate, l1_moe_w_up, l1_moe_w_down, l1_sh_gate, l1_sh_up, l1_sh_down, l1_ln_ffn_g, l1_ln_ffn_b):
    assert x.shape[0] == 1
    xt = x[0]
    pos = positions[0]

    out_a, out_b = _mla_dsa_mixer(xt, pos, l0_w_in, l0_mla_q_norm, l0_mla_w_uq, l0_mla_kv_norm, l0_mla_w_ukv)
    xt, xt_bf, *routing = out_proj_ln_route((out_a, out_b), l0_w_out, xt, l0_ln_mix_g, l0_ln_mix_b,
                                            router_w, router_b, name="l0_out_proj_ln_route")
    xt, xt_bf = _grouped_moe(xt, xt_bf, routing, l0_moe_w_gate, l0_moe_w_up, l0_moe_w_down,
                             l0_sh_gate, l0_sh_up, l0_sh_down, l0_ln_ffn_g, l0_ln_ffn_b)

    ret = _retention_mixer(xt_bf, pos, l1_w_in, l1_ret_gn_g, l1_ret_gn_b)
    xt, xt_bf, *routing = out_proj_ln_route((ret,), l1_w_out, xt, l1_ln_mix_g, l1_ln_mix_b,
                                            router_w, router_b, name="l1_out_proj_ln_route")
    xt, _ = _grouped_moe(xt, xt_bf, routing, l1_moe_w_gate, l1_moe_w_up, l1_moe_w_down,
                         l1_sh_gate, l1_sh_up, l1_sh_down, l1_ln_ffn_g, l1_ln_ffn_b)
    return xt[None]
```

```python
import functools
from typing import NamedTuple

import numpy as np
import jax
import jax.numpy as jnp
from jax import lax
from jax.experimental import pallas as pl
from jax.experimental.pallas import tpu as pltpu

F32 = jnp.float32
BF16 = jnp.bfloat16

DEPTH = 2
ROPE_THETA = 500000.0
MLA_HEADS = 8
MLA_Q_LORA = 512
MLA_KV_LORA = 256
MLA_NOPE = 128
MLA_ROPE = 64
MLA_V = 128
DSA_HEADS = 8
DSA_KV_HEADS = 2
DSA_HEAD_DIM = 128
DSA_ROT = DSA_HEAD_DIM // 4
IDX_HEADS = 16
IDX_DIM = 64
IDX_ROT = IDX_DIM // 4
IDX_TOPK_MAX = 256
RET_HEADS = 8
RET_QK_DIM = 256
RET_V_DIM = 512
RET_CHUNK = 128
RET_THETA = 10000.0
N_EXPERTS = 64
N_GROUPS = 8
EXPERTS_PER_GROUP = N_EXPERTS // N_GROUPS
TOP_K = 2
DEEPNORM_ALPHA = (2.0 * DEPTH) ** 0.25

SPLITS_AB = (MLA_Q_LORA, MLA_KV_LORA, MLA_ROPE, DSA_HEADS * DSA_HEAD_DIM,
             2 * DSA_KV_HEADS * DSA_HEAD_DIM, IDX_HEADS * IDX_DIM, IDX_DIM, IDX_HEADS)

VMEM_LIMIT_BYTES = 56 * 1024 * 1024
LANES = 128
MASKED_SCORE = -1e30
LOG2_E = 1.4426950408889634
DMA_ISSUE_UNROLL = 8
INT32_MIN = -2 ** 31


def _params(*sem):
    return pltpu.CompilerParams(dimension_semantics=sem, vmem_limit_bytes=VMEM_LIMIT_BYTES)


def _mm_kernel(a_ref, b_ref, o_ref):
    o_ref[...] = jnp.dot(a_ref[...].astype(BF16), b_ref[...].astype(BF16),
                         preferred_element_type=F32).astype(o_ref.dtype)


def matmul(a, b, out_dtype, tm, tn, *, name):
    m, k = a.shape
    n = b.shape[1]
    assert b.shape[0] == k and m % tm == 0 and n % tn == 0
    return pl.pallas_call(
        _mm_kernel,
        grid=(m // tm, n // tn),
        in_specs=[pl.BlockSpec((tm, k), lambda i, j: (i, 0)), pl.BlockSpec((k, tn), lambda i, j: (0, j))],
        out_specs=pl.BlockSpec((tm, tn), lambda i, j: (i, j)),
        out_shape=jax.ShapeDtypeStruct((m, n), out_dtype),
        compiler_params=_params("parallel", "parallel"),
        name=name,
    )(a, b)


_H0_CQ = 0
_H0_CKV = _H0_CQ + MLA_Q_LORA
_H0_DQ = _H0_CKV + MLA_KV_LORA
_H0_DKV = _H0_DQ + DSA_HEADS * DSA_HEAD_DIM
_H0_IQ = _H0_DKV + 2 * DSA_KV_HEADS * DSA_HEAD_DIM
_H0_KROPE_IK = _H0_IQ + IDX_HEADS * IDX_DIM
_H0_IW = _H0_KROPE_IK + LANES
_H0_WIDTH = _H0_IW + LANES


def _rope_lanes(x, tab_ref, shifts):
    out = x * tab_ref[0]
    for i, shift in enumerate(shifts):
        out = out + pltpu.roll(x, shift, 1) * tab_ref[1 + i]
    return out


def _l0_prep_kernel(h_ref, qn_ref, kvn_ref, wuq_ref, wukv_ref, tm_ref, td_ref, ti_ref, tki_ref,
                    qcat_ref, knope_ref, va_ref, kpe_ref, qb_ref, kb_ref, vb_ref, iq_ref, iklo_ref, ikhi_ref,
                    iw_ref):
    def rms(x, g_ref):
        return (x * lax.rsqrt(jnp.mean(x * x, axis=-1, keepdims=True) + 1e-6) * g_ref[...]).astype(BF16)

    def slab(col, j=0):
        return h_ref[:, col + j * LANES:col + (j + 1) * LANES]

    half_m, half_d, half_i = MLA_ROPE // 2, DSA_ROT // 2, IDX_ROT // 2
    shifts_m = (half_m, LANES - half_m)
    shifts_d = (half_d, LANES - half_d)
    shifts_i = (half_i, LANES - half_i)

    qa = jnp.dot(rms(h_ref[:, _H0_CQ:_H0_CQ + MLA_Q_LORA], qn_ref), wuq_ref[...], preferred_element_type=F32)
    kva = jnp.dot(rms(h_ref[:, _H0_CKV:_H0_CKV + MLA_KV_LORA], kvn_ref), wukv_ref[...],
                  preferred_element_type=F32)
    n_nope = MLA_HEADS * MLA_NOPE
    knope_ref[...] = kva[:, :n_nope].astype(BF16)
    va_ref[...] = kva[:, n_nope:].astype(BF16)
    q_scale = (MLA_NOPE + MLA_ROPE) ** -0.5 * LOG2_E
    for h in range(MLA_HEADS):
        nope = qa[:, h * LANES:(h + 1) * LANES]
        pe = qa[:, n_nope + h * LANES:n_nope + (h + 1) * LANES]
        qcat_ref[:, 2 * h * LANES:(2 * h + 1) * LANES] = (nope * q_scale).astype(BF16)
        qcat_ref[:, (2 * h + 1) * LANES:(2 * h + 2) * LANES] = (_rope_lanes(pe, tm_ref, shifts_m) * q_scale).astype(BF16)

    d_scale = DSA_HEAD_DIM ** -0.5 * LOG2_E
    for h in range(DSA_HEADS):
        qb_ref[:, h * LANES:(h + 1) * LANES] = (_rope_lanes(slab(_H0_DQ, h), td_ref, shifts_d) * d_scale).astype(BF16)
    for g in range(DSA_KV_HEADS):
        kb_ref[:, g * LANES:(g + 1) * LANES] = _rope_lanes(slab(_H0_DKV, g), td_ref, shifts_d).astype(BF16)
    vb_ref[...] = h_ref[:, _H0_DKV + DSA_KV_HEADS * LANES:_H0_DKV + 2 * DSA_KV_HEADS * LANES].astype(BF16)
    for j in range(IDX_HEADS * IDX_DIM // LANES):
        iq_ref[:, j * LANES:(j + 1) * LANES] = _rope_lanes(slab(_H0_IQ, j), ti_ref, shifts_i).astype(BF16)

    ki = _rope_lanes(slab(_H0_KROPE_IK), tki_ref, shifts_m + shifts_i)
    lane = lax.broadcasted_iota(jnp.int32, ki.shape, 1)
    kpe_ref[...] = jnp.where(lane < MLA_ROPE, ki, 0.0).astype(BF16)
    ik_hi = jnp.where(lane >= MLA_ROPE, ki, 0.0)
    ikhi_ref[...] = ik_hi.astype(BF16)
    iklo_ref[...] = pltpu.roll(ik_hi, LANES - MLA_ROPE, 1).astype(BF16)
    iw_ref[...] = h_ref[:, _H0_IW:_H0_IW + IDX_HEADS] * (IDX_HEADS ** -0.5 * IDX_DIM ** -0.5)


def l0_attention_operands(h0, q_norm, w_uq, kv_norm, w_ukv, tables, tm=256):
    s = h0.shape[0]
    bf = lambda n: jax.ShapeDtypeStruct((s, n), BF16)
    row = lambda n: pl.BlockSpec((tm, n), lambda i: (i, 0))
    whole = lambda a: pl.BlockSpec(a.shape, lambda i: (0,) * a.ndim)
    tab = lambda t: pl.BlockSpec((t.shape[0], tm, LANES), lambda i: (0, i, 0))
    widths = [2 * MLA_HEADS * LANES, MLA_HEADS * MLA_NOPE, MLA_HEADS * MLA_V, LANES, DSA_HEADS * DSA_HEAD_DIM,
              DSA_KV_HEADS * DSA_HEAD_DIM, DSA_KV_HEADS * DSA_HEAD_DIM, IDX_HEADS * IDX_DIM, LANES, LANES]
    w_uq = w_uq.astype(BF16)
    w_ukv = w_ukv.astype(BF16)
    q_norm = q_norm.reshape(1, -1)
    kv_norm = kv_norm.reshape(1, -1)
    return pl.pallas_call(
        _l0_prep_kernel,
        grid=(s // tm,),
        in_specs=[row(_H0_WIDTH), whole(q_norm), whole(kv_norm), whole(w_uq), whole(w_ukv)] + [tab(t) for t in tables],
        out_specs=[row(n) for n in widths] + [row(IDX_HEADS)],
        out_shape=[bf(n) for n in widths] + [jax.ShapeDtypeStruct((s, IDX_HEADS), F32)],
        compiler_params=_params("parallel"),
        name="l0_attention_operands",
    )(h0, q_norm, kv_norm, w_uq, w_ukv, *tables)


def _layer_norm(z, g, b):
    mu = jnp.mean(z, axis=-1, keepdims=True)
    zc = z - mu
    var = jnp.mean(zc * zc, axis=-1, keepdims=True)
    return zc * lax.rsqrt(var + 1e-5) * g + b


def _proj_ln_route_kernel(*refs, n_parts):
    a_refs = refs[:n_parts]
    (w_ref, x_ref, g_ref, b_ref, rwt_ref, rb_ref,
     o_ref, obf_ref, e_ref, gate_ref, rank_ref, count_ref, run_sc) = refs[n_parts:]
    mix = None
    k0 = 0
    for a_ref in a_refs:
        k = a_ref.shape[1]
        part = jnp.dot(a_ref[...], w_ref[k0:k0 + k, :], preferred_element_type=F32)
        mix = part if mix is None else mix + part
        k0 += k
    out = _layer_norm(DEEPNORM_ALPHA * x_ref[...] + mix, g_ref[...], b_ref[...])
    out_bf = out.astype(BF16)
    o_ref[...] = out
    obf_ref[...] = out_bf
    experts, gate_ref[...] = _route_tokens(out_bf, rwt_ref[...], rb_ref[...])
    e_ref[...] = experts

    @pl.when(pl.program_id(0) == 0)
    def _():
        run_sc[...] = jnp.zeros_like(run_sc)

    tm = out.shape[0]
    expert_id = lax.broadcasted_iota(jnp.int32, (N_EXPERTS, tm), 0)
    before = (lax.broadcasted_iota(jnp.int32, (tm, tm), 0) < lax.broadcasted_iota(jnp.int32, (tm, tm), 1))
    before = jnp.where(before, 1.0, 0.0).astype(BF16)
    base = run_sc[...][:, :1]
    ranks = []
    for k in range(TOP_K):
        onehot = expert_id == experts[k:k + 1, :]
        prefix = jnp.dot(jnp.where(onehot, 1.0, 0.0).astype(BF16), before, preferred_element_type=F32)
        ranks.append(jnp.sum(jnp.where(onehot, prefix + base, 0.0), axis=0, keepdims=True))
        base = base + jnp.sum(jnp.where(onehot, 1.0, 0.0), axis=1, keepdims=True)
    rank_ref[...] = jnp.concatenate(ranks, axis=0).astype(jnp.int32)
    run_sc[...] = jnp.broadcast_to(base, run_sc.shape)
    count_ref[...] = run_sc[...]


def out_proj_ln_route(a_parts, w, x, g, b, router_w, router_b, *, name, tm=512):
    s, d = x.shape
    w = w.astype(BF16)
    assert sum(a.shape[1] for a in a_parts) == w.shape[0] and w.shape[1] == d
    row = pl.BlockSpec((tm, d), lambda i: (i, 0))
    vec = pl.BlockSpec((1, d), lambda i: (0, 0))
    slots = pl.BlockSpec((TOP_K, tm), lambda i: (0, i))
    resident = lambda shape: pl.BlockSpec(shape, lambda i: (0, 0), pipeline_mode=pl.Buffered(1))
    return pl.pallas_call(
        functools.partial(_proj_ln_route_kernel, n_parts=len(a_parts)),
        grid=(s // tm,),
        in_specs=[pl.BlockSpec((tm, a.shape[1]), lambda i: (i, 0)) for a in a_parts]
        + [resident(w.shape), row, vec, vec, resident((N_EXPERTS, d)), resident((N_EXPERTS, 1))],
        out_specs=[row, row, slots, slots, slots, pl.BlockSpec((N_EXPERTS, LANES), lambda i: (0, 0))],
        out_shape=[jax.ShapeDtypeStruct((s, d), F32), jax.ShapeDtypeStruct((s, d), BF16),
                   jax.ShapeDtypeStruct((TOP_K, s), jnp.int32), jax.ShapeDtypeStruct((TOP_K, s), F32),
                   jax.ShapeDtypeStruct((TOP_K, s), jnp.int32), jax.ShapeDtypeStruct((N_EXPERTS, LANES), F32)],
        scratch_shapes=[pltpu.VMEM((N_EXPERTS, LANES), F32)],
        compiler_params=_params("arbitrary"),
        name=name,
    )(*a_parts, w, x, g.reshape(1, d), b.reshape(1, d), router_w.T.astype(BF16), router_b.reshape(N_EXPERTS, 1))


def _lane_tile(x, n):
    return jnp.tile(x, (1, n))


def _softmax_step(s, v_ext, m_ref, acc_ref):
    tk = s.shape[1]
    m_prev = m_ref[...]
    m_new = jnp.maximum(m_prev, jnp.max(s, axis=1)[:, None])
    p = jnp.exp2(s - _lane_tile(m_new, tk // LANES))
    alpha = jnp.exp2(m_prev - m_new)
    pv = jnp.dot(p.astype(BF16), v_ext, preferred_element_type=F32)
    acc_ref[...] = _lane_tile(alpha, 2) * acc_ref[...] + pv
    m_ref[...] = m_new


def _mla_kernel(q_ref, kn_ref, kp_ref, v_ref, o_ref, m_sc, acc_sc, *, tq, tk, hp):
    qi = pl.program_id(1)
    m_sc[...] = jnp.full_like(m_sc, MASKED_SCORE)
    acc_sc[...] = jnp.zeros_like(acc_sc)
    ones = jnp.ones((tk, LANES), BF16)
    per_block = tq // tk

    def attend(c, diagonal_offset):
        kp = kp_ref[c]
        kn = kn_ref[c]
        v = v_ref[c]
        for j in range(hp):
            q = q_ref[:, j * 2 * LANES:(j + 1) * 2 * LANES]
            k = jnp.concatenate([kn[:, j * LANES:(j + 1) * LANES], kp], axis=1)
            s = lax.dot_general(q, k, (((1,), (1,)), ((), ())), preferred_element_type=F32)
            if diagonal_offset is not None:
                row = lax.broadcasted_iota(jnp.int32, (tq, tk), 0)
                col = diagonal_offset * tk + lax.broadcasted_iota(jnp.int32, (tq, tk), 1)
                s = jnp.where(col <= row, s, MASKED_SCORE)
            v_ext = jnp.concatenate([v[:, j * LANES:(j + 1) * LANES], ones], axis=1)
            _softmax_step(s, v_ext, m_sc.at[j], acc_sc.at[j])

    def below_diagonal(c, carry):
        attend(c, None)
        return carry

    lax.fori_loop(0, qi * per_block, below_diagonal, 0)
    for d in range(per_block):
        attend(qi * per_block + d, d)
    for j in range(hp):
        acc = acc_sc[j]
        o_ref[:, j * LANES:(j + 1) * LANES] = (acc[:, :LANES] / acc[:, LANES:]).astype(o_ref.dtype)


def mla_attention(q, k_nope, k_pe, v, tq=1024, tk=512, hp=2):
    s = q.shape[0]
    h = MLA_HEADS
    nc = s // tk
    assert tq % tk == 0 and h % hp == 0
    per_head = pl.BlockSpec((nc, tk, hp * LANES), lambda hh, qi: (0, 0, hh))
    return pl.pallas_call(
        functools.partial(_mla_kernel, tq=tq, tk=tk, hp=hp),
        grid=(h // hp, s // tq),
        in_specs=[
            pl.BlockSpec((tq, hp * 2 * LANES), lambda hh, qi: (qi, hh)),
            per_head,
            pl.BlockSpec((nc, tk, LANES), lambda hh, qi: (0, 0, 0)),
            per_head,
        ],
        out_specs=pl.BlockSpec((tq, hp * LANES), lambda hh, qi: (qi, hh)),
        out_shape=jax.ShapeDtypeStruct((s, h * MLA_V), BF16),
        scratch_shapes=[pltpu.VMEM((hp, tq, LANES), F32), pltpu.VMEM((hp, tq, 2 * LANES), F32)],
        compiler_params=_params("parallel", "arbitrary"),
        name="mla_attention",
    )(q, k_nope.reshape(nc, tk, h * MLA_NOPE), k_pe.reshape(nc, tk, LANES), v.reshape(nc, tk, h * MLA_V))


def _sortable_key(x):
    bits = pltpu.bitcast(x, jnp.int32)
    return bits ^ ((bits >> 31) & 0x7FFFFFFF)


def _dsa_kernel(iq_ref, iw_ref, iklo_ref, ikhi_ref, q_ref, k_ref, v_ref, o_ref, key_sc, wb_sc, m_sc, acc_sc,
                *, tq, tk, n_sel):
    i = pl.program_id(0)
    n_chunks = ((i + 1) * tq + tk - 1) // tk
    rep = DSA_HEADS // DSA_KV_HEADS
    lane_tiles = tk // LANES
    heads_per_block = LANES // IDX_DIM

    assert heads_per_block == 2
    iq_blocks = jnp.concatenate(
        [iq_ref[:, j * LANES:(j + 1) * LANES] for j in range(IDX_HEADS // heads_per_block)], axis=0)
    iw = iw_ref[...]
    for h in range(IDX_HEADS):
        wb_sc[h] = jnp.broadcast_to(iw[:, h:h + 1], (tq, LANES))

    def index_chunk(c, carry):
        ik = jnp.concatenate([iklo_ref[c], ikhi_ref[c]], axis=0)
        d = lax.dot_general(iq_blocks, ik, (((1,), (1,)), ((), ())), preferred_element_type=F32)
        d = jnp.maximum(d, 0.0)
        score = None
        for h in range(IDX_HEADS):
            j, part = divmod(h, heads_per_block)
            term = _lane_tile(wb_sc[h], lane_tiles) * d[j * tq:(j + 1) * tq, part * tk:(part + 1) * tk]
            score = term if score is None else score + term
        key_sc[c] = _sortable_key(score)
        return carry

    lax.fori_loop(0, n_chunks, index_chunk, 0)
    last = n_chunks - 1
    row = i * tq + lax.broadcasted_iota(jnp.int32, (tq, tk), 0)
    col = last * tk + lax.broadcasted_iota(jnp.int32, (tq, tk), 1)
    key_sc[last] = jnp.where(col <= row, key_sc[last], INT32_MIN)

    def search_bit(b, thr):
        cand = thr + lax.shift_left(jnp.int32(1), 31 - b)

        def count_chunk(c, cnt):
            key = key_sc[c]
            for j in range(lane_tiles):
                cnt = cnt + jnp.where(key[:, j * LANES:(j + 1) * LANES] >= cand, 1.0, 0.0)
            return cnt

        cnt = lax.fori_loop(0, n_chunks, count_chunk, jnp.zeros((tq, LANES), F32))
        return jnp.where(jnp.sum(cnt, axis=1)[:, None] >= n_sel, cand, thr)

    thr = lax.fori_loop(0, 32, search_bit, jnp.full((tq, LANES), INT32_MIN, jnp.int32))
    thr = _lane_tile(jnp.maximum(thr, INT32_MIN + 1), lane_tiles)

    m_sc[...] = jnp.full_like(m_sc, MASKED_SCORE)
    acc_sc[...] = jnp.zeros_like(acc_sc)
    ones = jnp.ones((tk, LANES), BF16)

    def attend_chunk(c, carry):
        bias = jnp.where(key_sc[c] >= thr, 0.0, MASKED_SCORE)
        kc = k_ref[c]
        vc = v_ref[c]
        for g in range(DSA_KV_HEADS):
            cols = slice(g * DSA_HEAD_DIM, (g + 1) * DSA_HEAD_DIM)
            qg = jnp.concatenate(
                [q_ref[:, (g * rep + r) * DSA_HEAD_DIM:(g * rep + r + 1) * DSA_HEAD_DIM] for r in range(rep)], axis=0)
            s = lax.dot_general(qg, kc[:, cols], (((1,), (1,)), ((), ())), preferred_element_type=F32)
            s = (s.reshape(rep, tq, tk) + bias[None]).reshape(rep * tq, tk)
            _softmax_step(s, jnp.concatenate([vc[:, cols], ones], axis=1), m_sc.at[g], acc_sc.at[g])
        return carry

    lax.fori_loop(0, n_chunks, attend_chunk, 0)
    for g in range(DSA_KV_HEADS):
        acc = acc_sc[g]
        out = acc[:, :LANES] / acc[:, LANES:]
        for r in range(rep):
            hh = g * rep + r
            o_ref[:, hh * DSA_HEAD_DIM:(hh + 1) * DSA_HEAD_DIM] = out[r * tq:(r + 1) * tq].astype(o_ref.dtype)


def dsa_attention(iq, iw, ik_lo, ik_hi, q, k, v, tq=128, tk=512):
    s = q.shape[0]
    n_sel = min(IDX_TOPK_MAX, s // 4)
    nc = s // tk
    assert tk >= n_sel and s % tk == 0 and tk % tq == 0 and DSA_HEAD_DIM == LANES
    rep = DSA_HEADS // DSA_KV_HEADS
    kvw = DSA_KV_HEADS * DSA_HEAD_DIM
    whole3 = lambda i: (0, 0, 0)
    return pl.pallas_call(
        functools.partial(_dsa_kernel, tq=tq, tk=tk, n_sel=n_sel),
        grid=(s // tq,),
        in_specs=[
            pl.BlockSpec((tq, IDX_HEADS * IDX_DIM), lambda i: (i, 0)),
            pl.BlockSpec((tq, IDX_HEADS), lambda i: (i, 0)),
            pl.BlockSpec((nc, tk, LANES), whole3),
            pl.BlockSpec((nc, tk, LANES), whole3),
            pl.BlockSpec((tq, DSA_HEADS * DSA_HEAD_DIM), lambda i: (i, 0)),
            pl.BlockSpec((nc, tk, kvw), whole3),
            pl.BlockSpec((nc, tk, kvw), whole3),
        ],
        out_specs=pl.BlockSpec((tq, DSA_HEADS * DSA_HEAD_DIM), lambda i: (i, 0)),
        out_shape=jax.ShapeDtypeStruct((s, DSA_HEADS * DSA_HEAD_DIM), BF16),
        scratch_shapes=[
            pltpu.VMEM((nc, tq, tk), jnp.int32),
            pltpu.VMEM((IDX_HEADS, tq, LANES), F32),
            pltpu.VMEM((DSA_KV_HEADS, rep * tq, LANES), F32),
            pltpu.VMEM((DSA_KV_HEADS, rep * tq, 2 * LANES), F32),
        ],
        compiler_params=_params("parallel"),
        name="dsa_attention",
    )(iq, iw, ik_lo.reshape(nc, tk, LANES), ik_hi.reshape(nc, tk, LANES), q, k.reshape(nc, tk, kvw),
      v.reshape(nc, tk, kvw))


def _retention_tables():
    h, c = RET_HEADS, RET_CHUNK
    log_g = np.log(1.0 - 2.0 ** (-5.0 - np.arange(h, dtype=np.float32))).astype(np.float32).astype(np.float64)
    idx = np.arange(c, dtype=np.float64)
    diff = idx[:, None] - idx[None, :]
    decay_in = np.where(diff[None] >= 0, np.exp(np.maximum(diff, 0.0)[None] * log_g[:, None, None]), 0.0)
    xi = np.exp((idx + 1.0)[None, :] * log_g[:, None])
    zeta = np.exp((c - 1.0 - idx)[None, :] * log_g[:, None])
    chunk_decay = np.exp(c * log_g)
    lane = np.ones((1, 1, LANES))
    return (decay_in.astype(np.float32), (xi[:, :, None] * lane).astype(np.float32),
            (zeta[:, :, None] * lane).astype(np.float32),
            (chunk_decay[:, None, None] * np.ones((1, 8, LANES))).astype(np.float32))


def _retention_kernel(q_ref, k_ref, v_ref, g_ref, cos_ref, sin_ref, din_ref, xi_ref, zeta_ref, cd_ref,
                      gng_ref, gnb_ref, o_ref, r_sc):
    n = pl.program_id(1)
    half = RET_QK_DIM // 2

    @pl.when(n == 0)
    def _():
        r_sc[...] = jnp.zeros_like(r_sc)

    scale = RET_QK_DIM ** -0.5
    xi = jnp.concatenate([xi_ref[0]] * (RET_V_DIM // LANES), axis=1)
    zeta = jnp.concatenate([zeta_ref[0]] * (RET_QK_DIM // LANES), axis=1)

    for ci in range(q_ref.shape[0] // RET_CHUNK):
        rows = slice(ci * RET_CHUNK, (ci + 1) * RET_CHUNK)
        cos = cos_ref[rows]
        sin = sin_ref[rows]

        def rope(x):
            x1 = x[:, :half]
            x2 = x[:, half:]
            return x1 * cos - x2 * sin, x1 * sin + x2 * cos

        q1, q2 = rope(q_ref[rows].astype(F32))
        k1, k2 = rope(k_ref[rows].astype(F32))
        qr = jnp.concatenate([q1, q2], axis=1).astype(BF16)
        kr = jnp.concatenate([k1 * scale, k2 * scale], axis=1)
        v = v_ref[rows].astype(BF16)

        inner = lax.dot_general(qr, kr.astype(BF16), (((1,), (1,)), ((), ())), preferred_element_type=F32)
        inner = inner * din_ref[0]
        r = r_sc[...]
        cross = jnp.dot(qr, r.astype(BF16), preferred_element_type=F32)
        o = jnp.dot(inner.astype(BF16), v, preferred_element_type=F32) + cross * xi

        kz = (kr * zeta).astype(BF16)
        upd = lax.dot_general(kz, v, (((0,), (0,)), ((), ())), preferred_element_type=F32)
        r_sc[...] = cd_ref[0][:1, :1] * r + upd

        mu = jnp.mean(o, axis=-1, keepdims=True)
        oc = o - mu
        var = jnp.mean(oc * oc, axis=-1, keepdims=True)
        y = oc * lax.rsqrt(var + 1e-5) * gng_ref[...] + gnb_ref[...]
        gate = g_ref[rows].astype(F32)
        o_ref[rows] = (gate * jax.nn.sigmoid(gate) * y).astype(o_ref.dtype)


def retention(h1, cos, sin, gn_g, gn_b, chunks_per_step=8):
    s = h1.shape[0]
    hh, dk, dv = RET_HEADS, RET_QK_DIM, RET_V_DIM
    c = RET_CHUNK * chunks_per_step
    assert s % c == 0
    din, xi, zeta, cd = (jnp.asarray(t) for t in _retention_tables())
    k_blk0 = hh * dk // dk
    v_blk0 = 2 * hh * dk // dv
    g_blk0 = v_blk0 + hh
    per_head = lambda r, w: pl.BlockSpec((1, r, w), lambda h, n: (h, 0, 0))
    return pl.pallas_call(
        _retention_kernel,
        grid=(hh, s // c),
        in_specs=[
            pl.BlockSpec((c, dk), lambda h, n: (n, h)),
            pl.BlockSpec((c, dk), lambda h, n: (n, k_blk0 + h)),
            pl.BlockSpec((c, dv), lambda h, n: (n, v_blk0 + h)),
            pl.BlockSpec((c, dv), lambda h, n: (n, g_blk0 + h)),
            pl.BlockSpec((c, dk // 2), lambda h, n: (n, 0)),
            pl.BlockSpec((c, dk // 2), lambda h, n: (n, 0)),
            per_head(RET_CHUNK, RET_CHUNK), per_head(RET_CHUNK, LANES), per_head(RET_CHUNK, LANES),
            per_head(8, LANES),
            pl.BlockSpec((1, dv), lambda h, n: (0, h)),
            pl.BlockSpec((1, dv), lambda h, n: (0, h)),
        ],
        out_specs=pl.BlockSpec((c, dv), lambda h, n: (n, h)),
        out_shape=jax.ShapeDtypeStruct((s, hh * dv), BF16),
        scratch_shapes=[pltpu.VMEM((dk, dv), F32)],
        compiler_params=_params("parallel", "arbitrary"),
        name="retention",
    )(h1, h1, h1, h1, cos, sin, din, xi, zeta, cd, gn_g.reshape(1, -1), gn_b.reshape(1, -1))


def _first_argmax(v, idx, n):
    m = jnp.max(v, axis=0, keepdims=True)
    first = jnp.min(jnp.where(v == m, idx, n), axis=0, keepdims=True)
    return m, first


def _route_tokens(x, rwt, rb):
    tm = x.shape[0]
    epg = EXPERTS_PER_GROUP
    logits = lax.dot_general(rwt, x, (((1,), (1,)), ((), ())), preferred_element_type=F32)
    scores = jax.nn.sigmoid(logits)
    biased = scores + rb
    idx = lax.broadcasted_iota(jnp.int32, (epg, tm), 0)

    group_scores = []
    for g in range(N_GROUPS):
        v = biased[g * epg:(g + 1) * epg]
        m1, first = _first_argmax(v, idx, epg)
        m2 = jnp.max(jnp.where(idx == first, -jnp.inf, v), axis=0, keepdims=True)
        group_scores.append(m1 + m2)
    gmax = group_scores[0]
    for g in range(1, N_GROUPS):
        gmax = jnp.maximum(gmax, group_scores[g])
    gsel = jnp.full((1, tm), N_GROUPS, jnp.int32)
    for g in range(N_GROUPS - 1, -1, -1):
        gsel = jnp.where(group_scores[g] == gmax, g, gsel)

    in_biased = jnp.zeros((epg, tm), F32)
    in_scores = jnp.zeros((epg, tm), F32)
    for g in range(N_GROUPS):
        pick = gsel == g
        in_biased = jnp.where(pick, biased[g * epg:(g + 1) * epg], in_biased)
        in_scores = jnp.where(pick, scores[g * epg:(g + 1) * epg], in_scores)
    _, loc1 = _first_argmax(in_biased, idx, epg)
    _, loc2 = _first_argmax(jnp.where(idx == loc1, -jnp.inf, in_biased), idx, epg)
    s1 = jnp.sum(jnp.where(idx == loc1, in_scores, 0.0), axis=0, keepdims=True)
    s2 = jnp.sum(jnp.where(idx == loc2, in_scores, 0.0), axis=0, keepdims=True)
    denom = s1 + s2
    experts = jnp.concatenate([gsel * epg + loc1, gsel * epg + loc2], axis=0)
    return experts, jnp.concatenate([s1 / denom, s2 / denom], axis=0)


def _scatter_rows_kernel(pad_ref, pos_ref, x_ref, xs_hbm, zero_sc, sem, zsem, *, tm, n_pad):
    i = pl.program_id(0)

    def zero_copy(p):
        return pltpu.make_async_copy(zero_sc, xs_hbm.at[pl.ds(pad_ref[p], 1)], zsem)

    @pl.when(i == 0)
    def _():
        zero_sc[...] = jnp.zeros_like(zero_sc)

        def start_zero(p, carry):
            zero_copy(p).start()
            return carry

        lax.fori_loop(0, n_pad, start_zero, 0)

    def start(g, carry):
        for u in range(DMA_ISSUE_UNROLL):
            r = g * DMA_ISSUE_UNROLL + u
            for k in range(TOP_K):
                pltpu.make_async_copy(x_ref.at[pl.ds(r, 1)], xs_hbm.at[pl.ds(pos_ref[0, k, r], 1)],
                                      sem).start(priority=k % 2)
        return carry

    lax.fori_loop(0, tm // DMA_ISSUE_UNROLL, start, 0)
    for k in range(TOP_K):
        pltpu.make_async_copy(x_ref, xs_hbm.at[pl.ds(0, tm)], sem).wait()

    @pl.when(i == 0)
    def _():
        def wait_zero(p, carry):
            zero_copy(p).wait()
            return carry

        lax.fori_loop(0, n_pad, wait_zero, 0)


def scatter_rows(x, pos, pad_pos, n_rows, tm=512):
    s, d = x.shape
    nt = s // tm
    n_pad = pad_pos.shape[0]
    return pl.pallas_call(
        functools.partial(_scatter_rows_kernel, tm=tm, n_pad=n_pad),
        grid_spec=pltpu.PrefetchScalarGridSpec(
            num_scalar_prefetch=1,
            grid=(nt,),
            in_specs=[
                pl.BlockSpec((1, TOP_K, tm), lambda i, pad: (i, 0, 0), memory_space=pltpu.SMEM),
                pl.BlockSpec((tm, d), lambda i, pad: (i, 0)),
            ],
            out_specs=pl.BlockSpec(memory_space=pl.ANY),
            scratch_shapes=[pltpu.VMEM((1, d), x.dtype), pltpu.SemaphoreType.DMA(()), pltpu.SemaphoreType.DMA(())],
        ),
        out_shape=jax.ShapeDtypeStruct((n_rows, d), x.dtype),
        compiler_params=_params("arbitrary"),
        name="scatter_rows",
    )(pad_pos, pos.reshape(TOP_K, nt, tm).transpose(1, 0, 2), x)


def _expert_kernel(te_ref, rs_ref, nv_ref, na_ref, dst_ref, xs_hbm, wg_ref, wu_ref, wd_ref, y_hbm,
                   wg_sc, wu_sc, wd_sc, xbuf, ybuf, lsem, ssem, *, tm):
    j = pl.program_id(0)
    n_active = na_ref[0]
    active = j < n_active
    slot = j % 2
    new_expert = (j == 0) | (te_ref[j] != te_ref[jnp.maximum(j - 1, 0)])
    chunk_sizes = [c for c in (256, 128, 64, 32, 16, 8) if c <= tm]

    def load_tile(t, dst_slot, wait):
        n8 = ((nv_ref[t] + 7) // 8) * 8
        base = rs_ref[t]
        off = jnp.int32(0)
        for c in chunk_sizes:
            take = (n8 & c) != 0

            @pl.when(take)
            def _(off=off, c=c):
                copy = pltpu.make_async_copy(
                    xs_hbm.at[pl.ds(pl.multiple_of(base + off, 8), c)],
                    xbuf.at[dst_slot, pl.ds(pl.multiple_of(off, 8), c)], lsem.at[dst_slot])
                if wait:
                    copy.wait()
                else:
                    copy.start()

            off = off + jnp.where(take, c, 0)

    def wait_scatter(src_slot, n):
        c = tm
        while c >= 1:
            @pl.when((n & c) != 0)
            def _(c=c):
                pltpu.make_async_copy(ybuf.at[src_slot, pl.ds(0, c)], y_hbm.at[pl.ds(0, c)],
                                      ssem.at[src_slot]).wait()
            c //= 2

    @pl.when(j == 0)
    def _():
        xbuf[...] = jnp.zeros_like(xbuf)
        load_tile(0, 0, False)

    @pl.when(j + 1 < n_active)
    def _():
        load_tile(j + 1, 1 - slot, False)

    @pl.when(active & new_expert)
    def _():
        wg_sc[...] = wg_ref[0].astype(BF16)
        wu_sc[...] = wu_ref[0].astype(BF16)
        wd_sc[...] = wd_ref[0].astype(BF16)

    @pl.when(active)
    def _():
        load_tile(j, slot, True)

        @pl.when(j >= 2)
        def _():
            wait_scatter(slot, nv_ref[jnp.maximum(j - 2, 0)])

        x = xbuf[slot].astype(BF16)
        hg = jnp.dot(x, wg_sc[...], preferred_element_type=F32)
        hu = jnp.dot(x, wu_sc[...], preferred_element_type=F32)
        h = hg * jax.nn.sigmoid(hg) * hu
        ybuf[slot] = jnp.dot(h.astype(BF16), wd_sc[...], preferred_element_type=F32)

        def scatter_row(r):
            pltpu.make_async_copy(ybuf.at[slot, pl.ds(r, 1)], y_hbm.at[pl.ds(dst_ref[0, 0, r], 1)],
                                  ssem.at[slot]).start()

        def scatter_group(i, carry):
            for u in range(DMA_ISSUE_UNROLL):
                scatter_row(i * DMA_ISSUE_UNROLL + u)
            return carry

        def scatter_one(r, carry):
            scatter_row(r)
            return carry

        n_groups = lax.shift_right_logical(nv_ref[j], DMA_ISSUE_UNROLL.bit_length() - 1)
        lax.fori_loop(0, n_groups, scatter_group, 0)
        lax.fori_loop(n_groups * DMA_ISSUE_UNROLL, nv_ref[j], scatter_one, 0)

        @pl.when(j == n_active - 1)
        def _():
            wait_scatter(slot, nv_ref[j])

            @pl.when(j >= 1)
            def _():
                wait_scatter(1 - slot, nv_ref[jnp.maximum(j - 1, 0)])


def routed_experts(xs, plan, n_out_rows, w_gate, w_up, w_down, tm):
    d = xs.shape[1]
    f = w_gate.shape[2]
    nt = plan.tile_expert.shape[0]

    def dst_blk(j, te, rs, nv, na):
        return (jnp.minimum(j, na[0] - 1), 0, 0)

    def weight_blk(j, te, rs, nv, na):
        return (te[j], 0, 0)

    return pl.pallas_call(
        functools.partial(_expert_kernel, tm=tm),
        grid_spec=pltpu.PrefetchScalarGridSpec(
            num_scalar_prefetch=4,
            grid=(nt,),
            in_specs=[
                pl.BlockSpec((1, 1, tm), dst_blk, memory_space=pltpu.SMEM),
                pl.BlockSpec(memory_space=pl.ANY),
                pl.BlockSpec((1, d, f), weight_blk),
                pl.BlockSpec((1, d, f), weight_blk),
                pl.BlockSpec((1, f, d), weight_blk),
            ],
            out_specs=pl.BlockSpec(memory_space=pl.ANY),
            scratch_shapes=[pltpu.VMEM((d, f), BF16), pltpu.VMEM((d, f), BF16), pltpu.VMEM((f, d), BF16),
                            pltpu.VMEM((2, tm, d), xs.dtype), pltpu.VMEM((2, tm, d), F32),
                            pltpu.SemaphoreType.DMA((2,)), pltpu.SemaphoreType.DMA((2,))],
        ),
        out_shape=jax.ShapeDtypeStruct((n_out_rows, d), F32),
        compiler_params=_params("arbitrary"),
        name="routed_experts",
    )(plan.tile_expert, plan.tile_row_start, plan.n_valid, plan.n_active, plan.dst_tiles.reshape(nt, 1, tm),
      xs, w_gate, w_up, w_down)


def _shared_kernel(x_ref, wg_ref, wu_ref, wd_ref, o_ref):
    x = x_ref[...]
    hg = jnp.dot(x, wg_ref[...], preferred_element_type=F32)
    hu = jnp.dot(x, wu_ref[...], preferred_element_type=F32)
    h = hg * jax.nn.sigmoid(hg) * hu
    o_ref[...] = jnp.dot(h.astype(BF16), wd_ref[...], preferred_element_type=F32)


def shared_expert(x_bf, wg, wu, wd, tm=512):
    s, d = x_bf.shape
    f = wg.shape[1]
    whole = lambda r, c: pl.BlockSpec((r, c), lambda i: (0, 0))
    return pl.pallas_call(
        _shared_kernel,
        grid=(s // tm,),
        in_specs=[pl.BlockSpec((tm, d), lambda i: (i, 0)), whole(d, f), whole(d, f), whole(f, d)],
        out_specs=pl.BlockSpec((tm, d), lambda i: (i, 0)),
        out_shape=jax.ShapeDtypeStruct((s, d), F32),
        compiler_params=_params("parallel"),
        name="shared_expert",
    )(x_bf, wg.astype(BF16), wu.astype(BF16), wd.astype(BF16))


def _combine_kernel(x_ref, sh_ref, gate_ref, y0_ref, y1_ref, g_ref, b_ref, o_ref, obf_ref):
    gate = gate_ref[...]
    routed = gate[:, 0:1] * y0_ref[0] + gate[:, 1:2] * y1_ref[0]
    z = DEEPNORM_ALPHA * x_ref[...] + (routed + sh_ref[...])
    out = _layer_norm(z, g_ref[...], b_ref[...])
    o_ref[...] = out
    obf_ref[...] = out.astype(BF16)


def moe_combine(x, shared, gates, y, g, b, tm=512):
    s, d = x.shape
    assert TOP_K == 2
    row = pl.BlockSpec((tm, d), lambda i: (i, 0))
    vec = pl.BlockSpec((1, d), lambda i: (0, 0))
    return pl.pallas_call(
        _combine_kernel,
        grid=(s // tm,),
        in_specs=[
            row, row,
            pl.BlockSpec((tm, TOP_K), lambda i: (i, 0)),
            pl.BlockSpec((1, tm, d), lambda i: (0, i, 0)),
            pl.BlockSpec((1, tm, d), lambda i: (1, i, 0)),
            vec, vec,
        ],
        out_specs=[row, row],
        out_shape=[jax.ShapeDtypeStruct((s, d), F32), jax.ShapeDtypeStruct((s, d), BF16)],
        compiler_params=_params("parallel"),
        name="moe_combine",
    )(x, shared, gates.T, y, y, g.reshape(1, d), b.reshape(1, d))


def _rope_tables(pos, rot_dim, theta):
    half = rot_dim // 2
    inv = (1.0 / (theta ** (np.arange(half, dtype=np.float32) / half))).astype(np.float32)
    ang = pos.astype(F32)[:, None] * inv
    return jnp.cos(ang), jnp.sin(ang)


class _RoutingPlan(NamedTuple):
    pos: jax.Array
    pad_pos: jax.Array
    n_rows: int
    tile_expert: jax.Array
    tile_row_start: jax.Array
    n_valid: jax.Array
    n_active: jax.Array
    dst_tiles: jax.Array


ROW_ALIGN = 8


def _routing_plan(experts, ranks, sizes, tm):
    k, s = experts.shape
    flat = experts.reshape(-1)
    rank = ranks.reshape(-1)
    padded = (sizes + ROW_ALIGN - 1) // ROW_ALIGN * ROW_ALIGN
    seg_end = jnp.cumsum(padded)
    seg_start = seg_end - padded
    pos = seg_start[flat] + rank
    n_rows = k * s + N_EXPERTS * (ROW_ALIGN - 1)

    i_pad = jnp.arange(ROW_ALIGN - 1, dtype=jnp.int32)
    is_pad = i_pad[None, :] < (padded - sizes)[:, None]
    spare = jnp.cumsum(jnp.logical_not(is_pad).reshape(-1).astype(jnp.int32)).reshape(is_pad.shape) - 1
    pad_pos = jnp.where(is_pad, (seg_start + sizes)[:, None] + i_pad[None, :], seg_end[-1] + spare).reshape(-1)

    tiles = (sizes + tm - 1) // tm
    tile_end = jnp.cumsum(tiles)
    n_active = tile_end[-1]
    n_tiles = k * s // tm + N_EXPERTS
    tile_ids = jnp.minimum(jnp.arange(n_tiles, dtype=jnp.int32), n_active - 1)
    tile_expert = jnp.sum((tile_end[None, :] <= tile_ids[:, None]).astype(jnp.int32), axis=1)
    local = tile_ids - (tile_end - tiles)[tile_expert]
    tile_row_start = seg_start[tile_expert] + local * tm
    n_valid = jnp.clip(sizes[tile_expert] - local * tm, 0, tm)
    slot_of_row = jnp.zeros((n_rows,), jnp.int32).at[pos].set(jnp.arange(k * s, dtype=jnp.int32))
    rows = jnp.minimum(tile_row_start[:, None] + jnp.arange(tm, dtype=jnp.int32)[None, :], n_rows - 1)
    return _RoutingPlan(pos.reshape(k, s), pad_pos.astype(jnp.int32), n_rows, tile_expert,
                        tile_row_start.astype(jnp.int32), n_valid.astype(jnp.int32),
                        n_active.reshape(1).astype(jnp.int32), slot_of_row[rows])


def _grouped_moe(x, x_bf, routing, w_gate, w_up, w_down, sh_gate, sh_up, sh_down, ln_g, ln_b, expert_tm=256):
    s, d = x.shape
    experts, gates, ranks, counts = routing
    plan = _routing_plan(experts, ranks, counts[:, 0].astype(jnp.int32), expert_tm)
    xs = scatter_rows(x, plan.pos, plan.pad_pos, plan.n_rows)
    y = routed_experts(xs, plan, TOP_K * s, w_gate, w_up, w_down, expert_tm)
    shared = shared_expert(x_bf, sh_gate, sh_up, sh_down)
    return moe_combine(x, shared, gates, y.reshape(TOP_K, s, d), ln_g, ln_b)


def _rope_lane_tables(pos, half, theta, period, offset=0, limit=None):
    cos, sin = _rope_tables(pos, 2 * half, theta)
    lane = np.arange(LANES)
    p = lane % period - offset
    in_range = np.ones(LANES, bool) if limit is None else lane < limit
    first = (p >= 0) & (p < half) & in_range
    second = (p >= half) & (p < 2 * half) & in_range
    idx = np.where(first | second, p % half, 0)
    cos_l, sin_l = cos[:, idx], sin[:, idx]
    return jnp.stack([jnp.where(first | second, cos_l, 1.0),
                      jnp.where(second, sin_l, 0.0),
                      jnp.where(first, -sin_l, 0.0)])


def _mla_dsa_mixer(x, pos, w_in, q_norm, w_uq, kv_norm, w_ukv):
    cq, ckv, krope, dq, dkv, iq, ik, iw = jnp.split(w_in, [int(c) for c in np.cumsum(SPLITS_AB)[:-1]], axis=1)
    w_in_l = jnp.concatenate([cq, ckv, dq, dkv, iq, krope, ik, iw,
                              jnp.zeros((w_in.shape[0], LANES - IDX_HEADS), w_in.dtype)], axis=1)
    w_uq3 = w_uq.reshape(MLA_Q_LORA, MLA_HEADS, MLA_NOPE + MLA_ROPE)
    w_uq_l = jnp.concatenate([w_uq3[:, :, :MLA_NOPE].reshape(MLA_Q_LORA, -1),
                              jnp.pad(w_uq3[:, :, MLA_NOPE:], ((0, 0), (0, 0), (0, LANES - MLA_ROPE))
                                      ).reshape(MLA_Q_LORA, -1)], axis=1)
    w_ukv3 = w_ukv.reshape(MLA_KV_LORA, MLA_HEADS, MLA_NOPE + MLA_V)
    w_ukv_l = jnp.concatenate([w_ukv3[:, :, :MLA_NOPE].reshape(MLA_KV_LORA, -1),
                               w_ukv3[:, :, MLA_NOPE:].reshape(MLA_KV_LORA, -1)], axis=1)

    tab_m = _rope_lane_tables(pos, MLA_ROPE // 2, ROPE_THETA, LANES)
    tab_d = _rope_lane_tables(pos, DSA_ROT // 2, ROPE_THETA, LANES)
    tab_i = _rope_lane_tables(pos, IDX_ROT // 2, ROPE_THETA, IDX_DIM)
    tab_k = _rope_lane_tables(pos, MLA_ROPE // 2, ROPE_THETA, LANES, limit=MLA_ROPE)
    tab_ik = _rope_lane_tables(pos, IDX_ROT // 2, ROPE_THETA, LANES, offset=MLA_ROPE)
    tab_ki = jnp.concatenate([(tab_k[0] * tab_ik[0])[None], tab_k[1:], tab_ik[1:]])

    h0 = matmul(x, w_in_l, F32, 1024, 512, name="l0_in_proj")
    q_cat, k_nope, v_a, k_pe, qb, kb, vb, iq_r, ik_lo, ik_hi, iw_s = l0_attention_operands(
        h0, q_norm, w_uq_l, kv_norm, w_ukv_l, (tab_m, tab_d, tab_i, tab_ki))
    out_a = mla_attention(q_cat, k_nope, k_pe, v_a)
    out_b = dsa_attention(iq_r, iw_s, ik_lo, ik_hi, qb, kb, vb)
    return out_a, out_b


def _retention_mixer(x_bf, pos, w_in, gn_g, gn_b):
    h1 = matmul(x_bf, w_in, BF16, 2048, 512, name="l1_in_proj")
    cos, sin = _rope_tables(pos, RET_QK_DIM, RET_THETA)
    return retention(h1, cos, sin, gn_g, gn_b)


def kernel(x, positions, router_w, router_b, l0_w_in, l0_mla_q_norm, l0_mla_w_uq, l0_mla_kv_norm, l0_mla_w_ukv, l0_w_out, l1_w_in, l1_ret_gn_g, l1_ret_gn_b, l1_w_out, l0_ln_mix_g, l0_ln_mix_b, l0_moe_w_gate, l0_moe_w_up, l0_moe_w_down, l0_sh_gate, l0_sh_up, l0_sh_down, l0_ln_ffn_g, l0_ln_ffn_b, l1_ln_mix_g, l1_ln_mix_b, l1_moe_w_gate, l1_moe_w_up, l1_moe_w_down, l1_sh_gate, l1_sh_up, l1_sh_down, l1_ln_ffn_g, l1_ln_ffn_b):
    assert x.shape[0] == 1
    xt = x[0]
    pos = positions[0]

    out_a, out_b = _mla_dsa_mixer(xt, pos, l0_w_in, l0_mla_q_norm, l0_mla_w_uq, l0_mla_kv_norm, l0_mla_w_ukv)
    xt, xt_bf, *routing = out_proj_ln_route((out_a, out_b), l0_w_out, xt, l0_ln_mix_g, l0_ln_mix_b,
                                            router_w, router_b, name="l0_out_proj_ln_route")
    xt, xt_bf = _grouped_moe(xt, xt_bf, routing, l0_moe_w_gate, l0_moe_w_up, l0_moe_w_down,
                             l0_sh_gate, l0_sh_up, l0_sh_down, l0_ln_ffn_g, l0_ln_ffn_b)

    ret = _retention_mixer(xt_bf, pos, l1_w_in, l1_ret_gn_g, l1_ret_gn_b)
    xt, xt_bf, *routing = out_proj_ln_route((ret,), l1_w_out, xt, l1_ln_mix_g, l1_ln_mix_b,
                                            router_w, router_b, name="l1_out_proj_ln_route")
    xt, _ = _grouped_moe(xt, xt_bf, routing, l1_moe_w_gate, l1_moe_w_up, l1_moe_w_down,
                         l1_sh_gate, l1_sh_up, l1_sh_down, l1_ln_ffn_g, l1_ln_ffn_b)
    return xt[None]
```
